```python
import math
import functools
import jax
import jax.numpy as jnp
from jax import lax
import numpy as np

D_MODEL = 2048
BATCH = 1
SEQ = 16384
DEPTH = 1
DEC_BATCH = 32
DEC_SEQ = 4
PAST_LEN = 16384
PAGE_SIZE = 128

CM_HEADS = 8
CM_HEAD_DIM = D_MODEL // 16
CM_WIDTH = CM_HEADS * CM_HEAD_DIM
CHUNK = 128
NSA_HEADS = 16
NSA_KV_GROUPS = 4
NSA_HPG = NSA_HEADS // NSA_KV_GROUPS
NSA_HEAD_DIM = D_MODEL // 32
NSA_WIDTH = NSA_HEADS * NSA_HEAD_DIM
CMP_LEN = 32
CMP_STRIDE = 16
SEL_BLOCK = 64
N_SEL = 16
WINDOW = 512
QBLOCK = 128
N_BRANCH = 3
BIG = 1e4
REL_BUCKETS = 32
REL_MAX_DIST = 1024
FFN_DIM = 5632
N_SUB = 3
EPS = 1e-6
MIX_WIDTH = CM_WIDTH + NSA_WIDTH
KV_COLS = NSA_KV_GROUPS * NSA_HEAD_DIM
IN_COLS = 2 * CM_WIDTH + NSA_WIDTH + N_BRANCH * 2 * KV_COLS + N_BRANCH * NSA_HEADS

kernel_name = 'hymba_gmlp_nsa_macaron_step'


def rms_norm(x, g):
    xf = x.astype(jnp.float32)
    y = xf * lax.rsqrt(jnp.mean(xf * xf, -1, keepdims=True) + EPS)
    return (y * g.astype(jnp.float32)).astype(x.dtype)


def swiglu(x, w_gate, w_up, w_down):
    return (jax.nn.silu(x @ w_gate) * (x @ w_up)) @ w_down


def rel_bucket(dist):
    n = jnp.maximum(dist, 0)
    max_exact = REL_BUCKETS // 2
    nf = jnp.maximum(n, 1).astype(jnp.float32)
    large = max_exact + (jnp.log(nf / max_exact) / math.log(REL_MAX_DIST / max_exact)
                         * (REL_BUCKETS - max_exact)).astype(jnp.int32)
    large = jnp.minimum(large, REL_BUCKETS - 1)
    return jnp.where(n < max_exact, n, large)


def masked_softmax(logits, mask):
    logits = jnp.where(mask, logits.astype(jnp.float32), -1e30)
    m = jnp.max(logits, -1, keepdims=True)
    p = jnp.where(mask, jnp.exp(logits - m), 0.0)
    return p / jnp.maximum(jnp.sum(p, -1, keepdims=True), 1e-30)


def split_in_proj(a, w_in):
    B, T = a.shape[:2]
    sizes = (CM_WIDTH, CM_WIDTH, NSA_WIDTH, 2 * KV_COLS, 2 * KV_COLS, 2 * KV_COLS, N_BRANCH * NSA_HEADS)
    cuts = [int(c) for c in np.cumsum(sizes)[:-1]]
    u, v, q, kc, ks, kw, g = jnp.split(a @ w_in, cuts, axis=-1)
    kvs = (B, T, 2, NSA_KV_GROUPS, NSA_HEAD_DIM)
    return (u.reshape(B, T, CM_HEADS, CM_HEAD_DIM), v.reshape(B, T, CM_HEADS, CM_HEAD_DIM),
            q.reshape(B, T, NSA_KV_GROUPS, NSA_HPG, NSA_HEAD_DIM),
            kc.reshape(kvs), ks.reshape(kvs), kw.reshape(kvs),
            g.reshape(B, T, NSA_KV_GROUPS, NSA_HPG, N_BRANCH))


def chunk_mix(u, v, cm_norm, cm_ws, cm_bs):
    B, T = u.shape[:2]
    u = jax.nn.gelu(u)
    vf = jax.nn.gelu(v).astype(jnp.float32)
    mu = jnp.mean(vf, -1, keepdims=True)
    var = jnp.mean(jnp.square(vf - mu), -1, keepdims=True)
    vn = ((vf - mu) * lax.rsqrt(var + EPS) * cm_norm).astype(v.dtype)
    t_pad = -(-T // CHUNK) * CHUNK
    vc = jnp.pad(vn, ((0, 0), (0, t_pad - T), (0, 0), (0, 0))).reshape(
        B, t_pad // CHUNK, CHUNK, CM_HEADS, CM_HEAD_DIM)
    s = jnp.einsum('hij,bcjhd->bcihd', jnp.tril(cm_ws), vc) + cm_bs.T[:, :, None]
    s = s.reshape(B, t_pad, CM_HEADS, CM_HEAD_DIM)[:, :T]
    return (u * s).reshape(B, T, CM_WIDTH), vn


def compress(kv, phi_pos, phi_w1, phi_w2):
    B, T = kv.shape[:2]
    n_half = CMP_LEN // CMP_STRIDE
    ch = kv.reshape(B, T // CMP_STRIDE, CMP_STRIDE, 2, NSA_KV_GROUPS, NSA_HEAD_DIM)
    w1 = phi_w1.reshape(n_half, CMP_STRIDE, 2, NSA_HEAD_DIM, NSA_HEAD_DIM)
    parts = jnp.einsum('bnlcgd,mlcde->mbncge', ch, w1)
    n_cmp = T // CMP_STRIDE - n_half + 1
    pre = parts[0, :, 0:n_cmp]
    for m in range(1, n_half):
        pre = pre + parts[m, :, m:m + n_cmp]
    pre = pre + jnp.einsum('lcd,lcde->ce', phi_pos, phi_w1)[:, None, :]
    return jnp.einsum('bncgd,cde->bncge', jax.nn.gelu(pre), phi_w2)


def nsa_attend(q, q_pos, gates, kvc, c_end, fetch_sel, n_sel_blk, kvw, w_pos, rel_bias):
    G, HPG, DK = NSA_KV_GROUPS, NSA_HPG, NSA_HEAD_DIM
    B, Tq = q.shape[:2]
    f32 = jnp.float32
    scale = DK ** -0.5
    tab = rel_bias.reshape(REL_BUCKETS, G, HPG)

    dc = q_pos[:, None] - c_end[None, :]
    lc = (jnp.einsum('bqghd,bcgd->bqghc', q, kvc[:, :, 0]).astype(f32) * scale
          + jnp.transpose(tab[rel_bucket(dc)], (0, 2, 3, 1)))
    pc = masked_softmax(lc, (dc >= 0)[:, None, None, :])
    o_c = jnp.einsum('bqghc,bcgd->bqghd', pc, kvc[:, :, 1].astype(f32))

    ratio = SEL_BLOCK // CMP_STRIDE
    lead = CMP_LEN // CMP_STRIDE - 1
    span = ratio + lead
    imp = pc.sum(3)
    nc = imp.shape[-1]
    imp = jnp.pad(imp, ((0, 0), (0, 0), (0, 0), (lead, ratio * n_sel_blk + span - lead - nc)))
    p_slc = imp[..., 0:ratio * n_sel_blk:ratio]
    for m in range(1, span):
        p_slc = p_slc + imp[..., m:m + ratio * n_sel_blk:ratio]
    j = jnp.arange(n_sel_blk)
    cur = (q_pos // SEL_BLOCK)[:, None]
    forced = (j == 0) | (j == cur) | (j == cur - 1)
    valid = (j * SEL_BLOCK) <= q_pos[:, None]
    score = jnp.where(forced[None, :, None, :], BIG,
                      jnp.where(valid[None, :, None, :], p_slc, -BIG))
    _, idx = lax.top_k(score, min(N_SEL, n_sel_blk))
    K = idx.shape[-1]

    kvs = fetch_sel(idx)
    s_pos = idx[..., None] * SEL_BLOCK + jnp.arange(SEL_BLOCK)
    ds = q_pos[None, :, None, None, None] - s_pos
    g_i = jnp.arange(G)[None, None, :, None, None]
    bias_s = jnp.moveaxis(jnp.transpose(tab, (1, 0, 2))[g_i, rel_bucket(ds)], -1, 3)
    ls = jnp.einsum('bqghd,bqgksd->bqghks', q, kvs[..., 0, :]).astype(f32) * scale + bias_s
    ps = masked_softmax(ls.reshape(B, Tq, G, HPG, K * SEL_BLOCK),
                        (ds >= 0).reshape(B, Tq, G, 1, K * SEL_BLOCK))
    o_s = jnp.einsum('bqghn,bqgnd->bqghd', ps,
                     kvs[..., 1, :].reshape(B, Tq, G, K * SEL_BLOCK, DK).astype(f32))

    dw = q_pos[:, None] - w_pos[None, :]
    lw = (jnp.einsum('bqghd,bwgd->bqghw', q, kvw[:, :, 0]).astype(f32) * scale
          + jnp.transpose(tab[rel_bucket(dw)], (0, 2, 3, 1)))
    mw = (dw >= 0) & (dw <= WINDOW) & (w_pos >= 0)[None, :]
    pw = masked_softmax(lw, mw[:, None, None, :])
    o_w = jnp.einsum('bqghw,bwgd->bqghd', pw, kvw[:, :, 1].astype(f32))

    g = jax.nn.sigmoid(gates.astype(f32))
    o = g[..., 0:1] * o_c + g[..., 1:2] * o_s + g[..., 2:3] * o_w
    return o.astype(q.dtype)


def prompt_mixer(a, w_in, cm_norm, cm_ws, cm_bs, phi_pos, phi_w1, phi_w2, rel_bias):
    B, T = a.shape[:2]
    u, v, q, kv_c, kv_s, kv_w, gates = split_in_proj(a, w_in)
    y_cm, _ = chunk_mix(u, v, cm_norm, cm_ws, cm_bs)
    n_sel_blk = T // SEL_BLOCK
    kvc_tok = compress(kv_c, phi_pos, phi_w1, phi_w2)
    c_end = jnp.arange(kvc_tok.shape[1]) * CMP_STRIDE + CMP_LEN - 1
    sel_blocks = kv_s.reshape(B, n_sel_blk, SEL_BLOCK, 2, NSA_KV_GROUPS, NSA_HEAD_DIM)
    b_i = jnp.arange(B)[:, None, None, None]
    g_i = jnp.arange(NSA_KV_GROUPS)[None, None, :, None]

    def fetch(idx):
        return sel_blocks[b_i, idx, :, :, g_i, :]

    kvw_pad = jnp.pad(kv_w, ((0, 0), (WINDOW, 0), (0, 0), (0, 0), (0, 0)))
    n_qb = T // QBLOCK
    q_blk = q.reshape(B, n_qb, QBLOCK, NSA_KV_GROUPS, NSA_HPG, NSA_HEAD_DIM).swapaxes(0, 1)
    g_blk = gates.reshape(B, n_qb, QBLOCK, NSA_KV_GROUPS, NSA_HPG, N_BRANCH).swapaxes(0, 1)

    def one_block(args):
        qi, gi, bi = args
        qs = bi * QBLOCK
        kvw = lax.dynamic_slice_in_dim(kvw_pad, qs, WINDOW + QBLOCK, axis=1)
        return nsa_attend(qi, qs + jnp.arange(QBLOCK), gi, kvc_tok, c_end, fetch, n_sel_blk,
                          kvw, qs - WINDOW + jnp.arange(WINDOW + QBLOCK), rel_bias)

    o = lax.map(one_block, (q_blk, g_blk, jnp.arange(n_qb)))
    y_nsa = o.swapaxes(0, 1).reshape(B, T, NSA_WIDTH)
    mixed = jnp.concatenate([y_cm, y_nsa], -1)
    return mixed, (kv_c, kv_s, kv_w[:, T - min(WINDOW, T):])


def sample_mixer(a, cache_cmp, cache_slc, win_buf, page_table, w_in, cm_norm, cm_ws, cm_bs,
                 phi_pos, phi_w1, phi_w2, rel_bias):
    B, T = a.shape[:2]
    G, DK = NSA_KV_GROUPS, NSA_HEAD_DIM
    past = page_table.shape[1] * PAGE_SIZE
    u, v, q, kv_c, kv_s, kv_w, gates = split_in_proj(a, w_in)
    y_cm, v_rows = chunk_mix(u, v, cm_norm, cm_ws, cm_bs)

    n_new_blk = -(-T // SEL_BLOCK)
    n_past_blk = past // SEL_BLOCK
    n_sel_blk = n_past_blk + n_new_blk
    pad_new = ((0, 0), (0, n_new_blk * SEL_BLOCK - T), (0, 0), (0, 0), (0, 0))
    past_c = cache_cmp[page_table].reshape(B, past, 2, G, DK)
    kvc_tok = compress(jnp.concatenate([past_c, jnp.pad(kv_c, pad_new)], 1), phi_pos, phi_w1, phi_w2)
    c_end = jnp.arange(kvc_tok.shape[1]) * CMP_STRIDE + CMP_LEN - 1

    blk_per_page = PAGE_SIZE // SEL_BLOCK
    pool = cache_slc.reshape(cache_slc.shape[0], blk_per_page, SEL_BLOCK, 2, G, DK)
    new_blk = jnp.pad(kv_s, pad_new).reshape(B, n_new_blk, SEL_BLOCK, 2, G, DK)
    b_i = jnp.arange(B)[:, None, None, None]
    g_i = jnp.arange(G)[None, None, :, None]

    def fetch(idx):
        jp = jnp.minimum(idx, n_past_blk - 1)
        from_past = pool[page_table[b_i, jp // blk_per_page], jp % blk_per_page, :, :, g_i, :]
        from_new = new_blk[b_i, jnp.clip(idx - n_past_blk, 0, n_new_blk - 1), :, :, g_i, :]
        return jnp.where((idx >= n_past_blk)[..., None, None, None], from_new, from_past)

    wb = win_buf.shape[1]
    kvw = jnp.concatenate([win_buf, kv_w], 1)
    o = nsa_attend(q, past + jnp.arange(T), gates, kvc_tok, c_end, fetch, n_sel_blk,
                   kvw, past - wb + jnp.arange(wb + T), rel_bias)
    mixed = jnp.concatenate([y_cm, o.reshape(B, T, NSA_WIDTH)], -1)
    return mixed, (kv_c, kv_s, kvw[:, T:], v_rows)


def decoder_layer(x, c, mix_fn, w_mod, b_mod, norm_pre, norm_post, ffn_w_gate, ffn_w_up,
                  ffn_w_down, w_out):
    mod = (jax.nn.silu(c) @ w_mod + b_mod).reshape(c.shape[0], 1, N_SUB, 3, D_MODEL)

    def pre(h, i):
        return rms_norm(h, norm_pre[i]) * (1.0 + mod[:, :, i, 1]) + mod[:, :, i, 0]

    def post(h, i, out, res_w):
        return h + res_w * mod[:, :, i, 2] * rms_norm(out, norm_post[i])

    h = post(x, 0, swiglu(pre(x, 0), ffn_w_gate[0], ffn_w_up[0], ffn_w_down[0]), 0.5)
    mixed, state = mix_fn(pre(h, 1))
    h = post(h, 1, mixed @ w_out, 1.0)
    h = post(h, 2, swiglu(pre(h, 2), ffn_w_gate[1], ffn_w_up[1], ffn_w_down[1]), 0.5)
    return h, state


def setup_inputs(seed: int = 0) -> dict:
    key = jax.random.key(seed)
    ks = jax.random.split(key, 24)
    f32 = jnp.float32
    n_pages = PAST_LEN // PAGE_SIZE
    n_phys = (DEC_BATCH * n_pages * 5) // 4
    win_buf = min(WINDOW, PAST_LEN)
    G, DK = NSA_KV_GROUPS, NSA_HEAD_DIM

    def nrm(k, shape, scale=1.0):
        return jax.random.normal(k, shape, f32) * scale

    page_table = jax.random.permutation(ks[7], n_phys)[:DEC_BATCH * n_pages].reshape(
        DEC_BATCH, n_pages).astype(jnp.int32)
    return {
        'x_prompt': nrm(ks[0], (BATCH, SEQ, D_MODEL)),
        'x_sample': nrm(ks[1], (DEC_BATCH, DEC_SEQ, D_MODEL)),
        'cache_cmp_kv': nrm(ks[2], (DEPTH, n_phys, PAGE_SIZE, 2, G, DK)),
        'cache_slc_kv': nrm(ks[3], (DEPTH, n_phys, PAGE_SIZE, 2, G, DK)),
        'state_win_kv': nrm(ks[4], (DEPTH, DEC_BATCH, win_buf, 2, G, DK)),
        'page_table': page_table,
        'c_prompt': nrm(ks[5], (BATCH, D_MODEL)),
        'c_sample': nrm(ks[6], (DEC_BATCH, D_MODEL)),
        'w_mod': nrm(ks[8], (DEPTH, D_MODEL, N_SUB * 3 * D_MODEL), 0.5 * D_MODEL ** -0.5),
        'b_mod': nrm(ks[9], (DEPTH, N_SUB * 3 * D_MODEL), 0.02),
        'norm_pre': 1.0 + nrm(ks[10], (DEPTH, N_SUB, D_MODEL), 0.02),
        'norm_post': 1.0 + nrm(ks[11], (DEPTH, N_SUB, D_MODEL), 0.02),
        'ffn_w_gate': nrm(ks[12], (DEPTH, 2, D_MODEL, FFN_DIM), D_MODEL ** -0.5),
        'ffn_w_up': nrm(ks[13], (DEPTH, 2, D_MODEL, FFN_DIM), D_MODEL ** -0.5),
        'ffn_w_down': nrm(ks[14], (DEPTH, 2, FFN_DIM, D_MODEL), FFN_DIM ** -0.5),
        'w_in': nrm(ks[15], (DEPTH, D_MODEL, IN_COLS), D_MODEL ** -0.5),
        'w_out': nrm(ks[16], (DEPTH, MIX_WIDTH, D_MODEL), MIX_WIDTH ** -0.5),
        'cm_norm': 1.0 + nrm(ks[17], (DEPTH, CM_HEADS, CM_HEAD_DIM), 0.02),
        'cm_ws': nrm(ks[18], (DEPTH, CM_HEADS, CHUNK, CHUNK), CHUNK ** -0.5),
        'cm_bs': 1.0 + nrm(ks[19], (DEPTH, CM_HEADS, CHUNK), 0.02),
        'phi_pos': nrm(ks[20], (DEPTH, CMP_LEN, 2, DK), 0.02),
        'phi_w1': nrm(ks[21], (DEPTH, CMP_LEN, 2, DK, DK), (CMP_LEN * DK) ** -0.5),
        'phi_w2': nrm(ks[22], (DEPTH, 2, DK, DK), DK ** -0.5),
        'rel_bias': nrm(ks[23], (REL_BUCKETS, NSA_HEADS), 0.5),
    }


def reference(x_prompt, x_sample, cache_cmp_kv, cache_slc_kv, state_win_kv, page_table,
              c_prompt, c_sample, w_mod, b_mod, norm_pre, norm_post, ffn_w_gate, ffn_w_up,
              ffn_w_down, w_in, w_out, cm_norm, cm_ws, cm_bs, phi_pos, phi_w1, phi_w2, rel_bias):
    hp, hs = x_prompt, x_sample
    cmp_p, cmp_s, slc_p, slc_s, win_p, win_s, chv_s = [], [], [], [], [], [], []
    for l in range(DEPTH):
        layer = functools.partial(decoder_layer, w_mod=w_mod[l], b_mod=b_mod[l],
                                  norm_pre=norm_pre[l], norm_post=norm_post[l],
                                  ffn_w_gate=ffn_w_gate[l], ffn_w_up=ffn_w_up[l],
                                  ffn_w_down=ffn_w_down[l], w_out=w_out[l])
        mix_w = dict(w_in=w_in[l], cm_norm=cm_norm[l], cm_ws=cm_ws[l], cm_bs=cm_bs[l],
                     phi_pos=phi_pos[l], phi_w1=phi_w1[l], phi_w2=phi_w2[l], rel_bias=rel_bias)
        hp, (pc, ps, pw) = layer(hp, c_prompt, functools.partial(prompt_mixer, **mix_w))
        hs, (sc, ss, sw, sv) = layer(
            hs, c_sample,
            functools.partial(sample_mixer, cache_cmp=cache_cmp_kv[l], cache_slc=cache_slc_kv[l],
                              win_buf=state_win_kv[l], page_table=page_table, **mix_w))
        cmp_p.append(pc)
        cmp_s.append(sc)
        slc_p.append(ps)
        slc_s.append(ss)
        win_p.append(pw)
        win_s.append(sw)
        chv_s.append(sv)
    return (hp, hs, jnp.stack(cmp_p), jnp.stack(cmp_s), jnp.stack(slc_p), jnp.stack(slc_s),
            jnp.stack(win_p), jnp.stack(win_s), jnp.stack(chv_s))
```

```python
import functools
import math

import jax
import jax.numpy as jnp
from jax import lax
from jax.experimental import pallas as pl
from jax.experimental.pallas import tpu as pltpu

F32 = jnp.float32
BF16 = jnp.bfloat16

CM_HEADS = 8
CM_HEAD_DIM = 128
CM_WIDTH = CM_HEADS * CM_HEAD_DIM
CHUNK = 128
NSA_HEADS = 16
NSA_G = 4
NSA_HPG = 4
NSA_DK = 64
NSA_WIDTH = NSA_HEADS * NSA_DK
KV_COLS = NSA_G * NSA_DK
CMP_LEN = 32
CMP_STRIDE = 16
SEL_BLOCK = 64
N_SEL = 16
WINDOW = 512
N_BRANCH = 3
BIG = 1e4
REL_BUCKETS = 32
EPS = 1e-6
PAGE_SIZE = 128

NEG = -1e30
LANES = 128
QT = 128
HQ = NSA_HPG * QT
KT_SEL = 512
BLK_PER_KT = KT_SEL // SEL_BLOCK
N_NEAR = 8
REL_FAR = 790
TN_WIN, TN_MASKED, TN_ZERO, TN_COUNT = 8, 9, 10, 11
WT_LEAD = 64
WT_ROWS = 200
REL_THRESHOLDS = (21, 27, 35, 46, 59, 77, 99, 128, 166, 216, 280, 363, 470, 609, 790)
VMEM_LIMIT = 56 * 1024 * 1024


def _cparams(sem):
    return pltpu.CompilerParams(dimension_semantics=sem, vmem_limit_bytes=VMEM_LIMIT)


def _dot(a, b):
    return jnp.dot(a, b, preferred_element_type=F32)


def _dot_nt(a, b):
    return lax.dot_general(a, b, (((1,), (1,)), ((), ())), preferred_element_type=F32)


def _split_bf16(x):
    hi = x.astype(BF16)
    lo = (x - hi.astype(F32)).astype(BF16)
    return hi, lo


def _dot_f32(a, b):
    a_hi, a_lo = _split_bf16(a)
    b_hi, b_lo = _split_bf16(b)
    return _dot(a_hi, b_hi) + _dot(a_lo, b_hi) + _dot(a_hi, b_lo)


def _rms(x, g):
    return x * lax.rsqrt(jnp.mean(x * x, axis=-1, keepdims=True) + EPS) * g


def _row_spec(arr, tm, ngrid):
    d = arr.shape[-1]
    if arr.shape[0] == 1:
        return pl.BlockSpec((1, d), lambda *idx: (0, 0))
    return pl.BlockSpec((tm, d), lambda *idx: (idx[0], 0))


def _mod_kernel(c_ref, w_ref, b_ref, o_ref):
    c = c_ref[...]
    s = c * jax.nn.sigmoid(c)
    o_ref[...] = _dot_f32(s, w_ref[...]) + b_ref[...]


def _mod_proj(c, w_mod, b_mod):
    m, d = c.shape
    n = w_mod.shape[1]
    tn = 512
    return pl.pallas_call(
        _mod_kernel,
        grid=(n // tn,),
        in_specs=[pl.BlockSpec((m, d), lambda j: (0, 0)),
                  pl.BlockSpec((d, tn), lambda j: (0, j)),
                  pl.BlockSpec((1, tn), lambda j: (0, j))],
        out_specs=pl.BlockSpec((m, tn), lambda j: (0, j)),
        out_shape=jax.ShapeDtypeStruct((m, n), F32),
        compiler_params=_cparams(("arbitrary",)),
    )(c, w_mod, b_mod.reshape(1, n))


def _ffn_kernel(x_ref, sh_ref, sc_ref, gt_ref, gpre_ref, gpost_ref, wg_ref, wu_ref, wd_ref,
                o_ref, a_ref, acc_ref, *, res_w, nf):
    f = pl.program_id(1)

    @pl.when(f == 0)
    def _():
        y = _rms(x_ref[...], gpre_ref[...])
        a_ref[...] = (y * (1.0 + sc_ref[...]) + sh_ref[...]).astype(BF16)
        acc_ref[...] = jnp.zeros_like(acc_ref)

    a = a_ref[...]
    h = _dot(a, wg_ref[...])
    u = _dot(a, wu_ref[...])
    act = (h * jax.nn.sigmoid(h) * u).astype(BF16)
    acc_ref[...] += _dot(act, wd_ref[...])

    @pl.when(f == nf - 1)
    def _():
        o_ref[...] = x_ref[...] + res_w * gt_ref[...] * _rms(acc_ref[...], gpost_ref[...])


def _ffn(x, shift, scale, gate, g_pre, g_post, wg, wu, wd, res_w, tm, tf):
    r, d = x.shape
    fdim = wg.shape[1]
    nf = fdim // tf
    return pl.pallas_call(
        functools.partial(_ffn_kernel, res_w=res_w, nf=nf),
        grid=(r // tm, nf),
        in_specs=[pl.BlockSpec((tm, d), lambda i, f: (i, 0)),
                  _row_spec(shift, tm, 2), _row_spec(scale, tm, 2), _row_spec(gate, tm, 2),
                  pl.BlockSpec((1, d), lambda i, f: (0, 0)),
                  pl.BlockSpec((1, d), lambda i, f: (0, 0)),
                  pl.BlockSpec((d, tf), lambda i, f: (0, f)),
                  pl.BlockSpec((d, tf), lambda i, f: (0, f)),
                  pl.BlockSpec((tf, d), lambda i, f: (f, 0))],
        out_specs=pl.BlockSpec((tm, d), lambda i, f: (i, 0)),
        out_shape=jax.ShapeDtypeStruct((r, d), F32),
        scratch_shapes=[pltpu.VMEM((tm, d), BF16), pltpu.VMEM((tm, d), F32)],
        compiler_params=_cparams(("parallel", "arbitrary")),
    )(x, shift, scale, gate, g_pre, g_post, wg, wu, wd)


IN_SEGS = (CM_WIDTH, CM_WIDTH, NSA_WIDTH, 2 * KV_COLS, 2 * KV_COLS, 2 * KV_COLS, LANES)


def _in_proj_kernel(x_ref, sh_ref, sc_ref, gpre_ref, w_ref, *o_refs):
    y = _rms(x_ref[...], gpre_ref[...])
    a = (y * (1.0 + sc_ref[...]) + sh_ref[...]).astype(BF16)
    z = _dot(a, w_ref[...])
    off = 0
    for o_ref, width in zip(o_refs, IN_SEGS):
        o_ref[...] = z[:, off:off + width]
        off += width


def _in_proj(x, shift, scale, g_pre, w, tm):
    r, d = x.shape
    n = w.shape[1]
    return pl.pallas_call(
        _in_proj_kernel,
        grid=(r // tm,),
        in_specs=[pl.BlockSpec((tm, d), lambda i: (i, 0)),
                  _row_spec(shift, tm, 1), _row_spec(scale, tm, 1),
                  pl.BlockSpec((1, d), lambda i: (0, 0)),
                  pl.BlockSpec((d, n), lambda i: (0, 0), pipeline_mode=pl.Buffered(1))],
        out_specs=[pl.BlockSpec((tm, s), lambda i: (i, 0)) for s in IN_SEGS],
        out_shape=[jax.ShapeDtypeStruct((r, s), F32) for s in IN_SEGS],
        compiler_params=_cparams(("parallel",)),
    )(x, shift, scale, g_pre, w)


def _chunk_mix_kernel(u_ref, v_ref, nrm_ref, ws_ref, bst_ref, y_ref, vn_ref, *, chunk):
    ri = lax.broadcasted_iota(jnp.int32, (chunk, chunk), 0)
    ci = lax.broadcasted_iota(jnp.int32, (chunk, chunk), 1)
    lower = ci <= ri
    for h in range(CM_HEADS):
        sl = slice(h * CM_HEAD_DIM, (h + 1) * CM_HEAD_DIM)
        vf = jax.nn.gelu(v_ref[:, sl])
        mu = jnp.mean(vf, axis=-1, keepdims=True)
        var = jnp.mean(jnp.square(vf - mu), axis=-1, keepdims=True)
        vn = (vf - mu) * lax.rsqrt(var + EPS) * nrm_ref[:, sl]
        vn_ref[:, sl] = vn
        w = jnp.where(lower, ws_ref[h], 0.0).astype(BF16)
        s = _dot(w, vn.astype(BF16)) + bst_ref[:, h:h + 1]
        y_ref[:, sl] = jax.nn.gelu(u_ref[:, sl]) * s


def _chunk_mix(u, v, cm_norm, cm_ws, cm_bs, chunk):
    r = u.shape[0]
    ws = cm_ws[:, :chunk, :chunk]
    bst = cm_bs[:, :chunk].T
    return pl.pallas_call(
        functools.partial(_chunk_mix_kernel, chunk=chunk),
        grid=(r // chunk,),
        in_specs=[pl.BlockSpec((chunk, CM_WIDTH), lambda i: (i, 0)),
                  pl.BlockSpec((chunk, CM_WIDTH), lambda i: (i, 0)),
                  pl.BlockSpec((1, CM_WIDTH), lambda i: (0, 0)),
                  pl.BlockSpec((CM_HEADS, chunk, chunk), lambda i: (0, 0, 0)),
                  pl.BlockSpec((chunk, CM_HEADS), lambda i: (0, 0))],
        out_specs=[pl.BlockSpec((chunk, CM_WIDTH), lambda i: (i, 0)),
                   pl.BlockSpec((chunk, CM_WIDTH), lambda i: (i, 0))],
        out_shape=[jax.ShapeDtypeStruct((r, CM_WIDTH), F32),
                   jax.ShapeDtypeStruct((r, CM_WIDTH), F32)],
        compiler_params=_cparams(("parallel",)),
    )(u, v, cm_norm.reshape(1, CM_WIDTH), ws, bst)


ROW_W = 2 * KV_COLS
CMP_HALF = CMP_LEN // CMP_STRIDE


def _gather_cv(x, c):
    return jnp.concatenate(
        [x[:, l * ROW_W + c * KV_COLS:l * ROW_W + (c + 1) * KV_COLS] for l in range(CMP_STRIDE)],
        axis=1)


def _compress_kernel(x_ref, xn_ref, w1_ref, pos_ref, pw_ref, w2_ref, o_ref, *, rc):
    x = x_ref[0]
    xn = xn_ref[0]
    last_row = lax.broadcasted_iota(jnp.int32, (rc, KV_COLS), 0) == rc - 1
    for c in range(2):
        w1 = w1_ref[c]
        y = _dot(_gather_cv(x, c).astype(BF16), w1)
        yn = _dot(_gather_cv(xn, c).astype(BF16), w1)
        second = pltpu.roll(y[:, KV_COLS:], rc - 1, 0)
        second = jnp.where(last_row, yn[0:1, KV_COLS:], second)
        posb = _dot_f32(pos_ref[c], pw_ref[c])[0:1]
        posb = jnp.concatenate([posb] * NSA_G, axis=1)
        pre = y[:, :KV_COLS] + second + posb
        o_ref[0, :, c * KV_COLS:(c + 1) * KV_COLS] = _dot(jax.nn.gelu(pre).astype(BF16), w2_ref[c])


def _pick_rows(n, cap):
    best = 8
    for rc in range(8, cap + 1, 8):
        if n % rc == 0:
            best = rc
    return best


def _compress(x, phi_pos, phi_w1, phi_w2):
    bsz, n, _ = x.shape
    rc = _pick_rows(n, 256)
    eye = jnp.eye(NSA_G, dtype=F32)
    w1 = phi_w1.reshape(CMP_HALF, CMP_STRIDE, 2, NSA_DK, NSA_DK)
    w1big = jnp.einsum('mlcde,gh->clgdmhe', w1, eye).reshape(
        2, CMP_STRIDE * KV_COLS, CMP_HALF * KV_COLS).astype(BF16)
    w2big = jnp.einsum('cde,gh->cgdhe', phi_w2, eye).reshape(2, KV_COLS, KV_COLS).astype(BF16)
    pos = jnp.transpose(phi_pos, (1, 0, 2)).reshape(2, 1, CMP_LEN * NSA_DK)
    pos = jnp.broadcast_to(pos, (2, 8, CMP_LEN * NSA_DK))
    pw = jnp.transpose(phi_w1, (1, 0, 2, 3)).reshape(2, CMP_LEN * NSA_DK, NSA_DK)
    nb8 = n // 8
    return pl.pallas_call(
        functools.partial(_compress_kernel, rc=rc),
        grid=(bsz, n // rc),
        in_specs=[pl.BlockSpec((1, rc, x.shape[2]), lambda b, i: (b, i, 0)),
                  pl.BlockSpec((1, 8, x.shape[2]),
                               lambda b, i: (b, jnp.minimum((i + 1) * (rc // 8), nb8 - 1), 0)),
                  pl.BlockSpec(w1big.shape, lambda b, i: (0, 0, 0)),
                  pl.BlockSpec(pos.shape, lambda b, i: (0, 0, 0)),
                  pl.BlockSpec(pw.shape, lambda b, i: (0, 0, 0)),
                  pl.BlockSpec(w2big.shape, lambda b, i: (0, 0, 0))],
        out_specs=pl.BlockSpec((1, rc, ROW_W), lambda b, i: (b, i, 0)),
        out_shape=jax.ShapeDtypeStruct((bsz, n, ROW_W), F32),
        compiler_params=_cparams(("parallel", "arbitrary")),
    )(x, x, w1big, pos, pw, w2big)


def _rel_bias_rows(d, tab_ref):
    n = jnp.maximum(d, 0)
    large = jnp.full(d.shape, REL_BUCKETS // 2, jnp.int32)
    for thr in REL_THRESHOLDS:
        large = large + (n >= thr).astype(jnp.int32)
    bkt = jnp.where(n < REL_BUCKETS // 2, n, large)
    f = jnp.zeros(d.shape, F32)
    for k in range(REL_BUCKETS):
        f = jnp.where(bkt == k, tab_ref[0, k:k + 1, :], f)
    return f


def _near_tile_kernel(tab_ref, o_ref):
    t = pl.program_id(1)
    delta = jnp.where(t < N_NEAR, t * QT, jnp.where(t == TN_WIN, WINDOW, 0))
    dmax = jnp.where(t == TN_WIN, WINDOW, 1 << 30)
    j = lax.broadcasted_iota(jnp.int32, (QT, HQ), 0)
    i = lax.broadcasted_iota(jnp.int32, (QT, HQ), 1) & (QT - 1)
    d = delta + i - j
    far = tab_ref[0, REL_BUCKETS - 1:REL_BUCKETS, :]
    val = jnp.where((d >= 0) & (d <= dmax), _rel_bias_rows(d, tab_ref) - far, NEG)
    val = jnp.where(t == TN_MASKED, NEG, jnp.where(t == TN_ZERO, 0.0, val))
    o_ref[0, 0] = val


def _cmp_window_kernel(tab_ref, o_ref):
    e = lax.broadcasted_iota(jnp.int32, (WT_ROWS, HQ), 0) - WT_LEAD
    i = lax.broadcasted_iota(jnp.int32, (WT_ROWS, HQ), 1) & (QT - 1)
    d = i - (CMP_LEN - 1) - CMP_STRIDE * e
    far = tab_ref[0, REL_BUCKETS - 1:REL_BUCKETS, :]
    o_ref[0] = jnp.where(d >= 0, _rel_bias_rows(d, tab_ref) - far, 0.0)


def _bias_tiles(rel_bias):
    tab = jnp.repeat(rel_bias.reshape(REL_BUCKETS, NSA_G, NSA_HPG), QT, axis=2)
    tab = jnp.transpose(tab, (1, 0, 2))
    tab_spec2 = pl.BlockSpec((1, REL_BUCKETS, HQ), lambda g, t: (g, 0, 0))
    near = pl.pallas_call(
        _near_tile_kernel,
        grid=(NSA_G, TN_COUNT),
        in_specs=[tab_spec2],
        out_specs=pl.BlockSpec((1, 1, QT, HQ), lambda g, t: (g, t, 0, 0)),
        out_shape=jax.ShapeDtypeStruct((NSA_G, TN_COUNT, QT, HQ), F32),
        compiler_params=_cparams(("parallel", "arbitrary")),
    )(tab)
    cwin = pl.pallas_call(
        _cmp_window_kernel,
        grid=(NSA_G,),
        in_specs=[pl.BlockSpec((1, REL_BUCKETS, HQ), lambda g: (g, 0, 0))],
        out_specs=pl.BlockSpec((1, WT_ROWS, HQ), lambda g: (g, 0, 0)),
        out_shape=jax.ShapeDtypeStruct((NSA_G, WT_ROWS, HQ), F32),
        compiler_params=_cparams(("parallel",)),
    )(tab)
    cvec = tab[:, REL_BUCKETS - 1:REL_BUCKETS, :]
    return near, cwin, cvec


def _nsa_kernel(q_ref, kc_ref, vct_ref, ks_ref, vst_ref, kw_ref, vwt_ref, tn_ref, wt_ref,
                cvec_ref, g_ref, o_ref, s_ref, imp_ref, mb_ref, m_ref, l_ref, acc_ref,
                *, b_off, ncp, nblk, n_sel, win_rel):
    b = pl.program_id(1) + b_off
    qs = b * QT
    q = q_ref[0, 0]
    cvec = cvec_ref[0]

    def reset_state():
        m_ref[...] = jnp.full(m_ref.shape, NEG, F32)
        l_ref[...] = jnp.zeros(l_ref.shape, F32)
        acc_ref[...] = jnp.zeros(acc_ref.shape, F32)

    def tile_update(st, vt):
        m_old = m_ref[...]
        m_new = jnp.maximum(m_old, jnp.max(st, axis=0, keepdims=True))
        alpha = jnp.exp(m_old - m_new)
        p = jnp.exp(st - m_new)
        l_ref[...] = alpha * l_ref[...] + jnp.sum(p, axis=0, keepdims=True)
        acc_ref[...] = alpha * acc_ref[...] + _dot(vt, p.astype(BF16))
        m_ref[...] = m_new

    sc = _dot_nt(kc_ref[0], q)
    ci = lax.broadcasted_iota(jnp.int32, (ncp, HQ), 0)
    li = lax.broadcasted_iota(jnp.int32, (ncp, HQ), 1) & (QT - 1)
    vis = (CMP_STRIDE * ci + (CMP_LEN - 1) - li) <= qs
    s_ref[0:WT_LEAD, :] = jnp.full((WT_LEAD, HQ), NEG, F32)
    s_ref[WT_LEAD + ncp:, :] = jnp.full((s_ref.shape[0] - WT_LEAD - ncp, HQ), NEG, F32)
    s_ref[WT_LEAD:WT_LEAD + ncp, :] = jnp.where(vis, sc + cvec, NEG)
    r0 = pl.multiple_of(8 * b, 8)
    s_ref[pl.ds(r0, WT_ROWS), :] += wt_ref[0]
    s = s_ref[WT_LEAD:WT_LEAD + ncp, :]
    m = jnp.max(s, axis=0, keepdims=True)
    m = jnp.where(m <= 0.1 * NEG, 0.0, m)
    p = jnp.exp(s - m)
    rinv = 1.0 / jnp.maximum(jnp.sum(p, axis=0, keepdims=True), 1e-30)
    o_c = _dot(vct_ref[0], p.astype(BF16)) * rinv
    pn = p * rinv
    imp = pn[:, 0:QT]
    for h in range(1, NSA_HPG):
        imp = imp + pn[:, h * QT:(h + 1) * QT]
    imp_ref[0:8, :] = jnp.zeros((8, QT), F32)
    imp_ref[8:8 + ncp, :] = imp
    if imp_ref.shape[0] > 8 + ncp:
        imp_ref[8 + ncp:, :] = jnp.zeros((imp_ref.shape[0] - 8 - ncp, QT), F32)
    ratio = SEL_BLOCK // CMP_STRIDE
    span = ratio + CMP_HALF - 1
    psl = imp_ref[pl.ds(8 - (CMP_HALF - 1), nblk, stride=ratio), :]
    for mm in range(1, span):
        psl = psl + imp_ref[pl.ds(8 - (CMP_HALF - 1) + mm, nblk, stride=ratio), :]

    ji = lax.broadcasted_iota(jnp.int32, (nblk, QT), 0)
    ii = lax.broadcasted_iota(jnp.int32, (nblk, QT), 1)
    cur = (qs + ii) // SEL_BLOCK
    jf = ji.astype(F32)
    forced = (ji == 0) | (ji == cur) | (ji == cur - 1)
    score0 = jnp.where(forced, BIG, jnp.where(ji <= cur, psl, -BIG))

    def pick_one(_, carry):
        score, selb = carry
        mx = jnp.max(score, axis=0, keepdims=True)
        first = jnp.min(jnp.where(score == mx, jf, 1e9), axis=0, keepdims=True)
        pick = jf == first
        return jnp.where(pick, -jnp.inf, score), jnp.where(pick, 0.0, selb)

    _, selb = lax.fori_loop(0, n_sel, pick_one, (score0, jnp.full((nblk, QT), NEG, F32)))
    mb_ref[...] = jnp.concatenate([selb] * NSA_HPG, axis=1) + cvec

    reset_state()

    def sel_logits(kt):
        st = _dot_nt(ks_ref[0, kt], q)
        mb = mb_ref[pl.ds(pl.multiple_of(kt * BLK_PER_KT, BLK_PER_KT), BLK_PER_KT), :]
        return jnp.concatenate(
            [st[k * SEL_BLOCK:(k + 1) * SEL_BLOCK] + mb[k:k + 1] for k in range(BLK_PER_KT)], axis=0)

    def far_body(kt, carry):
        tile_update(sel_logits(kt), vst_ref[0, kt])
        return carry

    def near_body(kt, carry):
        st = sel_logits(kt)
        r = b - (KT_SEL // QT) * kt
        parts = []
        for u in range(KT_SEL // QT):
            ru = r - u
            idx = jnp.where(ru < 0, TN_MASKED, jnp.where(ru >= N_NEAR, TN_ZERO, ru))
            parts.append(tn_ref[0, idx])
        tile_update(st + jnp.concatenate(parts, axis=0), vst_ref[0, kt])
        return carry

    per = KT_SEL // QT
    n_far = jnp.maximum(b - (N_NEAR - 1), 0) // per
    lax.fori_loop(0, n_far, far_body, 0)
    lax.fori_loop(n_far, b // per + 1, near_body, 0)
    o_s = acc_ref[...] * (1.0 / l_ref[...])

    reset_state()
    n_win = WINDOW // QT
    for w in range(n_win + 1):
        kt = b - n_win + w
        tn_idx = TN_WIN if w == 0 else n_win - w

        @pl.when(kt >= 0)
        def _():
            tile = w if win_rel else kt
            st = _dot_nt(kw_ref[0, tile], q) + cvec + tn_ref[0, tn_idx]
            tile_update(st, vwt_ref[0, tile])

    o_w = acc_ref[...] * (1.0 / l_ref[...])

    g = jax.nn.sigmoid(g_ref[0, 0])
    o_ref[0, 0] = g[0:1] * o_c + g[1:2] * o_s + g[2:3] * o_w


def _nsa_attend(qt, kc, vct, ks, vst, kw, vwt, tn, wt, cvec, gt, *, b_off, n_sel_blk, win_rel):
    s_n, nq = qt.shape[:2]
    ncp = kc.shape[1]
    nkt = ks.shape[1]
    nblk = nkt * BLK_PER_KT
    assert ncp >= 8 * (b_off + nq - 1) and ncp % 8 == 0
    assert nkt * (KT_SEL // QT) >= b_off + nq
    n_sel = min(N_SEL, n_sel_blk)
    imp_rows = max(8 + ncp, 8 + (SEL_BLOCK // CMP_STRIDE) * nblk)
    kern = functools.partial(_nsa_kernel, b_off=b_off, ncp=ncp, nblk=nblk, n_sel=n_sel,
                             win_rel=win_rel)
    full = lambda a: pl.BlockSpec((1,) + a.shape[1:], lambda s, i: (s,) + (0,) * (a.ndim - 1))
    grp = lambda a: pl.BlockSpec((1,) + a.shape[1:], lambda s, i: (s % NSA_G,) + (0,) * (a.ndim - 1))
    return pl.pallas_call(
        kern,
        grid=(s_n, nq),
        in_specs=[pl.BlockSpec((1, 1, HQ, LANES), lambda s, i: (s, i, 0, 0)),
                  full(kc), full(vct), full(ks), full(vst), full(kw), full(vwt),
                  grp(tn), grp(wt), grp(cvec),
                  pl.BlockSpec((1, 1, 8, HQ), lambda s, i: (s, i, 0, 0))],
        out_specs=pl.BlockSpec((1, 1, NSA_DK, HQ), lambda s, i: (s, i, 0, 0)),
        out_shape=jax.ShapeDtypeStruct((s_n, nq, NSA_DK, HQ), F32),
        scratch_shapes=[pltpu.VMEM((WT_LEAD + ncp + WT_ROWS - WT_LEAD, HQ), F32),
                        pltpu.VMEM((imp_rows, QT), F32),
                        pltpu.VMEM((nblk, HQ), F32),
                        pltpu.VMEM((1, HQ), F32),
                        pltpu.VMEM((1, HQ), F32),
                        pltpu.VMEM((NSA_DK, HQ), F32)],
        compiler_params=_cparams(("parallel", "arbitrary")),
    )(qt, kc, vct, ks, vst, kw, vwt, tn, wt, cvec, gt)


def _out_proj_kernel(ycm_ref, ynsa_ref, h_ref, gt_ref, gpost_ref, w1_ref, w2_ref, o_ref):
    out = _dot(ycm_ref[...].astype(BF16), w1_ref[...]) + _dot(ynsa_ref[...].astype(BF16), w2_ref[...])
    o_ref[...] = h_ref[...] + gt_ref[...] * _rms(out, gpost_ref[...])


def _out_proj(ycm, ynsa, h, gate, g_post, w1, w2, tm):
    r, d = h.shape
    return pl.pallas_call(
        _out_proj_kernel,
        grid=(r // tm,),
        in_specs=[pl.BlockSpec((tm, CM_WIDTH), lambda i: (i, 0)),
                  pl.BlockSpec((tm, NSA_WIDTH), lambda i: (i, 0)),
                  pl.BlockSpec((tm, d), lambda i: (i, 0)),
                  _row_spec(gate, tm, 1),
                  pl.BlockSpec((1, d), lambda i: (0, 0)),
                  pl.BlockSpec(w1.shape, lambda i: (0, 0)),
                  pl.BlockSpec(w2.shape, lambda i: (0, 0))],
        out_specs=pl.BlockSpec((tm, d), lambda i: (i, 0)),
        out_shape=jax.ShapeDtypeStruct((r, d), F32),
        compiler_params=_cparams(("parallel",)),
    )(ycm, ynsa, h, gate, g_post, w1, w2)


PAGES_PER_STEP = 16


def _gather_kernel(pt_ref, *refs):
    o_ref = refs[-1]
    for k, c_ref in enumerate(refs[:-1]):
        o_ref[0, k * PAGE_SIZE:(k + 1) * PAGE_SIZE, :] = c_ref[0]


def _gather_pages(cache, page_table, rows_out):
    bsz, n_pages = page_table.shape
    w = cache.shape[2]
    pps = math.gcd(PAGES_PER_STEP, n_pages)
    in_specs = [pl.BlockSpec((1, PAGE_SIZE, w),
                             functools.partial(lambda b, j, pt, k: (pt[b, j * pps + k], 0, 0), k=k))
                for k in range(pps)]
    grid_spec = pltpu.PrefetchScalarGridSpec(
        num_scalar_prefetch=1, grid=(bsz, n_pages // pps), in_specs=in_specs,
        out_specs=pl.BlockSpec((1, pps * PAGE_SIZE, w), lambda b, j, pt: (b, j, 0)))
    return pl.pallas_call(
        _gather_kernel,
        grid_spec=grid_spec,
        out_shape=jax.ShapeDtypeStruct((bsz, rows_out, w), F32),
        compiler_params=_cparams(("parallel", "arbitrary")),
    )(page_table, *([cache] * pps))


def _round_up(x, m):
    return -(-x // m) * m


def _q_tiles(q, nq):
    bsz = q.shape[0]
    x = (q * (NSA_DK ** -0.5)).reshape(bsz, nq, QT, NSA_G, NSA_HPG, NSA_DK)
    x = jnp.transpose(x, (0, 3, 1, 4, 2, 5)).reshape(bsz * NSA_G, nq, HQ, NSA_DK)
    return jnp.pad(x, ((0, 0), (0, 0), (0, 0), (0, LANES - NSA_DK))).astype(BF16)


def _gate_tiles(gates, nq):
    bsz = gates.shape[0]
    x = gates[..., :N_BRANCH * NSA_HEADS].reshape(bsz, nq, QT, NSA_G, NSA_HPG, N_BRANCH)
    x = jnp.transpose(x, (0, 3, 1, 5, 4, 2)).reshape(bsz * NSA_G, nq, N_BRANCH, HQ)
    return jnp.pad(x, ((0, 0), (0, 0), (0, 8 - N_BRANCH), (0, 0)))


def _k_tiles(kv, tile):
    bsz, rows = kv.shape[:2]
    nt = rows // tile
    k = kv[..., :KV_COLS].reshape(bsz, nt, tile, NSA_G, NSA_DK)
    k = jnp.transpose(k, (0, 3, 1, 2, 4)).reshape(bsz * NSA_G, nt, tile, NSA_DK)
    k = jnp.pad(k, ((0, 0), (0, 0), (0, 0), (0, LANES - NSA_DK))).astype(BF16)
    v = kv[..., KV_COLS:].reshape(bsz, nt, tile, NSA_G, NSA_DK)
    v = jnp.transpose(v, (0, 3, 1, 4, 2)).reshape(bsz * NSA_G, nt, NSA_DK, tile).astype(BF16)
    return k, v


def _untile_out(ot, bsz, nq):
    x = ot.reshape(bsz, NSA_G, nq, NSA_DK, NSA_HPG, QT)
    return jnp.transpose(x, (0, 2, 5, 1, 4, 3)).reshape(bsz * nq * QT, NSA_WIDTH)


def _attend(q, gates, tok, kv_s, kv_w, bias, *, b_off, n_sel_blk, win_rel):
    bsz = q.shape[0]
    nq = q.shape[1] // QT
    near, cwin, cvec = bias
    ncp = max(_round_up(tok.shape[1] + 1, LANES), _round_up(8 * (b_off + nq), LANES))
    tokp = jnp.pad(tok, ((0, 0), (0, ncp - tok.shape[1]), (0, 0)))
    kc, vct = _k_tiles(tokp, ncp)
    kc, vct = kc[:, 0], vct[:, 0]
    rows_s = _round_up(max(kv_s.shape[1], (b_off + nq) * QT), KT_SEL)
    ks, vst = _k_tiles(jnp.pad(kv_s, ((0, 0), (0, rows_s - kv_s.shape[1]), (0, 0))), KT_SEL)
    kw, vwt = _k_tiles(kv_w, QT)
    ot = _nsa_attend(_q_tiles(q, nq), kc, vct, ks, vst, kw, vwt, near, cwin, cvec,
                     _gate_tiles(gates, nq), b_off=b_off, n_sel_blk=n_sel_blk, win_rel=win_rel)
    return _untile_out(ot, bsz, nq)


def _compress_tokens(kv_c, phi_pos, phi_w1, phi_w2):
    bsz, t = kv_c.shape[:2]
    n = t // CMP_STRIDE
    npad = _round_up(n, LANES if n >= LANES else 8)
    x = kv_c.reshape(bsz, n, CMP_STRIDE * ROW_W)
    x = jnp.pad(x, ((0, 0), (0, npad - n), (0, 0)))
    return _compress(x, phi_pos, phi_w1, phi_w2)[:, :n - (CMP_HALF - 1)]


def _row_tile(r, cap):
    tm = min(r, cap)
    while r % tm:
        tm //= 2
    return tm


def _layer(x, mod, per_row, weights, mixer):
    r, d = x.shape
    (norm_pre, norm_post, ffn, w_in, w_out1, w_out2) = weights
    if per_row is None:
        mrow = lambda i, j: mod[0, i, j].reshape(1, d)
    else:
        mrow = lambda i, j: jnp.repeat(mod[:, i, j], per_row, axis=0)
    tm = _row_tile(r, 512)
    tf = _row_tile(ffn[0][0].shape[1], 512)
    h = _ffn(x, mrow(0, 0), mrow(0, 1), mrow(0, 2), norm_pre[0:1], norm_post[0:1], *ffn[0],
             res_w=0.5, tm=tm, tf=tf)
    segs = _in_proj(h, mrow(1, 0), mrow(1, 1), norm_pre[1:2], w_in, _row_tile(r, 256))
    mixed_cm, mixed_nsa, state = mixer(*segs)
    h = _out_proj(mixed_cm, mixed_nsa, h, mrow(1, 2), norm_post[1:2], w_out1, w_out2, tm)
    h = _ffn(h, mrow(2, 0), mrow(2, 1), mrow(2, 2), norm_pre[2:3], norm_post[2:3], *ffn[1],
             res_w=0.5, tm=tm, tf=tf)
    return h, state


def kernel(x_prompt, x_sample, cache_cmp_kv, cache_slc_kv, state_win_kv, page_table, c_prompt,
           c_sample, w_mod, b_mod, norm_pre, norm_post, ffn_w_gate, ffn_w_up, ffn_w_down, w_in,
           w_out, cm_norm, cm_ws, cm_bs, phi_pos, phi_w1, phi_w2, rel_bias):
    depth = w_mod.shape[0]
    assert depth == 1
    bp, seq, d = x_prompt.shape
    bs, dseq, _ = x_sample.shape
    assert bp == 1 and seq % KT_SEL == 0 and dseq <= CMP_STRIDE
    n_pages = page_table.shape[1]
    past = n_pages * PAGE_SIZE
    l = 0

    c_all = jnp.concatenate([c_prompt, c_sample], axis=0)
    mpad = _round_up(c_all.shape[0], 8)
    mod = _mod_proj(jnp.pad(c_all, ((0, mpad - c_all.shape[0]), (0, 0))), w_mod[l], b_mod[l])
    mod = mod[:bp + bs].reshape(bp + bs, 3, 3, d)

    ffn = [(ffn_w_gate[l, i].astype(BF16), ffn_w_up[l, i].astype(BF16), ffn_w_down[l, i].astype(BF16))
           for i in range(2)]
    w_in_p = jnp.pad(w_in[l], ((0, 0), (0, sum(IN_SEGS) - w_in.shape[2]))).astype(BF16)
    w_o = w_out[l].astype(BF16)
    weights = (norm_pre[l], norm_post[l], ffn, w_in_p, w_o[:CM_WIDTH], w_o[CM_WIDTH:])
    bias = _bias_tiles(rel_bias)
    phi = (phi_pos[l], phi_w1[l], phi_w2[l])

    def prompt_mixer(u, v, q, kv_c, kv_s, kv_w, gates):
        y_cm, _ = _chunk_mix(u, v, cm_norm[l], cm_ws[l], cm_bs[l], CHUNK)
        tok = _compress_tokens(kv_c[None], *phi)
        y_nsa = _attend(q[None], gates[None], tok, kv_s[None], kv_w[None], bias,
                        b_off=0, n_sel_blk=seq // SEL_BLOCK, win_rel=False)
        return y_cm, y_nsa, (kv_c, kv_s, kv_w)

    hp, (pc, ps, pw) = _layer(x_prompt.reshape(seq, d), mod[:bp], None, weights, prompt_mixer)

    rows = _round_up(dseq, 16)

    def sample_mixer(u, v, q, kv_c, kv_s, kv_w, gates):
        pad_rows = lambda a, n: jnp.pad(a.reshape(bs, dseq, -1), ((0, 0), (0, n - dseq), (0, 0)))
        y_cm, vn = _chunk_mix(pad_rows(u, rows).reshape(bs * rows, CM_WIDTH),
                              pad_rows(v, rows).reshape(bs * rows, CM_WIDTH),
                              cm_norm[l], cm_ws[l], cm_bs[l], rows)
        y_cm = y_cm.reshape(bs, rows, CM_WIDTH)[:, :dseq].reshape(bs * dseq, CM_WIDTH)
        vn = vn.reshape(bs, rows, CM_WIDTH)[:, :dseq]
        past_c = _gather_pages(cache_cmp_kv[l].reshape(-1, PAGE_SIZE, ROW_W), page_table, past + QT)
        past_s = _gather_pages(cache_slc_kv[l].reshape(-1, PAGE_SIZE, ROW_W), page_table, past + QT)
        all_c = past_c.at[:, past:].set(pad_rows(kv_c, QT))
        all_s = past_s.at[:, past:].set(pad_rows(kv_s, QT))
        n_new_blk = -(-dseq // SEL_BLOCK)
        tok = _compress_tokens(all_c[:, :past + n_new_blk * SEL_BLOCK], *phi)
        wb = state_win_kv.shape[2]
        win = jnp.concatenate([state_win_kv[l].reshape(bs, wb, ROW_W), pad_rows(kv_w, QT)], axis=1)
        win = jnp.pad(win, ((0, 0), (WINDOW - wb, 0), (0, 0)))
        y_nsa = _attend(pad_rows(q, QT), pad_rows(gates, QT), tok, all_s, win, bias,
                        b_off=past // QT, n_sel_blk=past // SEL_BLOCK + n_new_blk, win_rel=True)
        y_nsa = y_nsa.reshape(bs, QT, NSA_WIDTH)[:, :dseq].reshape(bs * dseq, NSA_WIDTH)
        return y_cm, y_nsa, (kv_c, kv_s, kv_w, vn)

    assert past % QT == 0
    hs, (sc, ss, sw, sv) = _layer(x_sample.reshape(bs * dseq, d), mod[bp:], dseq, weights, sample_mixer)

    kvshape = lambda a, b, t: a.reshape(1, b, t, 2, NSA_G, NSA_DK)
    wp = min(WINDOW, seq)
    win_s = jnp.concatenate([state_win_kv[l].reshape(bs, -1, ROW_W), sw.reshape(bs, dseq, ROW_W)],
                            axis=1)[:, dseq:]
    return (hp.reshape(bp, seq, d), hs.reshape(bs, dseq, d),
            kvshape(pc, bp, seq), kvshape(sc, bs, dseq),
            kvshape(ps, bp, seq), kvshape(ss, bs, dseq),
            kvshape(pw[seq - wp:], bp, wp), kvshape(win_s, bs, win_s.shape[1]),
            sv.reshape(1, bs, dseq, CM_HEADS, CM_HEAD_DIM))
```

```python
import functools
import math

import jax
import jax.numpy as jnp
from jax import lax
from jax.experimental import pallas as pl
from jax.experimental.pallas import tpu as pltpu

F32 = jnp.float32
BF16 = jnp.bfloat16

CM_HEADS = 8
CM_HEAD_DIM = 128
CM_WIDTH = CM_HEADS * CM_HEAD_DIM
CHUNK = 128
NSA_HEADS = 16
NSA_G = 4
NSA_HPG = 4
NSA_DK = 64
NSA_WIDTH = NSA_HEADS * NSA_DK
KV_COLS = NSA_G * NSA_DK
ROW_W = 2 * KV_COLS
CMP_LEN = 32
CMP_STRIDE = 16
CMP_HALF = CMP_LEN // CMP_STRIDE
SEL_BLOCK = 64
N_SEL = 16
WINDOW = 512
N_BRANCH = 3
BIG = 1e4
REL_BUCKETS = 32
EPS = 1e-6
PAGE_SIZE = 128

LOG2E = 1.4426950408889634
Q_SCALE = NSA_DK ** -0.5 * LOG2E
NEG = -1e30
LANES = 128
QT = 128
HQ = NSA_HPG * QT
KT_SEL = 512
BLK_PER_KT = KT_SEL // SEL_BLOCK
SUB_PER_KT = KT_SEL // QT
N_NEAR = 8
NEAR_TILES = 3
TN_WIN, TN_MASKED, TN_ZERO, TN_COUNT = 8, 9, 10, 11
WT_LEAD = 64
WT_ROWS = 200
REL_THRESHOLDS = (21, 27, 35, 46, 59, 77, 99, 128, 166, 216, 280, 363, 470, 609, 790)
VMEM_LIMIT = 56 * 1024 * 1024
SP_PP = CMP_STRIDE
SP_KEYS = SP_PP * PAGE_SIZE
SP_BLKS = SP_KEYS // SEL_BLOCK


def _cparams(sem):
    return pltpu.CompilerParams(dimension_semantics=sem, vmem_limit_bytes=VMEM_LIMIT)


def _dot(a, b):
    return jnp.dot(a, b, preferred_element_type=F32)


def _dot_nt(a, b):
    return lax.dot_general(a, b, (((1,), (1,)), ((), ())), preferred_element_type=F32)


def _split_bf16(x):
    hi = x.astype(BF16)
    lo = (x - hi.astype(F32)).astype(BF16)
    return hi, lo


def _dot_f32(a, b):
    a_hi, a_lo = _split_bf16(a)
    b_hi, b_lo = _split_bf16(b)
    return _dot(a_hi, b_hi) + _dot(a_lo, b_hi) + _dot(a_hi, b_lo)


def _rms(x, g):
    return x * lax.rsqrt(jnp.mean(x * x, axis=-1, keepdims=True) + EPS) * g


def _row_spec(arr, tm):
    d = arr.shape[-1]
    if arr.shape[0] == 1:
        return pl.BlockSpec((1, d), lambda *idx: (0, 0))
    return pl.BlockSpec((tm, d), lambda *idx: (idx[0], 0))


def _round_up(x, m):
    return -(-x // m) * m


def _mod_kernel(c_ref, w_ref, b_ref, o_ref):
    c = c_ref[...]
    s = c * jax.nn.sigmoid(c)
    o_ref[...] = _dot_f32(s, w_ref[...]) + b_ref[...]


def _mod_proj(c, w_mod, b_mod):
    m, d = c.shape
    n = w_mod.shape[1]
    tn = 512
    return pl.pallas_call(
        _mod_kernel,
        grid=(n // tn,),
        in_specs=[pl.BlockSpec((m, d), lambda j: (0, 0)),
                  pl.BlockSpec((d, tn), lambda j: (0, j)),
                  pl.BlockSpec((1, tn), lambda j: (0, j))],
        out_specs=pl.BlockSpec((m, tn), lambda j: (0, j)),
        out_shape=jax.ShapeDtypeStruct((m, n), F32),
        compiler_params=_cparams(("arbitrary",)),
        name="mod_proj",
    )(c, w_mod, b_mod.reshape(1, n))


def _ffn_kernel(x_ref, sh_ref, sc_ref, gt_ref, gpre_ref, gpost_ref, wg_ref, wu_ref, wd_ref,
                o_ref, a_ref, acc_ref, *, res_w, nf):
    f = pl.program_id(1)

    @pl.when(f == 0)
    def _():
        y = _rms(x_ref[...], gpre_ref[...])
        a_ref[...] = (y * (1.0 + sc_ref[...]) + sh_ref[...]).astype(BF16)
        acc_ref[...] = jnp.zeros_like(acc_ref)

    a = a_ref[...]
    h = _dot(a, wg_ref[...])
    u = _dot(a, wu_ref[...])
    act = (h * jax.nn.sigmoid(h) * u).astype(BF16)
    acc_ref[...] += _dot(act, wd_ref[...])

    @pl.when(f == nf - 1)
    def _():
        o_ref[...] = x_ref[...] + res_w * gt_ref[...] * _rms(acc_ref[...], gpost_ref[...])


def _ffn(x, shift, scale, gate, g_pre, g_post, wg, wu, wd, res_w, tm, tf):
    r, d = x.shape
    fdim = wg.shape[1]
    nf = fdim // tf
    return pl.pallas_call(
        functools.partial(_ffn_kernel, res_w=res_w, nf=nf),
        grid=(r // tm, nf),
        in_specs=[pl.BlockSpec((tm, d), lambda i, f: (i, 0)),
                  _row_spec(shift, tm), _row_spec(scale, tm), _row_spec(gate, tm),
                  pl.BlockSpec((1, d), lambda i, f: (0, 0)),
                  pl.BlockSpec((1, d), lambda i, f: (0, 0)),
                  pl.BlockSpec((d, tf), lambda i, f: (0, f)),
                  pl.BlockSpec((d, tf), lambda i, f: (0, f)),
                  pl.BlockSpec((tf, d), lambda i, f: (f, 0))],
        out_specs=pl.BlockSpec((tm, d), lambda i, f: (i, 0)),
        out_shape=jax.ShapeDtypeStruct((r, d), F32),
        scratch_shapes=[pltpu.VMEM((tm, d), BF16), pltpu.VMEM((tm, d), F32)],
        compiler_params=_cparams(("parallel", "arbitrary")),
        name="ffn",
    )(x, shift, scale, gate, g_pre, g_post, wg, wu, wd)


IN_SEGS = (CM_WIDTH, CM_WIDTH, NSA_WIDTH, 2 * KV_COLS, 2 * KV_COLS, 2 * KV_COLS, LANES)


def _in_proj_kernel(x_ref, sh_ref, sc_ref, gpre_ref, w_ref, *o_refs):
    y = _rms(x_ref[...], gpre_ref[...])
    a = (y * (1.0 + sc_ref[...]) + sh_ref[...]).astype(BF16)
    z = _dot(a, w_ref[...])
    off = 0
    for o_ref, width in zip(o_refs, IN_SEGS):
        o_ref[...] = z[:, off:off + width]
        off += width


def _in_proj(x, shift, scale, g_pre, w, tm):
    r, d = x.shape
    n = w.shape[1]
    return pl.pallas_call(
        _in_proj_kernel,
        grid=(r // tm,),
        in_specs=[pl.BlockSpec((tm, d), lambda i: (i, 0)),
                  _row_spec(shift, tm), _row_spec(scale, tm),
                  pl.BlockSpec((1, d), lambda i: (0, 0)),
                  pl.BlockSpec((d, n), lambda i: (0, 0), pipeline_mode=pl.Buffered(1))],
        out_specs=[pl.BlockSpec((tm, s), lambda i: (i, 0)) for s in IN_SEGS],
        out_shape=[jax.ShapeDtypeStruct((r, s), F32) for s in IN_SEGS],
        compiler_params=_cparams(("parallel",)),
        name="in_proj",
    )(x, shift, scale, g_pre, w)


def _chunk_mix_kernel(u_ref, v_ref, nrm_ref, ws_ref, bst_ref, y_ref, vn_ref, *, chunk):
    ri = lax.broadcasted_iota(jnp.int32, (chunk, chunk), 0)
    ci = lax.broadcasted_iota(jnp.int32, (chunk, chunk), 1)
    lower = ci <= ri
    for h in range(CM_HEADS):
        sl = slice(h * CM_HEAD_DIM, (h + 1) * CM_HEAD_DIM)
        vf = jax.nn.gelu(v_ref[:, sl])
        mu = jnp.mean(vf, axis=-1, keepdims=True)
        var = jnp.mean(jnp.square(vf - mu), axis=-1, keepdims=True)
        vn = (vf - mu) * lax.rsqrt(var + EPS) * nrm_ref[:, sl]
        vn_ref[:, sl] = vn
        w = jnp.where(lower, ws_ref[h], 0.0).astype(BF16)
        s = _dot(w, vn.astype(BF16)) + bst_ref[:, h:h + 1]
        y_ref[:, sl] = jax.nn.gelu(u_ref[:, sl]) * s


def _chunk_mix(u, v, cm_norm, cm_ws, cm_bs, chunk):
    r = u.shape[0]
    ws = cm_ws[:, :chunk, :chunk]
    bst = cm_bs[:, :chunk].T
    return pl.pallas_call(
        functools.partial(_chunk_mix_kernel, chunk=chunk),
        grid=(r // chunk,),
        in_specs=[pl.BlockSpec((chunk, CM_WIDTH), lambda i: (i, 0)),
                  pl.BlockSpec((chunk, CM_WIDTH), lambda i: (i, 0)),
                  pl.BlockSpec((1, CM_WIDTH), lambda i: (0, 0)),
                  pl.BlockSpec((CM_HEADS, chunk, chunk), lambda i: (0, 0, 0)),
                  pl.BlockSpec((chunk, CM_HEADS), lambda i: (0, 0))],
        out_specs=[pl.BlockSpec((chunk, CM_WIDTH), lambda i: (i, 0)),
                   pl.BlockSpec((chunk, CM_WIDTH), lambda i: (i, 0))],
        out_shape=[jax.ShapeDtypeStruct((r, CM_WIDTH), F32),
                   jax.ShapeDtypeStruct((r, CM_WIDTH), F32)],
        compiler_params=_cparams(("parallel",)),
        name="chunk_mix",
    )(u, v, cm_norm.reshape(1, CM_WIDTH), ws, bst)


def _gather_cv(x, c):
    return jnp.concatenate(
        [x[:, l * ROW_W + c * KV_COLS:l * ROW_W + (c + 1) * KV_COLS] for l in range(CMP_STRIDE)],
        axis=1)


def _pos_bias(pos_ref, pw_ref, c):
    posb = _dot_f32(pos_ref[c], pw_ref[c])[0:1]
    return jnp.concatenate([posb] * NSA_G, axis=1)


def _compress_weights(phi_pos, phi_w1, phi_w2):
    eye = jnp.eye(NSA_G, dtype=F32)
    w1 = phi_w1.reshape(CMP_HALF, CMP_STRIDE, 2, NSA_DK, NSA_DK)
    w1big = jnp.einsum('mlcde,gh->clgdmhe', w1, eye).reshape(
        2, CMP_STRIDE * KV_COLS, CMP_HALF * KV_COLS).astype(BF16)
    w2big = jnp.einsum('cde,gh->cgdhe', phi_w2, eye).reshape(2, KV_COLS, KV_COLS).astype(BF16)
    pos = jnp.transpose(phi_pos, (1, 0, 2)).reshape(2, 1, CMP_LEN * NSA_DK)
    pos = jnp.broadcast_to(pos, (2, 8, CMP_LEN * NSA_DK))
    pw = jnp.transpose(phi_w1, (1, 0, 2, 3)).reshape(2, CMP_LEN * NSA_DK, NSA_DK)
    return w1big, pos, pw, w2big


def _compress_kernel(x_ref, xn_ref, w1_ref, pos_ref, pw_ref, w2_ref, o_ref, *, rc):
    x = x_ref[0]
    xn = xn_ref[0]
    last_row = lax.broadcasted_iota(jnp.int32, (rc, KV_COLS), 0) == rc - 1
    for c in range(2):
        w1 = w1_ref[c]
        y = _dot(_gather_cv(x, c).astype(BF16), w1)
        yn = _dot(_gather_cv(xn, c).astype(BF16), w1)
        second = pltpu.roll(y[:, KV_COLS:], rc - 1, 0)
        second = jnp.where(last_row, yn[0:1, KV_COLS:], second)
        pre = y[:, :KV_COLS] + second + _pos_bias(pos_ref, pw_ref, c)
        o_ref[0, :, c * KV_COLS:(c + 1) * KV_COLS] = _dot(jax.nn.gelu(pre).astype(BF16), w2_ref[c])


def _pick_rows(n, cap):
    best = 8
    for rc in range(8, cap + 1, 8):
        if n % rc == 0:
            best = rc
    return best


def _compress(x, cw):
    bsz, n, _ = x.shape
    rc = _pick_rows(n, 256)
    w1big, pos, pw, w2big = cw
    nb8 = n // 8
    return pl.pallas_call(
        functools.partial(_compress_kernel, rc=rc),
        grid=(bsz, n // rc),
        in_specs=[pl.BlockSpec((1, rc, x.shape[2]), lambda b, i: (b, i, 0)),
                  pl.BlockSpec((1, 8, x.shape[2]),
                               lambda b, i: (b, jnp.minimum((i + 1) * (rc // 8), nb8 - 1), 0)),
                  pl.BlockSpec(w1big.shape, lambda b, i: (0, 0, 0)),
                  pl.BlockSpec(pos.shape, lambda b, i: (0, 0, 0)),
                  pl.BlockSpec(pw.shape, lambda b, i: (0, 0, 0)),
                  pl.BlockSpec(w2big.shape, lambda b, i: (0, 0, 0))],
        out_specs=pl.BlockSpec((1, rc, ROW_W), lambda b, i: (b, i, 0)),
        out_shape=jax.ShapeDtypeStruct((bsz, n, ROW_W), F32),
        compiler_params=_cparams(("parallel", "arbitrary")),
        name="compress",
    )(x, x, w1big, pos, pw, w2big)


def _compress_tokens(kv_c, cw):
    bsz, t = kv_c.shape[:2]
    n = t // CMP_STRIDE
    npad = _round_up(n, LANES if n >= LANES else 8)
    x = kv_c.reshape(bsz, n, CMP_STRIDE * ROW_W)
    x = jnp.pad(x, ((0, 0), (0, npad - n), (0, 0)))
    return _compress(x, cw)[:, :n - (CMP_HALF - 1)]


def _rel_bucket(d):
    n = jnp.maximum(d, 0)
    large = jnp.full(d.shape, REL_BUCKETS // 2, jnp.int32)
    for thr in REL_THRESHOLDS:
        large = large + (n >= thr).astype(jnp.int32)
    return jnp.where(n < REL_BUCKETS // 2, n, large)


def _rel_bias_lanes(d, tab_ref):
    bkt = _rel_bucket(d)
    f = jnp.zeros(d.shape, F32)
    for k in range(REL_BUCKETS):
        f = jnp.where(bkt == k, tab_ref[0, k:k + 1, :], f)
    return f


def _rel_bias_rows(d, tab_ref):
    bkt = _rel_bucket(d)
    f = jnp.zeros(d.shape, F32)
    for k in range(REL_BUCKETS):
        f = jnp.where(bkt == k, tab_ref[:, k:k + 1], f)
    return f


def _near_tile_kernel(tab_ref, o_ref):
    t = pl.program_id(1)
    delta = jnp.where(t < N_NEAR, t * QT, jnp.where(t == TN_WIN, WINDOW, 0))
    dmax = jnp.where(t == TN_WIN, WINDOW, 1 << 30)
    j = lax.broadcasted_iota(jnp.int32, (QT, HQ), 0)
    i = lax.broadcasted_iota(jnp.int32, (QT, HQ), 1) & (QT - 1)
    d = delta + i - j
    far = tab_ref[0, REL_BUCKETS - 1:REL_BUCKETS, :]
    val = jnp.where((d >= 0) & (d <= dmax), _rel_bias_lanes(d, tab_ref) - far, NEG)
    val = jnp.where(t == TN_MASKED, NEG, jnp.where(t == TN_ZERO, 0.0, val))
    o_ref[0, 0] = val


def _cmp_window_kernel(tab_ref, o_ref):
    e = lax.broadcasted_iota(jnp.int32, (WT_ROWS, HQ), 0) - WT_LEAD
    i = lax.broadcasted_iota(jnp.int32, (WT_ROWS, HQ), 1) & (QT - 1)
    d = i - (CMP_LEN - 1) - CMP_STRIDE * e
    far = tab_ref[0, REL_BUCKETS - 1:REL_BUCKETS, :]
    o_ref[0] = jnp.where(d >= 0, _rel_bias_lanes(d, tab_ref) - far, 0.0)


def _prompt_bias_tiles(rel_bias):
    tab = jnp.repeat((rel_bias * LOG2E).reshape(REL_BUCKETS, NSA_G, NSA_HPG), QT, axis=2)
    tab = jnp.transpose(tab, (1, 0, 2))
    near = pl.pallas_call(
        _near_tile_kernel,
        grid=(NSA_G, TN_COUNT),
        in_specs=[pl.BlockSpec((1, REL_BUCKETS, HQ), lambda g, t: (g, 0, 0))],
        out_specs=pl.BlockSpec((1, 1, QT, HQ), lambda g, t: (g, t, 0, 0)),
        out_shape=jax.ShapeDtypeStruct((NSA_G, TN_COUNT, QT, HQ), F32),
        compiler_params=_cparams(("parallel", "arbitrary")),
        name="bias_near_tiles",
    )(tab)
    cwin = pl.pallas_call(
        _cmp_window_kernel,
        grid=(NSA_G,),
        in_specs=[pl.BlockSpec((1, REL_BUCKETS, HQ), lambda g: (g, 0, 0))],
        out_specs=pl.BlockSpec((1, WT_ROWS, HQ), lambda g: (g, 0, 0)),
        out_shape=jax.ShapeDtypeStruct((NSA_G, WT_ROWS, HQ), F32),
        compiler_params=_cparams(("parallel",)),
        name="bias_cmp_window",
    )(tab)
    cvec = tab[:, REL_BUCKETS - 1:REL_BUCKETS, :]
    return near, cwin, cvec


def _nsa_kernel(q_ref, kc_ref, vct_ref, ks_ref, vst_ref, kw_ref, vwt_ref, tn_ref, wt_ref,
                cvec_ref, g_ref, o_ref, s_ref, imp_ref, mb_ref, m_ref, l_ref, acc_ref,
                buf0_ref, buf1_ref, tmax_ref, *, ncp, nblk, n_sel):
    b = pl.program_id(1)
    qs = b * QT
    q = q_ref[0, 0]
    cvec = cvec_ref[0]

    sc = _dot_nt(kc_ref[0], q)
    ci = lax.broadcasted_iota(jnp.int32, (ncp, HQ), 0)
    li = lax.broadcasted_iota(jnp.int32, (ncp, HQ), 1) & (QT - 1)
    vis = (CMP_STRIDE * ci + (CMP_LEN - 1) - li) <= qs
    s_ref[0:WT_LEAD, :] = jnp.full((WT_LEAD, HQ), NEG, F32)
    s_ref[WT_LEAD + ncp:, :] = jnp.full((s_ref.shape[0] - WT_LEAD - ncp, HQ), NEG, F32)
    s_ref[WT_LEAD:WT_LEAD + ncp, :] = jnp.where(vis, sc + cvec, NEG)
    r0 = pl.multiple_of(8 * b, 8)
    s_ref[pl.ds(r0, WT_ROWS), :] += wt_ref[0]
    s = s_ref[WT_LEAD:WT_LEAD + ncp, :]
    m = jnp.max(s, axis=0, keepdims=True)
    m = jnp.where(m <= 0.1 * NEG, 0.0, m)
    p = jnp.exp2(s - m)
    rinv = 1.0 / jnp.maximum(jnp.sum(p, axis=0, keepdims=True), 1e-30)
    o_c = _dot(vct_ref[0], p.astype(BF16)) * rinv
    pn = p * rinv
    imp = pn[:, 0:QT]
    for h in range(1, NSA_HPG):
        imp = imp + pn[:, h * QT:(h + 1) * QT]
    imp_ref[0:8, :] = jnp.zeros((8, QT), F32)
    imp_ref[8:8 + ncp, :] = imp
    if imp_ref.shape[0] > 8 + ncp:
        imp_ref[8 + ncp:, :] = jnp.zeros((imp_ref.shape[0] - 8 - ncp, QT), F32)
    ratio = SEL_BLOCK // CMP_STRIDE
    span = ratio + CMP_HALF - 1
    psl = imp_ref[pl.ds(8 - (CMP_HALF - 1), nblk, stride=ratio), :]
    for mm in range(1, span):
        psl = psl + imp_ref[pl.ds(8 - (CMP_HALF - 1) + mm, nblk, stride=ratio), :]

    ji = lax.broadcasted_iota(jnp.int32, (nblk, QT), 0)
    ii = lax.broadcasted_iota(jnp.int32, (nblk, QT), 1)
    cur = (qs + ii) // SEL_BLOCK
    jf = ji.astype(F32)
    forced = (ji == 0) | (ji == cur) | (ji == cur - 1)
    score0 = jnp.where(forced, BIG, jnp.where(ji <= cur, psl, -BIG))

    def pick_one(_, carry):
        score, selb = carry
        mx = jnp.max(score, axis=0, keepdims=True)
        first = jnp.min(jnp.where(score == mx, jf, 1e9), axis=0, keepdims=True)
        pick = jf == first
        return jnp.where(pick, -jnp.inf, score), jnp.where(pick, 0.0, selb)

    _, selb = lax.fori_loop(0, n_sel, pick_one, (score0, jnp.full((nblk, QT), NEG, F32)))
    mb_ref[0:nblk, :] = jnp.concatenate([selb] * NSA_HPG, axis=1) + cvec
    mb_ref[nblk:, :] = jnp.full((BLK_PER_KT, HQ), NEG, F32)

    m_ref[...] = jnp.full(m_ref.shape, NEG, F32)
    l_ref[...] = jnp.zeros(l_ref.shape, F32)
    acc_ref[...] = jnp.zeros(acc_ref.shape, F32)

    def sel_logits(kt, valid):
        ktc = jnp.where(valid, kt, 0)
        st = _dot_nt(ks_ref[0, ktc], q)
        row0 = jnp.where(valid, ktc * BLK_PER_KT, nblk)
        mb = mb_ref[pl.ds(pl.multiple_of(row0, BLK_PER_KT), BLK_PER_KT), :]
        st = jnp.concatenate(
            [st[k * SEL_BLOCK:(k + 1) * SEL_BLOCK] + mb[k:k + 1] for k in range(BLK_PER_KT)], axis=0)
        return st, ktc

    def softmax_step(st, tmax, vt):
        m_old = m_ref[...]
        m_new = jnp.maximum(m_old, tmax)
        alpha = jnp.exp2(m_old - m_new)
        p = jnp.exp2(st - m_new)
        l_ref[...] = alpha * l_ref[...] + jnp.sum(p, axis=0, keepdims=True)
        acc_ref[...] = alpha * acc_ref[...] + _dot(vt, p.astype(BF16))
        m_ref[...] = m_new

    nt = b // SUB_PER_KT + 1
    n_far = jnp.maximum(nt - NEAR_TILES, 0)

    def far_logits(kt, buf_ref, slot):
        st, _ = sel_logits(kt, kt < n_far)
        buf_ref[...] = st
        tmax_ref[slot:slot + 1, :] = jnp.max(st, axis=0, keepdims=True)

    @pl.when(n_far > 0)
    def _():
        far_logits(0, buf0_ref, 0)

    def far_pair(pi, carry):
        k0 = 2 * pi
        far_logits(k0 + 1, buf1_ref, 1)
        softmax_step(buf0_ref[...], tmax_ref[0:1, :], vst_ref[0, k0])
        far_logits(k0 + 2, buf0_ref, 0)
        k1 = jnp.minimum(k0 + 1, jnp.maximum(n_far - 1, 0))
        softmax_step(buf1_ref[...], tmax_ref[1:2, :], vst_ref[0, k1])
        return carry

    lax.fori_loop(0, (n_far + 1) // 2, far_pair, 0)

    for w in range(NEAR_TILES):
        kt = nt - NEAR_TILES + w
        st, ktc = sel_logits(kt, kt >= 0)
        r = b - SUB_PER_KT * ktc
        parts = []
        for u in range(SUB_PER_KT):
            ru = r - u
            idx = jnp.where(ru < 0, TN_MASKED, jnp.where(ru >= N_NEAR, TN_ZERO, ru))
            parts.append(tn_ref[0, idx])
        st = st + jnp.concatenate(parts, axis=0)
        softmax_step(st, jnp.max(st, axis=0, keepdims=True), vst_ref[0, ktc])
    o_s = acc_ref[...] * (1.0 / l_ref[...])

    n_win = WINDOW // QT
    st_parts, vt_parts = [], []
    for w in range(n_win + 1):
        kt = b - n_win + w
        ktc = jnp.maximum(kt, 0)
        tn_idx = jnp.where(kt >= 0, TN_WIN if w == 0 else n_win - w, TN_MASKED)
        st_parts.append(_dot_nt(kw_ref[0, ktc], q) + tn_ref[0, tn_idx])
        vt_parts.append(vwt_ref[0, ktc])
    st = jnp.concatenate(st_parts, axis=0) + cvec
    p = jnp.exp2(st - jnp.max(st, axis=0, keepdims=True))
    o_w = _dot(jnp.concatenate(vt_parts, axis=1), p.astype(BF16)) * (
        1.0 / jnp.sum(p, axis=0, keepdims=True))

    g = jax.nn.sigmoid(g_ref[0, 0])
    o_ref[0, 0] = g[0:1] * o_c + g[1:2] * o_s + g[2:3] * o_w


def _nsa_attend(qt, kc, vct, ks, vst, kw, vwt, tn, wt, cvec, gt, *, n_sel_blk):
    s_n, nq = qt.shape[:2]
    ncp = kc.shape[1]
    nkt = ks.shape[1]
    nblk = nkt * BLK_PER_KT
    assert ncp >= 8 * (nq - 1) and ncp % 8 == 0 and nkt * SUB_PER_KT >= nq
    n_sel = min(N_SEL, n_sel_blk)
    imp_rows = max(8 + ncp, 8 + (SEL_BLOCK // CMP_STRIDE) * nblk)
    kern = functools.partial(_nsa_kernel, ncp=ncp, nblk=nblk, n_sel=n_sel)
    full = lambda a: pl.BlockSpec((1,) + a.shape[1:], lambda s, i: (s,) + (0,) * (a.ndim - 1))
    return pl.pallas_call(
        kern,
        grid=(s_n, nq),
        in_specs=[pl.BlockSpec((1, 1, HQ, LANES), lambda s, i: (s, i, 0, 0)),
                  full(kc), full(vct), full(ks), full(vst), full(kw), full(vwt),
                  full(tn), full(wt), full(cvec),
                  pl.BlockSpec((1, 1, 8, HQ), lambda s, i: (s, i, 0, 0))],
        out_specs=pl.BlockSpec((1, 1, NSA_DK, HQ), lambda s, i: (s, i, 0, 0)),
        out_shape=jax.ShapeDtypeStruct((s_n, nq, NSA_DK, HQ), F32),
        scratch_shapes=[pltpu.VMEM((ncp + WT_ROWS, HQ), F32),
                        pltpu.VMEM((imp_rows, QT), F32),
                        pltpu.VMEM((nblk + BLK_PER_KT, HQ), F32),
                        pltpu.VMEM((1, HQ), F32),
                        pltpu.VMEM((1, HQ), F32),
                        pltpu.VMEM((NSA_DK, HQ), F32),
                        pltpu.VMEM((KT_SEL, HQ), F32),
                        pltpu.VMEM((KT_SEL, HQ), F32),
                        pltpu.VMEM((8, HQ), F32)],
        compiler_params=_cparams(("parallel", "arbitrary")),
        name="nsa_prompt",
    )(qt, kc, vct, ks, vst, kw, vwt, tn, wt, cvec, gt)


def _q_tiles(q, nq):
    x = (q * Q_SCALE).reshape(nq, QT, NSA_G, NSA_HPG, NSA_DK)
    x = jnp.transpose(x, (2, 0, 3, 1, 4)).reshape(NSA_G, nq, HQ, NSA_DK)
    return jnp.pad(x, ((0, 0), (0, 0), (0, 0), (0, LANES - NSA_DK))).astype(BF16)


def _gate_tiles(gates, nq):
    x = gates[..., :N_BRANCH * NSA_HEADS].reshape(nq, QT, NSA_G, NSA_HPG, N_BRANCH)
    x = jnp.transpose(x, (2, 0, 4, 3, 1)).reshape(NSA_G, nq, N_BRANCH, HQ)
    return jnp.pad(x, ((0, 0), (0, 0), (0, 8 - N_BRANCH), (0, 0)))


def _k_tiles(kv, tile):
    nt = kv.shape[0] // tile
    k = kv[..., :KV_COLS].reshape(nt, tile, NSA_G, NSA_DK)
    k = jnp.transpose(k, (2, 0, 1, 3))
    k = jnp.pad(k, ((0, 0), (0, 0), (0, 0), (0, LANES - NSA_DK))).astype(BF16)
    v = kv[..., KV_COLS:].reshape(nt, tile, NSA_G, NSA_DK)
    v = jnp.transpose(v, (2, 0, 3, 1)).astype(BF16)
    return k, v


def _prompt_attend(q, gates, tok, kv_s, kv_w, bias):
    t = q.shape[0]
    nq = t // QT
    near, cwin, cvec = bias
    ncp = max(_round_up(tok.shape[0] + 1, LANES), _round_up(8 * nq, LANES))
    kc, vct = _k_tiles(jnp.pad(tok, ((0, ncp - tok.shape[0]), (0, 0))), ncp)
    ks, vst = _k_tiles(kv_s, KT_SEL)
    kw, vwt = _k_tiles(kv_w, QT)
    ot = _nsa_attend(_q_tiles(q, nq), kc[:, 0], vct[:, 0], ks, vst, kw, vwt, near, cwin, cvec,
                     _gate_tiles(gates, nq), n_sel_blk=t // SEL_BLOCK)
    x = ot.reshape(NSA_G, nq, NSA_DK, NSA_HPG, QT)
    return jnp.transpose(x, (1, 4, 0, 3, 2)).reshape(t, NSA_WIDTH)


def _sample_bias_kernel(tab_ref, cb_ref, sb_ref, snb_ref, wb_ref, *, past, dseq, ns):
    rows = tab_ref.shape[0]
    far = tab_ref[:, REL_BUCKETS - 1:REL_BUCKETS]

    def tok_of(n):
        return lax.broadcasted_iota(jnp.int32, (rows, n), 0) % dseq

    def lane(n):
        return lax.broadcasted_iota(jnp.int32, (rows, n), 1)

    for idx in range(2):
        rho = LANES * (ns - 1 + idx) + lane(LANES)
        d = past + tok_of(LANES) - (CMP_STRIDE * (rho - 1) + CMP_LEN - 1)
        cb_ref[idx] = jnp.where(d >= 0, _rel_bias_rows(d, tab_ref) - far, 0.0)
    d = SP_KEYS + tok_of(SP_KEYS) - lane(SP_KEYS)
    sb_ref[...] = _rel_bias_rows(d, tab_ref) - far
    d = tok_of(LANES) - lane(LANES)
    snb_ref[...] = jnp.where(d >= 0, _rel_bias_rows(d, tab_ref) - far, NEG)
    d = WINDOW + tok_of(WINDOW + LANES) - lane(WINDOW + LANES)
    wb_ref[...] = jnp.where((d >= 0) & (d <= WINDOW), _rel_bias_rows(d, tab_ref), NEG)


def _sample_bias(rel_bias, past, dseq, ns):
    rows = NSA_HEADS * dseq
    tab = (rel_bias * LOG2E).reshape(REL_BUCKETS, NSA_G, NSA_HPG)
    tab = jnp.transpose(tab, (2, 1, 0))
    tab = jnp.repeat(tab.reshape(NSA_HEADS, 1, REL_BUCKETS), dseq, axis=1).reshape(rows, REL_BUCKETS)
    tab = jnp.pad(tab, ((0, 0), (0, LANES - REL_BUCKETS)))
    shapes = [(2, rows, LANES), (rows, SP_KEYS), (rows, LANES), (rows, WINDOW + LANES)]
    cb, sb, snb, wb = pl.pallas_call(
        functools.partial(_sample_bias_kernel, past=past, dseq=dseq, ns=ns),
        out_shape=[jax.ShapeDtypeStruct(s, F32) for s in shapes],
        compiler_params=pltpu.CompilerParams(vmem_limit_bytes=VMEM_LIMIT),
        name="bias_sample",
    )(tab)
    cfar = jnp.broadcast_to(tab[:, REL_BUCKETS - 1:REL_BUCKETS], (rows, LANES))
    return cfar, cb, sb, snb, wb


def _sample_nsa_kernel(pt_ref, *refs, ns, past, dseq, n_sel, n_sel_blk):
    cmp_pages = refs[:SP_PP]
    slc_pages = refs[SP_PP:2 * SP_PP]
    (xn_ref, ksn_ref, win_ref, kwn_ref, q_ref, g_ref, w1_ref, pos_ref, pw_ref, w2_ref, e_ref,
     cfar_ref, cb_ref, sb_ref, snb_ref, wb_ref, o_ref,
     s_ref, vc_ref, carry_ref, sel_ref, m_ref, l_ref, acc_ref, oc_ref) = refs[2 * SP_PP:]
    j = pl.program_id(1)
    qbd = q_ref[0]
    rows = qbd.shape[0]
    cfar = cfar_ref[:, 0:1]
    ntile = ns + 1
    nblk_l = sel_ref.shape[0] * SP_BLKS

    def tokens_of(x):
        n = x.shape[0]
        row0 = lax.broadcasted_iota(jnp.int32, (n, KV_COLS), 0) == 0
        out = []
        for c in range(2):
            y = _dot(_gather_cv(x, c).astype(BF16), w1_ref[c])
            first = jnp.where(row0, carry_ref[c:c + 1, :], pltpu.roll(y[:, :KV_COLS], 1, 0))
            carry_ref[c:c + 1, :] = y[n - 1:n, :KV_COLS]
            pre = first + y[:, KV_COLS:] + _pos_bias(pos_ref, pw_ref, c)
            out.append(_dot(jax.nn.gelu(pre).astype(BF16), w2_ref[c]))
        return out

    def online_update(s, v):
        m_old = m_ref[...]
        m_new = jnp.maximum(m_old, jnp.max(s, axis=1, keepdims=True))
        alpha = jnp.exp2(m_old - m_new)
        p = jnp.exp2(s - m_new)
        l_ref[...] = alpha * l_ref[...] + jnp.sum(p, axis=1, keepdims=True)
        acc_ref[...] = alpha * acc_ref[...] + _dot(p.astype(BF16), v)
        m_ref[...] = m_new

    @pl.when(j == 0)
    def _():
        carry_ref[...] = jnp.zeros(carry_ref.shape, F32)

    @pl.when(j < ns)
    def _():
        x = jnp.concatenate([r[0] for r in cmp_pages], axis=0)
        tok_k, tok_v = tokens_of(x)
        s_ref[j] = _dot_nt(qbd, tok_k.astype(BF16))
        vc_ref[j] = tok_v.astype(BF16)

    @pl.when(j == ns - 1)
    def _():
        tok_k, tok_v = tokens_of(xn_ref[0])
        zpad = jnp.zeros((LANES - tok_k.shape[0], KV_COLS), F32)
        s_ref[ns] = _dot_nt(qbd, jnp.concatenate([tok_k, zpad], axis=0).astype(BF16))
        vc_ref[ns] = jnp.concatenate([tok_v, zpad], axis=0).astype(BF16)

        tiles = []
        for t in range(ntile):
            st = s_ref[t] + cfar
            if t >= ns - 1:
                st = st + cb_ref[t - (ns - 1)]
            tiles.append(st)
        s = jnp.concatenate(tiles, axis=1)
        width = ntile * LANES
        rho = lax.broadcasted_iota(jnp.int32, (rows, width), 1)
        tq = lax.broadcasted_iota(jnp.int32, (rows, width), 0) % dseq
        n_cmp = (past + _round_up(dseq, SEL_BLOCK)) // CMP_STRIDE - (CMP_HALF - 1)
        vis = (rho >= 1) & (rho <= n_cmp) & (CMP_STRIDE * (rho - 1) + CMP_LEN - 1 <= past + tq)
        s = jnp.where(vis, s, NEG)
        m = jnp.max(s, axis=1, keepdims=True)
        m = jnp.where(m <= 0.1 * NEG, 0.0, m)
        p = jnp.exp2(s - m)
        rinv = 1.0 / jnp.maximum(jnp.sum(p, axis=1, keepdims=True), 1e-30)
        vc = jnp.concatenate([vc_ref[t] for t in range(ntile)], axis=0)
        oc_ref[...] = _dot(p.astype(BF16), vc) * rinv
        pn = p * rinv
        gt = NSA_G * dseq
        imp = pn[0:gt]
        for h in range(1, NSA_HPG):
            imp = imp + pn[h * gt:(h + 1) * gt]
        ratio = SEL_BLOCK // CMP_STRIDE
        span = ratio + CMP_HALF - 1
        ar = lax.broadcasted_iota(jnp.int32, (width, nblk_l), 0)
        ac = lax.broadcasted_iota(jnp.int32, (width, nblk_l), 1)
        band = jnp.where((ar >= ratio * ac) & (ar < ratio * ac + span), 1.0, 0.0).astype(BF16)
        i_hi = imp.astype(BF16)
        r1 = imp - i_hi.astype(F32)
        i_mid = r1.astype(BF16)
        i_lo = (r1 - i_mid.astype(F32)).astype(BF16)
        psl = _dot(i_hi, band) + _dot(i_mid, band) + _dot(i_lo, band)
        ji = lax.broadcasted_iota(jnp.int32, (gt, nblk_l), 1)
        tq2 = lax.broadcasted_iota(jnp.int32, (gt, nblk_l), 0) % dseq
        cur = (past + tq2) // SEL_BLOCK
        jf = ji.astype(F32)
        forced = (ji == 0) | (ji == cur) | (ji == cur - 1)
        score0 = jnp.where(forced, BIG, jnp.where(ji <= cur, psl, -BIG))
        score0 = jnp.where(ji < n_sel_blk, score0, -jnp.inf)

        def pick_one(_, carry):
            score, sel = carry
            mx = jnp.max(score, axis=1, keepdims=True)
            first = jnp.min(jnp.where(score == mx, jf, 1e9), axis=1, keepdims=True)
            pick = jf == first
            return jnp.where(pick, -jnp.inf, score), jnp.where(pick, 1.0, sel)

        _, sel = lax.fori_loop(0, n_sel, pick_one, (score0, jnp.zeros((gt, nblk_l), F32)))
        for t in range(sel_ref.shape[0]):
            piece = sel[:, t * SP_BLKS:(t + 1) * SP_BLKS]
            piece = jnp.concatenate([piece, jnp.zeros((gt, LANES - SP_BLKS), F32)], axis=1)
            sel_ref[t] = jnp.concatenate([piece] * NSA_HPG, axis=0).astype(BF16)
        m_ref[...] = jnp.full(m_ref.shape, NEG, F32)
        l_ref[...] = jnp.zeros(l_ref.shape, F32)
        acc_ref[...] = jnp.zeros(acc_ref.shape, F32)

    @pl.when(j >= ns)
    def _():
        j2 = j - ns
        kv = jnp.concatenate([r[0] for r in slc_pages], axis=0)
        s = _dot_nt(qbd, kv[:, :KV_COLS].astype(BF16))
        chosen = _dot(sel_ref[j2], e_ref[...])
        s = s + cfar + (chosen - 1.0) * (-NEG)
        s = s + jnp.where(j2 == ns - 1, sb_ref[...], 0.0)
        online_update(s, kv[:, KV_COLS:].astype(BF16))

    @pl.when(j == 2 * ns - 1)
    def _():
        kn = ksn_ref[0]
        online_update(_dot_nt(qbd, kn[:, :KV_COLS].astype(BF16)) + cfar + snb_ref[...],
                      kn[:, KV_COLS:].astype(BF16))
        o_s = acc_ref[...] * (1.0 / l_ref[...])
        kw = jnp.concatenate([win_ref[0], kwn_ref[0]], axis=0)
        s = _dot_nt(qbd, kw[:, :KV_COLS].astype(BF16)) + wb_ref[...]
        p = jnp.exp2(s - jnp.max(s, axis=1, keepdims=True))
        o_w = _dot(p.astype(BF16), kw[:, KV_COLS:].astype(BF16)) * (
            1.0 / jnp.sum(p, axis=1, keepdims=True))
        g = jax.nn.sigmoid(g_ref[0])
        o_ref[0] = g[:, 0:1] * oc_ref[...] + g[:, 1:2] * o_s + g[:, 2:3] * o_w


def _sample_attend(q, gates, kv_c, kv_s, kv_w, cache_cmp, cache_slc, win_buf, page_table, cw,
                   rel_bias):
    bsz, dseq = q.shape[:2]
    n_pages = page_table.shape[1]
    past = n_pages * PAGE_SIZE
    assert n_pages % SP_PP == 0 and dseq <= SEL_BLOCK and win_buf.shape[1] == WINDOW
    ns = n_pages // SP_PP
    rows = NSA_HEADS * dseq
    n_sel_blk = past // SEL_BLOCK + 1
    n_sel = min(N_SEL, n_sel_blk)
    nsel_tiles = _round_up(-(-n_sel_blk // SP_BLKS), LANES // SP_BLKS)
    w1big, pos, pw, w2big = cw
    cfar, cb, sb, snb, wb = _sample_bias(rel_bias, past, dseq, ns)

    qr = jnp.transpose((q * Q_SCALE).reshape(bsz, dseq, NSA_G, NSA_HPG, NSA_DK), (0, 3, 2, 1, 4))
    qbd = jnp.einsum('bhgtd,gk->bhgtkd', qr, jnp.eye(NSA_G, dtype=F32))
    qbd = qbd.reshape(bsz, rows, KV_COLS).astype(BF16)
    gr = gates[..., :N_BRANCH * NSA_HEADS].reshape(bsz, dseq, NSA_G, NSA_HPG, N_BRANCH)
    gr = jnp.transpose(gr, (0, 3, 2, 1, 4)).reshape(bsz, rows, N_BRANCH)
    gr = jnp.pad(gr, ((0, 0), (0, 0), (0, LANES - N_BRANCH)))
    pad_rows = lambda a, n: jnp.pad(a, ((0, 0), (0, n - a.shape[1]), (0, 0)))
    xn = pad_rows(kv_c, SEL_BLOCK).reshape(bsz, SEL_BLOCK // CMP_STRIDE, CMP_STRIDE * ROW_W)
    xn = pad_rows(xn, 8)
    ksn = pad_rows(kv_s, LANES)
    kwn = pad_rows(kv_w, LANES)
    expand = (jnp.arange(LANES)[:, None] == (jnp.arange(SP_KEYS) // SEL_BLOCK)[None, :]).astype(BF16)
    cmp_view = cache_cmp.reshape(-1, PAGE_SIZE // CMP_STRIDE, CMP_STRIDE * ROW_W)

    def page_map(k, phase):
        if phase == 0:
            return lambda b, j, pt: (pt[b, jnp.minimum(j, ns - 1) * SP_PP + k], 0, 0)
        return lambda b, j, pt: (pt[b, jnp.maximum(j - ns, 0) * SP_PP + k], 0, 0)

    per_b = lambda a: pl.BlockSpec((1,) + a.shape[1:], lambda b, j, pt: (b,) + (0,) * (a.ndim - 1))
    const = lambda a: pl.BlockSpec(a.shape, lambda b, j, pt: (0,) * a.ndim)
    in_specs = ([pl.BlockSpec((1,) + cmp_view.shape[1:], page_map(k, 0)) for k in range(SP_PP)]
                + [pl.BlockSpec((1, PAGE_SIZE, ROW_W), page_map(k, 1)) for k in range(SP_PP)]
                + [per_b(xn), per_b(ksn), per_b(win_buf), per_b(kwn), per_b(qbd), per_b(gr),
                   pl.BlockSpec(w1big.shape, lambda b, j, pt: (0, 0, 0), pipeline_mode=pl.Buffered(1)),
                   const(pos), const(pw), const(w2big), const(expand),
                   const(cfar), const(cb), const(sb), const(snb), const(wb)])
    grid_spec = pltpu.PrefetchScalarGridSpec(
        num_scalar_prefetch=1, grid=(bsz, 2 * ns), in_specs=in_specs,
        out_specs=pl.BlockSpec((1, rows, KV_COLS), lambda b, j, pt: (b, 0, 0)),
        scratch_shapes=[pltpu.VMEM((ns + 1, rows, LANES), F32),
                        pltpu.VMEM((ns + 1, LANES, KV_COLS), BF16),
                        pltpu.VMEM((8, KV_COLS), F32),
                        pltpu.VMEM((nsel_tiles, rows, LANES), BF16),
                        pltpu.VMEM((rows, 1), F32),
                        pltpu.VMEM((rows, 1), F32),
                        pltpu.VMEM((rows, KV_COLS), F32),
                        pltpu.VMEM((rows, KV_COLS), F32)])
    out = pl.pallas_call(
        functools.partial(_sample_nsa_kernel, ns=ns, past=past, dseq=dseq, n_sel=n_sel,
                          n_sel_blk=n_sel_blk),
        grid_spec=grid_spec,
        out_shape=jax.ShapeDtypeStruct((bsz, rows, KV_COLS), F32),
        compiler_params=_cparams(("parallel", "arbitrary")),
        name="nsa_sample",
    )(page_table, *([cmp_view] * SP_PP), *([cache_slc] * SP_PP), xn, ksn, win_buf, kwn, qbd, gr,
      w1big, pos, pw, w2big, expand, cfar, cb, sb, snb, wb)
    o = out.reshape(bsz, NSA_HPG, NSA_G, dseq, NSA_G, NSA_DK)
    o = jnp.einsum('bhgtge->btghe', o)
    return o.reshape(bsz * dseq, NSA_WIDTH)


def _out_proj_kernel(ycm_ref, ynsa_ref, h_ref, gt_ref, gpost_ref, w1_ref, w2_ref, o_ref):
    out = _dot(ycm_ref[...].astype(BF16), w1_ref[...]) + _dot(ynsa_ref[...].astype(BF16), w2_ref[...])
    o_ref[...] = h_ref[...] + gt_ref[...] * _rms(out, gpost_ref[...])


def _out_proj(ycm, ynsa, h, gate, g_post, w1, w2, tm):
    r, d = h.shape
    return pl.pallas_call(
        _out_proj_kernel,
        grid=(r // tm,),
        in_specs=[pl.BlockSpec((tm, CM_WIDTH), lambda i: (i, 0)),
                  pl.BlockSpec((tm, NSA_WIDTH), lambda i: (i, 0)),
                  pl.BlockSpec((tm, d), lambda i: (i, 0)),
                  _row_spec(gate, tm),
                  pl.BlockSpec((1, d), lambda i: (0, 0)),
                  pl.BlockSpec(w1.shape, lambda i: (0, 0)),
                  pl.BlockSpec(w2.shape, lambda i: (0, 0))],
        out_specs=pl.BlockSpec((tm, d), lambda i: (i, 0)),
        out_shape=jax.ShapeDtypeStruct((r, d), F32),
        compiler_params=_cparams(("parallel",)),
        name="out_proj",
    )(ycm, ynsa, h, gate, g_post, w1, w2)


def _row_tile(r, cap):
    tm = min(r, cap)
    while r % tm:
        tm //= 2
    return tm


def _layer(x, mod, per_row, weights, mixer):
    r, d = x.shape
    (norm_pre, norm_post, ffn, w_in, w_out1, w_out2) = weights
    if per_row is None:
        mrow = lambda i, j: mod[0, i, j].reshape(1, d)
    else:
        mrow = lambda i, j: jnp.repeat(mod[:, i, j], per_row, axis=0)
    tm = _row_tile(r, 512)
    tf = _row_tile(ffn[0][0].shape[1], 512)
    h = _ffn(x, mrow(0, 0), mrow(0, 1), mrow(0, 2), norm_pre[0:1], norm_post[0:1], *ffn[0],
             res_w=0.5, tm=tm, tf=tf)
    segs = _in_proj(h, mrow(1, 0), mrow(1, 1), norm_pre[1:2], w_in, _row_tile(r, 256))
    mixed_cm, mixed_nsa, state = mixer(*segs)
    h = _out_proj(mixed_cm, mixed_nsa, h, mrow(1, 2), norm_post[1:2], w_out1, w_out2, tm)
    h = _ffn(h, mrow(2, 0), mrow(2, 1), mrow(2, 2), norm_pre[2:3], norm_post[2:3], *ffn[1],
             res_w=0.5, tm=tm, tf=tf)
    return h, state


def kernel(x_prompt, x_sample, cache_cmp_kv, cache_slc_kv, state_win_kv, page_table, c_prompt,
           c_sample, w_mod, b_mod, norm_pre, norm_post, ffn_w_gate, ffn_w_up, ffn_w_down, w_in,
           w_out, cm_norm, cm_ws, cm_bs, phi_pos, phi_w1, phi_w2, rel_bias):
    depth = w_mod.shape[0]
    assert depth == 1
    bp, seq, d = x_prompt.shape
    bs, dseq, _ = x_sample.shape
    assert bp == 1 and seq % KT_SEL == 0 and seq >= WINDOW
    l = 0

    c_all = jnp.concatenate([c_prompt, c_sample], axis=0)
    mpad = _round_up(c_all.shape[0], 8)
    mod = _mod_proj(jnp.pad(c_all, ((0, mpad - c_all.shape[0]), (0, 0))), w_mod[l], b_mod[l])
    mod = mod[:bp + bs].reshape(bp + bs, 3, 3, d)

    ffn = [(ffn_w_gate[l, i].astype(BF16), ffn_w_up[l, i].astype(BF16), ffn_w_down[l, i].astype(BF16))
           for i in range(2)]
    w_in_p = jnp.pad(w_in[l], ((0, 0), (0, sum(IN_SEGS) - w_in.shape[2]))).astype(BF16)
    w_o = w_out[l].astype(BF16)
    weights = (norm_pre[l], norm_post[l], ffn, w_in_p, w_o[:CM_WIDTH], w_o[CM_WIDTH:])
    cw = _compress_weights(phi_pos[l], phi_w1[l], phi_w2[l])
    prompt_bias = _prompt_bias_tiles(rel_bias)

    def prompt_mixer(u, v, q, kv_c, kv_s, kv_w, gates):
        y_cm, _ = _chunk_mix(u, v, cm_norm[l], cm_ws[l], cm_bs[l], CHUNK)
        tok = _compress_tokens(kv_c[None], cw)[0]
        y_nsa = _prompt_attend(q, gates, tok, kv_s, kv_w, prompt_bias)
        return y_cm, y_nsa, (kv_c, kv_s, kv_w)

    hp, (pc, ps, pw) = _layer(x_prompt.reshape(seq, d), mod[:bp], None, weights, prompt_mixer)

    rows = _round_up(dseq, 16)

    def sample_mixer(u, v, q, kv_c, kv_s, kv_w, gates):
        per_seq = lambda a: a.reshape(bs, dseq, -1)
        pad_rows = lambda a, n: jnp.pad(per_seq(a), ((0, 0), (0, n - dseq), (0, 0)))
        y_cm, vn = _chunk_mix(pad_rows(u, rows).reshape(bs * rows, CM_WIDTH),
                              pad_rows(v, rows).reshape(bs * rows, CM_WIDTH),
                              cm_norm[l], cm_ws[l], cm_bs[l], rows)
        y_cm = y_cm.reshape(bs, rows, CM_WIDTH)[:, :dseq].reshape(bs * dseq, CM_WIDTH)
        vn = vn.reshape(bs, rows, CM_WIDTH)[:, :dseq]
        y_nsa = _sample_attend(per_seq(q), per_seq(gates), per_seq(kv_c), per_seq(kv_s), per_seq(kv_w),
                               cache_cmp_kv[l].reshape(-1, PAGE_SIZE, ROW_W),
                               cache_slc_kv[l].reshape(-1, PAGE_SIZE, ROW_W),
                               state_win_kv[l].reshape(bs, -1, ROW_W), page_table, cw, rel_bias)
        return y_cm, y_nsa, (kv_c, kv_s, kv_w, vn)

    hs, (sc, ss, sw, sv) = _layer(x_sample.reshape(bs * dseq, d), mod[bp:], dseq, weights, sample_mixer)

    kvshape = lambda a, b, t: a.reshape(1, b, t, 2, NSA_G, NSA_DK)
    wp = min(WINDOW, seq)
    win_s = jnp.concatenate([state_win_kv[l].reshape(bs, -1, ROW_W), sw.reshape(bs, dseq, ROW_W)],
                            axis=1)[:, dseq:]
    return (hp.reshape(bp, seq, d), hs.reshape(bs, dseq, d),
            kvshape(pc, bp, seq), kvshape(sc, bs, dseq),
            kvshape(ps, bp, seq), kvshape(ss, bs, dseq),
            kvshape(pw[seq - wp:], bp, wp), kvshape(win_s, bs, win_s.shape[1]),
            sv.reshape(1, bs, dseq, CM_HEADS, CM_HEAD_DIM))
```

```python
import functools
import math

import jax
import jax.numpy as jnp
from jax import lax
from jax.experimental import pallas as pl
from jax.experimental.pallas import tpu as pltpu

F32 = jnp.float32
BF16 = jnp.bfloat16

CM_HEADS = 8
CM_HEAD_DIM = 128
CM_WIDTH = CM_HEADS * CM_HEAD_DIM
CHUNK = 128
NSA_HEADS = 16
NSA_G = 4
NSA_HPG = 4
NSA_DK = 64
NSA_WIDTH = NSA_HEADS * NSA_DK
KV_COLS = NSA_G * NSA_DK
ROW_W = 2 * KV_COLS
CMP_LEN = 32
CMP_STRIDE = 16
CMP_HALF = CMP_LEN // CMP_STRIDE
SEL_BLOCK = 64
N_SEL = 16
WINDOW = 512
N_BRANCH = 3
BIG = 1e4
REL_BUCKETS = 32
EPS = 1e-6
PAGE_SIZE = 128

LOG2E = 1.4426950408889634
Q_SCALE = NSA_DK ** -0.5 * LOG2E
NEG = -1e30
LANES = 128
QT = 128
HQ = NSA_HPG * QT
KT_SEL = 512
BLK_PER_KT = KT_SEL // SEL_BLOCK
SUB_PER_KT = KT_SEL // QT
N_NEAR = 8
NEAR_TILES = 3
V_ROWS = NSA_DK + 16
TN_WIN, TN_MASKED, TN_ZERO, TN_COUNT = 8, 9, 10, 11
WT_LEAD = 64
WT_ROWS = 200
REL_THRESHOLDS = (21, 27, 35, 46, 59, 77, 99, 128, 166, 216, 280, 363, 470, 609, 790)
VMEM_LIMIT = 56 * 1024 * 1024
SP_PP = CMP_STRIDE
SP_KEYS = SP_PP * PAGE_SIZE
SP_BLKS = SP_KEYS // SEL_BLOCK


def _cparams(sem):
    return pltpu.CompilerParams(dimension_semantics=sem, vmem_limit_bytes=VMEM_LIMIT)


def _dot(a, b):
    return jnp.dot(a, b, preferred_element_type=F32)


def _dot_nt(a, b):
    return lax.dot_general(a, b, (((1,), (1,)), ((), ())), preferred_element_type=F32)


def _split_bf16(x):
    hi = x.astype(BF16)
    lo = (x - hi.astype(F32)).astype(BF16)
    return hi, lo


def _dot_f32(a, b):
    a_hi, a_lo = _split_bf16(a)
    b_hi, b_lo = _split_bf16(b)
    return _dot(a_hi, b_hi) + _dot(a_lo, b_hi) + _dot(a_hi, b_lo)


def _rms(x, g):
    return x * lax.rsqrt(jnp.mean(x * x, axis=-1, keepdims=True) + EPS) * g


def _row_spec(arr, tm):
    d = arr.shape[-1]
    if arr.shape[0] == 1:
        return pl.BlockSpec((1, d), lambda *idx: (0, 0))
    return pl.BlockSpec((tm, d), lambda *idx: (idx[0], 0))


def _round_up(x, m):
    return -(-x // m) * m


def _mod_kernel(c_ref, w_ref, b_ref, o_ref):
    c = c_ref[...]
    s = c * jax.nn.sigmoid(c)
    o_ref[...] = _dot_f32(s, w_ref[...]) + b_ref[...]


def _mod_proj(c, w_mod, b_mod):
    m, d = c.shape
    n = w_mod.shape[1]
    tn = 512
    return pl.pallas_call(
        _mod_kernel,
        grid=(n // tn,),
        in_specs=[pl.BlockSpec((m, d), lambda j: (0, 0)),
                  pl.BlockSpec((d, tn), lambda j: (0, j)),
                  pl.BlockSpec((1, tn), lambda j: (0, j))],
        out_specs=pl.BlockSpec((m, tn), lambda j: (0, j)),
        out_shape=jax.ShapeDtypeStruct((m, n), F32),
        compiler_params=_cparams(("arbitrary",)),
        name="mod_proj",
    )(c, w_mod, b_mod.reshape(1, n))


def _ffn_kernel(x_ref, sh_ref, sc_ref, gt_ref, gpre_ref, gpost_ref, wg_ref, wu_ref, wd_ref,
                o_ref, a_ref, acc_ref, *, res_w, nf):
    f = pl.program_id(1)

    @pl.when(f == 0)
    def _():
        y = _rms(x_ref[...], gpre_ref[...])
        a_ref[...] = (y * (1.0 + sc_ref[...]) + sh_ref[...]).astype(BF16)
        acc_ref[...] = jnp.zeros_like(acc_ref)

    a = a_ref[...]
    h = _dot(a, wg_ref[...])
    u = _dot(a, wu_ref[...])
    act = (h * jax.nn.sigmoid(h) * u).astype(BF16)
    acc_ref[...] += _dot(act, wd_ref[...])

    @pl.when(f == nf - 1)
    def _():
        o_ref[...] = x_ref[...] + res_w * gt_ref[...] * _rms(acc_ref[...], gpost_ref[...])


def _ffn(x, shift, scale, gate, g_pre, g_post, wg, wu, wd, res_w, tm, tf):
    r, d = x.shape
    fdim = wg.shape[1]
    nf = fdim // tf
    return pl.pallas_call(
        functools.partial(_ffn_kernel, res_w=res_w, nf=nf),
        grid=(r // tm, nf),
        in_specs=[pl.BlockSpec((tm, d), lambda i, f: (i, 0)),
                  _row_spec(shift, tm), _row_spec(scale, tm), _row_spec(gate, tm),
                  pl.BlockSpec((1, d), lambda i, f: (0, 0)),
                  pl.BlockSpec((1, d), lambda i, f: (0, 0)),
                  pl.BlockSpec((d, tf), lambda i, f: (0, f)),
                  pl.BlockSpec((d, tf), lambda i, f: (0, f)),
                  pl.BlockSpec((tf, d), lambda i, f: (f, 0))],
        out_specs=pl.BlockSpec((tm, d), lambda i, f: (i, 0)),
        out_shape=jax.ShapeDtypeStruct((r, d), F32),
        scratch_shapes=[pltpu.VMEM((tm, d), BF16), pltpu.VMEM((tm, d), F32)],
        compiler_params=_cparams(("parallel", "arbitrary")),
        name="ffn",
    )(x, shift, scale, gate, g_pre, g_post, wg, wu, wd)


IN_SEGS = (CM_WIDTH, CM_WIDTH, NSA_WIDTH, 2 * KV_COLS, 2 * KV_COLS, 2 * KV_COLS, LANES)


def _in_proj_kernel(x_ref, sh_ref, sc_ref, gpre_ref, w_ref, *o_refs):
    y = _rms(x_ref[...], gpre_ref[...])
    a = (y * (1.0 + sc_ref[...]) + sh_ref[...]).astype(BF16)
    z = _dot(a, w_ref[...])
    off = 0
    for o_ref, width in zip(o_refs, IN_SEGS):
        o_ref[...] = z[:, off:off + width]
        off += width


def _in_proj(x, shift, scale, g_pre, w, tm):
    r, d = x.shape
    n = w.shape[1]
    return pl.pallas_call(
        _in_proj_kernel,
        grid=(r // tm,),
        in_specs=[pl.BlockSpec((tm, d), lambda i: (i, 0)),
                  _row_spec(shift, tm), _row_spec(scale, tm),
                  pl.BlockSpec((1, d), lambda i: (0, 0)),
                  pl.BlockSpec((d, n), lambda i: (0, 0), pipeline_mode=pl.Buffered(1))],
        out_specs=[pl.BlockSpec((tm, s), lambda i: (i, 0)) for s in IN_SEGS],
        out_shape=[jax.ShapeDtypeStruct((r, s), F32) for s in IN_SEGS],
        compiler_params=_cparams(("parallel",)),
        name="in_proj",
    )(x, shift, scale, g_pre, w)


def _chunk_mix_kernel(u_ref, v_ref, nrm_ref, ws_ref, bst_ref, y_ref, vn_ref, *, chunk):
    ri = lax.broadcasted_iota(jnp.int32, (chunk, chunk), 0)
    ci = lax.broadcasted_iota(jnp.int32, (chunk, chunk), 1)
    lower = ci <= ri
    for h in range(CM_HEADS):
        sl = slice(h * CM_HEAD_DIM, (h + 1) * CM_HEAD_DIM)
        vf = jax.nn.gelu(v_ref[:, sl])
        mu = jnp.mean(vf, axis=-1, keepdims=True)
        var = jnp.mean(jnp.square(vf - mu), axis=-1, keepdims=True)
        vn = (vf - mu) * lax.rsqrt(var + EPS) * nrm_ref[:, sl]
        vn_ref[:, sl] = vn
        w = jnp.where(lower, ws_ref[h], 0.0).astype(BF16)
        s = _dot(w, vn.astype(BF16)) + bst_ref[:, h:h + 1]
        y_ref[:, sl] = jax.nn.gelu(u_ref[:, sl]) * s


def _chunk_mix(u, v, cm_norm, cm_ws, cm_bs, chunk):
    r = u.shape[0]
    ws = cm_ws[:, :chunk, :chunk]
    bst = cm_bs[:, :chunk].T
    return pl.pallas_call(
        functools.partial(_chunk_mix_kernel, chunk=chunk),
        grid=(r // chunk,),
        in_specs=[pl.BlockSpec((chunk, CM_WIDTH), lambda i: (i, 0)),
                  pl.BlockSpec((chunk, CM_WIDTH), lambda i: (i, 0)),
                  pl.BlockSpec((1, CM_WIDTH), lambda i: (0, 0)),
                  pl.BlockSpec((CM_HEADS, chunk, chunk), lambda i: (0, 0, 0)),
                  pl.BlockSpec((chunk, CM_HEADS), lambda i: (0, 0))],
        out_specs=[pl.BlockSpec((chunk, CM_WIDTH), lambda i: (i, 0)),
                   pl.BlockSpec((chunk, CM_WIDTH), lambda i: (i, 0))],
        out_shape=[jax.ShapeDtypeStruct((r, CM_WIDTH), F32),
                   jax.ShapeDtypeStruct((r, CM_WIDTH), F32)],
        compiler_params=_cparams(("parallel",)),
        name="chunk_mix",
    )(u, v, cm_norm.reshape(1, CM_WIDTH), ws, bst)


def _gather_cv(x, c):
    return jnp.concatenate(
        [x[:, l * ROW_W + c * KV_COLS:l * ROW_W + (c + 1) * KV_COLS] for l in range(CMP_STRIDE)],
        axis=1)


def _pos_bias(pos_ref, pw_ref, c):
    posb = _dot_f32(pos_ref[c], pw_ref[c])[0:1]
    return jnp.concatenate([posb] * NSA_G, axis=1)


def _compress_weights(phi_pos, phi_w1, phi_w2):
    eye = jnp.eye(NSA_G, dtype=F32)
    w1 = phi_w1.reshape(CMP_HALF, CMP_STRIDE, 2, NSA_DK, NSA_DK)
    w1big = jnp.einsum('mlcde,gh->clgdmhe', w1, eye).reshape(
        2, CMP_STRIDE * KV_COLS, CMP_HALF * KV_COLS).astype(BF16)
    w2big = jnp.einsum('cde,gh->cgdhe', phi_w2, eye).reshape(2, KV_COLS, KV_COLS).astype(BF16)
    pos = jnp.transpose(phi_pos, (1, 0, 2)).reshape(2, 1, CMP_LEN * NSA_DK)
    pos = jnp.broadcast_to(pos, (2, 8, CMP_LEN * NSA_DK))
    pw = jnp.transpose(phi_w1, (1, 0, 2, 3)).reshape(2, CMP_LEN * NSA_DK, NSA_DK)
    return w1big, pos, pw, w2big


def _compress_kernel(x_ref, xn_ref, w1_ref, pos_ref, pw_ref, w2_ref, o_ref, *, rc):
    x = x_ref[0]
    xn = xn_ref[0]
    last_row = lax.broadcasted_iota(jnp.int32, (rc, KV_COLS), 0) == rc - 1
    for c in range(2):
        w1 = w1_ref[c]
        y = _dot(_gather_cv(x, c).astype(BF16), w1)
        yn = _dot(_gather_cv(xn, c).astype(BF16), w1)
        second = pltpu.roll(y[:, KV_COLS:], rc - 1, 0)
        second = jnp.where(last_row, yn[0:1, KV_COLS:], second)
        pre = y[:, :KV_COLS] + second + _pos_bias(pos_ref, pw_ref, c)
        o_ref[0, :, c * KV_COLS:(c + 1) * KV_COLS] = _dot(jax.nn.gelu(pre).astype(BF16), w2_ref[c])


def _pick_rows(n, cap):
    best = 8
    for rc in range(8, cap + 1, 8):
        if n % rc == 0:
            best = rc
    return best


def _compress(x, cw):
    bsz, n, _ = x.shape
    rc = _pick_rows(n, 256)
    w1big, pos, pw, w2big = cw
    nb8 = n // 8
    return pl.pallas_call(
        functools.partial(_compress_kernel, rc=rc),
        grid=(bsz, n // rc),
        in_specs=[pl.BlockSpec((1, rc, x.shape[2]), lambda b, i: (b, i, 0)),
                  pl.BlockSpec((1, 8, x.shape[2]),
                               lambda b, i: (b, jnp.minimum((i + 1) * (rc // 8), nb8 - 1), 0)),
                  pl.BlockSpec(w1big.shape, lambda b, i: (0, 0, 0)),
                  pl.BlockSpec(pos.shape, lambda b, i: (0, 0, 0)),
                  pl.BlockSpec(pw.shape, lambda b, i: (0, 0, 0)),
                  pl.BlockSpec(w2big.shape, lambda b, i: (0, 0, 0))],
        out_specs=pl.BlockSpec((1, rc, ROW_W), lambda b, i: (b, i, 0)),
        out_shape=jax.ShapeDtypeStruct((bsz, n, ROW_W), F32),
        compiler_params=_cparams(("parallel", "arbitrary")),
        name="compress",
    )(x, x, w1big, pos, pw, w2big)


def _compress_tokens(kv_c, cw):
    bsz, t = kv_c.shape[:2]
    n = t // CMP_STRIDE
    npad = _round_up(n, LANES if n >= LANES else 8)
    x = kv_c.reshape(bsz, n, CMP_STRIDE * ROW_W)
    x = jnp.pad(x, ((0, 0), (0, npad - n), (0, 0)))
    return _compress(x, cw)[:, :n - (CMP_HALF - 1)]


def _rel_bucket(d):
    n = jnp.maximum(d, 0)
    large = jnp.full(d.shape, REL_BUCKETS // 2, jnp.int32)
    for thr in REL_THRESHOLDS:
        large = large + (n >= thr).astype(jnp.int32)
    return jnp.where(n < REL_BUCKETS // 2, n, large)


def _rel_bias_lanes(d, tab_ref):
    bkt = _rel_bucket(d)
    f = jnp.zeros(d.shape, F32)
    for k in range(REL_BUCKETS):
        f = jnp.where(bkt == k, tab_ref[0, k:k + 1, :], f)
    return f


def _rel_bias_rows(d, tab_ref):
    bkt = _rel_bucket(d)
    f = jnp.zeros(d.shape, F32)
    for k in range(REL_BUCKETS):
        f = jnp.where(bkt == k, tab_ref[:, k:k + 1], f)
    return f


def _near_tile_kernel(tab_ref, o_ref):
    t = pl.program_id(1)
    delta = jnp.where(t < N_NEAR, t * QT, jnp.where(t == TN_WIN, WINDOW, 0))
    dmax = jnp.where(t == TN_WIN, WINDOW, 1 << 30)
    j = lax.broadcasted_iota(jnp.int32, (QT, HQ), 0)
    i = lax.broadcasted_iota(jnp.int32, (QT, HQ), 1) & (QT - 1)
    d = delta + i - j
    far = tab_ref[0, REL_BUCKETS - 1:REL_BUCKETS, :]
    val = jnp.where((d >= 0) & (d <= dmax), _rel_bias_lanes(d, tab_ref) - far, NEG)
    val = jnp.where(t == TN_MASKED, NEG, jnp.where(t == TN_ZERO, 0.0, val))
    o_ref[0, 0] = val


def _cmp_window_kernel(tab_ref, o_ref):
    e = lax.broadcasted_iota(jnp.int32, (WT_ROWS, HQ), 0) - WT_LEAD
    i = lax.broadcasted_iota(jnp.int32, (WT_ROWS, HQ), 1) & (QT - 1)
    d = i - (CMP_LEN - 1) - CMP_STRIDE * e
    far = tab_ref[0, REL_BUCKETS - 1:REL_BUCKETS, :]
    o_ref[0] = jnp.where(d >= 0, _rel_bias_lanes(d, tab_ref) - far, 0.0)


def _prompt_bias_tiles(rel_bias):
    tab = jnp.repeat((rel_bias * LOG2E).reshape(REL_BUCKETS, NSA_G, NSA_HPG), QT, axis=2)
    tab = jnp.transpose(tab, (1, 0, 2))
    near = pl.pallas_call(
        _near_tile_kernel,
        grid=(NSA_G, TN_COUNT),
        in_specs=[pl.BlockSpec((1, REL_BUCKETS, HQ), lambda g, t: (g, 0, 0))],
        out_specs=pl.BlockSpec((1, 1, QT, HQ), lambda g, t: (g, t, 0, 0)),
        out_shape=jax.ShapeDtypeStruct((NSA_G, TN_COUNT, QT, HQ), F32),
        compiler_params=_cparams(("parallel", "arbitrary")),
        name="bias_near_tiles",
    )(tab)
    cwin = pl.pallas_call(
        _cmp_window_kernel,
        grid=(NSA_G,),
        in_specs=[pl.BlockSpec((1, REL_BUCKETS, HQ), lambda g: (g, 0, 0))],
        out_specs=pl.BlockSpec((1, WT_ROWS, HQ), lambda g: (g, 0, 0)),
        out_shape=jax.ShapeDtypeStruct((NSA_G, WT_ROWS, HQ), F32),
        compiler_params=_cparams(("parallel",)),
        name="bias_cmp_window",
    )(tab)
    return near, cwin


def _nsa_kernel(q_ref, kc_ref, vct_ref, ks_ref, vst_ref, kw_ref, vwt_ref, tn_ref, wt_ref,
                g_ref, o_ref, s_ref, imp_ref, mb_ref, m_ref, acc_ref,
                buf0_ref, buf1_ref, tmax_ref, *, ncp, nblk, n_sel):
    b = pl.program_id(1)
    qs = b * QT
    q = q_ref[0, 0]

    sc = _dot_nt(kc_ref[0], q)
    ci = lax.broadcasted_iota(jnp.int32, (ncp, HQ), 0)
    li = lax.broadcasted_iota(jnp.int32, (ncp, HQ), 1) & (QT - 1)
    vis = (CMP_STRIDE * ci + (CMP_LEN - 1) - li) <= qs
    s_ref[0:WT_LEAD, :] = jnp.full((WT_LEAD, HQ), NEG, F32)
    s_ref[WT_LEAD + ncp:, :] = jnp.full((s_ref.shape[0] - WT_LEAD - ncp, HQ), NEG, F32)
    s_ref[WT_LEAD:WT_LEAD + ncp, :] = jnp.where(vis, sc, NEG)
    r0 = pl.multiple_of(8 * b, 8)
    s_ref[pl.ds(r0, WT_ROWS), :] += wt_ref[0]
    s = s_ref[WT_LEAD:WT_LEAD + ncp, :]
    m = jnp.max(s, axis=0, keepdims=True)
    m = jnp.where(m <= 0.1 * NEG, 0.0, m)
    p = jnp.exp2(s - m)
    rinv = 1.0 / jnp.maximum(jnp.sum(p, axis=0, keepdims=True), 1e-30)
    o_c = _dot(vct_ref[0], p.astype(BF16)) * rinv
    pn = p * rinv
    imp = pn[:, 0:QT]
    for h in range(1, NSA_HPG):
        imp = imp + pn[:, h * QT:(h + 1) * QT]
    imp_ref[0:8, :] = jnp.zeros((8, QT), F32)
    imp_ref[8:8 + ncp, :] = imp
    if imp_ref.shape[0] > 8 + ncp:
        imp_ref[8 + ncp:, :] = jnp.zeros((imp_ref.shape[0] - 8 - ncp, QT), F32)
    ratio = SEL_BLOCK // CMP_STRIDE
    span = ratio + CMP_HALF - 1
    psl = imp_ref[pl.ds(8 - (CMP_HALF - 1), nblk, stride=ratio), :]
    for mm in range(1, span):
        psl = psl + imp_ref[pl.ds(8 - (CMP_HALF - 1) + mm, nblk, stride=ratio), :]

    ji = lax.broadcasted_iota(jnp.int32, (nblk, QT), 0)
    ii = lax.broadcasted_iota(jnp.int32, (nblk, QT), 1)
    cur = (qs + ii) // SEL_BLOCK
    jf = ji.astype(F32)
    forced = (ji == 0) | (ji == cur) | (ji == cur - 1)
    score0 = jnp.where(forced, BIG, jnp.where(ji <= cur, psl, -BIG))

    def pick_one(_, carry):
        score, selb = carry
        mx = jnp.max(score, axis=0, keepdims=True)
        first = jnp.min(jnp.where(score == mx, jf, 1e9), axis=0, keepdims=True)
        pick = jf == first
        return jnp.where(pick, -jnp.inf, score), jnp.where(pick, 0.0, selb)

    _, selb = lax.fori_loop(0, n_sel, pick_one, (score0, jnp.full((nblk, QT), NEG, F32)))
    mb_ref[0:nblk, :] = jnp.concatenate([selb] * NSA_HPG, axis=1)
    mb_ref[nblk:, :] = jnp.full((BLK_PER_KT, HQ), NEG, F32)

    m_ref[...] = jnp.full(m_ref.shape, NEG, F32)
    acc_ref[...] = jnp.zeros(acc_ref.shape, F32)

    def sel_logits(kt, valid):
        ktc = jnp.where(valid, kt, 0)
        st = _dot_nt(ks_ref[0, ktc], q)
        row0 = jnp.where(valid, ktc * BLK_PER_KT, nblk)
        mb = mb_ref[pl.ds(pl.multiple_of(row0, BLK_PER_KT), BLK_PER_KT), :]
        st = jnp.concatenate(
            [st[k * SEL_BLOCK:(k + 1) * SEL_BLOCK] + mb[k:k + 1] for k in range(BLK_PER_KT)], axis=0)
        return st, ktc

    def softmax_step(st, tmax, vt):
        m_old = m_ref[...]
        m_new = jnp.maximum(m_old, tmax)
        alpha = jnp.exp2(m_old - m_new)
        p = jnp.exp2(st - m_new)
        acc_ref[...] = alpha * acc_ref[...] + _dot(vt, p.astype(BF16))
        m_ref[...] = m_new

    nt = b // SUB_PER_KT + 1
    n_far = jnp.maximum(nt - NEAR_TILES, 0)

    def far_logits(kt, buf_ref, slot):
        st, _ = sel_logits(kt, kt < n_far)
        buf_ref[...] = st
        tmax_ref[slot:slot + 1, :] = jnp.max(st, axis=0, keepdims=True)

    @pl.when(n_far > 0)
    def _():
        far_logits(0, buf0_ref, 0)

    def far_pair(pi, carry):
        k0 = 2 * pi
        far_logits(k0 + 1, buf1_ref, 1)
        softmax_step(buf0_ref[...], tmax_ref[0:1, :], vst_ref[0, k0])
        far_logits(k0 + 2, buf0_ref, 0)
        k1 = jnp.minimum(k0 + 1, jnp.maximum(n_far - 1, 0))
        softmax_step(buf1_ref[...], tmax_ref[1:2, :], vst_ref[0, k1])
        return carry

    lax.fori_loop(0, (n_far + 1) // 2, far_pair, 0)

    st_parts, vt_parts = [], []
    for w in range(NEAR_TILES):
        kt = nt - NEAR_TILES + w
        st, ktc = sel_logits(kt, kt >= 0)
        r = b - SUB_PER_KT * ktc
        parts = []
        for u in range(SUB_PER_KT):
            ru = r - u
            idx = jnp.where(ru < 0, TN_MASKED, jnp.where(ru >= N_NEAR, TN_ZERO, ru))
            parts.append(tn_ref[0, idx])
        st_parts.append(st + jnp.concatenate(parts, axis=0))
        vt_parts.append(vst_ref[0, ktc])
    st = jnp.concatenate(st_parts, axis=0)
    softmax_step(st, jnp.max(st, axis=0, keepdims=True), jnp.concatenate(vt_parts, axis=1))
    acc = acc_ref[...]
    o_s = acc[0:NSA_DK] * (1.0 / acc[NSA_DK:NSA_DK + 1])

    n_win = WINDOW // QT
    st_parts, vt_parts = [], []
    for w in range(n_win + 1):
        kt = b - n_win + w
        ktc = jnp.maximum(kt, 0)
        tn_idx = jnp.where(kt >= 0, TN_WIN if w == 0 else n_win - w, TN_MASKED)
        st_parts.append(_dot_nt(kw_ref[0, ktc], q) + tn_ref[0, tn_idx])
        vt_parts.append(vwt_ref[0, ktc])
    st = jnp.concatenate(st_parts, axis=0)
    p = jnp.exp2(st - jnp.max(st, axis=0, keepdims=True))
    o_w = _dot(jnp.concatenate(vt_parts, axis=1), p.astype(BF16)) * (
        1.0 / jnp.sum(p, axis=0, keepdims=True))

    g = jax.nn.sigmoid(g_ref[0, 0])
    o_ref[0, 0] = g[0:1] * o_c + g[1:2] * o_s + g[2:3] * o_w


def _nsa_attend(qt, kc, vct, ks, vst, kw, vwt, tn, wt, gt, *, n_sel_blk):
    s_n, nq = qt.shape[:2]
    ncp = kc.shape[1]
    nkt = ks.shape[1]
    nblk = nkt * BLK_PER_KT
    assert ncp >= 8 * (nq - 1) and ncp % 8 == 0 and nkt * SUB_PER_KT >= nq
    n_sel = min(N_SEL, n_sel_blk)
    imp_rows = max(8 + ncp, 8 + (SEL_BLOCK // CMP_STRIDE) * nblk)
    kern = functools.partial(_nsa_kernel, ncp=ncp, nblk=nblk, n_sel=n_sel)
    full = lambda a: pl.BlockSpec((1,) + a.shape[1:], lambda s, i: (s,) + (0,) * (a.ndim - 1))
    return pl.pallas_call(
        kern,
        grid=(s_n, nq),
        in_specs=[pl.BlockSpec((1, 1, HQ, LANES), lambda s, i: (s, i, 0, 0)),
                  full(kc), full(vct), full(ks), full(vst), full(kw), full(vwt),
                  full(tn), full(wt),
                  pl.BlockSpec((1, 1, 8, HQ), lambda s, i: (s, i, 0, 0))],
        out_specs=pl.BlockSpec((1, 1, NSA_DK, HQ), lambda s, i: (s, i, 0, 0)),
        out_shape=jax.ShapeDtypeStruct((s_n, nq, NSA_DK, HQ), F32),
        scratch_shapes=[pltpu.VMEM((ncp + WT_ROWS, HQ), F32),
                        pltpu.VMEM((imp_rows, QT), F32),
                        pltpu.VMEM((nblk + BLK_PER_KT, HQ), F32),
                        pltpu.VMEM((1, HQ), F32),
                        pltpu.VMEM((V_ROWS, HQ), F32),
                        pltpu.VMEM((KT_SEL, HQ), F32),
                        pltpu.VMEM((KT_SEL, HQ), F32),
                        pltpu.VMEM((8, HQ), F32)],
        compiler_params=_cparams(("parallel", "arbitrary")),
        name="nsa_prompt",
    )(qt, kc, vct, ks, vst, kw, vwt, tn, wt, gt)


def _q_tiles(q, nq):
    x = (q * Q_SCALE).reshape(nq, QT, NSA_G, NSA_HPG, NSA_DK)
    x = jnp.transpose(x, (2, 0, 3, 1, 4)).reshape(NSA_G, nq, HQ, NSA_DK)
    return jnp.pad(x, ((0, 0), (0, 0), (0, 0), (0, LANES - NSA_DK))).astype(BF16)


def _gate_tiles(gates, nq):
    x = gates[..., :N_BRANCH * NSA_HEADS].reshape(nq, QT, NSA_G, NSA_HPG, N_BRANCH)
    x = jnp.transpose(x, (2, 0, 4, 3, 1)).reshape(NSA_G, nq, N_BRANCH, HQ)
    return jnp.pad(x, ((0, 0), (0, 0), (0, 8 - N_BRANCH), (0, 0)))


def _k_tiles(kv, tile):
    nt = kv.shape[0] // tile
    k = kv[..., :KV_COLS].reshape(nt, tile, NSA_G, NSA_DK)
    k = jnp.transpose(k, (2, 0, 1, 3))
    k = jnp.pad(k, ((0, 0), (0, 0), (0, 0), (0, LANES - NSA_DK))).astype(BF16)
    v = kv[..., KV_COLS:].reshape(nt, tile, NSA_G, NSA_DK)
    v = jnp.transpose(v, (2, 0, 3, 1)).astype(BF16)
    return k, v


def _prompt_attend(q, gates, tok, kv_s, kv_w, bias):
    t = q.shape[0]
    nq = t // QT
    near, cwin = bias
    ncp = max(_round_up(tok.shape[0] + 1, LANES), _round_up(8 * nq, LANES))
    kc, vct = _k_tiles(jnp.pad(tok, ((0, ncp - tok.shape[0]), (0, 0))), ncp)
    ks, vst = _k_tiles(kv_s, KT_SEL)
    ones = jnp.zeros(vst.shape[:2] + (V_ROWS - NSA_DK, KT_SEL), BF16).at[:, :, 0].set(1.0)
    vst = jnp.concatenate([vst, ones], axis=2)
    kw, vwt = _k_tiles(kv_w, QT)
    ot = _nsa_attend(_q_tiles(q, nq), kc[:, 0], vct[:, 0], ks, vst, kw, vwt, near, cwin,
                     _gate_tiles(gates, nq), n_sel_blk=t // SEL_BLOCK)
    x = ot.reshape(NSA_G, nq, NSA_DK, NSA_HPG, QT)
    return jnp.transpose(x, (1, 4, 0, 3, 2)).reshape(t, NSA_WIDTH)


def _sample_bias_kernel(tab_ref, cb_ref, sb_ref, snb_ref, wb_ref, *, past, dseq, ns):
    rows = tab_ref.shape[0]
    far = tab_ref[:, REL_BUCKETS - 1:REL_BUCKETS]

    def tok_of(n):
        return lax.broadcasted_iota(jnp.int32, (rows, n), 0) % dseq

    def lane(n):
        return lax.broadcasted_iota(jnp.int32, (rows, n), 1)

    for idx in range(2):
        rho = LANES * (ns - 1 + idx) + lane(LANES)
        d = past + tok_of(LANES) - (CMP_STRIDE * (rho - 1) + CMP_LEN - 1)
        cb_ref[idx] = jnp.where(d >= 0, _rel_bias_rows(d, tab_ref) - far, 0.0)
    d = SP_KEYS + tok_of(SP_KEYS) - lane(SP_KEYS)
    sb_ref[...] = _rel_bias_rows(d, tab_ref) - far
    d = tok_of(LANES) - lane(LANES)
    snb_ref[...] = jnp.where(d >= 0, _rel_bias_rows(d, tab_ref) - far, NEG)
    d = WINDOW + tok_of(WINDOW + LANES) - lane(WINDOW + LANES)
    wb_ref[...] = jnp.where((d >= 0) & (d <= WINDOW), _rel_bias_rows(d, tab_ref), NEG)


def _sample_bias(rel_bias, past, dseq, ns):
    rows = NSA_HEADS * dseq
    tab = (rel_bias * LOG2E).reshape(REL_BUCKETS, NSA_G, NSA_HPG)
    tab = jnp.transpose(tab, (2, 1, 0))
    tab = jnp.repeat(tab.reshape(NSA_HEADS, 1, REL_BUCKETS), dseq, axis=1).reshape(rows, REL_BUCKETS)
    tab = jnp.pad(tab, ((0, 0), (0, LANES - REL_BUCKETS)))
    shapes = [(2, rows, LANES), (rows, SP_KEYS), (rows, LANES), (rows, WINDOW + LANES)]
    cb, sb, snb, wb = pl.pallas_call(
        functools.partial(_sample_bias_kernel, past=past, dseq=dseq, ns=ns),
        out_shape=[jax.ShapeDtypeStruct(s, F32) for s in shapes],
        compiler_params=pltpu.CompilerParams(vmem_limit_bytes=VMEM_LIMIT),
        name="bias_sample",
    )(tab)
    return cb, sb, snb, wb


def _sample_nsa_kernel(pt_ref, *refs, ns, past, dseq, n_sel, n_sel_blk):
    cmp_pages = refs[:SP_PP]
    slc_pages = refs[SP_PP:2 * SP_PP]
    (xn_ref, ksn_ref, win_ref, kwn_ref, q_ref, g_ref, w1_ref, pos_ref, pw_ref, w2_ref, e_ref,
     cb_ref, sb_ref, snb_ref, wb_ref, o_ref,
     s_ref, vc_ref, carry_ref, sel_ref, m_ref, l_ref, acc_ref, oc_ref, x_ref) = refs[2 * SP_PP:]
    j = pl.program_id(1)
    qbd = q_ref[0]
    rows = qbd.shape[0]
    ntile = ns + 1
    nblk_l = sel_ref.shape[0] * SP_BLKS

    def feat_major(refs_, c):
        return jnp.concatenate([r[0, c].reshape(KV_COLS, r.shape[-1]) for r in refs_], axis=1)

    def tokens_of(gather):
        out = []
        for c in range(2):
            xc = gather(c)
            n = xc.shape[0]
            row0 = lax.broadcasted_iota(jnp.int32, (n, KV_COLS), 0) == 0
            y = _dot(xc.astype(BF16), w1_ref[c])
            first = jnp.where(row0, carry_ref[c:c + 1, :], pltpu.roll(y[:, :KV_COLS], 1, 0))
            carry_ref[c:c + 1, :] = y[n - 1:n, :KV_COLS]
            pre = first + y[:, KV_COLS:] + _pos_bias(pos_ref, pw_ref, c)
            out.append(_dot(jax.nn.gelu(pre).astype(BF16), w2_ref[c]))
        return out

    def online_update(s, v):
        m_old = m_ref[...]
        m_new = jnp.maximum(m_old, jnp.max(s, axis=1, keepdims=True))
        alpha = jnp.exp2(m_old - m_new)
        p = jnp.exp2(s - m_new)
        l_ref[...] = alpha * l_ref[...] + jnp.sum(p, axis=1, keepdims=True)
        acc_ref[...] = alpha * acc_ref[...] + _dot_nt(p.astype(BF16), v)
        m_ref[...] = m_new

    @pl.when(j == 0)
    def _():
        carry_ref[...] = jnp.zeros(carry_ref.shape, F32)

    @pl.when(j < ns)
    def _():
        for p, r in enumerate(cmp_pages):
            for c in range(2):
                for gp in range(2):
                    tile = r[0, c, 2 * gp:2 * gp + 2].reshape(LANES, PAGE_SIZE)
                    x_ref[2 * c + gp, p * PAGE_SIZE:(p + 1) * PAGE_SIZE, :] = tile.T

        def gather(c):
            return jnp.concatenate(
                [x_ref[2 * c + gp, pl.ds(l, SP_KEYS // CMP_STRIDE, stride=CMP_STRIDE), :]
                 for l in range(CMP_STRIDE) for gp in range(2)], axis=1)

        tok_k, tok_v = tokens_of(gather)
        s_ref[j] = _dot_nt(qbd, tok_k.astype(BF16))
        vc_ref[j] = tok_v.astype(BF16)

    @pl.when(j == ns - 1)
    def _():
        tok_k, tok_v = tokens_of(lambda c: _gather_cv(xn_ref[0], c))
        zpad = jnp.zeros((LANES - tok_k.shape[0], KV_COLS), F32)
        s_ref[ns] = _dot_nt(qbd, jnp.concatenate([tok_k, zpad], axis=0).astype(BF16))
        vc_ref[ns] = jnp.concatenate([tok_v, zpad], axis=0).astype(BF16)

        tiles = []
        for t in range(ntile):
            st = s_ref[t]
            if t >= ns - 1:
                st = st + cb_ref[t - (ns - 1)]
            tiles.append(st)
        s = jnp.concatenate(tiles, axis=1)
        width = ntile * LANES
        rho = lax.broadcasted_iota(jnp.int32, (rows, width), 1)
        tq = lax.broadcasted_iota(jnp.int32, (rows, width), 0) % dseq
        n_cmp = (past + _round_up(dseq, SEL_BLOCK)) // CMP_STRIDE - (CMP_HALF - 1)
        vis = (rho >= 1) & (rho <= n_cmp) & (CMP_STRIDE * (rho - 1) + CMP_LEN - 1 <= past + tq)
        s = jnp.where(vis, s, NEG)
        m = jnp.max(s, axis=1, keepdims=True)
        m = jnp.where(m <= 0.1 * NEG, 0.0, m)
        p = jnp.exp2(s - m)
        rinv = 1.0 / jnp.maximum(jnp.sum(p, axis=1, keepdims=True), 1e-30)
        vc = jnp.concatenate([vc_ref[t] for t in range(ntile)], axis=0)
        oc_ref[...] = _dot(p.astype(BF16), vc) * rinv
        pn = p * rinv
        gt = NSA_G * dseq
        imp = pn[0:gt]
        for h in range(1, NSA_HPG):
            imp = imp + pn[h * gt:(h + 1) * gt]
        ratio = SEL_BLOCK // CMP_STRIDE
        span = ratio + CMP_HALF - 1
        ar = lax.broadcasted_iota(jnp.int32, (width, nblk_l), 0)
        ac = lax.broadcasted_iota(jnp.int32, (width, nblk_l), 1)
        band = jnp.where((ar >= ratio * ac) & (ar < ratio * ac + span), 1.0, 0.0).astype(BF16)
        i_hi = imp.astype(BF16)
        r1 = imp - i_hi.astype(F32)
        i_mid = r1.astype(BF16)
        i_lo = (r1 - i_mid.astype(F32)).astype(BF16)
        psl = _dot(i_hi, band) + _dot(i_mid, band) + _dot(i_lo, band)
        ji = lax.broadcasted_iota(jnp.int32, (gt, nblk_l), 1)
        tq2 = lax.broadcasted_iota(jnp.int32, (gt, nblk_l), 0) % dseq
        cur = (past + tq2) // SEL_BLOCK
        jf = ji.astype(F32)
        forced = (ji == 0) | (ji == cur) | (ji == cur - 1)
        score0 = jnp.where(forced, BIG, jnp.where(ji <= cur, psl, -BIG))
        score0 = jnp.where(ji < n_sel_blk, score0, -jnp.inf)

        def pick_one(_, carry):
            score, sel = carry
            mx = jnp.max(score, axis=1, keepdims=True)
            first = jnp.min(jnp.where(score == mx, jf, 1e9), axis=1, keepdims=True)
            pick = jf == first
            return jnp.where(pick, -jnp.inf, score), jnp.where(pick, 1.0, sel)

        _, sel = lax.fori_loop(0, n_sel, pick_one, (score0, jnp.zeros((gt, nblk_l), F32)))
        for t in range(sel_ref.shape[0]):
            piece = sel[:, t * SP_BLKS:(t + 1) * SP_BLKS]
            piece = jnp.concatenate([piece, jnp.zeros((gt, LANES - SP_BLKS), F32)], axis=1)
            sel_ref[t] = jnp.concatenate([piece] * NSA_HPG, axis=0).astype(BF16)
        m_ref[...] = jnp.full(m_ref.shape, NEG, F32)
        l_ref[...] = jnp.zeros(l_ref.shape, F32)
        acc_ref[...] = jnp.zeros(acc_ref.shape, F32)

    @pl.when(j >= ns)
    def _():
        j2 = j - ns
        s = _dot(qbd, feat_major(slc_pages, 0).astype(BF16))
        chosen = _dot(sel_ref[j2], e_ref[...])
        s = s + (chosen - 1.0) * (-NEG)
        s = s + jnp.where(j2 == ns - 1, sb_ref[...], 0.0)
        online_update(s, feat_major(slc_pages, 1).astype(BF16))

    @pl.when(j == 2 * ns - 1)
    def _():
        online_update(_dot(qbd, ksn_ref[0, 0].astype(BF16)) + snb_ref[...],
                      ksn_ref[0, 1].astype(BF16))
        o_s = acc_ref[...] * (1.0 / l_ref[...])
        kw = jnp.concatenate([feat_major([win_ref], 0), kwn_ref[0, 0]], axis=1)
        vw = jnp.concatenate([feat_major([win_ref], 1), kwn_ref[0, 1]], axis=1)
        s = _dot(qbd, kw.astype(BF16)) + wb_ref[...]
        p = jnp.exp2(s - jnp.max(s, axis=1, keepdims=True))
        o_w = _dot_nt(p.astype(BF16), vw.astype(BF16)) * (1.0 / jnp.sum(p, axis=1, keepdims=True))
        g = jax.nn.sigmoid(g_ref[0])
        o_ref[0] = g[:, 0:1] * oc_ref[...] + g[:, 1:2] * o_s + g[:, 2:3] * o_w


def _sample_attend(q, gates, kv_c, kv_s, kv_w, cache_cmp, cache_slc, win_buf, page_table, cw,
                   rel_bias):
    bsz, dseq = q.shape[:2]
    n_pages = page_table.shape[1]
    past = n_pages * PAGE_SIZE
    assert n_pages % SP_PP == 0 and dseq <= SEL_BLOCK and win_buf.shape[1] == WINDOW
    feat_major = lambda a: jnp.transpose(a, (0, 2, 3, 4, 1))
    cache_cmp, cache_slc, win_buf = feat_major(cache_cmp), feat_major(cache_slc), feat_major(win_buf)
    ns = n_pages // SP_PP
    rows = NSA_HEADS * dseq
    n_sel_blk = past // SEL_BLOCK + 1
    n_sel = min(N_SEL, n_sel_blk)
    nsel_tiles = _round_up(-(-n_sel_blk // SP_BLKS), LANES // SP_BLKS)
    w1big, pos, pw, w2big = cw
    cb, sb, snb, wb = _sample_bias(rel_bias, past, dseq, ns)

    qr = jnp.transpose((q * Q_SCALE).reshape(bsz, dseq, NSA_G, NSA_HPG, NSA_DK), (0, 3, 2, 1, 4))
    qbd = jnp.einsum('bhgtd,gk->bhgtkd', qr, jnp.eye(NSA_G, dtype=F32))
    qbd = qbd.reshape(bsz, rows, KV_COLS).astype(BF16)
    gr = gates[..., :N_BRANCH * NSA_HEADS].reshape(bsz, dseq, NSA_G, NSA_HPG, N_BRANCH)
    gr = jnp.transpose(gr, (0, 3, 2, 1, 4)).reshape(bsz, rows, N_BRANCH)
    gr = jnp.pad(gr, ((0, 0), (0, 0), (0, LANES - N_BRANCH)))
    pad_rows = lambda a, n: jnp.pad(a, ((0, 0), (0, n - a.shape[1]), (0, 0)))
    xn = pad_rows(kv_c, SEL_BLOCK).reshape(bsz, SEL_BLOCK // CMP_STRIDE, CMP_STRIDE * ROW_W)
    xn = pad_rows(xn, 8)
    new_feat_major = lambda a: jnp.transpose(
        pad_rows(a, LANES).reshape(bsz, LANES, 2, KV_COLS), (0, 2, 3, 1))
    ksn = new_feat_major(kv_s)
    kwn = new_feat_major(kv_w)
    expand = (jnp.arange(LANES)[:, None] == (jnp.arange(SP_KEYS) // SEL_BLOCK)[None, :]).astype(BF16)
    page_block = (1, 2, NSA_G, NSA_DK, PAGE_SIZE)

    def page_map(k, phase):
        if phase == 0:
            return lambda b, j, pt: (pt[b, jnp.minimum(j, ns - 1) * SP_PP + k], 0, 0, 0, 0)
        return lambda b, j, pt: (pt[b, jnp.maximum(j - ns, 0) * SP_PP + k], 0, 0, 0, 0)

    per_b = lambda a: pl.BlockSpec((1,) + a.shape[1:], lambda b, j, pt: (b,) + (0,) * (a.ndim - 1))
    const = lambda a: pl.BlockSpec(a.shape, lambda b, j, pt: (0,) * a.ndim)
    in_specs = ([pl.BlockSpec(page_block, page_map(k, 0)) for k in range(SP_PP)]
                + [pl.BlockSpec(page_block, page_map(k, 1)) for k in range(SP_PP)]
                + [per_b(xn), per_b(ksn), per_b(win_buf), per_b(kwn), per_b(qbd), per_b(gr),
                   pl.BlockSpec(w1big.shape, lambda b, j, pt: (0, 0, 0), pipeline_mode=pl.Buffered(1)),
                   const(pos), const(pw), const(w2big), const(expand),
                   const(cb), const(sb), const(snb), const(wb)])
    grid_spec = pltpu.PrefetchScalarGridSpec(
        num_scalar_prefetch=1, grid=(bsz, 2 * ns), in_specs=in_specs,
        out_specs=pl.BlockSpec((1, rows, KV_COLS), lambda b, j, pt: (b, 0, 0)),
        scratch_shapes=[pltpu.VMEM((ns + 1, rows, LANES), F32),
                        pltpu.VMEM((ns + 1, LANES, KV_COLS), BF16),
                        pltpu.VMEM((8, KV_COLS), F32),
                        pltpu.VMEM((nsel_tiles, rows, LANES), BF16),
                        pltpu.VMEM((rows, 1), F32),
                        pltpu.VMEM((rows, 1), F32),
                        pltpu.VMEM((rows, KV_COLS), F32),
                        pltpu.VMEM((rows, KV_COLS), F32),
                        pltpu.VMEM((ROW_W // LANES, SP_KEYS, LANES), F32)])
    out = pl.pallas_call(
        functools.partial(_sample_nsa_kernel, ns=ns, past=past, dseq=dseq, n_sel=n_sel,
                          n_sel_blk=n_sel_blk),
        grid_spec=grid_spec,
        out_shape=jax.ShapeDtypeStruct((bsz, rows, KV_COLS), F32),
        compiler_params=_cparams(("parallel", "arbitrary")),
        name="nsa_sample",
    )(page_table, *([cache_cmp] * SP_PP), *([cache_slc] * SP_PP), xn, ksn, win_buf, kwn, qbd, gr,
      w1big, pos, pw, w2big, expand, cb, sb, snb, wb)
    o = out.reshape(bsz, NSA_HPG, NSA_G, dseq, NSA_G, NSA_DK)
    o = jnp.einsum('bhgtge->btghe', o)
    return o.reshape(bsz * dseq, NSA_WIDTH)


def _out_proj_kernel(ycm_ref, ynsa_ref, h_ref, gt_ref, gpost_ref, w1_ref, w2_ref, o_ref):
    out = _dot(ycm_ref[...].astype(BF16), w1_ref[...]) + _dot(ynsa_ref[...].astype(BF16), w2_ref[...])
    o_ref[...] = h_ref[...] + gt_ref[...] * _rms(out, gpost_ref[...])


def _out_proj(ycm, ynsa, h, gate, g_post, w1, w2, tm):
    r, d = h.shape
    return pl.pallas_call(
        _out_proj_kernel,
        grid=(r // tm,),
        in_specs=[pl.BlockSpec((tm, CM_WIDTH), lambda i: (i, 0)),
                  pl.BlockSpec((tm, NSA_WIDTH), lambda i: (i, 0)),
                  pl.BlockSpec((tm, d), lambda i: (i, 0)),
                  _row_spec(gate, tm),
                  pl.BlockSpec((1, d), lambda i: (0, 0)),
                  pl.BlockSpec(w1.shape, lambda i: (0, 0)),
                  pl.BlockSpec(w2.shape, lambda i: (0, 0))],
        out_specs=pl.BlockSpec((tm, d), lambda i: (i, 0)),
        out_shape=jax.ShapeDtypeStruct((r, d), F32),
        compiler_params=_cparams(("parallel",)),
        name="out_proj",
    )(ycm, ynsa, h, gate, g_post, w1, w2)


def _row_tile(r, cap):
    tm = min(r, cap)
    while r % tm:
        tm //= 2
    return tm


def _layer(x, mod, per_row, weights, mixer):
    r, d = x.shape
    (norm_pre, norm_post, ffn, w_in, w_out1, w_out2) = weights
    if per_row is None:
        mrow = lambda i, j: mod[0, i, j].reshape(1, d)
    else:
        mrow = lambda i, j: jnp.repeat(mod[:, i, j], per_row, axis=0)
    tm = _row_tile(r, 512)
    tf = _row_tile(ffn[0][0].shape[1], 512)
    h = _ffn(x, mrow(0, 0), mrow(0, 1), mrow(0, 2), norm_pre[0:1], norm_post[0:1], *ffn[0],
             res_w=0.5, tm=tm, tf=tf)
    segs = _in_proj(h, mrow(1, 0), mrow(1, 1), norm_pre[1:2], w_in, _row_tile(r, 256))
    mixed_cm, mixed_nsa, state = mixer(*segs)
    h = _out_proj(mixed_cm, mixed_nsa, h, mrow(1, 2), norm_post[1:2], w_out1, w_out2, tm)
    h = _ffn(h, mrow(2, 0), mrow(2, 1), mrow(2, 2), norm_pre[2:3], norm_post[2:3], *ffn[1],
             res_w=0.5, tm=tm, tf=tf)
    return h, state


def kernel(x_prompt, x_sample, cache_cmp_kv, cache_slc_kv, state_win_kv, page_table, c_prompt,
           c_sample, w_mod, b_mod, norm_pre, norm_post, ffn_w_gate, ffn_w_up, ffn_w_down, w_in,
           w_out, cm_norm, cm_ws, cm_bs, phi_pos, phi_w1, phi_w2, rel_bias):
    depth = w_mod.shape[0]
    assert depth == 1
    bp, seq, d = x_prompt.shape
    bs, dseq, _ = x_sample.shape
    assert bp == 1 and seq % KT_SEL == 0 and seq >= WINDOW
    l = 0

    c_all = jnp.concatenate([c_prompt, c_sample], axis=0)
    mpad = _round_up(c_all.shape[0], 8)
    mod = _mod_proj(jnp.pad(c_all, ((0, mpad - c_all.shape[0]), (0, 0))), w_mod[l], b_mod[l])
    mod = mod[:bp + bs].reshape(bp + bs, 3, 3, d)

    ffn = [(ffn_w_gate[l, i].astype(BF16), ffn_w_up[l, i].astype(BF16), ffn_w_down[l, i].astype(BF16))
           for i in range(2)]
    w_in_p = jnp.pad(w_in[l], ((0, 0), (0, sum(IN_SEGS) - w_in.shape[2]))).astype(BF16)
    w_o = w_out[l].astype(BF16)
    weights = (norm_pre[l], norm_post[l], ffn, w_in_p, w_o[:CM_WIDTH], w_o[CM_WIDTH:])
    cw = _compress_weights(phi_pos[l], phi_w1[l], phi_w2[l])
    prompt_bias = _prompt_bias_tiles(rel_bias)

    def prompt_mixer(u, v, q, kv_c, kv_s, kv_w, gates):
        y_cm, _ = _chunk_mix(u, v, cm_norm[l], cm_ws[l], cm_bs[l], CHUNK)
        tok = _compress_tokens(kv_c[None], cw)[0]
        y_nsa = _prompt_attend(q, gates, tok, kv_s, kv_w, prompt_bias)
        return y_cm, y_nsa, (kv_c, kv_s, kv_w)

    hp, (pc, ps, pw) = _layer(x_prompt.reshape(seq, d), mod[:bp], None, weights, prompt_mixer)

    rows = _round_up(dseq, 16)

    def sample_mixer(u, v, q, kv_c, kv_s, kv_w, gates):
        per_seq = lambda a: a.reshape(bs, dseq, -1)
        pad_rows = lambda a, n: jnp.pad(per_seq(a), ((0, 0), (0, n - dseq), (0, 0)))
        y_cm, vn = _chunk_mix(pad_rows(u, rows).reshape(bs * rows, CM_WIDTH),
                              pad_rows(v, rows).reshape(bs * rows, CM_WIDTH),
                              cm_norm[l], cm_ws[l], cm_bs[l], rows)
        y_cm = y_cm.reshape(bs, rows, CM_WIDTH)[:, :dseq].reshape(bs * dseq, CM_WIDTH)
        vn = vn.reshape(bs, rows, CM_WIDTH)[:, :dseq]
        y_nsa = _sample_attend(per_seq(q), per_seq(gates), per_seq(kv_c), per_seq(kv_s), per_seq(kv_w),
                               cache_cmp_kv[l], cache_slc_kv[l], state_win_kv[l], page_table, cw,
                               rel_bias)
        return y_cm, y_nsa, (kv_c, kv_s, kv_w, vn)

    hs, (sc, ss, sw, sv) = _layer(x_sample.reshape(bs * dseq, d), mod[bp:], dseq, weights, sample_mixer)

    kvshape = lambda a, b, t: a.reshape(1, b, t, 2, NSA_G, NSA_DK)
    wp = min(WINDOW, seq)
    win_s = jnp.concatenate([state_win_kv[l].reshape(bs, -1, ROW_W), sw.reshape(bs, dseq, ROW_W)],
                            axis=1)[:, dseq:]
    return (hp.reshape(bp, seq, d), hs.reshape(bs, dseq, d),
            kvshape(pc, bp, seq), kvshape(sc, bs, dseq),
            kvshape(ps, bp, seq), kvshape(ss, bs, dseq),
            kvshape(pw[seq - wp:], bp, wp), kvshape(win_s, bs, win_s.shape[1]),
            sv.reshape(1, bs, dseq, CM_HEADS, CM_HEAD_DIM))
```

```python
import functools
import math

import jax
import jax.numpy as jnp
from jax import lax
from jax.experimental import pallas as pl
from jax.experimental.pallas import tpu as pltpu

F32 = jnp.float32
BF16 = jnp.bfloat16

CM_HEADS = 8
CM_HEAD_DIM = 128
CM_WIDTH = CM_HEADS * CM_HEAD_DIM
CHUNK = 128
NSA_HEADS = 16
NSA_G = 4
NSA_HPG = 4
NSA_DK = 64
NSA_WIDTH = NSA_HEADS * NSA_DK
KV_COLS = NSA_G * NSA_DK
ROW_W = 2 * KV_COLS
CMP_LEN = 32
CMP_STRIDE = 16
CMP_HALF = CMP_LEN // CMP_STRIDE
SEL_BLOCK = 64
N_SEL = 16
WINDOW = 512
N_BRANCH = 3
BIG = 1e4
REL_BUCKETS = 32
EPS = 1e-6
PAGE_SIZE = 128

LOG2E = 1.4426950408889634
Q_SCALE = NSA_DK ** -0.5 * LOG2E
NEG = -1e30
LANES = 128
QT = 128
HQ = NSA_HPG * QT
KT_SEL = 512
BLK_PER_KT = KT_SEL // SEL_BLOCK
SUB_PER_KT = KT_SEL // QT
N_NEAR = 8
NEAR_TILES = 3
V_ROWS = NSA_DK + 16
TN_WIN, TN_MASKED, TN_ZERO, TN_COUNT = 8, 9, 10, 11
WT_LEAD = 64
WT_ROWS = 200
REL_THRESHOLDS = (21, 27, 35, 46, 59, 77, 99, 128, 166, 216, 280, 363, 470, 609, 790)
VMEM_LIMIT = 56 * 1024 * 1024
SP_PP = CMP_STRIDE
SP_KEYS = SP_PP * PAGE_SIZE
SP_BLKS = SP_KEYS // SEL_BLOCK


def _cparams(sem):
    return pltpu.CompilerParams(dimension_semantics=sem, vmem_limit_bytes=VMEM_LIMIT)


def _dot(a, b):
    return jnp.dot(a, b, preferred_element_type=F32)


def _dot_nt(a, b):
    return lax.dot_general(a, b, (((1,), (1,)), ((), ())), preferred_element_type=F32)


def _split_bf16(x):
    hi = x.astype(BF16)
    lo = (x - hi.astype(F32)).astype(BF16)
    return hi, lo


def _dot_f32(a, b):
    a_hi, a_lo = _split_bf16(a)
    b_hi, b_lo = _split_bf16(b)
    return _dot(a_hi, b_hi) + _dot(a_lo, b_hi) + _dot(a_hi, b_lo)


def _rms(x, g):
    return x * lax.rsqrt(jnp.mean(x * x, axis=-1, keepdims=True) + EPS) * g


def _row_spec(arr, tm):
    d = arr.shape[-1]
    if arr.shape[0] == 1:
        return pl.BlockSpec((1, d), lambda *idx: (0, 0))
    return pl.BlockSpec((tm, d), lambda *idx: (idx[0], 0))


def _round_up(x, m):
    return -(-x // m) * m


def _mod_kernel(c_ref, w_ref, b_ref, o_ref):
    c = c_ref[...]
    s = c * jax.nn.sigmoid(c)
    o_ref[...] = _dot_f32(s, w_ref[...]) + b_ref[...]


def _mod_proj(c, w_mod, b_mod):
    m, d = c.shape
    n = w_mod.shape[1]
    tn = 512
    return pl.pallas_call(
        _mod_kernel,
        grid=(n // tn,),
        in_specs=[pl.BlockSpec((m, d), lambda j: (0, 0)),
                  pl.BlockSpec((d, tn), lambda j: (0, j)),
                  pl.BlockSpec((1, tn), lambda j: (0, j))],
        out_specs=pl.BlockSpec((m, tn), lambda j: (0, j)),
        out_shape=jax.ShapeDtypeStruct((m, n), F32),
        compiler_params=_cparams(("arbitrary",)),
        name="mod_proj",
    )(c, w_mod, b_mod.reshape(1, n))


def _ffn_kernel(x_ref, sh_ref, sc_ref, gt_ref, gpre_ref, gpost_ref, wg_ref, wu_ref, wd_ref,
                o_ref, a_ref, acc_ref, *, res_w, nf):
    f = pl.program_id(1)

    @pl.when(f == 0)
    def _():
        y = _rms(x_ref[...], gpre_ref[...])
        a_ref[...] = (y * (1.0 + sc_ref[...]) + sh_ref[...]).astype(BF16)
        acc_ref[...] = jnp.zeros_like(acc_ref)

    a = a_ref[...]
    h = _dot(a, wg_ref[...])
    u = _dot(a, wu_ref[...])
    act = (h * jax.nn.sigmoid(h) * u).astype(BF16)
    acc_ref[...] += _dot(act, wd_ref[...])

    @pl.when(f == nf - 1)
    def _():
        o_ref[...] = x_ref[...] + res_w * gt_ref[...] * _rms(acc_ref[...], gpost_ref[...])


def _ffn(x, shift, scale, gate, g_pre, g_post, wg, wu, wd, res_w, tm, tf):
    r, d = x.shape
    fdim = wg.shape[1]
    nf = fdim // tf
    return pl.pallas_call(
        functools.partial(_ffn_kernel, res_w=res_w, nf=nf),
        grid=(r // tm, nf),
        in_specs=[pl.BlockSpec((tm, d), lambda i, f: (i, 0)),
                  _row_spec(shift, tm), _row_spec(scale, tm), _row_spec(gate, tm),
                  pl.BlockSpec((1, d), lambda i, f: (0, 0)),
                  pl.BlockSpec((1, d), lambda i, f: (0, 0)),
                  pl.BlockSpec((d, tf), lambda i, f: (0, f)),
                  pl.BlockSpec((d, tf), lambda i, f: (0, f)),
                  pl.BlockSpec((tf, d), lambda i, f: (f, 0))],
        out_specs=pl.BlockSpec((tm, d), lambda i, f: (i, 0)),
        out_shape=jax.ShapeDtypeStruct((r, d), F32),
        scratch_shapes=[pltpu.VMEM((tm, d), BF16), pltpu.VMEM((tm, d), F32)],
        compiler_params=_cparams(("parallel", "arbitrary")),
        name="ffn",
    )(x, shift, scale, gate, g_pre, g_post, wg, wu, wd)


IN_SEGS = (CM_WIDTH, CM_WIDTH, NSA_WIDTH, 2 * KV_COLS, 2 * KV_COLS, 2 * KV_COLS, LANES)


def _in_proj_kernel(x_ref, sh_ref, sc_ref, gpre_ref, w_ref, *o_refs):
    y = _rms(x_ref[...], gpre_ref[...])
    a = (y * (1.0 + sc_ref[...]) + sh_ref[...]).astype(BF16)
    z = _dot(a, w_ref[...])
    off = 0
    for o_ref, width in zip(o_refs, IN_SEGS):
        o_ref[...] = z[:, off:off + width]
        off += width


def _in_proj(x, shift, scale, g_pre, w, tm):
    r, d = x.shape
    n = w.shape[1]
    return pl.pallas_call(
        _in_proj_kernel,
        grid=(r // tm,),
        in_specs=[pl.BlockSpec((tm, d), lambda i: (i, 0)),
                  _row_spec(shift, tm), _row_spec(scale, tm),
                  pl.BlockSpec((1, d), lambda i: (0, 0)),
                  pl.BlockSpec((d, n), lambda i: (0, 0), pipeline_mode=pl.Buffered(1))],
        out_specs=[pl.BlockSpec((tm, s), lambda i: (i, 0)) for s in IN_SEGS],
        out_shape=[jax.ShapeDtypeStruct((r, s), F32) for s in IN_SEGS],
        compiler_params=_cparams(("parallel",)),
        name="in_proj",
    )(x, shift, scale, g_pre, w)


def _chunk_mix_kernel(u_ref, v_ref, nrm_ref, ws_ref, bst_ref, y_ref, vn_ref, *, chunk):
    ri = lax.broadcasted_iota(jnp.int32, (chunk, chunk), 0)
    ci = lax.broadcasted_iota(jnp.int32, (chunk, chunk), 1)
    lower = ci <= ri
    for h in range(CM_HEADS):
        sl = slice(h * CM_HEAD_DIM, (h + 1) * CM_HEAD_DIM)
        vf = jax.nn.gelu(v_ref[:, sl])
        mu = jnp.mean(vf, axis=-1, keepdims=True)
        var = jnp.mean(jnp.square(vf - mu), axis=-1, keepdims=True)
        vn = (vf - mu) * lax.rsqrt(var + EPS) * nrm_ref[:, sl]
        vn_ref[:, sl] = vn
        w = jnp.where(lower, ws_ref[h], 0.0).astype(BF16)
        s = _dot(w, vn.astype(BF16)) + bst_ref[:, h:h + 1]
        y_ref[:, sl] = jax.nn.gelu(u_ref[:, sl]) * s


def _chunk_mix(u, v, cm_norm, cm_ws, cm_bs, chunk):
    r = u.shape[0]
    ws = cm_ws[:, :chunk, :chunk]
    bst = cm_bs[:, :chunk].T
    return pl.pallas_call(
        functools.partial(_chunk_mix_kernel, chunk=chunk),
        grid=(r // chunk,),
        in_specs=[pl.BlockSpec((chunk, CM_WIDTH), lambda i: (i, 0)),
                  pl.BlockSpec((chunk, CM_WIDTH), lambda i: (i, 0)),
                  pl.BlockSpec((1, CM_WIDTH), lambda i: (0, 0)),
                  pl.BlockSpec((CM_HEADS, chunk, chunk), lambda i: (0, 0, 0)),
                  pl.BlockSpec((chunk, CM_HEADS), lambda i: (0, 0))],
        out_specs=[pl.BlockSpec((chunk, CM_WIDTH), lambda i: (i, 0)),
                   pl.BlockSpec((chunk, CM_WIDTH), lambda i: (i, 0))],
        out_shape=[jax.ShapeDtypeStruct((r, CM_WIDTH), F32),
                   jax.ShapeDtypeStruct((r, CM_WIDTH), F32)],
        compiler_params=_cparams(("parallel",)),
        name="chunk_mix",
    )(u, v, cm_norm.reshape(1, CM_WIDTH), ws, bst)


def _gather_cv(x, c):
    return jnp.concatenate(
        [x[:, l * ROW_W + c * KV_COLS:l * ROW_W + (c + 1) * KV_COLS] for l in range(CMP_STRIDE)],
        axis=1)


def _pos_bias(pos_ref, pw_ref, c):
    posb = _dot_f32(pos_ref[c], pw_ref[c])[0:1]
    return jnp.concatenate([posb] * NSA_G, axis=1)


def _compress_weights(phi_pos, phi_w1, phi_w2):
    eye = jnp.eye(NSA_G, dtype=F32)
    w1 = phi_w1.reshape(CMP_HALF, CMP_STRIDE, 2, NSA_DK, NSA_DK)
    w1big = jnp.einsum('mlcde,gh->clgdmhe', w1, eye).reshape(
        2, CMP_STRIDE * KV_COLS, CMP_HALF * KV_COLS).astype(BF16)
    w2big = jnp.einsum('cde,gh->cgdhe', phi_w2, eye).reshape(2, KV_COLS, KV_COLS).astype(BF16)
    pos = jnp.transpose(phi_pos, (1, 0, 2)).reshape(2, 1, CMP_LEN * NSA_DK)
    pos = jnp.broadcast_to(pos, (2, 8, CMP_LEN * NSA_DK))
    pw = jnp.transpose(phi_w1, (1, 0, 2, 3)).reshape(2, CMP_LEN * NSA_DK, NSA_DK)
    return w1big, pos, pw, w2big


def _compress_kernel(x_ref, xn_ref, w1_ref, pos_ref, pw_ref, w2_ref, o_ref, *, rc):
    x = x_ref[0]
    xn = xn_ref[0]
    last_row = lax.broadcasted_iota(jnp.int32, (rc, KV_COLS), 0) == rc - 1
    for c in range(2):
        w1 = w1_ref[c]
        y = _dot(_gather_cv(x, c).astype(BF16), w1)
        yn = _dot(_gather_cv(xn, c).astype(BF16), w1)
        second = pltpu.roll(y[:, KV_COLS:], rc - 1, 0)
        second = jnp.where(last_row, yn[0:1, KV_COLS:], second)
        pre = y[:, :KV_COLS] + second + _pos_bias(pos_ref, pw_ref, c)
        o_ref[0, :, c * KV_COLS:(c + 1) * KV_COLS] = _dot(jax.nn.gelu(pre).astype(BF16), w2_ref[c])


def _pick_rows(n, cap):
    best = 8
    for rc in range(8, cap + 1, 8):
        if n % rc == 0:
            best = rc
    return best


def _compress(x, cw):
    bsz, n, _ = x.shape
    rc = _pick_rows(n, 256)
    w1big, pos, pw, w2big = cw
    nb8 = n // 8
    return pl.pallas_call(
        functools.partial(_compress_kernel, rc=rc),
        grid=(bsz, n // rc),
        in_specs=[pl.BlockSpec((1, rc, x.shape[2]), lambda b, i: (b, i, 0)),
                  pl.BlockSpec((1, 8, x.shape[2]),
                               lambda b, i: (b, jnp.minimum((i + 1) * (rc // 8), nb8 - 1), 0)),
                  pl.BlockSpec(w1big.shape, lambda b, i: (0, 0, 0)),
                  pl.BlockSpec(pos.shape, lambda b, i: (0, 0, 0)),
                  pl.BlockSpec(pw.shape, lambda b, i: (0, 0, 0)),
                  pl.BlockSpec(w2big.shape, lambda b, i: (0, 0, 0))],
        out_specs=pl.BlockSpec((1, rc, ROW_W), lambda b, i: (b, i, 0)),
        out_shape=jax.ShapeDtypeStruct((bsz, n, ROW_W), F32),
        compiler_params=_cparams(("parallel", "arbitrary")),
        name="compress",
    )(x, x, w1big, pos, pw, w2big)


def _compress_tokens(kv_c, cw):
    bsz, t = kv_c.shape[:2]
    n = t // CMP_STRIDE
    npad = _round_up(n, LANES if n >= LANES else 8)
    x = kv_c.reshape(bsz, n, CMP_STRIDE * ROW_W)
    x = jnp.pad(x, ((0, 0), (0, npad - n), (0, 0)))
    return _compress(x, cw)[:, :n - (CMP_HALF - 1)]


def _rel_bucket(d):
    n = jnp.maximum(d, 0)
    large = jnp.full(d.shape, REL_BUCKETS // 2, jnp.int32)
    for thr in REL_THRESHOLDS:
        large = large + (n >= thr).astype(jnp.int32)
    return jnp.where(n < REL_BUCKETS // 2, n, large)


def _rel_bias_lanes(d, tab_ref):
    bkt = _rel_bucket(d)
    f = jnp.zeros(d.shape, F32)
    for k in range(REL_BUCKETS):
        f = jnp.where(bkt == k, tab_ref[0, k:k + 1, :], f)
    return f


def _rel_bias_rows(d, tab_ref):
    bkt = _rel_bucket(d)
    f = jnp.zeros(d.shape, F32)
    for k in range(REL_BUCKETS):
        f = jnp.where(bkt == k, tab_ref[:, k:k + 1], f)
    return f


def _near_tile_kernel(tab_ref, o_ref):
    t = pl.program_id(1)
    delta = jnp.where(t < N_NEAR, t * QT, jnp.where(t == TN_WIN, WINDOW, 0))
    dmax = jnp.where(t == TN_WIN, WINDOW, 1 << 30)
    j = lax.broadcasted_iota(jnp.int32, (QT, HQ), 0)
    i = lax.broadcasted_iota(jnp.int32, (QT, HQ), 1) & (QT - 1)
    d = delta + i - j
    far = tab_ref[0, REL_BUCKETS - 1:REL_BUCKETS, :]
    val = jnp.where((d >= 0) & (d <= dmax), _rel_bias_lanes(d, tab_ref) - far, NEG)
    val = jnp.where(t == TN_MASKED, NEG, jnp.where(t == TN_ZERO, 0.0, val))
    o_ref[0, 0] = val


def _cmp_window_kernel(tab_ref, o_ref):
    e = lax.broadcasted_iota(jnp.int32, (WT_ROWS, HQ), 0) - WT_LEAD
    i = lax.broadcasted_iota(jnp.int32, (WT_ROWS, HQ), 1) & (QT - 1)
    d = i - (CMP_LEN - 1) - CMP_STRIDE * e
    far = tab_ref[0, REL_BUCKETS - 1:REL_BUCKETS, :]
    o_ref[0] = jnp.where(d >= 0, _rel_bias_lanes(d, tab_ref) - far, 0.0)


def _prompt_bias_tiles(rel_bias):
    tab = jnp.repeat((rel_bias * LOG2E).reshape(REL_BUCKETS, NSA_G, NSA_HPG), QT, axis=2)
    tab = jnp.transpose(tab, (1, 0, 2))
    near = pl.pallas_call(
        _near_tile_kernel,
        grid=(NSA_G, TN_COUNT),
        in_specs=[pl.BlockSpec((1, REL_BUCKETS, HQ), lambda g, t: (g, 0, 0))],
        out_specs=pl.BlockSpec((1, 1, QT, HQ), lambda g, t: (g, t, 0, 0)),
        out_shape=jax.ShapeDtypeStruct((NSA_G, TN_COUNT, QT, HQ), F32),
        compiler_params=_cparams(("parallel", "arbitrary")),
        name="bias_near_tiles",
    )(tab)
    cwin = pl.pallas_call(
        _cmp_window_kernel,
        grid=(NSA_G,),
        in_specs=[pl.BlockSpec((1, REL_BUCKETS, HQ), lambda g: (g, 0, 0))],
        out_specs=pl.BlockSpec((1, WT_ROWS, HQ), lambda g: (g, 0, 0)),
        out_shape=jax.ShapeDtypeStruct((NSA_G, WT_ROWS, HQ), F32),
        compiler_params=_cparams(("parallel",)),
        name="bias_cmp_window",
    )(tab)
    return near, cwin


GROUPS_PER_STEP = 2
N_WIN_TILES = WINDOW // QT + 1


def _nsa_group(q_ref, kc_ref, vct_ref, ks_ref, vst_ref, kw_refs, vwt_refs, tn_ref, wt_ref,
               g_ref, o_ref, s_ref, imp_ref, mb_ref, m_ref, acc_ref,
               buf0_ref, buf1_ref, tmax_ref, *, ncp, nblk, n_sel):
    b = pl.program_id(1)
    qs = b * QT
    q = q_ref[0, 0]

    sc = _dot_nt(kc_ref[0], q)
    ci = lax.broadcasted_iota(jnp.int32, (ncp, HQ), 0)
    li = lax.broadcasted_iota(jnp.int32, (ncp, HQ), 1) & (QT - 1)
    vis = (CMP_STRIDE * ci + (CMP_LEN - 1) - li) <= qs
    s_ref[0:WT_LEAD, :] = jnp.full((WT_LEAD, HQ), NEG, F32)
    s_ref[WT_LEAD + ncp:, :] = jnp.full((s_ref.shape[0] - WT_LEAD - ncp, HQ), NEG, F32)
    s_ref[WT_LEAD:WT_LEAD + ncp, :] = jnp.where(vis, sc, NEG)
    r0 = pl.multiple_of(8 * b, 8)
    s_ref[pl.ds(r0, WT_ROWS), :] += wt_ref[0]
    s = s_ref[WT_LEAD:WT_LEAD + ncp, :]
    m = jnp.max(s, axis=0, keepdims=True)
    m = jnp.where(m <= 0.1 * NEG, 0.0, m)
    p = jnp.exp2(s - m)
    rinv = 1.0 / jnp.maximum(jnp.sum(p, axis=0, keepdims=True), 1e-30)
    o_c = _dot(vct_ref[0], p.astype(BF16)) * rinv
    pn = p * rinv
    imp = pn[:, 0:QT]
    for h in range(1, NSA_HPG):
        imp = imp + pn[:, h * QT:(h + 1) * QT]
    imp_ref[0:8, :] = jnp.zeros((8, QT), F32)
    imp_ref[8:8 + ncp, :] = imp
    if imp_ref.shape[0] > 8 + ncp:
        imp_ref[8 + ncp:, :] = jnp.zeros((imp_ref.shape[0] - 8 - ncp, QT), F32)
    ratio = SEL_BLOCK // CMP_STRIDE
    span = ratio + CMP_HALF - 1
    psl = imp_ref[pl.ds(8 - (CMP_HALF - 1), nblk, stride=ratio), :]
    for mm in range(1, span):
        psl = psl + imp_ref[pl.ds(8 - (CMP_HALF - 1) + mm, nblk, stride=ratio), :]

    n_win = WINDOW // QT
    st_parts, vt_parts = [], []
    for w in range(n_win + 1):
        kt = b - n_win + w
        tn_idx = jnp.where(kt >= 0, TN_WIN if w == 0 else n_win - w, TN_MASKED)
        st_parts.append(_dot_nt(kw_refs[w][0, 0], q) + tn_ref[0, tn_idx])
        vt_parts.append(vwt_refs[w][0, 0])
    st = jnp.concatenate(st_parts, axis=0)
    p = jnp.exp2(st - jnp.max(st, axis=0, keepdims=True))
    o_w = _dot(jnp.concatenate(vt_parts, axis=1), p.astype(BF16)) * (
        1.0 / jnp.sum(p, axis=0, keepdims=True))
    g = jax.nn.sigmoid(g_ref[0, 0])
    o_ref[0, 0] = g[0:1] * o_c + g[2:3] * o_w

    ji = lax.broadcasted_iota(jnp.int32, (nblk, QT), 0)
    ii = lax.broadcasted_iota(jnp.int32, (nblk, QT), 1)
    cur = (qs + ii) // SEL_BLOCK
    jf = ji.astype(F32)
    forced = (ji == 0) | (ji == cur) | (ji == cur - 1)
    score0 = jnp.where(forced, BIG, jnp.where(ji <= cur, psl, -BIG))

    def pick_one(_, carry):
        score, selb = carry
        mx = jnp.max(score, axis=0, keepdims=True)
        first = jnp.min(jnp.where(score == mx, jf, 1e9), axis=0, keepdims=True)
        pick = jf == first
        return jnp.where(pick, -jnp.inf, score), jnp.where(pick, 0.0, selb)

    _, selb = lax.fori_loop(0, n_sel, pick_one, (score0, jnp.full((nblk, QT), NEG, F32)),
                            unroll=True)
    mb_ref[0:nblk, :] = jnp.concatenate([selb] * NSA_HPG, axis=1)
    mb_ref[nblk:, :] = jnp.full((BLK_PER_KT, HQ), NEG, F32)

    m_ref[...] = jnp.full(m_ref.shape, NEG, F32)
    acc_ref[...] = jnp.zeros(acc_ref.shape, F32)

    def sel_logits(kt, valid):
        ktc = jnp.where(valid, kt, 0)
        st = _dot_nt(ks_ref[0, ktc], q)
        row0 = jnp.where(valid, ktc * BLK_PER_KT, nblk)
        mb = mb_ref[pl.ds(pl.multiple_of(row0, BLK_PER_KT), BLK_PER_KT), :]
        st = jnp.concatenate(
            [st[k * SEL_BLOCK:(k + 1) * SEL_BLOCK] + mb[k:k + 1] for k in range(BLK_PER_KT)], axis=0)
        return st, ktc

    def softmax_step(st, tmax, vt):
        m_old = m_ref[...]
        m_new = jnp.maximum(m_old, tmax)
        alpha = jnp.exp2(m_old - m_new)
        p = jnp.exp2(st - m_new)
        acc_ref[...] = alpha * acc_ref[...] + _dot(vt, p.astype(BF16))
        m_ref[...] = m_new

    nt = b // SUB_PER_KT + 1
    n_far = jnp.maximum(nt - NEAR_TILES, 0)

    def far_logits(kt, buf_ref, slot):
        st, _ = sel_logits(kt, kt < n_far)
        buf_ref[...] = st
        tmax_ref[slot:slot + 1, :] = jnp.max(st, axis=0, keepdims=True)

    def far_first():
        far_logits(0, buf0_ref, 0)

    def far_pair(pi):
        k0 = 2 * pi
        far_logits(k0 + 1, buf1_ref, 1)
        softmax_step(buf0_ref[...], tmax_ref[0:1, :], vst_ref[0, k0])
        far_logits(k0 + 2, buf0_ref, 0)
        k1 = jnp.minimum(k0 + 1, jnp.maximum(n_far - 1, 0))
        softmax_step(buf1_ref[...], tmax_ref[1:2, :], vst_ref[0, k1])

    yield far_first, far_pair

    st_parts, vt_parts = [], []
    for w in range(NEAR_TILES):
        kt = nt - NEAR_TILES + w
        st, ktc = sel_logits(kt, kt >= 0)
        r = b - SUB_PER_KT * ktc
        parts = []
        for u in range(SUB_PER_KT):
            ru = r - u
            idx = jnp.where(ru < 0, TN_MASKED, jnp.where(ru >= N_NEAR, TN_ZERO, ru))
            parts.append(tn_ref[0, idx])
        st_parts.append(st + jnp.concatenate(parts, axis=0))
        vt_parts.append(vst_ref[0, ktc])
    st = jnp.concatenate(st_parts, axis=0)
    softmax_step(st, jnp.max(st, axis=0, keepdims=True), jnp.concatenate(vt_parts, axis=1))
    acc = acc_ref[...]
    o_s = acc[0:NSA_DK] * (1.0 / acc[NSA_DK:NSA_DK + 1])
    o_ref[0, 0] += jax.nn.sigmoid(g_ref[0, 0, 1:2]) * o_s


def _nsa_kernel(*refs, n_scratch, **static):
    gp = GROUPS_PER_STEP
    io, scratch = refs[:len(refs) - n_scratch], refs[len(refs) - n_scratch:]
    b = pl.program_id(1)
    n_far = jnp.maximum(b // SUB_PER_KT + 1 - NEAR_TILES, 0)
    programs = []
    for gi in range(gp):
        v = [r.at[pl.ds(gi, 1)] for r in io]
        kw_refs, vwt_refs = v[5:5 + N_WIN_TILES], v[5 + N_WIN_TILES:5 + 2 * N_WIN_TILES]
        rest = v[5 + 2 * N_WIN_TILES:]
        programs.append(_nsa_group(*v[:5], kw_refs, vwt_refs, *rest, *[r.at[gi] for r in scratch],
                                   **static))
    loops = [next(p) for p in programs]

    @pl.when(n_far > 0)
    def _():
        for far_first, _ in loops:
            far_first()

    def far_body(pi, carry):
        for _, far_pair in loops:
            far_pair(pi)
        return carry

    lax.fori_loop(0, (n_far + 1) // 2, far_body, 0)
    for p in programs:
        for _ in p:
            pass


def _nsa_attend(qt, kc, vct, ks, vst, kw, vwt, tn, wt, gt, *, n_sel_blk):
    s_n, nq = qt.shape[:2]
    gp = GROUPS_PER_STEP
    ncp = kc.shape[1]
    nkt = ks.shape[1]
    nblk = nkt * BLK_PER_KT
    assert ncp >= 8 * (nq - 1) and ncp % 8 == 0 and nkt * SUB_PER_KT >= nq and s_n % gp == 0
    n_sel = min(N_SEL, n_sel_blk)
    imp_rows = max(8 + ncp, 8 + (SEL_BLOCK // CMP_STRIDE) * nblk)
    scratch = [(ncp + WT_ROWS, HQ), (imp_rows, QT), (nblk + BLK_PER_KT, HQ), (1, HQ), (V_ROWS, HQ),
               (KT_SEL, HQ), (KT_SEL, HQ), (8, HQ)]
    kern = functools.partial(_nsa_kernel, n_scratch=len(scratch), ncp=ncp, nblk=nblk, n_sel=n_sel)
    full = lambda a: pl.BlockSpec((gp,) + a.shape[1:], lambda s, i: (s,) + (0,) * (a.ndim - 1),
                                  pipeline_mode=pl.Buffered(1))
    per_tile = lambda a: pl.BlockSpec((gp, 1) + a.shape[2:], lambda s, i: (s, i, 0, 0))
    win = lambda a, w: pl.BlockSpec((gp, 1) + a.shape[2:],
                                    lambda s, i: (s, jnp.maximum(i - (N_WIN_TILES - 1) + w, 0), 0, 0))
    return pl.pallas_call(
        kern,
        grid=(s_n // gp, nq),
        in_specs=([per_tile(qt), full(kc), full(vct), full(ks), full(vst)]
                  + [win(kw, w) for w in range(N_WIN_TILES)]
                  + [win(vwt, w) for w in range(N_WIN_TILES)]
                  + [full(tn), full(wt), per_tile(gt)]),
        out_specs=pl.BlockSpec((gp, 1, NSA_DK, HQ), lambda s, i: (s, i, 0, 0)),
        out_shape=jax.ShapeDtypeStruct((s_n, nq, NSA_DK, HQ), F32),
        scratch_shapes=[pltpu.VMEM((gp,) + s, F32) for s in scratch],
        compiler_params=_cparams(("parallel", "arbitrary")),
        name="nsa_prompt",
    )(qt, kc, vct, ks, vst, *([kw] * N_WIN_TILES), *([vwt] * N_WIN_TILES), tn, wt, gt)


def _q_tiles(q, nq):
    x = (q * Q_SCALE).reshape(nq, QT, NSA_G, NSA_HPG, NSA_DK)
    x = jnp.transpose(x, (2, 0, 3, 1, 4)).reshape(NSA_G, nq, HQ, NSA_DK)
    return jnp.pad(x, ((0, 0), (0, 0), (0, 0), (0, LANES - NSA_DK))).astype(BF16)


def _gate_tiles(gates, nq):
    x = gates[..., :N_BRANCH * NSA_HEADS].reshape(nq, QT, NSA_G, NSA_HPG, N_BRANCH)
    x = jnp.transpose(x, (2, 0, 4, 3, 1)).reshape(NSA_G, nq, N_BRANCH, HQ)
    return jnp.pad(x, ((0, 0), (0, 0), (0, 8 - N_BRANCH), (0, 0)))


def _k_tiles(kv, tile):
    nt = kv.shape[0] // tile
    k = kv[..., :KV_COLS].reshape(nt, tile, NSA_G, NSA_DK)
    k = jnp.transpose(k, (2, 0, 1, 3))
    k = jnp.pad(k, ((0, 0), (0, 0), (0, 0), (0, LANES - NSA_DK))).astype(BF16)
    v = kv[..., KV_COLS:].reshape(nt, tile, NSA_G, NSA_DK)
    v = jnp.transpose(v, (2, 0, 3, 1)).astype(BF16)
    return k, v


def _prompt_attend(q, gates, tok, kv_s, kv_w, bias):
    t = q.shape[0]
    nq = t // QT
    near, cwin = bias
    ncp = max(_round_up(tok.shape[0] + 1, LANES), _round_up(8 * nq, LANES))
    kc, vct = _k_tiles(jnp.pad(tok, ((0, ncp - tok.shape[0]), (0, 0))), ncp)
    ks, vst = _k_tiles(kv_s, KT_SEL)
    ones = jnp.zeros(vst.shape[:2] + (V_ROWS - NSA_DK, KT_SEL), BF16).at[:, :, 0].set(1.0)
    vst = jnp.concatenate([vst, ones], axis=2)
    kw, vwt = _k_tiles(kv_w, QT)
    ot = _nsa_attend(_q_tiles(q, nq), kc[:, 0], vct[:, 0], ks, vst, kw, vwt, near, cwin,
                     _gate_tiles(gates, nq), n_sel_blk=t // SEL_BLOCK)
    x = ot.reshape(NSA_G, nq, NSA_DK, NSA_HPG, QT)
    return jnp.transpose(x, (1, 4, 0, 3, 2)).reshape(t, NSA_WIDTH)


def _sample_bias_kernel(tab_ref, cb_ref, sb_ref, snb_ref, wb_ref, *, past, dseq, ns):
    rows = tab_ref.shape[0]
    far = tab_ref[:, REL_BUCKETS - 1:REL_BUCKETS]

    def tok_of(n):
        return lax.broadcasted_iota(jnp.int32, (rows, n), 0) % dseq

    def lane(n):
        return lax.broadcasted_iota(jnp.int32, (rows, n), 1)

    for idx in range(2):
        rho = LANES * (ns - 1 + idx) + lane(LANES)
        d = past + tok_of(LANES) - (CMP_STRIDE * (rho - 1) + CMP_LEN - 1)
        cb_ref[idx] = jnp.where(d >= 0, _rel_bias_rows(d, tab_ref) - far, 0.0)
    d = SP_KEYS + tok_of(SP_KEYS) - lane(SP_KEYS)
    sb_ref[...] = _rel_bias_rows(d, tab_ref) - far
    d = tok_of(LANES) - lane(LANES)
    snb_ref[...] = jnp.where(d >= 0, _rel_bias_rows(d, tab_ref) - far, NEG)
    d = WINDOW + tok_of(WINDOW + LANES) - lane(WINDOW + LANES)
    wb_ref[...] = jnp.where((d >= 0) & (d <= WINDOW), _rel_bias_rows(d, tab_ref), NEG)


def _sample_bias(rel_bias, past, dseq, ns):
    rows = NSA_HEADS * dseq
    tab = (rel_bias * LOG2E).reshape(REL_BUCKETS, NSA_G, NSA_HPG)
    tab = jnp.transpose(tab, (2, 1, 0))
    tab = jnp.repeat(tab.reshape(NSA_HEADS, 1, REL_BUCKETS), dseq, axis=1).reshape(rows, REL_BUCKETS)
    tab = jnp.pad(tab, ((0, 0), (0, LANES - REL_BUCKETS)))
    shapes = [(2, rows, LANES), (rows, SP_KEYS), (rows, LANES), (rows, WINDOW + LANES)]
    cb, sb, snb, wb = pl.pallas_call(
        functools.partial(_sample_bias_kernel, past=past, dseq=dseq, ns=ns),
        out_shape=[jax.ShapeDtypeStruct(s, F32) for s in shapes],
        compiler_params=pltpu.CompilerParams(vmem_limit_bytes=VMEM_LIMIT),
        name="bias_sample",
    )(tab)
    return cb, sb, snb, wb


def _sample_nsa_kernel(pt_ref, *refs, ns, past, dseq, n_sel, n_sel_blk):
    cmp_pages = refs[:SP_PP]
    slc_pages = refs[SP_PP:2 * SP_PP]
    (xn_ref, ksn_ref, win_ref, kwn_ref, q_ref, g_ref, w1_ref, pos_ref, pw_ref, w2_ref, e_ref,
     cb_ref, sb_ref, snb_ref, wb_ref, o_ref,
     s_ref, vc_ref, carry_ref, sel_ref, m_ref, l_ref, acc_ref, oc_ref, x_ref) = refs[2 * SP_PP:]
    j = pl.program_id(1)
    qbd = q_ref[0]
    rows = qbd.shape[0]
    ntile = ns + 1
    nblk_l = sel_ref.shape[0] * SP_BLKS

    def feat_major(refs_, c):
        return jnp.concatenate([r[0, c].reshape(KV_COLS, r.shape[-1]) for r in refs_], axis=1)

    def tokens_of(gather, prepare=None):
        out = []
        for c in range(2):
            if prepare is not None:
                prepare(c)
            xc = gather(c)
            n = xc.shape[0]
            row0 = lax.broadcasted_iota(jnp.int32, (n, KV_COLS), 0) == 0
            y = _dot(xc.astype(BF16), w1_ref[c])
            first = jnp.where(row0, carry_ref[c:c + 1, :], pltpu.roll(y[:, :KV_COLS], 1, 0))
            carry_ref[c:c + 1, :] = y[n - 1:n, :KV_COLS]
            pre = first + y[:, KV_COLS:] + _pos_bias(pos_ref, pw_ref, c)
            out.append(_dot(jax.nn.gelu(pre).astype(BF16), w2_ref[c]))
        return out

    def online_update(s, v):
        m_old = m_ref[...]
        m_new = jnp.maximum(m_old, jnp.max(s, axis=1, keepdims=True))
        alpha = jnp.exp2(m_old - m_new)
        p = jnp.exp2(s - m_new)
        l_ref[...] = alpha * l_ref[...] + jnp.sum(p, axis=1, keepdims=True)
        acc_ref[...] = alpha * acc_ref[...] + _dot_nt(p.astype(BF16), v)
        m_ref[...] = m_new

    @pl.when(j == 0)
    def _():
        carry_ref[...] = jnp.zeros(carry_ref.shape, F32)

    @pl.when(j < ns)
    def _():
        def to_rows(c):
            for p, r in enumerate(cmp_pages):
                for gp in range(2):
                    tile = r[0, c, 2 * gp:2 * gp + 2].reshape(LANES, PAGE_SIZE)
                    x_ref[2 * c + gp, p * PAGE_SIZE:(p + 1) * PAGE_SIZE, :] = tile.T

        def gather(c):
            return jnp.concatenate(
                [x_ref[2 * c + gp, pl.ds(l, SP_KEYS // CMP_STRIDE, stride=CMP_STRIDE), :]
                 for l in range(CMP_STRIDE) for gp in range(2)], axis=1)

        tok_k, tok_v = tokens_of(gather, to_rows)
        s_ref[j] = _dot_nt(qbd, tok_k.astype(BF16))
        vc_ref[j] = tok_v.astype(BF16)

    @pl.when(j == ns - 1)
    def _():
        tok_k, tok_v = tokens_of(lambda c: _gather_cv(xn_ref[0], c))
        zpad = jnp.zeros((LANES - tok_k.shape[0], KV_COLS), F32)
        s_ref[ns] = _dot_nt(qbd, jnp.concatenate([tok_k, zpad], axis=0).astype(BF16))
        vc_ref[ns] = jnp.concatenate([tok_v, zpad], axis=0).astype(BF16)

        tiles = []
        for t in range(ntile):
            st = s_ref[t]
            if t >= ns - 1:
                st = st + cb_ref[t - (ns - 1)]
            tiles.append(st)
        s = jnp.concatenate(tiles, axis=1)
        width = ntile * LANES
        rho = lax.broadcasted_iota(jnp.int32, (rows, width), 1)
        tq = lax.broadcasted_iota(jnp.int32, (rows, width), 0) % dseq
        n_cmp = (past + _round_up(dseq, SEL_BLOCK)) // CMP_STRIDE - (CMP_HALF - 1)
        vis = (rho >= 1) & (rho <= n_cmp) & (CMP_STRIDE * (rho - 1) + CMP_LEN - 1 <= past + tq)
        s = jnp.where(vis, s, NEG)
        m = jnp.max(s, axis=1, keepdims=True)
        m = jnp.where(m <= 0.1 * NEG, 0.0, m)
        p = jnp.exp2(s - m)
        rinv = 1.0 / jnp.maximum(jnp.sum(p, axis=1, keepdims=True), 1e-30)
        vc = jnp.concatenate([vc_ref[t] for t in range(ntile)], axis=0)
        oc_ref[...] = _dot(p.astype(BF16), vc) * rinv
        pn = p * rinv
        gt = NSA_G * dseq
        imp = pn[0:gt]
        for h in range(1, NSA_HPG):
            imp = imp + pn[h * gt:(h + 1) * gt]
        ratio = SEL_BLOCK // CMP_STRIDE
        span = ratio + CMP_HALF - 1
        ar = lax.broadcasted_iota(jnp.int32, (width, nblk_l), 0)
        ac = lax.broadcasted_iota(jnp.int32, (width, nblk_l), 1)
        band = jnp.where((ar >= ratio * ac) & (ar < ratio * ac + span), 1.0, 0.0).astype(BF16)
        i_hi = imp.astype(BF16)
        r1 = imp - i_hi.astype(F32)
        i_mid = r1.astype(BF16)
        i_lo = (r1 - i_mid.astype(F32)).astype(BF16)
        psl = _dot(i_hi, band) + _dot(i_mid, band) + _dot(i_lo, band)
        ji = lax.broadcasted_iota(jnp.int32, (gt, nblk_l), 1)
        tq2 = lax.broadcasted_iota(jnp.int32, (gt, nblk_l), 0) % dseq
        cur = (past + tq2) // SEL_BLOCK
        jf = ji.astype(F32)
        forced = (ji == 0) | (ji == cur) | (ji == cur - 1)
        score0 = jnp.where(forced, BIG, jnp.where(ji <= cur, psl, -BIG))
        score0 = jnp.where(ji < n_sel_blk, score0, -jnp.inf)

        def pick_one(_, carry):
            score, sel = carry
            mx = jnp.max(score, axis=1, keepdims=True)
            first = jnp.min(jnp.where(score == mx, jf, 1e9), axis=1, keepdims=True)
            pick = jf == first
            return jnp.where(pick, -jnp.inf, score), jnp.where(pick, 1.0, sel)

        _, sel = lax.fori_loop(0, n_sel, pick_one, (score0, jnp.zeros((gt, nblk_l), F32)))
        for t in range(sel_ref.shape[0]):
            piece = sel[:, t * SP_BLKS:(t + 1) * SP_BLKS]
            piece = jnp.concatenate([piece, jnp.zeros((gt, LANES - SP_BLKS), F32)], axis=1)
            sel_ref[t] = jnp.concatenate([piece] * NSA_HPG, axis=0).astype(BF16)
        m_ref[...] = jnp.full(m_ref.shape, NEG, F32)
        l_ref[...] = jnp.zeros(l_ref.shape, F32)
        acc_ref[...] = jnp.zeros(acc_ref.shape, F32)

    @pl.when(j >= ns)
    def _():
        j2 = j - ns
        s = _dot(qbd, feat_major(slc_pages, 0).astype(BF16))
        chosen = _dot(sel_ref[j2], e_ref[...])
        s = s + (chosen - 1.0) * (-NEG)
        s = s + jnp.where(j2 == ns - 1, sb_ref[...], 0.0)
        online_update(s, feat_major(slc_pages, 1).astype(BF16))

    @pl.when(j == 2 * ns - 1)
    def _():
        online_update(_dot(qbd, ksn_ref[0, 0].astype(BF16)) + snb_ref[...],
                      ksn_ref[0, 1].astype(BF16))
        o_s = acc_ref[...] * (1.0 / l_ref[...])
        kw = jnp.concatenate([feat_major([win_ref], 0), kwn_ref[0, 0]], axis=1)
        vw = jnp.concatenate([feat_major([win_ref], 1), kwn_ref[0, 1]], axis=1)
        s = _dot(qbd, kw.astype(BF16)) + wb_ref[...]
        p = jnp.exp2(s - jnp.max(s, axis=1, keepdims=True))
        o_w = _dot_nt(p.astype(BF16), vw.astype(BF16)) * (1.0 / jnp.sum(p, axis=1, keepdims=True))
        g = jax.nn.sigmoid(g_ref[0])
        o_ref[0] = g[:, 0:1] * oc_ref[...] + g[:, 1:2] * o_s + g[:, 2:3] * o_w


def _sample_attend(q, gates, kv_c, kv_s, kv_w, cache_cmp, cache_slc, win_buf, page_table, cw,
                   rel_bias):
    bsz, dseq = q.shape[:2]
    n_pages = page_table.shape[1]
    past = n_pages * PAGE_SIZE
    assert n_pages % SP_PP == 0 and dseq <= SEL_BLOCK and win_buf.shape[1] == WINDOW
    feat_major = lambda a: jnp.transpose(a, (0, 2, 3, 4, 1))
    cache_cmp, cache_slc, win_buf = feat_major(cache_cmp), feat_major(cache_slc), feat_major(win_buf)
    ns = n_pages // SP_PP
    rows = NSA_HEADS * dseq
    n_sel_blk = past // SEL_BLOCK + 1
    n_sel = min(N_SEL, n_sel_blk)
    nsel_tiles = _round_up(-(-n_sel_blk // SP_BLKS), LANES // SP_BLKS)
    w1big, pos, pw, w2big = cw
    cb, sb, snb, wb = _sample_bias(rel_bias, past, dseq, ns)

    qr = jnp.transpose((q * Q_SCALE).reshape(bsz, dseq, NSA_G, NSA_HPG, NSA_DK), (0, 3, 2, 1, 4))
    qbd = jnp.einsum('bhgtd,gk->bhgtkd', qr, jnp.eye(NSA_G, dtype=F32))
    qbd = qbd.reshape(bsz, rows, KV_COLS).astype(BF16)
    gr = gates[..., :N_BRANCH * NSA_HEADS].reshape(bsz, dseq, NSA_G, NSA_HPG, N_BRANCH)
    gr = jnp.transpose(gr, (0, 3, 2, 1, 4)).reshape(bsz, rows, N_BRANCH)
    gr = jnp.pad(gr, ((0, 0), (0, 0), (0, LANES - N_BRANCH)))
    pad_rows = lambda a, n: jnp.pad(a, ((0, 0), (0, n - a.shape[1]), (0, 0)))
    xn = pad_rows(kv_c, SEL_BLOCK).reshape(bsz, SEL_BLOCK // CMP_STRIDE, CMP_STRIDE * ROW_W)
    xn = pad_rows(xn, 8)
    new_feat_major = lambda a: jnp.transpose(
        pad_rows(a, LANES).reshape(bsz, LANES, 2, KV_COLS), (0, 2, 3, 1))
    ksn = new_feat_major(kv_s)
    kwn = new_feat_major(kv_w)
    expand = (jnp.arange(LANES)[:, None] == (jnp.arange(SP_KEYS) // SEL_BLOCK)[None, :]).astype(BF16)
    page_block = (1, 2, NSA_G, NSA_DK, PAGE_SIZE)

    def page_map(k, phase):
        if phase == 0:
            return lambda b, j, pt: (pt[b, jnp.minimum(j, ns - 1) * SP_PP + k], 0, 0, 0, 0)
        return lambda b, j, pt: (pt[b, jnp.maximum(j - ns, 0) * SP_PP + k], 0, 0, 0, 0)

    per_b = lambda a: pl.BlockSpec((1,) + a.shape[1:], lambda b, j, pt: (b,) + (0,) * (a.ndim - 1))
    const = lambda a: pl.BlockSpec(a.shape, lambda b, j, pt: (0,) * a.ndim)
    in_specs = ([pl.BlockSpec(page_block, page_map(k, 0)) for k in range(SP_PP)]
                + [pl.BlockSpec(page_block, page_map(k, 1)) for k in range(SP_PP)]
                + [per_b(xn), per_b(ksn), per_b(win_buf), per_b(kwn), per_b(qbd), per_b(gr),
                   pl.BlockSpec(w1big.shape, lambda b, j, pt: (0, 0, 0), pipeline_mode=pl.Buffered(1)),
                   const(pos), const(pw), const(w2big), const(expand),
                   const(cb), const(sb), const(snb), const(wb)])
    grid_spec = pltpu.PrefetchScalarGridSpec(
        num_scalar_prefetch=1, grid=(bsz, 2 * ns), in_specs=in_specs,
        out_specs=pl.BlockSpec((1, rows, KV_COLS), lambda b, j, pt: (b, 0, 0)),
        scratch_shapes=[pltpu.VMEM((ns + 1, rows, LANES), F32),
                        pltpu.VMEM((ns + 1, LANES, KV_COLS), BF16),
                        pltpu.VMEM((8, KV_COLS), F32),
                        pltpu.VMEM((nsel_tiles, rows, LANES), BF16),
                        pltpu.VMEM((rows, 1), F32),
                        pltpu.VMEM((rows, 1), F32),
                        pltpu.VMEM((rows, KV_COLS), F32),
                        pltpu.VMEM((rows, KV_COLS), F32),
                        pltpu.VMEM((ROW_W // LANES, SP_KEYS, LANES), F32)])
    out = pl.pallas_call(
        functools.partial(_sample_nsa_kernel, ns=ns, past=past, dseq=dseq, n_sel=n_sel,
                          n_sel_blk=n_sel_blk),
        grid_spec=grid_spec,
        out_shape=jax.ShapeDtypeStruct((bsz, rows, KV_COLS), F32),
        compiler_params=_cparams(("parallel", "arbitrary")),
        name="nsa_sample",
    )(page_table, *([cache_cmp] * SP_PP), *([cache_slc] * SP_PP), xn, ksn, win_buf, kwn, qbd, gr,
      w1big, pos, pw, w2big, expand, cb, sb, snb, wb)
    o = out.reshape(bsz, NSA_HPG, NSA_G, dseq, NSA_G, NSA_DK)
    o = jnp.einsum('bhgtge->btghe', o)
    return o.reshape(bsz * dseq, NSA_WIDTH)


def _out_proj_kernel(ycm_ref, ynsa_ref, h_ref, gt_ref, gpost_ref, w1_ref, w2_ref, o_ref):
    out = _dot(ycm_ref[...].astype(BF16), w1_ref[...]) + _dot(ynsa_ref[...].astype(BF16), w2_ref[...])
    o_ref[...] = h_ref[...] + gt_ref[...] * _rms(out, gpost_ref[...])


def _out_proj(ycm, ynsa, h, gate, g_post, w1, w2, tm):
    r, d = h.shape
    return pl.pallas_call(
        _out_proj_kernel,
        grid=(r // tm,),
        in_specs=[pl.BlockSpec((tm, CM_WIDTH), lambda i: (i, 0)),
                  pl.BlockSpec((tm, NSA_WIDTH), lambda i: (i, 0)),
                  pl.BlockSpec((tm, d), lambda i: (i, 0)),
                  _row_spec(gate, tm),
                  pl.BlockSpec((1, d), lambda i: (0, 0)),
                  pl.BlockSpec(w1.shape, lambda i: (0, 0)),
                  pl.BlockSpec(w2.shape, lambda i: (0, 0))],
        out_specs=pl.BlockSpec((tm, d), lambda i: (i, 0)),
        out_shape=jax.ShapeDtypeStruct((r, d), F32),
        compiler_params=_cparams(("parallel",)),
        name="out_proj",
    )(ycm, ynsa, h, gate, g_post, w1, w2)


def _row_tile(r, cap):
    tm = min(r, cap)
    while r % tm:
        tm //= 2
    return tm


def _layer(x, mod, per_row, weights, mixer):
    r, d = x.shape
    (norm_pre, norm_post, ffn, w_in, w_out1, w_out2) = weights
    if per_row is None:
        mrow = lambda i, j: mod[0, i, j].reshape(1, d)
    else:
        mrow = lambda i, j: jnp.repeat(mod[:, i, j], per_row, axis=0)
    tm = _row_tile(r, 512)
    tf = _row_tile(ffn[0][0].shape[1], 512)
    h = _ffn(x, mrow(0, 0), mrow(0, 1), mrow(0, 2), norm_pre[0:1], norm_post[0:1], *ffn[0],
             res_w=0.5, tm=tm, tf=tf)
    segs = _in_proj(h, mrow(1, 0), mrow(1, 1), norm_pre[1:2], w_in, _row_tile(r, 256))
    mixed_cm, mixed_nsa, state = mixer(*segs)
    h = _out_proj(mixed_cm, mixed_nsa, h, mrow(1, 2), norm_post[1:2], w_out1, w_out2, tm)
    h = _ffn(h, mrow(2, 0), mrow(2, 1), mrow(2, 2), norm_pre[2:3], norm_post[2:3], *ffn[1],
             res_w=0.5, tm=tm, tf=tf)
    return h, state


def kernel(x_prompt, x_sample, cache_cmp_kv, cache_slc_kv, state_win_kv, page_table, c_prompt,
           c_sample, w_mod, b_mod, norm_pre, norm_post, ffn_w_gate, ffn_w_up, ffn_w_down, w_in,
           w_out, cm_norm, cm_ws, cm_bs, phi_pos, phi_w1, phi_w2, rel_bias):
    depth = w_mod.shape[0]
    assert depth == 1
    bp, seq, d = x_prompt.shape
    bs, dseq, _ = x_sample.shape
    assert bp == 1 and seq % KT_SEL == 0 and seq >= WINDOW
    l = 0

    c_all = jnp.concatenate([c_prompt, c_sample], axis=0)
    mpad = _round_up(c_all.shape[0], 8)
    mod = _mod_proj(jnp.pad(c_all, ((0, mpad - c_all.shape[0]), (0, 0))), w_mod[l], b_mod[l])
    mod = mod[:bp + bs].reshape(bp + bs, 3, 3, d)

    ffn = [(ffn_w_gate[l, i].astype(BF16), ffn_w_up[l, i].astype(BF16), ffn_w_down[l, i].astype(BF16))
           for i in range(2)]
    w_in_p = jnp.pad(w_in[l], ((0, 0), (0, sum(IN_SEGS) - w_in.shape[2]))).astype(BF16)
    w_o = w_out[l].astype(BF16)
    weights = (norm_pre[l], norm_post[l], ffn, w_in_p, w_o[:CM_WIDTH], w_o[CM_WIDTH:])
    cw = _compress_weights(phi_pos[l], phi_w1[l], phi_w2[l])
    prompt_bias = _prompt_bias_tiles(rel_bias)

    def prompt_mixer(u, v, q, kv_c, kv_s, kv_w, gates):
        y_cm, _ = _chunk_mix(u, v, cm_norm[l], cm_ws[l], cm_bs[l], CHUNK)
        tok = _compress_tokens(kv_c[None], cw)[0]
        y_nsa = _prompt_attend(q, gates, tok, kv_s, kv_w, prompt_bias)
        return y_cm, y_nsa, (kv_c, kv_s, kv_w)

    hp, (pc, ps, pw) = _layer(x_prompt.reshape(seq, d), mod[:bp], None, weights, prompt_mixer)

    rows = _round_up(dseq, 16)

    def sample_mixer(u, v, q, kv_c, kv_s, kv_w, gates):
        per_seq = lambda a: a.reshape(bs, dseq, -1)
        pad_rows = lambda a, n: jnp.pad(per_seq(a), ((0, 0), (0, n - dseq), (0, 0)))
        y_cm, vn = _chunk_mix(pad_rows(u, rows).reshape(bs * rows, CM_WIDTH),
                              pad_rows(v, rows).reshape(bs * rows, CM_WIDTH),
                              cm_norm[l], cm_ws[l], cm_bs[l], rows)
        y_cm = y_cm.reshape(bs, rows, CM_WIDTH)[:, :dseq].reshape(bs * dseq, CM_WIDTH)
        vn = vn.reshape(bs, rows, CM_WIDTH)[:, :dseq]
        y_nsa = _sample_attend(per_seq(q), per_seq(gates), per_seq(kv_c), per_seq(kv_s), per_seq(kv_w),
                               cache_cmp_kv[l], cache_slc_kv[l], state_win_kv[l], page_table, cw,
                               rel_bias)
        return y_cm, y_nsa, (kv_c, kv_s, kv_w, vn)

    hs, (sc, ss, sw, sv) = _layer(x_sample.reshape(bs * dseq, d), mod[bp:], dseq, weights, sample_mixer)

    kvshape = lambda a, b, t: a.reshape(1, b, t, 2, NSA_G, NSA_DK)
    wp = min(WINDOW, seq)
    win_s = jnp.concatenate([state_win_kv[l].reshape(bs, -1, ROW_W), sw.reshape(bs, dseq, ROW_W)],
                            axis=1)[:, dseq:]
    return (hp.reshape(bp, seq, d), hs.reshape(bs, dseq, d),
            kvshape(pc, bp, seq), kvshape(sc, bs, dseq),
            kvshape(ps, bp, seq), kvshape(ss, bs, dseq),
            kvshape(pw[seq - wp:], bp, wp), kvshape(win_s, bs, win_s.shape[1]),
            sv.reshape(1, bs, dseq, CM_HEADS, CM_HEAD_DIM))
```

```python
import functools
import math

import jax
import jax.numpy as jnp
from jax import lax
from jax.experimental import pallas as pl
from jax.experimental.pallas import tpu as pltpu

F32 = jnp.float32
BF16 = jnp.bfloat16

CM_HEADS = 8
CM_HEAD_DIM = 128
CM_WIDTH = CM_HEADS * CM_HEAD_DIM
CHUNK = 128
NSA_HEADS = 16
NSA_G = 4
NSA_HPG = 4
NSA_DK = 64
NSA_WIDTH = NSA_HEADS * NSA_DK
KV_COLS = NSA_G * NSA_DK
ROW_W = 2 * KV_COLS
CMP_LEN = 32
CMP_STRIDE = 16
CMP_HALF = CMP_LEN // CMP_STRIDE
SEL_BLOCK = 64
N_SEL = 16
WINDOW = 512
N_BRANCH = 3
BIG = 1e4
REL_BUCKETS = 32
EPS = 1e-6
PAGE_SIZE = 128

LOG2E = 1.4426950408889634
Q_SCALE = NSA_DK ** -0.5 * LOG2E
NEG = -1e30
LANES = 128
QT = 128
HQ = NSA_HPG * QT
KT_SEL = 512
BLK_PER_KT = KT_SEL // SEL_BLOCK
SUB_PER_KT = KT_SEL // QT
N_NEAR = 8
NEAR_TILES = 3
V_ROWS = NSA_DK + 16
TN_WIN, TN_MASKED, TN_ZERO, TN_COUNT = 8, 9, 10, 11
WT_LEAD = 64
WT_ROWS = 200
REL_THRESHOLDS = (21, 27, 35, 46, 59, 77, 99, 128, 166, 216, 280, 363, 470, 609, 790)
VMEM_LIMIT = 56 * 1024 * 1024
SP_PP = CMP_STRIDE
SP_KEYS = SP_PP * PAGE_SIZE
SP_BLKS = SP_KEYS // SEL_BLOCK


def _cparams(sem):
    return pltpu.CompilerParams(dimension_semantics=sem, vmem_limit_bytes=VMEM_LIMIT)


def _dot(a, b):
    return jnp.dot(a, b, preferred_element_type=F32)


def _dot_nt(a, b):
    return lax.dot_general(a, b, (((1,), (1,)), ((), ())), preferred_element_type=F32)


def _split_bf16(x):
    hi = x.astype(BF16)
    lo = (x - hi.astype(F32)).astype(BF16)
    return hi, lo


def _dot_f32(a, b):
    a_hi, a_lo = _split_bf16(a)
    b_hi, b_lo = _split_bf16(b)
    return _dot(a_hi, b_hi) + _dot(a_lo, b_hi) + _dot(a_hi, b_lo)


def _rms(x, g):
    return x * lax.rsqrt(jnp.mean(x * x, axis=-1, keepdims=True) + EPS) * g


def _row_spec(arr, tm):
    d = arr.shape[-1]
    if arr.shape[0] == 1:
        return pl.BlockSpec((1, d), lambda *idx: (0, 0))
    return pl.BlockSpec((tm, d), lambda *idx: (idx[0], 0))


def _round_up(x, m):
    return -(-x // m) * m


def _mod_kernel(c_ref, w_ref, b_ref, o_ref):
    c = c_ref[...]
    s = c * jax.nn.sigmoid(c)
    o_ref[...] = _dot_f32(s, w_ref[...]) + b_ref[...]


def _mod_proj(c, w_mod, b_mod):
    m, d = c.shape
    n = w_mod.shape[1]
    tn = 512
    return pl.pallas_call(
        _mod_kernel,
        grid=(n // tn,),
        in_specs=[pl.BlockSpec((m, d), lambda j: (0, 0)),
                  pl.BlockSpec((d, tn), lambda j: (0, j)),
                  pl.BlockSpec((1, tn), lambda j: (0, j))],
        out_specs=pl.BlockSpec((m, tn), lambda j: (0, j)),
        out_shape=jax.ShapeDtypeStruct((m, n), F32),
        compiler_params=_cparams(("arbitrary",)),
        name="mod_proj",
    )(c, w_mod, b_mod.reshape(1, n))


def _ffn_kernel(x_ref, sh_ref, sc_ref, gt_ref, gpre_ref, gpost_ref, wg_ref, wu_ref, wd_ref,
                o_ref, a_ref, acc_ref, *, res_w, nf):
    f = pl.program_id(1)

    @pl.when(f == 0)
    def _():
        y = _rms(x_ref[...], gpre_ref[...])
        a_ref[...] = (y * (1.0 + sc_ref[...]) + sh_ref[...]).astype(BF16)
        acc_ref[...] = jnp.zeros_like(acc_ref)

    a = a_ref[...]
    h = _dot(a, wg_ref[...])
    u = _dot(a, wu_ref[...])
    act = (h * jax.nn.sigmoid(h) * u).astype(BF16)
    acc_ref[...] += _dot(act, wd_ref[...])

    @pl.when(f == nf - 1)
    def _():
        o_ref[...] = x_ref[...] + res_w * gt_ref[...] * _rms(acc_ref[...], gpost_ref[...])


def _ffn(x, shift, scale, gate, g_pre, g_post, wg, wu, wd, res_w, tm, tf):
    r, d = x.shape
    fdim = wg.shape[1]
    nf = fdim // tf
    return pl.pallas_call(
        functools.partial(_ffn_kernel, res_w=res_w, nf=nf),
        grid=(r // tm, nf),
        in_specs=[pl.BlockSpec((tm, d), lambda i, f: (i, 0)),
                  _row_spec(shift, tm), _row_spec(scale, tm), _row_spec(gate, tm),
                  pl.BlockSpec((1, d), lambda i, f: (0, 0)),
                  pl.BlockSpec((1, d), lambda i, f: (0, 0)),
                  pl.BlockSpec((d, tf), lambda i, f: (0, f)),
                  pl.BlockSpec((d, tf), lambda i, f: (0, f)),
                  pl.BlockSpec((tf, d), lambda i, f: (f, 0))],
        out_specs=pl.BlockSpec((tm, d), lambda i, f: (i, 0)),
        out_shape=jax.ShapeDtypeStruct((r, d), F32),
        scratch_shapes=[pltpu.VMEM((tm, d), BF16), pltpu.VMEM((tm, d), F32)],
        compiler_params=_cparams(("parallel", "arbitrary")),
        name="ffn",
    )(x, shift, scale, gate, g_pre, g_post, wg, wu, wd)


IN_SEGS = (CM_WIDTH, CM_WIDTH, NSA_WIDTH, 2 * KV_COLS, 2 * KV_COLS, 2 * KV_COLS, LANES)


def _in_proj_kernel(x_ref, sh_ref, sc_ref, gpre_ref, w_ref, *o_refs):
    y = _rms(x_ref[...], gpre_ref[...])
    a = (y * (1.0 + sc_ref[...]) + sh_ref[...]).astype(BF16)
    z = _dot(a, w_ref[...])
    off = 0
    for o_ref, width in zip(o_refs, IN_SEGS):
        o_ref[...] = z[:, off:off + width]
        off += width


def _in_proj(x, shift, scale, g_pre, w, tm):
    r, d = x.shape
    n = w.shape[1]
    return pl.pallas_call(
        _in_proj_kernel,
        grid=(r // tm,),
        in_specs=[pl.BlockSpec((tm, d), lambda i: (i, 0)),
                  _row_spec(shift, tm), _row_spec(scale, tm),
                  pl.BlockSpec((1, d), lambda i: (0, 0)),
                  pl.BlockSpec((d, n), lambda i: (0, 0), pipeline_mode=pl.Buffered(1))],
        out_specs=[pl.BlockSpec((tm, s), lambda i: (i, 0)) for s in IN_SEGS],
        out_shape=[jax.ShapeDtypeStruct((r, s), F32) for s in IN_SEGS],
        compiler_params=_cparams(("parallel",)),
        name="in_proj",
    )(x, shift, scale, g_pre, w)


def _chunk_mix_kernel(u_ref, v_ref, nrm_ref, ws_ref, bst_ref, y_ref, vn_ref, *, chunk):
    ri = lax.broadcasted_iota(jnp.int32, (chunk, chunk), 0)
    ci = lax.broadcasted_iota(jnp.int32, (chunk, chunk), 1)
    lower = ci <= ri
    for h in range(CM_HEADS):
        sl = slice(h * CM_HEAD_DIM, (h + 1) * CM_HEAD_DIM)
        vf = jax.nn.gelu(v_ref[:, sl])
        mu = jnp.mean(vf, axis=-1, keepdims=True)
        var = jnp.mean(jnp.square(vf - mu), axis=-1, keepdims=True)
        vn = (vf - mu) * lax.rsqrt(var + EPS) * nrm_ref[:, sl]
        vn_ref[:, sl] = vn
        w = jnp.where(lower, ws_ref[h], 0.0).astype(BF16)
        s = _dot(w, vn.astype(BF16)) + bst_ref[:, h:h + 1]
        y_ref[:, sl] = jax.nn.gelu(u_ref[:, sl]) * s


def _chunk_mix(u, v, cm_norm, cm_ws, cm_bs, chunk):
    r = u.shape[0]
    ws = cm_ws[:, :chunk, :chunk]
    bst = cm_bs[:, :chunk].T
    return pl.pallas_call(
        functools.partial(_chunk_mix_kernel, chunk=chunk),
        grid=(r // chunk,),
        in_specs=[pl.BlockSpec((chunk, CM_WIDTH), lambda i: (i, 0)),
                  pl.BlockSpec((chunk, CM_WIDTH), lambda i: (i, 0)),
                  pl.BlockSpec((1, CM_WIDTH), lambda i: (0, 0)),
                  pl.BlockSpec((CM_HEADS, chunk, chunk), lambda i: (0, 0, 0)),
                  pl.BlockSpec((chunk, CM_HEADS), lambda i: (0, 0))],
        out_specs=[pl.BlockSpec((chunk, CM_WIDTH), lambda i: (i, 0)),
                   pl.BlockSpec((chunk, CM_WIDTH), lambda i: (i, 0))],
        out_shape=[jax.ShapeDtypeStruct((r, CM_WIDTH), F32),
                   jax.ShapeDtypeStruct((r, CM_WIDTH), F32)],
        compiler_params=_cparams(("parallel",)),
        name="chunk_mix",
    )(u, v, cm_norm.reshape(1, CM_WIDTH), ws, bst)


def _gather_cv(x, c):
    return jnp.concatenate(
        [x[:, l * ROW_W + c * KV_COLS:l * ROW_W + (c + 1) * KV_COLS] for l in range(CMP_STRIDE)],
        axis=1)


def _pos_bias(pos_ref, pw_ref, c):
    posb = _dot_f32(pos_ref[c], pw_ref[c])[0:1]
    return jnp.concatenate([posb] * NSA_G, axis=1)


def _compress_weights(phi_pos, phi_w1, phi_w2):
    eye = jnp.eye(NSA_G, dtype=F32)
    w1 = phi_w1.reshape(CMP_HALF, CMP_STRIDE, 2, NSA_DK, NSA_DK)
    w1big = jnp.einsum('mlcde,gh->clgdmhe', w1, eye).reshape(
        2, CMP_STRIDE * KV_COLS, CMP_HALF * KV_COLS).astype(BF16)
    w2big = jnp.einsum('cde,gh->cgdhe', phi_w2, eye).reshape(2, KV_COLS, KV_COLS).astype(BF16)
    pos = jnp.transpose(phi_pos, (1, 0, 2)).reshape(2, 1, CMP_LEN * NSA_DK)
    pos = jnp.broadcast_to(pos, (2, 8, CMP_LEN * NSA_DK))
    pw = jnp.transpose(phi_w1, (1, 0, 2, 3)).reshape(2, CMP_LEN * NSA_DK, NSA_DK)
    return w1big, pos, pw, w2big


def _compress_kernel(x_ref, xn_ref, w1_ref, pos_ref, pw_ref, w2_ref, o_ref, *, rc):
    x = x_ref[0]
    xn = xn_ref[0]
    last_row = lax.broadcasted_iota(jnp.int32, (rc, KV_COLS), 0) == rc - 1
    for c in range(2):
        w1 = w1_ref[c]
        y = _dot(_gather_cv(x, c).astype(BF16), w1)
        yn = _dot(_gather_cv(xn, c).astype(BF16), w1)
        second = pltpu.roll(y[:, KV_COLS:], rc - 1, 0)
        second = jnp.where(last_row, yn[0:1, KV_COLS:], second)
        pre = y[:, :KV_COLS] + second + _pos_bias(pos_ref, pw_ref, c)
        o_ref[0, :, c * KV_COLS:(c + 1) * KV_COLS] = _dot(jax.nn.gelu(pre).astype(BF16), w2_ref[c])


def _pick_rows(n, cap):
    best = 8
    for rc in range(8, cap + 1, 8):
        if n % rc == 0:
            best = rc
    return best


def _compress(x, cw):
    bsz, n, _ = x.shape
    rc = _pick_rows(n, 256)
    w1big, pos, pw, w2big = cw
    nb8 = n // 8
    return pl.pallas_call(
        functools.partial(_compress_kernel, rc=rc),
        grid=(bsz, n // rc),
        in_specs=[pl.BlockSpec((1, rc, x.shape[2]), lambda b, i: (b, i, 0)),
                  pl.BlockSpec((1, 8, x.shape[2]),
                               lambda b, i: (b, jnp.minimum((i + 1) * (rc // 8), nb8 - 1), 0)),
                  pl.BlockSpec(w1big.shape, lambda b, i: (0, 0, 0)),
                  pl.BlockSpec(pos.shape, lambda b, i: (0, 0, 0)),
                  pl.BlockSpec(pw.shape, lambda b, i: (0, 0, 0)),
                  pl.BlockSpec(w2big.shape, lambda b, i: (0, 0, 0))],
        out_specs=pl.BlockSpec((1, rc, ROW_W), lambda b, i: (b, i, 0)),
        out_shape=jax.ShapeDtypeStruct((bsz, n, ROW_W), F32),
        compiler_params=_cparams(("parallel", "arbitrary")),
        name="compress",
    )(x, x, w1big, pos, pw, w2big)


def _compress_tokens(kv_c, cw):
    bsz, t = kv_c.shape[:2]
    n = t // CMP_STRIDE
    npad = _round_up(n, LANES if n >= LANES else 8)
    x = kv_c.reshape(bsz, n, CMP_STRIDE * ROW_W)
    x = jnp.pad(x, ((0, 0), (0, npad - n), (0, 0)))
    return _compress(x, cw)[:, :n - (CMP_HALF - 1)]


def _rel_bucket(d):
    n = jnp.maximum(d, 0)
    large = jnp.full(d.shape, REL_BUCKETS // 2, jnp.int32)
    for thr in REL_THRESHOLDS:
        large = large + (n >= thr).astype(jnp.int32)
    return jnp.where(n < REL_BUCKETS // 2, n, large)


def _rel_bias_lanes(d, tab_ref):
    bkt = _rel_bucket(d)
    f = jnp.zeros(d.shape, F32)
    for k in range(REL_BUCKETS):
        f = jnp.where(bkt == k, tab_ref[0, k:k + 1, :], f)
    return f


def _rel_bias_rows(d, tab_ref):
    bkt = _rel_bucket(d)
    f = jnp.zeros(d.shape, F32)
    for k in range(REL_BUCKETS):
        f = jnp.where(bkt == k, tab_ref[:, k:k + 1], f)
    return f


def _near_tile_kernel(tab_ref, o_ref):
    t = pl.program_id(1)
    delta = jnp.where(t < N_NEAR, t * QT, jnp.where(t == TN_WIN, WINDOW, 0))
    dmax = jnp.where(t == TN_WIN, WINDOW, 1 << 30)
    j = lax.broadcasted_iota(jnp.int32, (QT, HQ), 0)
    i = lax.broadcasted_iota(jnp.int32, (QT, HQ), 1) & (QT - 1)
    d = delta + i - j
    far = tab_ref[0, REL_BUCKETS - 1:REL_BUCKETS, :]
    val = jnp.where((d >= 0) & (d <= dmax), _rel_bias_lanes(d, tab_ref) - far, NEG)
    val = jnp.where(t == TN_MASKED, NEG, jnp.where(t == TN_ZERO, 0.0, val))
    o_ref[0, 0] = val


def _cmp_window_kernel(tab_ref, o_ref):
    e = lax.broadcasted_iota(jnp.int32, (WT_ROWS, HQ), 0) - WT_LEAD
    i = lax.broadcasted_iota(jnp.int32, (WT_ROWS, HQ), 1) & (QT - 1)
    d = i - (CMP_LEN - 1) - CMP_STRIDE * e
    far = tab_ref[0, REL_BUCKETS - 1:REL_BUCKETS, :]
    o_ref[0] = jnp.where(d >= 0, _rel_bias_lanes(d, tab_ref) - far, 0.0)


def _prompt_bias_tiles(rel_bias):
    tab = jnp.repeat((rel_bias * LOG2E).reshape(REL_BUCKETS, NSA_G, NSA_HPG), QT, axis=2)
    tab = jnp.transpose(tab, (1, 0, 2))
    near = pl.pallas_call(
        _near_tile_kernel,
        grid=(NSA_G, TN_COUNT),
        in_specs=[pl.BlockSpec((1, REL_BUCKETS, HQ), lambda g, t: (g, 0, 0))],
        out_specs=pl.BlockSpec((1, 1, QT, HQ), lambda g, t: (g, t, 0, 0)),
        out_shape=jax.ShapeDtypeStruct((NSA_G, TN_COUNT, QT, HQ), F32),
        compiler_params=_cparams(("parallel", "arbitrary")),
        name="bias_near_tiles",
    )(tab)
    cwin = pl.pallas_call(
        _cmp_window_kernel,
        grid=(NSA_G,),
        in_specs=[pl.BlockSpec((1, REL_BUCKETS, HQ), lambda g: (g, 0, 0))],
        out_specs=pl.BlockSpec((1, WT_ROWS, HQ), lambda g: (g, 0, 0)),
        out_shape=jax.ShapeDtypeStruct((NSA_G, WT_ROWS, HQ), F32),
        compiler_params=_cparams(("parallel",)),
        name="bias_cmp_window",
    )(tab)
    return near, cwin


GROUPS_PER_STEP = 2
N_WIN_TILES = WINDOW // QT + 1
NSA_SEGMENTS = 4


def _nsa_group(q_ref, kc_ref, vct_ref, ks_ref, vst_ref, kw_refs, vwt_refs, tn_ref, wt_ref,
               g_ref, o_ref, s_ref, imp_ref, mb_ref, m_ref, acc_ref,
               buf0_ref, buf1_ref, tmax_ref, *, ncp, nblk, n_sel, b_off):
    b = pl.program_id(1) + b_off
    qs = b * QT
    q = q_ref[0, 0]

    sc = _dot_nt(kc_ref[0], q)
    ci = lax.broadcasted_iota(jnp.int32, (ncp, HQ), 0)
    li = lax.broadcasted_iota(jnp.int32, (ncp, HQ), 1) & (QT - 1)
    vis = (CMP_STRIDE * ci + (CMP_LEN - 1) - li) <= qs
    s_ref[0:WT_LEAD, :] = jnp.full((WT_LEAD, HQ), NEG, F32)
    s_ref[WT_LEAD + ncp:, :] = jnp.full((s_ref.shape[0] - WT_LEAD - ncp, HQ), NEG, F32)
    s_ref[WT_LEAD:WT_LEAD + ncp, :] = jnp.where(vis, sc, NEG)
    r0 = pl.multiple_of(8 * b, 8)
    s_ref[pl.ds(r0, WT_ROWS), :] += wt_ref[0]
    s = s_ref[WT_LEAD:WT_LEAD + ncp, :]
    m = jnp.max(s, axis=0, keepdims=True)
    m = jnp.where(m <= 0.1 * NEG, 0.0, m)
    p = jnp.exp2(s - m)
    rinv = 1.0 / jnp.maximum(jnp.sum(p, axis=0, keepdims=True), 1e-30)
    o_c = _dot(vct_ref[0], p.astype(BF16)) * rinv
    pn = p * rinv
    imp = pn[:, 0:QT]
    for h in range(1, NSA_HPG):
        imp = imp + pn[:, h * QT:(h + 1) * QT]
    imp_ref[0:8, :] = jnp.zeros((8, QT), F32)
    imp_ref[8:8 + ncp, :] = imp
    if imp_ref.shape[0] > 8 + ncp:
        imp_ref[8 + ncp:, :] = jnp.zeros((imp_ref.shape[0] - 8 - ncp, QT), F32)
    ratio = SEL_BLOCK // CMP_STRIDE
    span = ratio + CMP_HALF - 1
    psl = imp_ref[pl.ds(8 - (CMP_HALF - 1), nblk, stride=ratio), :]
    for mm in range(1, span):
        psl = psl + imp_ref[pl.ds(8 - (CMP_HALF - 1) + mm, nblk, stride=ratio), :]

    n_win = WINDOW // QT
    st_parts, vt_parts = [], []
    for w in range(n_win + 1):
        kt = b - n_win + w
        tn_idx = jnp.where(kt >= 0, TN_WIN if w == 0 else n_win - w, TN_MASKED)
        st_parts.append(_dot_nt(kw_refs[w][0, 0], q) + tn_ref[0, tn_idx])
        vt_parts.append(vwt_refs[w][0, 0])
    st = jnp.concatenate(st_parts, axis=0)
    p = jnp.exp2(st - jnp.max(st, axis=0, keepdims=True))
    o_w = _dot(jnp.concatenate(vt_parts, axis=1), p.astype(BF16)) * (
        1.0 / jnp.sum(p, axis=0, keepdims=True))
    g = jax.nn.sigmoid(g_ref[0, 0])
    o_ref[0, 0] = g[0:1] * o_c + g[2:3] * o_w

    ji = lax.broadcasted_iota(jnp.int32, (nblk, QT), 0)
    ii = lax.broadcasted_iota(jnp.int32, (nblk, QT), 1)
    cur = (qs + ii) // SEL_BLOCK
    jf = ji.astype(F32)
    forced = (ji == 0) | (ji == cur) | (ji == cur - 1)
    score0 = jnp.where(forced, BIG, jnp.where(ji <= cur, psl, -BIG))

    def pick_one(_, carry):
        score, selb = carry
        mx = jnp.max(score, axis=0, keepdims=True)
        first = jnp.min(jnp.where(score == mx, jf, 1e9), axis=0, keepdims=True)
        pick = jf == first
        return jnp.where(pick, -jnp.inf, score), jnp.where(pick, 0.0, selb)

    _, selb = lax.fori_loop(0, n_sel, pick_one, (score0, jnp.full((nblk, QT), NEG, F32)),
                            unroll=True)
    mb_ref[0:nblk, :] = jnp.concatenate([selb] * NSA_HPG, axis=1)
    mb_ref[nblk:, :] = jnp.full((BLK_PER_KT, HQ), NEG, F32)

    m_ref[...] = jnp.full(m_ref.shape, NEG, F32)
    acc_ref[...] = jnp.zeros(acc_ref.shape, F32)

    def sel_logits(kt, valid):
        ktc = jnp.where(valid, kt, 0)
        st = _dot_nt(ks_ref[0, ktc], q)
        row0 = jnp.where(valid, ktc * BLK_PER_KT, nblk)
        mb = mb_ref[pl.ds(pl.multiple_of(row0, BLK_PER_KT), BLK_PER_KT), :]
        st = jnp.concatenate(
            [st[k * SEL_BLOCK:(k + 1) * SEL_BLOCK] + mb[k:k + 1] for k in range(BLK_PER_KT)], axis=0)
        return st, ktc

    def softmax_step(st, tmax, vt):
        m_old = m_ref[...]
        m_new = jnp.maximum(m_old, tmax)
        alpha = jnp.exp2(m_old - m_new)
        p = jnp.exp2(st - m_new)
        acc_ref[...] = alpha * acc_ref[...] + _dot(vt, p.astype(BF16))
        m_ref[...] = m_new

    nt = b // SUB_PER_KT + 1
    n_far = jnp.maximum(nt - NEAR_TILES, 0)

    def far_logits(kt, buf_ref, slot):
        st, _ = sel_logits(kt, kt < n_far)
        buf_ref[...] = st
        tmax_ref[slot:slot + 1, :] = jnp.max(st, axis=0, keepdims=True)

    def far_first():
        far_logits(0, buf0_ref, 0)

    def far_pair(pi):
        k0 = 2 * pi
        far_logits(k0 + 1, buf1_ref, 1)
        softmax_step(buf0_ref[...], tmax_ref[0:1, :], vst_ref[0, k0])
        far_logits(k0 + 2, buf0_ref, 0)
        k1 = jnp.minimum(k0 + 1, jnp.maximum(n_far - 1, 0))
        softmax_step(buf1_ref[...], tmax_ref[1:2, :], vst_ref[0, k1])

    yield far_first, far_pair

    st_parts, vt_parts = [], []
    for w in range(NEAR_TILES):
        kt = nt - NEAR_TILES + w
        st, ktc = sel_logits(kt, kt >= 0)
        r = b - SUB_PER_KT * ktc
        parts = []
        for u in range(SUB_PER_KT):
            ru = r - u
            idx = jnp.where(ru < 0, TN_MASKED, jnp.where(ru >= N_NEAR, TN_ZERO, ru))
            parts.append(tn_ref[0, idx])
        st_parts.append(st + jnp.concatenate(parts, axis=0))
        vt_parts.append(vst_ref[0, ktc])
    st = jnp.concatenate(st_parts, axis=0)
    softmax_step(st, jnp.max(st, axis=0, keepdims=True), jnp.concatenate(vt_parts, axis=1))
    acc = acc_ref[...]
    o_s = acc[0:NSA_DK] * (1.0 / acc[NSA_DK:NSA_DK + 1])
    o_ref[0, 0] += jax.nn.sigmoid(g_ref[0, 0, 1:2]) * o_s


def _nsa_kernel(*refs, n_scratch, **static):
    gp = GROUPS_PER_STEP
    io, scratch = refs[:len(refs) - n_scratch], refs[len(refs) - n_scratch:]
    b = pl.program_id(1) + static["b_off"]
    n_far = jnp.maximum(b // SUB_PER_KT + 1 - NEAR_TILES, 0)
    programs = []
    for gi in range(gp):
        v = [r.at[pl.ds(gi, 1)] for r in io]
        kw_refs, vwt_refs = v[5:5 + N_WIN_TILES], v[5 + N_WIN_TILES:5 + 2 * N_WIN_TILES]
        rest = v[5 + 2 * N_WIN_TILES:]
        programs.append(_nsa_group(*v[:5], kw_refs, vwt_refs, *rest, *[r.at[gi] for r in scratch],
                                   **static))
    loops = [next(p) for p in programs]

    @pl.when(n_far > 0)
    def _():
        for far_first, _ in loops:
            far_first()

    def far_body(pi, carry):
        for _, far_pair in loops:
            far_pair(pi)
        return carry

    lax.fori_loop(0, (n_far + 1) // 2, far_body, 0)
    for p in programs:
        for _ in p:
            pass


def _nsa_attend(qt, kc, vct, ks, vst, kw, vwt, tn, wt, gt, *, n_sel_blk):
    s_n, nq = qt.shape[:2]
    gp = GROUPS_PER_STEP
    assert kc.shape[1] >= 8 * nq and ks.shape[1] * SUB_PER_KT >= nq and s_n % gp == 0
    n_sel = min(N_SEL, n_sel_blk)
    n_seg = NSA_SEGMENTS if nq % (NSA_SEGMENTS * SUB_PER_KT) == 0 else 1
    nq_seg = nq // n_seg
    outs = []
    for seg in range(n_seg):
        b_off = seg * nq_seg
        b_end = b_off + nq_seg
        ncp = min(_round_up(8 * b_end, LANES), kc.shape[1])
        nkt = b_end // SUB_PER_KT
        nblk = nkt * BLK_PER_KT
        imp_rows = max(8 + ncp, 8 + (SEL_BLOCK // CMP_STRIDE) * nblk)
        scratch = [(ncp + WT_ROWS, HQ), (imp_rows, QT), (nblk + BLK_PER_KT, HQ), (1, HQ), (V_ROWS, HQ),
                   (KT_SEL, HQ), (KT_SEL, HQ), (8, HQ)]
        kern = functools.partial(_nsa_kernel, n_scratch=len(scratch), ncp=ncp, nblk=nblk, n_sel=n_sel,
                                 b_off=b_off)
        lead = lambda a, n: pl.BlockSpec((gp, n) + a.shape[2:], lambda s, i: (s,) + (0,) * (a.ndim - 1),
                                         pipeline_mode=pl.Buffered(1))
        per_tile = lambda a: pl.BlockSpec((gp, 1) + a.shape[2:], lambda s, i: (s, i + b_off, 0, 0))
        win = lambda a, w: pl.BlockSpec(
            (gp, 1) + a.shape[2:],
            lambda s, i: (s, jnp.maximum(i + b_off - (N_WIN_TILES - 1) + w, 0), 0, 0))
        outs.append(pl.pallas_call(
            kern,
            grid=(s_n // gp, nq_seg),
            in_specs=([per_tile(qt), lead(kc, ncp),
                       pl.BlockSpec((gp, NSA_DK, ncp), lambda s, i: (s, 0, 0), pipeline_mode=pl.Buffered(1)),
                       lead(ks, nkt), lead(vst, nkt)]
                      + [win(kw, w) for w in range(N_WIN_TILES)]
                      + [win(vwt, w) for w in range(N_WIN_TILES)]
                      + [lead(tn, tn.shape[1]), lead(wt, wt.shape[1]), per_tile(gt)]),
            out_specs=pl.BlockSpec((gp, 1, NSA_DK, HQ), lambda s, i: (s, i, 0, 0)),
            out_shape=jax.ShapeDtypeStruct((s_n, nq_seg, NSA_DK, HQ), F32),
            scratch_shapes=[pltpu.VMEM((gp,) + s, F32) for s in scratch],
            compiler_params=_cparams(("parallel", "arbitrary")),
            name="nsa_prompt",
        )(qt, kc, vct, ks, vst, *([kw] * N_WIN_TILES), *([vwt] * N_WIN_TILES), tn, wt, gt))
    return jnp.concatenate(outs, axis=1)


def _q_tiles(q, nq):
    x = (q * Q_SCALE).reshape(nq, QT, NSA_G, NSA_HPG, NSA_DK)
    x = jnp.transpose(x, (2, 0, 3, 1, 4)).reshape(NSA_G, nq, HQ, NSA_DK)
    return jnp.pad(x, ((0, 0), (0, 0), (0, 0), (0, LANES - NSA_DK))).astype(BF16)


def _gate_tiles(gates, nq):
    x = gates[..., :N_BRANCH * NSA_HEADS].reshape(nq, QT, NSA_G, NSA_HPG, N_BRANCH)
    x = jnp.transpose(x, (2, 0, 4, 3, 1)).reshape(NSA_G, nq, N_BRANCH, HQ)
    return jnp.pad(x, ((0, 0), (0, 0), (0, 8 - N_BRANCH), (0, 0)))


def _k_tiles(kv, tile):
    nt = kv.shape[0] // tile
    k = kv[..., :KV_COLS].reshape(nt, tile, NSA_G, NSA_DK)
    k = jnp.transpose(k, (2, 0, 1, 3))
    k = jnp.pad(k, ((0, 0), (0, 0), (0, 0), (0, LANES - NSA_DK))).astype(BF16)
    v = kv[..., KV_COLS:].reshape(nt, tile, NSA_G, NSA_DK)
    v = jnp.transpose(v, (2, 0, 3, 1)).astype(BF16)
    return k, v


def _prompt_attend(q, gates, tok, kv_s, kv_w, bias):
    t = q.shape[0]
    nq = t // QT
    near, cwin = bias
    ncp = max(_round_up(tok.shape[0] + 1, LANES), _round_up(8 * nq, LANES))
    kc, vct = _k_tiles(jnp.pad(tok, ((0, ncp - tok.shape[0]), (0, 0))), ncp)
    ks, vst = _k_tiles(kv_s, KT_SEL)
    ones = jnp.zeros(vst.shape[:2] + (V_ROWS - NSA_DK, KT_SEL), BF16).at[:, :, 0].set(1.0)
    vst = jnp.concatenate([vst, ones], axis=2)
    kw, vwt = _k_tiles(kv_w, QT)
    ot = _nsa_attend(_q_tiles(q, nq), kc[:, 0], vct[:, 0], ks, vst, kw, vwt, near, cwin,
                     _gate_tiles(gates, nq), n_sel_blk=t // SEL_BLOCK)
    x = ot.reshape(NSA_G, nq, NSA_DK, NSA_HPG, QT)
    return jnp.transpose(x, (1, 4, 0, 3, 2)).reshape(t, NSA_WIDTH)


def _sample_bias_kernel(tab_ref, cb_ref, sb_ref, snb_ref, wb_ref, *, past, dseq, ns):
    rows = tab_ref.shape[0]
    far = tab_ref[:, REL_BUCKETS - 1:REL_BUCKETS]

    def tok_of(n):
        return lax.broadcasted_iota(jnp.int32, (rows, n), 0) % dseq

    def lane(n):
        return lax.broadcasted_iota(jnp.int32, (rows, n), 1)

    for idx in range(2):
        rho = LANES * (ns - 1 + idx) + lane(LANES)
        d = past + tok_of(LANES) - (CMP_STRIDE * (rho - 1) + CMP_LEN - 1)
        cb_ref[idx] = jnp.where(d >= 0, _rel_bias_rows(d, tab_ref) - far, 0.0)
    d = SP_KEYS + tok_of(SP_KEYS) - lane(SP_KEYS)
    sb_ref[...] = _rel_bias_rows(d, tab_ref) - far
    d = tok_of(LANES) - lane(LANES)
    snb_ref[...] = jnp.where(d >= 0, _rel_bias_rows(d, tab_ref) - far, NEG)
    d = WINDOW + tok_of(WINDOW + LANES) - lane(WINDOW + LANES)
    wb_ref[...] = jnp.where((d >= 0) & (d <= WINDOW), _rel_bias_rows(d, tab_ref), NEG)


def _sample_bias(rel_bias, past, dseq, ns):
    rows = NSA_HEADS * dseq
    tab = (rel_bias * LOG2E).reshape(REL_BUCKETS, NSA_G, NSA_HPG)
    tab = jnp.transpose(tab, (2, 1, 0))
    tab = jnp.repeat(tab.reshape(NSA_HEADS, 1, REL_BUCKETS), dseq, axis=1).reshape(rows, REL_BUCKETS)
    tab = jnp.pad(tab, ((0, 0), (0, LANES - REL_BUCKETS)))
    shapes = [(2, rows, LANES), (rows, SP_KEYS), (rows, LANES), (rows, WINDOW + LANES)]
    cb, sb, snb, wb = pl.pallas_call(
        functools.partial(_sample_bias_kernel, past=past, dseq=dseq, ns=ns),
        out_shape=[jax.ShapeDtypeStruct(s, F32) for s in shapes],
        compiler_params=pltpu.CompilerParams(vmem_limit_bytes=VMEM_LIMIT),
        name="bias_sample",
    )(tab)
    return cb, sb, snb, wb


def _sample_nsa_kernel(pt_ref, *refs, ns, past, dseq, n_sel, n_sel_blk):
    cmp_pages = refs[:SP_PP]
    slc_pages = refs[SP_PP:2 * SP_PP]
    (xn_ref, ksn_ref, win_ref, kwn_ref, q_ref, g_ref, w1_ref, pos_ref, pw_ref, w2_ref, e_ref,
     cb_ref, sb_ref, snb_ref, wb_ref, o_ref,
     s_ref, vc_ref, carry_ref, sel_ref, m_ref, l_ref, acc_ref, oc_ref, x_ref) = refs[2 * SP_PP:]
    j = pl.program_id(1)
    qbd = q_ref[0]
    rows = qbd.shape[0]
    ntile = ns + 1
    nblk_l = sel_ref.shape[0] * SP_BLKS

    def feat_major(refs_, c):
        return jnp.concatenate([r[0, c].reshape(KV_COLS, r.shape[-1]) for r in refs_], axis=1)

    def tokens_of(gather, prepare=None):
        out = []
        for c in range(2):
            if prepare is not None:
                prepare(c)
            xc = gather(c)
            n = xc.shape[0]
            row0 = lax.broadcasted_iota(jnp.int32, (n, KV_COLS), 0) == 0
            y = _dot(xc.astype(BF16), w1_ref[c])
            first = jnp.where(row0, carry_ref[c:c + 1, :], pltpu.roll(y[:, :KV_COLS], 1, 0))
            carry_ref[c:c + 1, :] = y[n - 1:n, :KV_COLS]
            pre = first + y[:, KV_COLS:] + _pos_bias(pos_ref, pw_ref, c)
            out.append(_dot(jax.nn.gelu(pre).astype(BF16), w2_ref[c]))
        return out

    def online_update(s, v):
        m_old = m_ref[...]
        m_new = jnp.maximum(m_old, jnp.max(s, axis=1, keepdims=True))
        alpha = jnp.exp2(m_old - m_new)
        p = jnp.exp2(s - m_new)
        l_ref[...] = alpha * l_ref[...] + jnp.sum(p, axis=1, keepdims=True)
        acc_ref[...] = alpha * acc_ref[...] + _dot_nt(p.astype(BF16), v)
        m_ref[...] = m_new

    @pl.when(j == 0)
    def _():
        carry_ref[...] = jnp.zeros(carry_ref.shape, F32)

    @pl.when(j < ns)
    def _():
        def to_rows(c):
            for p, r in enumerate(cmp_pages):
                for gp in range(2):
                    tile = r[0, c, 2 * gp:2 * gp + 2].reshape(LANES, PAGE_SIZE)
                    x_ref[2 * c + gp, p * PAGE_SIZE:(p + 1) * PAGE_SIZE, :] = tile.T

        def gather(c):
            return jnp.concatenate(
                [x_ref[2 * c + gp, pl.ds(l, SP_KEYS // CMP_STRIDE, stride=CMP_STRIDE), :]
                 for l in range(CMP_STRIDE) for gp in range(2)], axis=1)

        tok_k, tok_v = tokens_of(gather, to_rows)
        s_ref[j] = _dot_nt(qbd, tok_k.astype(BF16))
        vc_ref[j] = tok_v.astype(BF16)

    @pl.when(j == ns - 1)
    def _():
        tok_k, tok_v = tokens_of(lambda c: _gather_cv(xn_ref[0], c))
        zpad = jnp.zeros((LANES - tok_k.shape[0], KV_COLS), F32)
        s_ref[ns] = _dot_nt(qbd, jnp.concatenate([tok_k, zpad], axis=0).astype(BF16))
        vc_ref[ns] = jnp.concatenate([tok_v, zpad], axis=0).astype(BF16)

        tiles = []
        for t in range(ntile):
            st = s_ref[t]
            if t >= ns - 1:
                st = st + cb_ref[t - (ns - 1)]
            tiles.append(st)
        s = jnp.concatenate(tiles, axis=1)
        width = ntile * LANES
        rho = lax.broadcasted_iota(jnp.int32, (rows, width), 1)
        tq = lax.broadcasted_iota(jnp.int32, (rows, width), 0) % dseq
        n_cmp = (past + _round_up(dseq, SEL_BLOCK)) // CMP_STRIDE - (CMP_HALF - 1)
        vis = (rho >= 1) & (rho <= n_cmp) & (CMP_STRIDE * (rho - 1) + CMP_LEN - 1 <= past + tq)
        s = jnp.where(vis, s, NEG)
        m = jnp.max(s, axis=1, keepdims=True)
        m = jnp.where(m <= 0.1 * NEG, 0.0, m)
        p = jnp.exp2(s - m)
        rinv = 1.0 / jnp.maximum(jnp.sum(p, axis=1, keepdims=True), 1e-30)
        vc = jnp.concatenate([vc_ref[t] for t in range(ntile)], axis=0)
        oc_ref[...] = _dot(p.astype(BF16), vc) * rinv
        pn = p * rinv
        gt = NSA_G * dseq
        imp = pn[0:gt]
        for h in range(1, NSA_HPG):
            imp = imp + pn[h * gt:(h + 1) * gt]
        ratio = SEL_BLOCK // CMP_STRIDE
        span = ratio + CMP_HALF - 1
        ar = lax.broadcasted_iota(jnp.int32, (width, nblk_l), 0)
        ac = lax.broadcasted_iota(jnp.int32, (width, nblk_l), 1)
        band = jnp.where((ar >= ratio * ac) & (ar < ratio * ac + span), 1.0, 0.0).astype(BF16)
        i_hi = imp.astype(BF16)
        r1 = imp - i_hi.astype(F32)
        i_mid = r1.astype(BF16)
        i_lo = (r1 - i_mid.astype(F32)).astype(BF16)
        psl = _dot(i_hi, band) + _dot(i_mid, band) + _dot(i_lo, band)
        ji = lax.broadcasted_iota(jnp.int32, (gt, nblk_l), 1)
        tq2 = lax.broadcasted_iota(jnp.int32, (gt, nblk_l), 0) % dseq
        cur = (past + tq2) // SEL_BLOCK
        jf = ji.astype(F32)
        forced = (ji == 0) | (ji == cur) | (ji == cur - 1)
        score0 = jnp.where(forced, BIG, jnp.where(ji <= cur, psl, -BIG))
        score0 = jnp.where(ji < n_sel_blk, score0, -jnp.inf)

        def pick_one(_, carry):
            score, sel = carry
            mx = jnp.max(score, axis=1, keepdims=True)
            first = jnp.min(jnp.where(score == mx, jf, 1e9), axis=1, keepdims=True)
            pick = jf == first
            return jnp.where(pick, -jnp.inf, score), jnp.where(pick, 1.0, sel)

        _, sel = lax.fori_loop(0, n_sel, pick_one, (score0, jnp.zeros((gt, nblk_l), F32)))
        for t in range(sel_ref.shape[0]):
            piece = sel[:, t * SP_BLKS:(t + 1) * SP_BLKS]
            piece = jnp.concatenate([piece, jnp.zeros((gt, LANES - SP_BLKS), F32)], axis=1)
            sel_ref[t] = jnp.concatenate([piece] * NSA_HPG, axis=0).astype(BF16)
        m_ref[...] = jnp.full(m_ref.shape, NEG, F32)
        l_ref[...] = jnp.zeros(l_ref.shape, F32)
        acc_ref[...] = jnp.zeros(acc_ref.shape, F32)

    @pl.when(j >= ns)
    def _():
        j2 = j - ns
        s = _dot(qbd, feat_major(slc_pages, 0).astype(BF16))
        chosen = _dot(sel_ref[j2], e_ref[...])
        s = s + (chosen - 1.0) * (-NEG)
        s = s + jnp.where(j2 == ns - 1, sb_ref[...], 0.0)
        online_update(s, feat_major(slc_pages, 1).astype(BF16))

    @pl.when(j == 2 * ns - 1)
    def _():
        online_update(_dot(qbd, ksn_ref[0, 0].astype(BF16)) + snb_ref[...],
                      ksn_ref[0, 1].astype(BF16))
        o_s = acc_ref[...] * (1.0 / l_ref[...])
        kw = jnp.concatenate([feat_major([win_ref], 0), kwn_ref[0, 0]], axis=1)
        vw = jnp.concatenate([feat_major([win_ref], 1), kwn_ref[0, 1]], axis=1)
        s = _dot(qbd, kw.astype(BF16)) + wb_ref[...]
        p = jnp.exp2(s - jnp.max(s, axis=1, keepdims=True))
        o_w = _dot_nt(p.astype(BF16), vw.astype(BF16)) * (1.0 / jnp.sum(p, axis=1, keepdims=True))
        g = jax.nn.sigmoid(g_ref[0])
        o_ref[0] = g[:, 0:1] * oc_ref[...] + g[:, 1:2] * o_s + g[:, 2:3] * o_w


def _sample_attend(q, gates, kv_c, kv_s, kv_w, cache_cmp, cache_slc, win_buf, page_table, cw,
                   rel_bias):
    bsz, dseq = q.shape[:2]
    n_pages = page_table.shape[1]
    past = n_pages * PAGE_SIZE
    assert n_pages % SP_PP == 0 and dseq <= SEL_BLOCK and win_buf.shape[1] == WINDOW
    feat_major = lambda a: jnp.transpose(a, (0, 2, 3, 4, 1))
    cache_cmp, cache_slc, win_buf = feat_major(cache_cmp), feat_major(cache_slc), feat_major(win_buf)
    ns = n_pages // SP_PP
    rows = NSA_HEADS * dseq
    n_sel_blk = past // SEL_BLOCK + 1
    n_sel = min(N_SEL, n_sel_blk)
    nsel_tiles = _round_up(-(-n_sel_blk // SP_BLKS), LANES // SP_BLKS)
    w1big, pos, pw, w2big = cw
    cb, sb, snb, wb = _sample_bias(rel_bias, past, dseq, ns)

    qr = jnp.transpose((q * Q_SCALE).reshape(bsz, dseq, NSA_G, NSA_HPG, NSA_DK), (0, 3, 2, 1, 4))
    qbd = jnp.einsum('bhgtd,gk->bhgtkd', qr, jnp.eye(NSA_G, dtype=F32))
    qbd = qbd.reshape(bsz, rows, KV_COLS).astype(BF16)
    gr = gates[..., :N_BRANCH * NSA_HEADS].reshape(bsz, dseq, NSA_G, NSA_HPG, N_BRANCH)
    gr = jnp.transpose(gr, (0, 3, 2, 1, 4)).reshape(bsz, rows, N_BRANCH)
    gr = jnp.pad(gr, ((0, 0), (0, 0), (0, LANES - N_BRANCH)))
    pad_rows = lambda a, n: jnp.pad(a, ((0, 0), (0, n - a.shape[1]), (0, 0)))
    xn = pad_rows(kv_c, SEL_BLOCK).reshape(bsz, SEL_BLOCK // CMP_STRIDE, CMP_STRIDE * ROW_W)
    xn = pad_rows(xn, 8)
    new_feat_major = lambda a: jnp.transpose(
        pad_rows(a, LANES).reshape(bsz, LANES, 2, KV_COLS), (0, 2, 3, 1))
    ksn = new_feat_major(kv_s)
    kwn = new_feat_major(kv_w)
    expand = (jnp.arange(LANES)[:, None] == (jnp.arange(SP_KEYS) // SEL_BLOCK)[None, :]).astype(BF16)
    page_block = (1, 2, NSA_G, NSA_DK, PAGE_SIZE)

    def page_map(k, phase):
        if phase == 0:
            return lambda b, j, pt: (pt[b, jnp.minimum(j, ns - 1) * SP_PP + k], 0, 0, 0, 0)
        return lambda b, j, pt: (pt[b, jnp.maximum(j - ns, 0) * SP_PP + k], 0, 0, 0, 0)

    per_b = lambda a: pl.BlockSpec((1,) + a.shape[1:], lambda b, j, pt: (b,) + (0,) * (a.ndim - 1))
    const = lambda a: pl.BlockSpec(a.shape, lambda b, j, pt: (0,) * a.ndim)
    in_specs = ([pl.BlockSpec(page_block, page_map(k, 0)) for k in range(SP_PP)]
                + [pl.BlockSpec(page_block, page_map(k, 1)) for k in range(SP_PP)]
                + [per_b(xn), per_b(ksn), per_b(win_buf), per_b(kwn), per_b(qbd), per_b(gr),
                   pl.BlockSpec(w1big.shape, lambda b, j, pt: (0, 0, 0), pipeline_mode=pl.Buffered(1)),
                   const(pos), const(pw), const(w2big), const(expand),
                   const(cb), const(sb), const(snb), const(wb)])
    grid_spec = pltpu.PrefetchScalarGridSpec(
        num_scalar_prefetch=1, grid=(bsz, 2 * ns), in_specs=in_specs,
        out_specs=pl.BlockSpec((1, rows, KV_COLS), lambda b, j, pt: (b, 0, 0)),
        scratch_shapes=[pltpu.VMEM((ns + 1, rows, LANES), F32),
                        pltpu.VMEM((ns + 1, LANES, KV_COLS), BF16),
                        pltpu.VMEM((8, KV_COLS), F32),
                        pltpu.VMEM((nsel_tiles, rows, LANES), BF16),
                        pltpu.VMEM((rows, 1), F32),
                        pltpu.VMEM((rows, 1), F32),
                        pltpu.VMEM((rows, KV_COLS), F32),
                        pltpu.VMEM((rows, KV_COLS), F32),
                        pltpu.VMEM((ROW_W // LANES, SP_KEYS, LANES), F32)])
    out = pl.pallas_call(
        functools.partial(_sample_nsa_kernel, ns=ns, past=past, dseq=dseq, n_sel=n_sel,
                          n_sel_blk=n_sel_blk),
        grid_spec=grid_spec,
        out_shape=jax.ShapeDtypeStruct((bsz, rows, KV_COLS), F32),
        compiler_params=_cparams(("parallel", "arbitrary")),
        name="nsa_sample",
    )(page_table, *([cache_cmp] * SP_PP), *([cache_slc] * SP_PP), xn, ksn, win_buf, kwn, qbd, gr,
      w1big, pos, pw, w2big, expand, cb, sb, snb, wb)
    o = out.reshape(bsz, NSA_HPG, NSA_G, dseq, NSA_G, NSA_DK)
    o = jnp.einsum('bhgtge->btghe', o)
    return o.reshape(bsz * dseq, NSA_WIDTH)


def _out_proj_kernel(ycm_ref, ynsa_ref, h_ref, gt_ref, gpost_ref, w1_ref, w2_ref, o_ref):
    out = _dot(ycm_ref[...].astype(BF16), w1_ref[...]) + _dot(ynsa_ref[...].astype(BF16), w2_ref[...])
    o_ref[...] = h_ref[...] + gt_ref[...] * _rms(out, gpost_ref[...])


def _out_proj(ycm, ynsa, h, gate, g_post, w1, w2, tm):
    r, d = h.shape
    return pl.pallas_call(
        _out_proj_kernel,
        grid=(r // tm,),
        in_specs=[pl.BlockSpec((tm, CM_WIDTH), lambda i: (i, 0)),
                  pl.BlockSpec((tm, NSA_WIDTH), lambda i: (i, 0)),
                  pl.BlockSpec((tm, d), lambda i: (i, 0)),
                  _row_spec(gate, tm),
                  pl.BlockSpec((1, d), lambda i: (0, 0)),
                  pl.BlockSpec(w1.shape, lambda i: (0, 0)),
                  pl.BlockSpec(w2.shape, lambda i: (0, 0))],
        out_specs=pl.BlockSpec((tm, d), lambda i: (i, 0)),
        out_shape=jax.ShapeDtypeStruct((r, d), F32),
        compiler_params=_cparams(("parallel",)),
        name="out_proj",
    )(ycm, ynsa, h, gate, g_post, w1, w2)


def _row_tile(r, cap):
    tm = min(r, cap)
    while r % tm:
        tm //= 2
    return tm


def _layer(x, mod, per_row, weights, mixer):
    r, d = x.shape
    (norm_pre, norm_post, ffn, w_in, w_out1, w_out2) = weights
    if per_row is None:
        mrow = lambda i, j: mod[0, i, j].reshape(1, d)
    else:
        mrow = lambda i, j: jnp.repeat(mod[:, i, j], per_row, axis=0)
    tm = _row_tile(r, 512)
    tf = _row_tile(ffn[0][0].shape[1], 512)
    h = _ffn(x, mrow(0, 0), mrow(0, 1), mrow(0, 2), norm_pre[0:1], norm_post[0:1], *ffn[0],
             res_w=0.5, tm=tm, tf=tf)
    segs = _in_proj(h, mrow(1, 0), mrow(1, 1), norm_pre[1:2], w_in, _row_tile(r, 256))
    mixed_cm, mixed_nsa, state = mixer(*segs)
    h = _out_proj(mixed_cm, mixed_nsa, h, mrow(1, 2), norm_post[1:2], w_out1, w_out2, tm)
    h = _ffn(h, mrow(2, 0), mrow(2, 1), mrow(2, 2), norm_pre[2:3], norm_post[2:3], *ffn[1],
             res_w=0.5, tm=tm, tf=tf)
    return h, state


def kernel(x_prompt, x_sample, cache_cmp_kv, cache_slc_kv, state_win_kv, page_table, c_prompt,
           c_sample, w_mod, b_mod, norm_pre, norm_post, ffn_w_gate, ffn_w_up, ffn_w_down, w_in,
           w_out, cm_norm, cm_ws, cm_bs, phi_pos, phi_w1, phi_w2, rel_bias):
    depth = w_mod.shape[0]
    assert depth == 1
    bp, seq, d = x_prompt.shape
    bs, dseq, _ = x_sample.shape
    assert bp == 1 and seq % KT_SEL == 0 and seq >= WINDOW
    l = 0

    c_all = jnp.concatenate([c_prompt, c_sample], axis=0)
    mpad = _round_up(c_all.shape[0], 8)
    mod = _mod_proj(jnp.pad(c_all, ((0, mpad - c_all.shape[0]), (0, 0))), w_mod[l], b_mod[l])
    mod = mod[:bp + bs].reshape(bp + bs, 3, 3, d)

    ffn = [(ffn_w_gate[l, i].astype(BF16), ffn_w_up[l, i].astype(BF16), ffn_w_down[l, i].astype(BF16))
           for i in range(2)]
    w_in_p = jnp.pad(w_in[l], ((0, 0), (0, sum(IN_SEGS) - w_in.shape[2]))).astype(BF16)
    w_o = w_out[l].astype(BF16)
    weights = (norm_pre[l], norm_post[l], ffn, w_in_p, w_o[:CM_WIDTH], w_o[CM_WIDTH:])
    cw = _compress_weights(phi_pos[l], phi_w1[l], phi_w2[l])
    prompt_bias = _prompt_bias_tiles(rel_bias)

    def prompt_mixer(u, v, q, kv_c, kv_s, kv_w, gates):
        y_cm, _ = _chunk_mix(u, v, cm_norm[l], cm_ws[l], cm_bs[l], CHUNK)
        tok = _compress_tokens(kv_c[None], cw)[0]
        y_nsa = _prompt_attend(q, gates, tok, kv_s, kv_w, prompt_bias)
        return y_cm, y_nsa, (kv_c, kv_s, kv_w)

    hp, (pc, ps, pw) = _layer(x_prompt.reshape(seq, d), mod[:bp], None, weights, prompt_mixer)

    rows = _round_up(dseq, 16)

    def sample_mixer(u, v, q, kv_c, kv_s, kv_w, gates):
        per_seq = lambda a: a.reshape(bs, dseq, -1)
        pad_rows = lambda a, n: jnp.pad(per_seq(a), ((0, 0), (0, n - dseq), (0, 0)))
        y_cm, vn = _chunk_mix(pad_rows(u, rows).reshape(bs * rows, CM_WIDTH),
                              pad_rows(v, rows).reshape(bs * rows, CM_WIDTH),
                              cm_norm[l], cm_ws[l], cm_bs[l], rows)
        y_cm = y_cm.reshape(bs, rows, CM_WIDTH)[:, :dseq].reshape(bs * dseq, CM_WIDTH)
        vn = vn.reshape(bs, rows, CM_WIDTH)[:, :dseq]
        y_nsa = _sample_attend(per_seq(q), per_seq(gates), per_seq(kv_c), per_seq(kv_s), per_seq(kv_w),
                               cache_cmp_kv[l], cache_slc_kv[l], state_win_kv[l], page_table, cw,
                               rel_bias)
        return y_cm, y_nsa, (kv_c, kv_s, kv_w, vn)

    hs, (sc, ss, sw, sv) = _layer(x_sample.reshape(bs * dseq, d), mod[bp:], dseq, weights, sample_mixer)

    kvshape = lambda a, b, t: a.reshape(1, b, t, 2, NSA_G, NSA_DK)
    wp = min(WINDOW, seq)
    win_s = jnp.concatenate([state_win_kv[l].reshape(bs, -1, ROW_W), sw.reshape(bs, dseq, ROW_W)],
                            axis=1)[:, dseq:]
    return (hp.reshape(bp, seq, d), hs.reshape(bs, dseq, d),
            kvshape(pc, bp, seq), kvshape(sc, bs, dseq),
            kvshape(ps, bp, seq), kvshape(ss, bs, dseq),
            kvshape(pw[seq - wp:], bp, wp), kvshape(win_s, bs, win_s.shape[1]),
            sv.reshape(1, bs, dseq, CM_HEADS, CM_HEAD_DIM))
```

```python
import functools
import math

import jax
import jax.numpy as jnp
from jax import lax
from jax.experimental import pallas as pl
from jax.experimental.pallas import tpu as pltpu

F32 = jnp.float32
BF16 = jnp.bfloat16

CM_HEADS = 8
CM_HEAD_DIM = 128
CM_WIDTH = CM_HEADS * CM_HEAD_DIM
CHUNK = 128
NSA_HEADS = 16
NSA_G = 4
NSA_HPG = 4
NSA_DK = 64
NSA_WIDTH = NSA_HEADS * NSA_DK
KV_COLS = NSA_G * NSA_DK
ROW_W = 2 * KV_COLS
CMP_LEN = 32
CMP_STRIDE = 16
CMP_HALF = CMP_LEN // CMP_STRIDE
SEL_BLOCK = 64
N_SEL = 16
WINDOW = 512
N_BRANCH = 3
BIG = 1e4
REL_BUCKETS = 32
EPS = 1e-6
PAGE_SIZE = 128

LOG2E = 1.4426950408889634
Q_SCALE = NSA_DK ** -0.5 * LOG2E
NEG = -1e30
LANES = 128
QT = 128
HQ = NSA_HPG * QT
KT_SEL = 512
BLK_PER_KT = KT_SEL // SEL_BLOCK
SUB_PER_KT = KT_SEL // QT
N_NEAR = 8
NEAR_TILES = 3
V_ROWS = NSA_DK + 16
TN_WIN, TN_MASKED, TN_ZERO, TN_COUNT = 8, 9, 10, 11
WT_LEAD = 64
WT_ROWS = 200
REL_THRESHOLDS = (21, 27, 35, 46, 59, 77, 99, 128, 166, 216, 280, 363, 470, 609, 790)
VMEM_LIMIT = 56 * 1024 * 1024
SP_PP = CMP_STRIDE
SP_KEYS = SP_PP * PAGE_SIZE
SP_BLKS = SP_KEYS // SEL_BLOCK


def _cparams(sem):
    return pltpu.CompilerParams(dimension_semantics=sem, vmem_limit_bytes=VMEM_LIMIT)


def _dot(a, b):
    return jnp.dot(a, b, preferred_element_type=F32)


def _dot_nt(a, b):
    return lax.dot_general(a, b, (((1,), (1,)), ((), ())), preferred_element_type=F32)


def _split_bf16(x):
    hi = x.astype(BF16)
    lo = (x - hi.astype(F32)).astype(BF16)
    return hi, lo


def _dot_f32(a, b):
    a_hi, a_lo = _split_bf16(a)
    b_hi, b_lo = _split_bf16(b)
    return _dot(a_hi, b_hi) + _dot(a_lo, b_hi) + _dot(a_hi, b_lo)


def _rms(x, g):
    return x * lax.rsqrt(jnp.mean(x * x, axis=-1, keepdims=True) + EPS) * g


def _row_spec(arr, tm):
    d = arr.shape[-1]
    if arr.shape[0] == 1:
        return pl.BlockSpec((1, d), lambda *idx: (0, 0))
    return pl.BlockSpec((tm, d), lambda *idx: (idx[0], 0))


def _round_up(x, m):
    return -(-x // m) * m


def _mod_kernel(c_ref, w_ref, b_ref, o_ref):
    c = c_ref[...]
    s = c * jax.nn.sigmoid(c)
    o_ref[...] = _dot_f32(s, w_ref[...]) + b_ref[...]


def _mod_proj(c, w_mod, b_mod):
    m, d = c.shape
    n = w_mod.shape[1]
    tn = 512
    return pl.pallas_call(
        _mod_kernel,
        grid=(n // tn,),
        in_specs=[pl.BlockSpec((m, d), lambda j: (0, 0)),
                  pl.BlockSpec((d, tn), lambda j: (0, j)),
                  pl.BlockSpec((1, tn), lambda j: (0, j))],
        out_specs=pl.BlockSpec((m, tn), lambda j: (0, j)),
        out_shape=jax.ShapeDtypeStruct((m, n), F32),
        compiler_params=_cparams(("arbitrary",)),
        name="mod_proj",
    )(c, w_mod, b_mod.reshape(1, n))


def _ffn_kernel(x_ref, sh_ref, sc_ref, gt_ref, gpre_ref, gpost_ref, wg_ref, wu_ref, wd_ref,
                o_ref, a_ref, acc_ref, *, res_w, nf):
    f = pl.program_id(1)

    @pl.when(f == 0)
    def _():
        y = _rms(x_ref[...], gpre_ref[...])
        a_ref[...] = (y * (1.0 + sc_ref[...]) + sh_ref[...]).astype(BF16)
        acc_ref[...] = jnp.zeros_like(acc_ref)

    a = a_ref[...]
    h = _dot(a, wg_ref[...])
    u = _dot(a, wu_ref[...])
    act = (h * jax.nn.sigmoid(h) * u).astype(BF16)
    acc_ref[...] += _dot(act, wd_ref[...])

    @pl.when(f == nf - 1)
    def _():
        o_ref[...] = x_ref[...] + res_w * gt_ref[...] * _rms(acc_ref[...], gpost_ref[...])


def _ffn(x, shift, scale, gate, g_pre, g_post, wg, wu, wd, res_w, tm, tf):
    r, d = x.shape
    fdim = wg.shape[1]
    nf = fdim // tf
    return pl.pallas_call(
        functools.partial(_ffn_kernel, res_w=res_w, nf=nf),
        grid=(r // tm, nf),
        in_specs=[pl.BlockSpec((tm, d), lambda i, f: (i, 0)),
                  _row_spec(shift, tm), _row_spec(scale, tm), _row_spec(gate, tm),
                  pl.BlockSpec((1, d), lambda i, f: (0, 0)),
                  pl.BlockSpec((1, d), lambda i, f: (0, 0)),
                  pl.BlockSpec((d, tf), lambda i, f: (0, f)),
                  pl.BlockSpec((d, tf), lambda i, f: (0, f)),
                  pl.BlockSpec((tf, d), lambda i, f: (f, 0))],
        out_specs=pl.BlockSpec((tm, d), lambda i, f: (i, 0)),
        out_shape=jax.ShapeDtypeStruct((r, d), F32),
        scratch_shapes=[pltpu.VMEM((tm, d), BF16), pltpu.VMEM((tm, d), F32)],
        compiler_params=_cparams(("parallel", "arbitrary")),
        name="ffn",
    )(x, shift, scale, gate, g_pre, g_post, wg, wu, wd)


IN_SEGS = (CM_WIDTH, CM_WIDTH, NSA_WIDTH, 2 * KV_COLS, 2 * KV_COLS, 2 * KV_COLS, LANES)


def _in_proj_kernel(x_ref, sh_ref, sc_ref, gpre_ref, w_ref, *o_refs):
    y = _rms(x_ref[...], gpre_ref[...])
    a = (y * (1.0 + sc_ref[...]) + sh_ref[...]).astype(BF16)
    z = _dot(a, w_ref[...])
    off = 0
    for o_ref, width in zip(o_refs, IN_SEGS):
        o_ref[...] = z[:, off:off + width]
        off += width


def _in_proj(x, shift, scale, g_pre, w, tm):
    r, d = x.shape
    n = w.shape[1]
    return pl.pallas_call(
        _in_proj_kernel,
        grid=(r // tm,),
        in_specs=[pl.BlockSpec((tm, d), lambda i: (i, 0)),
                  _row_spec(shift, tm), _row_spec(scale, tm),
                  pl.BlockSpec((1, d), lambda i: (0, 0)),
                  pl.BlockSpec((d, n), lambda i: (0, 0), pipeline_mode=pl.Buffered(1))],
        out_specs=[pl.BlockSpec((tm, s), lambda i: (i, 0)) for s in IN_SEGS],
        out_shape=[jax.ShapeDtypeStruct((r, s), F32) for s in IN_SEGS],
        compiler_params=_cparams(("parallel",)),
        name="in_proj",
    )(x, shift, scale, g_pre, w)


IN_P_TM = 2 * QT
IN_P_QW = NSA_HEADS * LANES
IN_P_KW = NSA_G * LANES
IN_P_ROWS_F = N_BRANCH * ROW_W
IN_P_GATE_ROWS = 64


def _in_proj_prompt_kernel(x_ref, sh_ref, sc_ref, gpre_ref, wr_ref, wf_ref, u_ref, v_ref, kvc_ref,
                           qt_ref, gt_ref, ks_ref, vst_ref, kw_ref, vwt_ref, kvt_ref):
    y = _rms(x_ref[...], gpre_ref[...])
    a = (y * (1.0 + sc_ref[...]) + sh_ref[...]).astype(BF16)
    z = _dot(a, wr_ref[...])
    zt = _dot_nt(wf_ref[...], a)
    nt = IN_P_TM // QT
    tile = lambda t: slice(t * QT, (t + 1) * QT)
    u_ref[...] = z[:, 0:CM_WIDTH]
    v_ref[...] = z[:, CM_WIDTH:2 * CM_WIDTH]
    off = 2 * CM_WIDTH
    for t in range(nt):
        for gh in range(NSA_HEADS):
            g, h = divmod(gh, NSA_HPG)
            qt_ref[g, t, h * QT:(h + 1) * QT, :] = z[tile(t), off + gh * LANES:off + (gh + 1) * LANES].astype(BF16)
    off += IN_P_QW
    kvc_ref[...] = z[:, off:off + ROW_W]
    off += ROW_W
    for g in range(NSA_G):
        ks_ref[g, 0] = z[:, off + g * LANES:off + (g + 1) * LANES].astype(BF16)
    off += IN_P_KW
    for t in range(nt):
        for g in range(NSA_G):
            kw_ref[g, t] = z[tile(t), off + g * LANES:off + (g + 1) * LANES].astype(BF16)
    kvt_ref[...] = zt[0:IN_P_ROWS_F]
    v_rows = lambda branch, g: slice(branch * ROW_W + KV_COLS + g * NSA_DK,
                                     branch * ROW_W + KV_COLS + (g + 1) * NSA_DK)
    for g in range(NSA_G):
        vst_ref[g, 0] = zt[v_rows(1, g), :].astype(BF16)
        for t in range(nt):
            vwt_ref[g, t] = zt[v_rows(2, g), tile(t)].astype(BF16)
            gt_ref[g, t, N_BRANCH:, :] = jnp.zeros((8 - N_BRANCH, HQ), F32)
            for br in range(N_BRANCH):
                for h in range(NSA_HPG):
                    row = IN_P_ROWS_F + (g * N_BRANCH + br) * NSA_HPG + h
                    gt_ref[g, t, br:br + 1, h * QT:(h + 1) * QT] = zt[row:row + 1, tile(t)]


def _in_proj_prompt_weights(w):
    d = w.shape[0]
    q0 = 2 * CM_WIDTH
    k0 = q0 + NSA_WIDTH
    pad_lanes = lambda x: jnp.pad(x, ((0, 0), (0, 0), (0, LANES - NSA_DK))).reshape(d, -1)
    wq = pad_lanes(w[:, q0:k0].reshape(d, NSA_HEADS, NSA_DK) * Q_SCALE)
    k_of = lambda branch: pad_lanes(
        w[:, k0 + branch * ROW_W:k0 + branch * ROW_W + KV_COLS].reshape(d, NSA_G, NSA_DK))
    w_rows = jnp.concatenate([w[:, :q0], wq, w[:, k0:k0 + ROW_W], k_of(1), k_of(2)], axis=1).astype(BF16)
    g0 = k0 + N_BRANCH * ROW_W
    wg = jnp.transpose(w[:, g0:g0 + N_BRANCH * NSA_HEADS].reshape(d, NSA_G, NSA_HPG, N_BRANCH),
                       (0, 1, 3, 2)).reshape(d, N_BRANCH * NSA_HEADS)
    wg = jnp.pad(wg, ((0, 0), (0, IN_P_GATE_ROWS - N_BRANCH * NSA_HEADS)))
    w_feat = jnp.concatenate([w[:, k0:g0], wg], axis=1).T.astype(BF16)
    return w_rows, w_feat


def _in_proj_prompt(x, shift, scale, g_pre, w_rows, w_feat):
    t, d = x.shape
    tm = IN_P_TM
    nq = t // QT
    per = KT_SEL // tm
    const = lambda a: pl.BlockSpec(a.shape, lambda i: (0, 0), pipeline_mode=pl.Buffered(1))
    rows = lambda n: pl.BlockSpec((tm, n), lambda i: (i, 0))
    qtile = lambda a: pl.BlockSpec((NSA_G, tm // QT) + a[2:], lambda i: (0, i, 0, 0))
    shapes = [((t, CM_WIDTH), F32), ((t, CM_WIDTH), F32), ((t, ROW_W), F32),
              ((NSA_G, nq, HQ, LANES), BF16), ((NSA_G, nq, 8, HQ), F32),
              ((NSA_G, t // KT_SEL, KT_SEL, LANES), BF16), ((NSA_G, t // KT_SEL, NSA_DK, KT_SEL), BF16),
              ((NSA_G, nq, QT, LANES), BF16), ((NSA_G, nq, NSA_DK, QT), BF16),
              ((IN_P_ROWS_F, t), F32)]
    out_specs = [rows(CM_WIDTH), rows(CM_WIDTH), rows(ROW_W),
                 qtile(shapes[3][0]), qtile(shapes[4][0]),
                 pl.BlockSpec((NSA_G, 1, tm, LANES), lambda i: (0, i // per, i % per, 0)),
                 pl.BlockSpec((NSA_G, 1, NSA_DK, tm), lambda i: (0, i // per, 0, i % per)),
                 qtile(shapes[7][0]), qtile(shapes[8][0]),
                 pl.BlockSpec((IN_P_ROWS_F, tm), lambda i: (0, i))]
    return pl.pallas_call(
        _in_proj_prompt_kernel,
        grid=(t // tm,),
        in_specs=[rows(d), _row_spec(shift, tm), _row_spec(scale, tm),
                  pl.BlockSpec((1, d), lambda i: (0, 0)), const(w_rows), const(w_feat)],
        out_specs=out_specs,
        out_shape=[jax.ShapeDtypeStruct(s, dt) for s, dt in shapes],
        compiler_params=_cparams(("parallel",)),
        name="in_proj_prompt",
    )(x, shift, scale, g_pre, w_rows, w_feat)


def _chunk_mix_kernel(u_ref, v_ref, nrm_ref, ws_ref, bst_ref, y_ref, vn_ref, *, chunk):
    ri = lax.broadcasted_iota(jnp.int32, (chunk, chunk), 0)
    ci = lax.broadcasted_iota(jnp.int32, (chunk, chunk), 1)
    lower = ci <= ri
    for h in range(CM_HEADS):
        sl = slice(h * CM_HEAD_DIM, (h + 1) * CM_HEAD_DIM)
        vf = jax.nn.gelu(v_ref[:, sl])
        mu = jnp.mean(vf, axis=-1, keepdims=True)
        var = jnp.mean(jnp.square(vf - mu), axis=-1, keepdims=True)
        vn = (vf - mu) * lax.rsqrt(var + EPS) * nrm_ref[:, sl]
        vn_ref[:, sl] = vn
        w = jnp.where(lower, ws_ref[h], 0.0).astype(BF16)
        s = _dot(w, vn.astype(BF16)) + bst_ref[:, h:h + 1]
        y_ref[:, sl] = jax.nn.gelu(u_ref[:, sl]) * s


def _chunk_mix(u, v, cm_norm, cm_ws, cm_bs, chunk):
    r = u.shape[0]
    ws = cm_ws[:, :chunk, :chunk]
    bst = cm_bs[:, :chunk].T
    return pl.pallas_call(
        functools.partial(_chunk_mix_kernel, chunk=chunk),
        grid=(r // chunk,),
        in_specs=[pl.BlockSpec((chunk, CM_WIDTH), lambda i: (i, 0)),
                  pl.BlockSpec((chunk, CM_WIDTH), lambda i: (i, 0)),
                  pl.BlockSpec((1, CM_WIDTH), lambda i: (0, 0)),
                  pl.BlockSpec((CM_HEADS, chunk, chunk), lambda i: (0, 0, 0)),
                  pl.BlockSpec((chunk, CM_HEADS), lambda i: (0, 0))],
        out_specs=[pl.BlockSpec((chunk, CM_WIDTH), lambda i: (i, 0)),
                   pl.BlockSpec((chunk, CM_WIDTH), lambda i: (i, 0))],
        out_shape=[jax.ShapeDtypeStruct((r, CM_WIDTH), F32),
                   jax.ShapeDtypeStruct((r, CM_WIDTH), F32)],
        compiler_params=_cparams(("parallel",)),
        name="chunk_mix",
    )(u, v, cm_norm.reshape(1, CM_WIDTH), ws, bst)


def _gather_cv(x, c):
    return jnp.concatenate(
        [x[:, l * ROW_W + c * KV_COLS:l * ROW_W + (c + 1) * KV_COLS] for l in range(CMP_STRIDE)],
        axis=1)


def _pos_bias(pos_ref, pw_ref, c):
    posb = _dot_f32(pos_ref[c], pw_ref[c])[0:1]
    return jnp.concatenate([posb] * NSA_G, axis=1)


def _compress_weights(phi_pos, phi_w1, phi_w2):
    eye = jnp.eye(NSA_G, dtype=F32)
    w1 = phi_w1.reshape(CMP_HALF, CMP_STRIDE, 2, NSA_DK, NSA_DK)
    w1big = jnp.einsum('mlcde,gh->clgdmhe', w1, eye).reshape(
        2, CMP_STRIDE * KV_COLS, CMP_HALF * KV_COLS).astype(BF16)
    w2big = jnp.einsum('cde,gh->cgdhe', phi_w2, eye).reshape(2, KV_COLS, KV_COLS).astype(BF16)
    pos = jnp.transpose(phi_pos, (1, 0, 2)).reshape(2, 1, CMP_LEN * NSA_DK)
    pos = jnp.broadcast_to(pos, (2, 8, CMP_LEN * NSA_DK))
    pw = jnp.transpose(phi_w1, (1, 0, 2, 3)).reshape(2, CMP_LEN * NSA_DK, NSA_DK)
    return w1big, pos, pw, w2big


def _compress_kernel(x_ref, xn_ref, w1_ref, pos_ref, pw_ref, w2_ref, o_ref, *, rc):
    x = x_ref[0]
    xn = xn_ref[0]
    last_row = lax.broadcasted_iota(jnp.int32, (rc, KV_COLS), 0) == rc - 1
    for c in range(2):
        w1 = w1_ref[c]
        y = _dot(_gather_cv(x, c).astype(BF16), w1)
        yn = _dot(_gather_cv(xn, c).astype(BF16), w1)
        second = pltpu.roll(y[:, KV_COLS:], rc - 1, 0)
        second = jnp.where(last_row, yn[0:1, KV_COLS:], second)
        pre = y[:, :KV_COLS] + second + _pos_bias(pos_ref, pw_ref, c)
        o_ref[0, :, c * KV_COLS:(c + 1) * KV_COLS] = _dot(jax.nn.gelu(pre).astype(BF16), w2_ref[c])


def _pick_rows(n, cap):
    best = 8
    for rc in range(8, cap + 1, 8):
        if n % rc == 0:
            best = rc
    return best


def _compress(x, cw):
    bsz, n, _ = x.shape
    rc = _pick_rows(n, 256)
    w1big, pos, pw, w2big = cw
    nb8 = n // 8
    return pl.pallas_call(
        functools.partial(_compress_kernel, rc=rc),
        grid=(bsz, n // rc),
        in_specs=[pl.BlockSpec((1, rc, x.shape[2]), lambda b, i: (b, i, 0)),
                  pl.BlockSpec((1, 8, x.shape[2]),
                               lambda b, i: (b, jnp.minimum((i + 1) * (rc // 8), nb8 - 1), 0)),
                  pl.BlockSpec(w1big.shape, lambda b, i: (0, 0, 0)),
                  pl.BlockSpec(pos.shape, lambda b, i: (0, 0, 0)),
                  pl.BlockSpec(pw.shape, lambda b, i: (0, 0, 0)),
                  pl.BlockSpec(w2big.shape, lambda b, i: (0, 0, 0))],
        out_specs=pl.BlockSpec((1, rc, ROW_W), lambda b, i: (b, i, 0)),
        out_shape=jax.ShapeDtypeStruct((bsz, n, ROW_W), F32),
        compiler_params=_cparams(("parallel", "arbitrary")),
        name="compress",
    )(x, x, w1big, pos, pw, w2big)


def _compress_tokens(kv_c, cw):
    bsz, t = kv_c.shape[:2]
    n = t // CMP_STRIDE
    npad = _round_up(n, LANES if n >= LANES else 8)
    x = kv_c.reshape(bsz, n, CMP_STRIDE * ROW_W)
    x = jnp.pad(x, ((0, 0), (0, npad - n), (0, 0)))
    return _compress(x, cw)[:, :n - (CMP_HALF - 1)]


def _rel_bucket(d):
    n = jnp.maximum(d, 0)
    large = jnp.full(d.shape, REL_BUCKETS // 2, jnp.int32)
    for thr in REL_THRESHOLDS:
        large = large + (n >= thr).astype(jnp.int32)
    return jnp.where(n < REL_BUCKETS // 2, n, large)


def _rel_bias_lanes(d, tab_ref):
    bkt = _rel_bucket(d)
    f = jnp.zeros(d.shape, F32)
    for k in range(REL_BUCKETS):
        f = jnp.where(bkt == k, tab_ref[0, k:k + 1, :], f)
    return f


def _rel_bias_rows(d, tab_ref):
    bkt = _rel_bucket(d)
    f = jnp.zeros(d.shape, F32)
    for k in range(REL_BUCKETS):
        f = jnp.where(bkt == k, tab_ref[:, k:k + 1], f)
    return f


def _near_tile_kernel(tab_ref, o_ref):
    t = pl.program_id(1)
    delta = jnp.where(t < N_NEAR, t * QT, jnp.where(t == TN_WIN, WINDOW, 0))
    dmax = jnp.where(t == TN_WIN, WINDOW, 1 << 30)
    j = lax.broadcasted_iota(jnp.int32, (QT, HQ), 0)
    i = lax.broadcasted_iota(jnp.int32, (QT, HQ), 1) & (QT - 1)
    d = delta + i - j
    far = tab_ref[0, REL_BUCKETS - 1:REL_BUCKETS, :]
    val = jnp.where((d >= 0) & (d <= dmax), _rel_bias_lanes(d, tab_ref) - far, NEG)
    val = jnp.where(t == TN_MASKED, NEG, jnp.where(t == TN_ZERO, 0.0, val))
    o_ref[0, 0] = val


def _cmp_window_kernel(tab_ref, o_ref):
    e = lax.broadcasted_iota(jnp.int32, (WT_ROWS, HQ), 0) - WT_LEAD
    i = lax.broadcasted_iota(jnp.int32, (WT_ROWS, HQ), 1) & (QT - 1)
    d = i - (CMP_LEN - 1) - CMP_STRIDE * e
    far = tab_ref[0, REL_BUCKETS - 1:REL_BUCKETS, :]
    o_ref[0] = jnp.where(d >= 0, _rel_bias_lanes(d, tab_ref) - far, 0.0)


def _prompt_bias_tiles(rel_bias):
    tab = jnp.repeat((rel_bias * LOG2E).reshape(REL_BUCKETS, NSA_G, NSA_HPG), QT, axis=2)
    tab = jnp.transpose(tab, (1, 0, 2))
    near = pl.pallas_call(
        _near_tile_kernel,
        grid=(NSA_G, TN_COUNT),
        in_specs=[pl.BlockSpec((1, REL_BUCKETS, HQ), lambda g, t: (g, 0, 0))],
        out_specs=pl.BlockSpec((1, 1, QT, HQ), lambda g, t: (g, t, 0, 0)),
        out_shape=jax.ShapeDtypeStruct((NSA_G, TN_COUNT, QT, HQ), F32),
        compiler_params=_cparams(("parallel", "arbitrary")),
        name="bias_near_tiles",
    )(tab)
    cwin = pl.pallas_call(
        _cmp_window_kernel,
        grid=(NSA_G,),
        in_specs=[pl.BlockSpec((1, REL_BUCKETS, HQ), lambda g: (g, 0, 0))],
        out_specs=pl.BlockSpec((1, WT_ROWS, HQ), lambda g: (g, 0, 0)),
        out_shape=jax.ShapeDtypeStruct((NSA_G, WT_ROWS, HQ), F32),
        compiler_params=_cparams(("parallel",)),
        name="bias_cmp_window",
    )(tab)
    return near, cwin


GROUPS_PER_STEP = 2
N_WIN_TILES = WINDOW // QT + 1
NSA_SEGMENTS = 4


def _nsa_group(q_ref, kc_ref, vct_ref, ks_ref, vst_ref, kw_refs, vwt_refs, tn_ref, wt_ref,
               g_ref, o_ref, s_ref, imp_ref, mb_ref, m_ref, acc_ref,
               buf0_ref, buf1_ref, tmax_ref, part_ref, *, ncp, nblk, n_sel, b_off):
    b = pl.program_id(1) + b_off
    qs = b * QT
    q = q_ref[0, 0]

    sc = _dot_nt(kc_ref[0], q)
    ci = lax.broadcasted_iota(jnp.int32, (ncp, HQ), 0)
    li = lax.broadcasted_iota(jnp.int32, (ncp, HQ), 1) & (QT - 1)
    vis = (CMP_STRIDE * ci + (CMP_LEN - 1) - li) <= qs
    s_ref[0:WT_LEAD, :] = jnp.full((WT_LEAD, HQ), NEG, F32)
    s_ref[WT_LEAD + ncp:, :] = jnp.full((s_ref.shape[0] - WT_LEAD - ncp, HQ), NEG, F32)
    s_ref[WT_LEAD:WT_LEAD + ncp, :] = jnp.where(vis, sc, NEG)
    r0 = pl.multiple_of(8 * b, 8)
    s_ref[pl.ds(r0, WT_ROWS), :] += wt_ref[0]
    s = s_ref[WT_LEAD:WT_LEAD + ncp, :]
    m = jnp.max(s, axis=0, keepdims=True)
    m = jnp.where(m <= 0.1 * NEG, 0.0, m)
    p = jnp.exp2(s - m)
    rinv = 1.0 / jnp.maximum(jnp.sum(p, axis=0, keepdims=True), 1e-30)
    o_c = _dot(vct_ref[0], p.astype(BF16)) * rinv
    pn = p * rinv
    imp = pn[:, 0:QT]
    for h in range(1, NSA_HPG):
        imp = imp + pn[:, h * QT:(h + 1) * QT]
    imp_ref[0:8, :] = jnp.zeros((8, QT), F32)
    imp_ref[8:8 + ncp, :] = imp
    if imp_ref.shape[0] > 8 + ncp:
        imp_ref[8 + ncp:, :] = jnp.zeros((imp_ref.shape[0] - 8 - ncp, QT), F32)
    ratio = SEL_BLOCK // CMP_STRIDE
    span = ratio + CMP_HALF - 1
    psl = imp_ref[pl.ds(8 - (CMP_HALF - 1), nblk, stride=ratio), :]
    for mm in range(1, span):
        psl = psl + imp_ref[pl.ds(8 - (CMP_HALF - 1) + mm, nblk, stride=ratio), :]

    n_win = WINDOW // QT
    st_parts, vt_parts = [], []
    for w in range(n_win + 1):
        kt = b - n_win + w
        tn_idx = jnp.where(kt >= 0, TN_WIN if w == 0 else n_win - w, TN_MASKED)
        st_parts.append(_dot_nt(kw_refs[w][0, 0], q) + tn_ref[0, tn_idx])
        vt_parts.append(vwt_refs[w][0, 0])
    st = jnp.concatenate(st_parts, axis=0)
    p = jnp.exp2(st - jnp.max(st, axis=0, keepdims=True))
    o_w = _dot(jnp.concatenate(vt_parts, axis=1), p.astype(BF16)) * (
        1.0 / jnp.sum(p, axis=0, keepdims=True))
    g = jax.nn.sigmoid(g_ref[0, 0])
    part_ref[...] = g[0:1] * o_c + g[2:3] * o_w

    ji = lax.broadcasted_iota(jnp.int32, (nblk, QT), 0)
    ii = lax.broadcasted_iota(jnp.int32, (nblk, QT), 1)
    cur = (qs + ii) // SEL_BLOCK
    jf = ji.astype(F32)
    forced = (ji == 0) | (ji == cur) | (ji == cur - 1)
    score0 = jnp.where(forced, BIG, jnp.where(ji <= cur, psl, -BIG))

    def pick_one(_, carry):
        score, selb = carry
        mx = jnp.max(score, axis=0, keepdims=True)
        first = jnp.min(jnp.where(score == mx, jf, 1e9), axis=0, keepdims=True)
        pick = jf == first
        return jnp.where(pick, -jnp.inf, score), jnp.where(pick, 0.0, selb)

    _, selb = lax.fori_loop(0, n_sel, pick_one, (score0, jnp.full((nblk, QT), NEG, F32)),
                            unroll=True)
    mb_ref[0:nblk, :] = jnp.concatenate([selb] * NSA_HPG, axis=1)
    mb_ref[nblk:, :] = jnp.full((BLK_PER_KT, HQ), NEG, F32)

    m_ref[...] = jnp.full(m_ref.shape, NEG, F32)
    acc_ref[...] = jnp.zeros(acc_ref.shape, F32)

    def sel_logits(kt, valid):
        ktc = jnp.where(valid, kt, 0)
        st = _dot_nt(ks_ref[0, ktc], q)
        row0 = jnp.where(valid, ktc * BLK_PER_KT, nblk)
        mb = mb_ref[pl.ds(pl.multiple_of(row0, BLK_PER_KT), BLK_PER_KT), :]
        st = jnp.concatenate(
            [st[k * SEL_BLOCK:(k + 1) * SEL_BLOCK] + mb[k:k + 1] for k in range(BLK_PER_KT)], axis=0)
        return st, ktc

    def softmax_step(st, tmax, vt):
        nk = vt.shape[1]
        ones = jnp.where(lax.broadcasted_iota(jnp.int32, (V_ROWS - NSA_DK, nk), 0) == 0, 1.0, 0.0)
        vt = jnp.concatenate([vt, ones.astype(BF16)], axis=0)
        m_old = m_ref[...]
        m_new = jnp.maximum(m_old, tmax)
        alpha = jnp.exp2(m_old - m_new)
        p = jnp.exp2(st - m_new)
        acc_ref[...] = alpha * acc_ref[...] + _dot(vt, p.astype(BF16))
        m_ref[...] = m_new

    nt = b // SUB_PER_KT + 1
    n_far = jnp.maximum(nt - NEAR_TILES, 0)

    def far_logits(kt, buf_ref, slot):
        st, _ = sel_logits(kt, kt < n_far)
        buf_ref[...] = st
        tmax_ref[slot:slot + 1, :] = jnp.max(st, axis=0, keepdims=True)

    def far_first():
        far_logits(0, buf0_ref, 0)

    def far_pair(pi):
        k0 = 2 * pi
        far_logits(k0 + 1, buf1_ref, 1)
        softmax_step(buf0_ref[...], tmax_ref[0:1, :], vst_ref[0, k0])
        far_logits(k0 + 2, buf0_ref, 0)
        k1 = jnp.minimum(k0 + 1, jnp.maximum(n_far - 1, 0))
        softmax_step(buf1_ref[...], tmax_ref[1:2, :], vst_ref[0, k1])

    yield far_first, far_pair

    st_parts, vt_parts = [], []
    for w in range(NEAR_TILES):
        kt = nt - NEAR_TILES + w
        st, ktc = sel_logits(kt, kt >= 0)
        r = b - SUB_PER_KT * ktc
        parts = []
        for u in range(SUB_PER_KT):
            ru = r - u
            idx = jnp.where(ru < 0, TN_MASKED, jnp.where(ru >= N_NEAR, TN_ZERO, ru))
            parts.append(tn_ref[0, idx])
        st_parts.append(st + jnp.concatenate(parts, axis=0))
        vt_parts.append(vst_ref[0, ktc])
    st = jnp.concatenate(st_parts, axis=0)
    softmax_step(st, jnp.max(st, axis=0, keepdims=True), jnp.concatenate(vt_parts, axis=1))
    acc = acc_ref[...]
    o_s = acc[0:NSA_DK] * (1.0 / acc[NSA_DK:NSA_DK + 1])
    total = part_ref[...] + jax.nn.sigmoid(g_ref[0, 0, 1:2]) * o_s
    pairs = [jnp.concatenate([total[:, h * QT:(h + 1) * QT], total[:, (h + 1) * QT:(h + 2) * QT]], axis=0).T
             for h in range(0, NSA_HPG, 2)]
    o_ref[...] = jnp.concatenate(pairs, axis=1)


GROUP_COLS = NSA_HPG * NSA_DK


def _nsa_kernel(*refs, n_scratch, aliased, **static):
    gp = GROUPS_PER_STEP
    refs = refs[1:] if aliased else refs
    ins, o_ref = refs[:len(refs) - n_scratch - 1], refs[len(refs) - n_scratch - 1]
    scratch = refs[len(refs) - n_scratch:]
    b = pl.program_id(1) + static["b_off"]
    n_far = jnp.maximum(b // SUB_PER_KT + 1 - NEAR_TILES, 0)
    programs = []
    for gi in range(gp):
        v = [r.at[pl.ds(gi, 1)] for r in ins]
        kw_refs, vwt_refs = v[5:5 + N_WIN_TILES], v[5 + N_WIN_TILES:5 + 2 * N_WIN_TILES]
        rest = v[5 + 2 * N_WIN_TILES:] + [o_ref.at[:, pl.ds(gi * GROUP_COLS, GROUP_COLS)]]
        programs.append(_nsa_group(*v[:5], kw_refs, vwt_refs, *rest, *[r.at[gi] for r in scratch],
                                   **static))
    loops = [next(p) for p in programs]

    @pl.when(n_far > 0)
    def _():
        for far_first, _ in loops:
            far_first()

    def far_body(pi, carry):
        for _, far_pair in loops:
            far_pair(pi)
        return carry

    lax.fori_loop(0, (n_far + 1) // 2, far_body, 0)
    for p in programs:
        for _ in p:
            pass


def _nsa_attend(qt, kc, vct, ks, vst, kw, vwt, tn, wt, gt, *, n_sel_blk):
    s_n, nq = qt.shape[:2]
    gp = GROUPS_PER_STEP
    assert kc.shape[1] >= 8 * nq and ks.shape[1] * SUB_PER_KT >= nq and s_n % gp == 0
    n_sel = min(N_SEL, n_sel_blk)
    n_seg = NSA_SEGMENTS if nq % (NSA_SEGMENTS * SUB_PER_KT) == 0 else 1
    nq_seg = nq // n_seg
    y = None
    for seg in range(n_seg):
        b_off = seg * nq_seg
        b_end = b_off + nq_seg
        ncp = min(_round_up(8 * b_end, LANES), kc.shape[1])
        nkt = b_end // SUB_PER_KT
        nblk = nkt * BLK_PER_KT
        imp_rows = max(8 + ncp, 8 + (SEL_BLOCK // CMP_STRIDE) * nblk)
        scratch = [(ncp + WT_ROWS, HQ), (imp_rows, QT), (nblk + BLK_PER_KT, HQ), (1, HQ), (V_ROWS, HQ),
                   (KT_SEL, HQ), (KT_SEL, HQ), (8, HQ), (NSA_DK, HQ)]
        aliased = y is not None
        kern = functools.partial(_nsa_kernel, n_scratch=len(scratch), aliased=aliased, ncp=ncp,
                                 nblk=nblk, n_sel=n_sel, b_off=b_off)
        lead = lambda a, n: pl.BlockSpec((gp, n) + a.shape[2:], lambda s, i: (s,) + (0,) * (a.ndim - 1),
                                         pipeline_mode=pl.Buffered(1))
        per_tile = lambda a: pl.BlockSpec((gp, 1) + a.shape[2:], lambda s, i: (s, i + b_off, 0, 0))
        win = lambda a, w: pl.BlockSpec(
            (gp, 1) + a.shape[2:],
            lambda s, i: (s, jnp.maximum(i + b_off - (N_WIN_TILES - 1) + w, 0), 0, 0))
        in_specs = ([per_tile(qt), lead(kc, ncp),
                     pl.BlockSpec((gp, NSA_DK, ncp), lambda s, i: (s, 0, 0), pipeline_mode=pl.Buffered(1)),
                     lead(ks, nkt), lead(vst, nkt)]
                    + [win(kw, w) for w in range(N_WIN_TILES)]
                    + [win(vwt, w) for w in range(N_WIN_TILES)]
                    + [lead(tn, tn.shape[1]), lead(wt, wt.shape[1]), per_tile(gt)])
        args = (qt, kc, vct, ks, vst, *([kw] * N_WIN_TILES), *([vwt] * N_WIN_TILES), tn, wt, gt)
        if aliased:
            in_specs = [pl.BlockSpec(memory_space=pl.ANY)] + in_specs
            args = (y,) + args
        y = pl.pallas_call(
            kern,
            grid=(s_n // gp, nq_seg),
            in_specs=in_specs,
            out_specs=pl.BlockSpec((QT, gp * GROUP_COLS), lambda s, i: (i + b_off, s)),
            out_shape=jax.ShapeDtypeStruct((nq * QT, s_n * GROUP_COLS), F32),
            scratch_shapes=[pltpu.VMEM((gp,) + s, F32) for s in scratch],
            input_output_aliases={0: 0} if aliased else {},
            compiler_params=_cparams(("parallel", "arbitrary")),
            name="nsa_prompt",
        )(*args)
    return y


def _k_tiles(kv, tile):
    nt = kv.shape[0] // tile
    k = kv[..., :KV_COLS].reshape(nt, tile, NSA_G, NSA_DK)
    k = jnp.transpose(k, (2, 0, 1, 3))
    k = jnp.pad(k, ((0, 0), (0, 0), (0, 0), (0, LANES - NSA_DK))).astype(BF16)
    v = kv[..., KV_COLS:].reshape(nt, tile, NSA_G, NSA_DK)
    v = jnp.transpose(v, (2, 0, 3, 1)).astype(BF16)
    return k, v


def _prompt_attend(qt, gt, tok, ks, vst, kw, vwt, bias):
    nq = qt.shape[1]
    t = nq * QT
    near, cwin = bias
    ncp = max(_round_up(tok.shape[0] + 1, LANES), _round_up(8 * nq, LANES))
    kc, vct = _k_tiles(jnp.pad(tok, ((0, ncp - tok.shape[0]), (0, 0))), ncp)
    return _nsa_attend(qt, kc[:, 0], vct[:, 0], ks, vst, kw, vwt, near, cwin, gt,
                       n_sel_blk=t // SEL_BLOCK)


def _sample_bias_kernel(tab_ref, cb_ref, sb_ref, snb_ref, wb_ref, *, past, dseq, ns):
    rows = tab_ref.shape[0]
    far = tab_ref[:, REL_BUCKETS - 1:REL_BUCKETS]

    def tok_of(n):
        return lax.broadcasted_iota(jnp.int32, (rows, n), 0) % dseq

    def lane(n):
        return lax.broadcasted_iota(jnp.int32, (rows, n), 1)

    for idx in range(2):
        rho = LANES * (ns - 1 + idx) + lane(LANES)
        d = past + tok_of(LANES) - (CMP_STRIDE * (rho - 1) + CMP_LEN - 1)
        cb_ref[idx] = jnp.where(d >= 0, _rel_bias_rows(d, tab_ref) - far, 0.0)
    d = SP_KEYS + tok_of(SP_KEYS) - lane(SP_KEYS)
    sb_ref[...] = _rel_bias_rows(d, tab_ref) - far
    d = tok_of(LANES) - lane(LANES)
    snb_ref[...] = jnp.where(d >= 0, _rel_bias_rows(d, tab_ref) - far, NEG)
    d = WINDOW + tok_of(WINDOW + LANES) - lane(WINDOW + LANES)
    wb_ref[...] = jnp.where((d >= 0) & (d <= WINDOW), _rel_bias_rows(d, tab_ref), NEG)


def _sample_bias(rel_bias, past, dseq, ns):
    rows = NSA_HEADS * dseq
    tab = (rel_bias * LOG2E).reshape(REL_BUCKETS, NSA_G, NSA_HPG)
    tab = jnp.transpose(tab, (2, 1, 0))
    tab = jnp.repeat(tab.reshape(NSA_HEADS, 1, REL_BUCKETS), dseq, axis=1).reshape(rows, REL_BUCKETS)
    tab = jnp.pad(tab, ((0, 0), (0, LANES - REL_BUCKETS)))
    shapes = [(2, rows, LANES), (rows, SP_KEYS), (rows, LANES), (rows, WINDOW + LANES)]
    cb, sb, snb, wb = pl.pallas_call(
        functools.partial(_sample_bias_kernel, past=past, dseq=dseq, ns=ns),
        out_shape=[jax.ShapeDtypeStruct(s, F32) for s in shapes],
        compiler_params=pltpu.CompilerParams(vmem_limit_bytes=VMEM_LIMIT),
        name="bias_sample",
    )(tab)
    return cb, sb, snb, wb


def _sample_nsa_kernel(pt_ref, *refs, ns, past, dseq, n_sel, n_sel_blk):
    cmp_pages = refs[:SP_PP]
    slc_pages = refs[SP_PP:2 * SP_PP]
    (xn_ref, ksn_ref, win_ref, kwn_ref, q_ref, g_ref, w1_ref, pos_ref, pw_ref, w2_ref, e_ref,
     cb_ref, sb_ref, snb_ref, wb_ref, o_ref,
     s_ref, vc_ref, carry_ref, sel_ref, m_ref, l_ref, acc_ref, oc_ref, x_ref) = refs[2 * SP_PP:]
    j = pl.program_id(1)
    qbd = q_ref[0]
    rows = qbd.shape[0]
    ntile = ns + 1
    nblk_l = sel_ref.shape[0] * SP_BLKS

    def feat_major(refs_, c):
        return jnp.concatenate([r[0, c].reshape(KV_COLS, r.shape[-1]) for r in refs_], axis=1)

    def tokens_of(gather, prepare=None):
        out = []
        for c in range(2):
            if prepare is not None:
                prepare(c)
            xc = gather(c)
            n = xc.shape[0]
            row0 = lax.broadcasted_iota(jnp.int32, (n, KV_COLS), 0) == 0
            y = _dot(xc.astype(BF16), w1_ref[c])
            first = jnp.where(row0, carry_ref[c:c + 1, :], pltpu.roll(y[:, :KV_COLS], 1, 0))
            carry_ref[c:c + 1, :] = y[n - 1:n, :KV_COLS]
            pre = first + y[:, KV_COLS:] + _pos_bias(pos_ref, pw_ref, c)
            out.append(_dot(jax.nn.gelu(pre).astype(BF16), w2_ref[c]))
        return out

    def online_update(s, v):
        m_old = m_ref[...]
        m_new = jnp.maximum(m_old, jnp.max(s, axis=1, keepdims=True))
        alpha = jnp.exp2(m_old - m_new)
        p = jnp.exp2(s - m_new)
        l_ref[...] = alpha * l_ref[...] + jnp.sum(p, axis=1, keepdims=True)
        acc_ref[...] = alpha * acc_ref[...] + _dot_nt(p.astype(BF16), v)
        m_ref[...] = m_new

    @pl.when(j == 0)
    def _():
        carry_ref[...] = jnp.zeros(carry_ref.shape, F32)

    @pl.when(j < ns)
    def _():
        def to_rows(c):
            for p, r in enumerate(cmp_pages):
                for gp in range(2):
                    tile = r[0, c, 2 * gp:2 * gp + 2].reshape(LANES, PAGE_SIZE)
                    x_ref[2 * c + gp, p * PAGE_SIZE:(p + 1) * PAGE_SIZE, :] = tile.T

        def gather(c):
            return jnp.concatenate(
                [x_ref[2 * c + gp, pl.ds(l, SP_KEYS // CMP_STRIDE, stride=CMP_STRIDE), :]
                 for l in range(CMP_STRIDE) for gp in range(2)], axis=1)

        tok_k, tok_v = tokens_of(gather, to_rows)
        s_ref[j] = _dot_nt(qbd, tok_k.astype(BF16))
        vc_ref[j] = tok_v.astype(BF16)

    @pl.when(j == ns - 1)
    def _():
        tok_k, tok_v = tokens_of(lambda c: _gather_cv(xn_ref[0], c))
        zpad = jnp.zeros((LANES - tok_k.shape[0], KV_COLS), F32)
        s_ref[ns] = _dot_nt(qbd, jnp.concatenate([tok_k, zpad], axis=0).astype(BF16))
        vc_ref[ns] = jnp.concatenate([tok_v, zpad], axis=0).astype(BF16)

        tiles = []
        for t in range(ntile):
            st = s_ref[t]
            if t >= ns - 1:
                st = st + cb_ref[t - (ns - 1)]
            tiles.append(st)
        s = jnp.concatenate(tiles, axis=1)
        width = ntile * LANES
        rho = lax.broadcasted_iota(jnp.int32, (rows, width), 1)
        tq = lax.broadcasted_iota(jnp.int32, (rows, width), 0) % dseq
        n_cmp = (past + _round_up(dseq, SEL_BLOCK)) // CMP_STRIDE - (CMP_HALF - 1)
        vis = (rho >= 1) & (rho <= n_cmp) & (CMP_STRIDE * (rho - 1) + CMP_LEN - 1 <= past + tq)
        s = jnp.where(vis, s, NEG)
        m = jnp.max(s, axis=1, keepdims=True)
        m = jnp.where(m <= 0.1 * NEG, 0.0, m)
        p = jnp.exp2(s - m)
        rinv = 1.0 / jnp.maximum(jnp.sum(p, axis=1, keepdims=True), 1e-30)
        vc = jnp.concatenate([vc_ref[t] for t in range(ntile)], axis=0)
        oc_ref[...] = _dot(p.astype(BF16), vc) * rinv
        pn = p * rinv
        gt = NSA_G * dseq
        imp = pn[0:gt]
        for h in range(1, NSA_HPG):
            imp = imp + pn[h * gt:(h + 1) * gt]
        ratio = SEL_BLOCK // CMP_STRIDE
        span = ratio + CMP_HALF - 1
        ar = lax.broadcasted_iota(jnp.int32, (width, nblk_l), 0)
        ac = lax.broadcasted_iota(jnp.int32, (width, nblk_l), 1)
        band = jnp.where((ar >= ratio * ac) & (ar < ratio * ac + span), 1.0, 0.0).astype(BF16)
        i_hi = imp.astype(BF16)
        r1 = imp - i_hi.astype(F32)
        i_mid = r1.astype(BF16)
        i_lo = (r1 - i_mid.astype(F32)).astype(BF16)
        psl = _dot(i_hi, band) + _dot(i_mid, band) + _dot(i_lo, band)
        ji = lax.broadcasted_iota(jnp.int32, (gt, nblk_l), 1)
        tq2 = lax.broadcasted_iota(jnp.int32, (gt, nblk_l), 0) % dseq
        cur = (past + tq2) // SEL_BLOCK
        jf = ji.astype(F32)
        forced = (ji == 0) | (ji == cur) | (ji == cur - 1)
        score0 = jnp.where(forced, BIG, jnp.where(ji <= cur, psl, -BIG))
        score0 = jnp.where(ji < n_sel_blk, score0, -jnp.inf)

        def pick_one(_, carry):
            score, sel = carry
            mx = jnp.max(score, axis=1, keepdims=True)
            first = jnp.min(jnp.where(score == mx, jf, 1e9), axis=1, keepdims=True)
            pick = jf == first
            return jnp.where(pick, -jnp.inf, score), jnp.where(pick, 1.0, sel)

        _, sel = lax.fori_loop(0, n_sel, pick_one, (score0, jnp.zeros((gt, nblk_l), F32)))
        for t in range(sel_ref.shape[0]):
            piece = sel[:, t * SP_BLKS:(t + 1) * SP_BLKS]
            piece = jnp.concatenate([piece, jnp.zeros((gt, LANES - SP_BLKS), F32)], axis=1)
            sel_ref[t] = jnp.concatenate([piece] * NSA_HPG, axis=0).astype(BF16)
        m_ref[...] = jnp.full(m_ref.shape, NEG, F32)
        l_ref[...] = jnp.zeros(l_ref.shape, F32)
        acc_ref[...] = jnp.zeros(acc_ref.shape, F32)

    @pl.when(j >= ns)
    def _():
        j2 = j - ns
        s = _dot(qbd, feat_major(slc_pages, 0).astype(BF16))
        chosen = _dot(sel_ref[j2], e_ref[...])
        s = s + (chosen - 1.0) * (-NEG)
        s = s + jnp.where(j2 == ns - 1, sb_ref[...], 0.0)
        online_update(s, feat_major(slc_pages, 1).astype(BF16))

    @pl.when(j == 2 * ns - 1)
    def _():
        online_update(_dot(qbd, ksn_ref[0, 0].astype(BF16)) + snb_ref[...],
                      ksn_ref[0, 1].astype(BF16))
        o_s = acc_ref[...] * (1.0 / l_ref[...])
        kw = jnp.concatenate([feat_major([win_ref], 0), kwn_ref[0, 0]], axis=1)
        vw = jnp.concatenate([feat_major([win_ref], 1), kwn_ref[0, 1]], axis=1)
        s = _dot(qbd, kw.astype(BF16)) + wb_ref[...]
        p = jnp.exp2(s - jnp.max(s, axis=1, keepdims=True))
        o_w = _dot_nt(p.astype(BF16), vw.astype(BF16)) * (1.0 / jnp.sum(p, axis=1, keepdims=True))
        g = jax.nn.sigmoid(g_ref[0])
        o_ref[0] = g[:, 0:1] * oc_ref[...] + g[:, 1:2] * o_s + g[:, 2:3] * o_w


def _sample_attend(q, gates, kv_c, kv_s, kv_w, cache_cmp, cache_slc, win_buf, page_table, cw,
                   rel_bias):
    bsz, dseq = q.shape[:2]
    n_pages = page_table.shape[1]
    past = n_pages * PAGE_SIZE
    assert n_pages % SP_PP == 0 and dseq <= SEL_BLOCK and win_buf.shape[1] == WINDOW
    feat_major = lambda a: jnp.transpose(a, (0, 2, 3, 4, 1))
    cache_cmp, cache_slc, win_buf = feat_major(cache_cmp), feat_major(cache_slc), feat_major(win_buf)
    ns = n_pages // SP_PP
    rows = NSA_HEADS * dseq
    n_sel_blk = past // SEL_BLOCK + 1
    n_sel = min(N_SEL, n_sel_blk)
    nsel_tiles = _round_up(-(-n_sel_blk // SP_BLKS), LANES // SP_BLKS)
    w1big, pos, pw, w2big = cw
    cb, sb, snb, wb = _sample_bias(rel_bias, past, dseq, ns)

    qr = jnp.transpose((q * Q_SCALE).reshape(bsz, dseq, NSA_G, NSA_HPG, NSA_DK), (0, 3, 2, 1, 4))
    qbd = jnp.einsum('bhgtd,gk->bhgtkd', qr, jnp.eye(NSA_G, dtype=F32))
    qbd = qbd.reshape(bsz, rows, KV_COLS).astype(BF16)
    gr = gates[..., :N_BRANCH * NSA_HEADS].reshape(bsz, dseq, NSA_G, NSA_HPG, N_BRANCH)
    gr = jnp.transpose(gr, (0, 3, 2, 1, 4)).reshape(bsz, rows, N_BRANCH)
    gr = jnp.pad(gr, ((0, 0), (0, 0), (0, LANES - N_BRANCH)))
    pad_rows = lambda a, n: jnp.pad(a, ((0, 0), (0, n - a.shape[1]), (0, 0)))
    xn = pad_rows(kv_c, SEL_BLOCK).reshape(bsz, SEL_BLOCK // CMP_STRIDE, CMP_STRIDE * ROW_W)
    xn = pad_rows(xn, 8)
    new_feat_major = lambda a: jnp.transpose(
        pad_rows(a, LANES).reshape(bsz, LANES, 2, KV_COLS), (0, 2, 3, 1))
    ksn = new_feat_major(kv_s)
    kwn = new_feat_major(kv_w)
    expand = (jnp.arange(LANES)[:, None] == (jnp.arange(SP_KEYS) // SEL_BLOCK)[None, :]).astype(BF16)
    page_block = (1, 2, NSA_G, NSA_DK, PAGE_SIZE)

    def page_map(k, phase):
        if phase == 0:
            return lambda b, j, pt: (pt[b, jnp.minimum(j, ns - 1) * SP_PP + k], 0, 0, 0, 0)
        return lambda b, j, pt: (pt[b, jnp.maximum(j - ns, 0) * SP_PP + k], 0, 0, 0, 0)

    per_b = lambda a: pl.BlockSpec((1,) + a.shape[1:], lambda b, j, pt: (b,) + (0,) * (a.ndim - 1))
    const = lambda a: pl.BlockSpec(a.shape, lambda b, j, pt: (0,) * a.ndim)
    in_specs = ([pl.BlockSpec(page_block, page_map(k, 0)) for k in range(SP_PP)]
                + [pl.BlockSpec(page_block, page_map(k, 1)) for k in range(SP_PP)]
                + [per_b(xn), per_b(ksn), per_b(win_buf), per_b(kwn), per_b(qbd), per_b(gr),
                   pl.BlockSpec(w1big.shape, lambda b, j, pt: (0, 0, 0), pipeline_mode=pl.Buffered(1)),
                   const(pos), const(pw), const(w2big), const(expand),
                   const(cb), const(sb), const(snb), const(wb)])
    grid_spec = pltpu.PrefetchScalarGridSpec(
        num_scalar_prefetch=1, grid=(bsz, 2 * ns), in_specs=in_specs,
        out_specs=pl.BlockSpec((1, rows, KV_COLS), lambda b, j, pt: (b, 0, 0)),
        scratch_shapes=[pltpu.VMEM((ns + 1, rows, LANES), F32),
                        pltpu.VMEM((ns + 1, LANES, KV_COLS), BF16),
                        pltpu.VMEM((8, KV_COLS), F32),
                        pltpu.VMEM((nsel_tiles, rows, LANES), BF16),
                        pltpu.VMEM((rows, 1), F32),
                        pltpu.VMEM((rows, 1), F32),
                        pltpu.VMEM((rows, KV_COLS), F32),
                        pltpu.VMEM((rows, KV_COLS), F32),
                        pltpu.VMEM((ROW_W // LANES, SP_KEYS, LANES), F32)])
    out = pl.pallas_call(
        functools.partial(_sample_nsa_kernel, ns=ns, past=past, dseq=dseq, n_sel=n_sel,
                          n_sel_blk=n_sel_blk),
        grid_spec=grid_spec,
        out_shape=jax.ShapeDtypeStruct((bsz, rows, KV_COLS), F32),
        compiler_params=_cparams(("parallel", "arbitrary")),
        name="nsa_sample",
    )(page_table, *([cache_cmp] * SP_PP), *([cache_slc] * SP_PP), xn, ksn, win_buf, kwn, qbd, gr,
      w1big, pos, pw, w2big, expand, cb, sb, snb, wb)
    o = out.reshape(bsz, NSA_HPG, NSA_G, dseq, NSA_G, NSA_DK)
    o = jnp.einsum('bhgtge->btghe', o)
    return o.reshape(bsz * dseq, NSA_WIDTH)


def _out_proj_kernel(ycm_ref, ynsa_ref, h_ref, gt_ref, gpost_ref, w1_ref, w2_ref, o_ref):
    out = _dot(ycm_ref[...].astype(BF16), w1_ref[...]) + _dot(ynsa_ref[...].astype(BF16), w2_ref[...])
    o_ref[...] = h_ref[...] + gt_ref[...] * _rms(out, gpost_ref[...])


def _out_proj(ycm, ynsa, h, gate, g_post, w1, w2, tm):
    r, d = h.shape
    return pl.pallas_call(
        _out_proj_kernel,
        grid=(r // tm,),
        in_specs=[pl.BlockSpec((tm, CM_WIDTH), lambda i: (i, 0)),
                  pl.BlockSpec((tm, NSA_WIDTH), lambda i: (i, 0)),
                  pl.BlockSpec((tm, d), lambda i: (i, 0)),
                  _row_spec(gate, tm),
                  pl.BlockSpec((1, d), lambda i: (0, 0)),
                  pl.BlockSpec(w1.shape, lambda i: (0, 0)),
                  pl.BlockSpec(w2.shape, lambda i: (0, 0))],
        out_specs=pl.BlockSpec((tm, d), lambda i: (i, 0)),
        out_shape=jax.ShapeDtypeStruct((r, d), F32),
        compiler_params=_cparams(("parallel",)),
        name="out_proj",
    )(ycm, ynsa, h, gate, g_post, w1, w2)


def _row_tile(r, cap):
    tm = min(r, cap)
    while r % tm:
        tm //= 2
    return tm


def _layer(x, mod, per_row, weights, in_proj, mixer):
    r, d = x.shape
    (norm_pre, norm_post, ffn, w_out1, w_out2) = weights
    if per_row is None:
        mrow = lambda i, j: mod[0, i, j].reshape(1, d)
    else:
        mrow = lambda i, j: jnp.repeat(mod[:, i, j], per_row, axis=0)
    tm = _row_tile(r, 512)
    tf = _row_tile(ffn[0][0].shape[1], 512)
    h = _ffn(x, mrow(0, 0), mrow(0, 1), mrow(0, 2), norm_pre[0:1], norm_post[0:1], *ffn[0],
             res_w=0.5, tm=tm, tf=tf)
    mixed_cm, mixed_nsa, state = mixer(*in_proj(h, mrow(1, 0), mrow(1, 1), norm_pre[1:2]))
    h = _out_proj(mixed_cm, mixed_nsa, h, mrow(1, 2), norm_post[1:2], w_out1, w_out2, tm)
    h = _ffn(h, mrow(2, 0), mrow(2, 1), mrow(2, 2), norm_pre[2:3], norm_post[2:3], *ffn[1],
             res_w=0.5, tm=tm, tf=tf)
    return h, state


def kernel(x_prompt, x_sample, cache_cmp_kv, cache_slc_kv, state_win_kv, page_table, c_prompt,
           c_sample, w_mod, b_mod, norm_pre, norm_post, ffn_w_gate, ffn_w_up, ffn_w_down, w_in,
           w_out, cm_norm, cm_ws, cm_bs, phi_pos, phi_w1, phi_w2, rel_bias):
    depth = w_mod.shape[0]
    assert depth == 1
    bp, seq, d = x_prompt.shape
    bs, dseq, _ = x_sample.shape
    assert bp == 1 and seq % KT_SEL == 0 and seq >= WINDOW
    l = 0

    c_all = jnp.concatenate([c_prompt, c_sample], axis=0)
    mpad = _round_up(c_all.shape[0], 8)
    mod = _mod_proj(jnp.pad(c_all, ((0, mpad - c_all.shape[0]), (0, 0))), w_mod[l], b_mod[l])
    mod = mod[:bp + bs].reshape(bp + bs, 3, 3, d)

    ffn = [(ffn_w_gate[l, i].astype(BF16), ffn_w_up[l, i].astype(BF16), ffn_w_down[l, i].astype(BF16))
           for i in range(2)]
    w_in_p = jnp.pad(w_in[l], ((0, 0), (0, sum(IN_SEGS) - w_in.shape[2]))).astype(BF16)
    w_rows, w_feat = _in_proj_prompt_weights(w_in[l])
    w_o = w_out[l].astype(BF16)
    weights = (norm_pre[l], norm_post[l], ffn, w_o[:CM_WIDTH], w_o[CM_WIDTH:])
    cw = _compress_weights(phi_pos[l], phi_w1[l], phi_w2[l])
    prompt_bias = _prompt_bias_tiles(rel_bias)

    def prompt_in_proj(h, shift, scale, g_pre):
        return _in_proj_prompt(h, shift, scale, g_pre, w_rows, w_feat)

    def prompt_mixer(u, v, kv_c, qt, gt, ks, vst, kw, vwt, kvt):
        y_cm, _ = _chunk_mix(u, v, cm_norm[l], cm_ws[l], cm_bs[l], CHUNK)
        tok = _compress_tokens(kv_c[None], cw)[0]
        y_nsa = _prompt_attend(qt, gt, tok, ks, vst, kw, vwt, prompt_bias)
        return y_cm, y_nsa, kvt

    assert seq % KT_SEL == 0
    hp, kvt = _layer(x_prompt.reshape(seq, d), mod[:bp], None, weights, prompt_in_proj, prompt_mixer)
    kvt = jnp.transpose(kvt.reshape(N_BRANCH, 2, NSA_G, NSA_DK, seq), (0, 4, 1, 2, 3))
    pc, ps, pw = kvt[0], kvt[1], kvt[2]

    rows = _round_up(dseq, 16)

    def sample_mixer(u, v, q, kv_c, kv_s, kv_w, gates):
        per_seq = lambda a: a.reshape(bs, dseq, -1)
        pad_rows = lambda a, n: jnp.pad(per_seq(a), ((0, 0), (0, n - dseq), (0, 0)))
        y_cm, vn = _chunk_mix(pad_rows(u, rows).reshape(bs * rows, CM_WIDTH),
                              pad_rows(v, rows).reshape(bs * rows, CM_WIDTH),
                              cm_norm[l], cm_ws[l], cm_bs[l], rows)
        y_cm = y_cm.reshape(bs, rows, CM_WIDTH)[:, :dseq].reshape(bs * dseq, CM_WIDTH)
        vn = vn.reshape(bs, rows, CM_WIDTH)[:, :dseq]
        y_nsa = _sample_attend(per_seq(q), per_seq(gates), per_seq(kv_c), per_seq(kv_s), per_seq(kv_w),
                               cache_cmp_kv[l], cache_slc_kv[l], state_win_kv[l], page_table, cw,
                               rel_bias)
        return y_cm, y_nsa, (kv_c, kv_s, kv_w, vn)

    def sample_in_proj(h, shift, scale, g_pre):
        return _in_proj(h, shift, scale, g_pre, w_in_p, _row_tile(h.shape[0], 256))

    hs, (sc, ss, sw, sv) = _layer(x_sample.reshape(bs * dseq, d), mod[bp:], dseq, weights,
                                  sample_in_proj, sample_mixer)

    kvshape = lambda a, b, t: a.reshape(1, b, t, 2, NSA_G, NSA_DK)
    wp = min(WINDOW, seq)
    win_s = jnp.concatenate([state_win_kv[l].reshape(bs, -1, ROW_W), sw.reshape(bs, dseq, ROW_W)],
                            axis=1)[:, dseq:]
    return (hp.reshape(bp, seq, d), hs.reshape(bs, dseq, d),
            kvshape(pc, bp, seq), kvshape(sc, bs, dseq),
            kvshape(ps, bp, seq), kvshape(ss, bs, dseq),
            kvshape(pw[seq - wp:], bp, wp), kvshape(win_s, bs, win_s.shape[1]),
            sv.reshape(1, bs, dseq, CM_HEADS, CM_HEAD_DIM))
```

```python
import functools
import math

import jax
import jax.numpy as jnp
from jax import lax
from jax.experimental import pallas as pl
from jax.experimental.pallas import tpu as pltpu

F32 = jnp.float32
BF16 = jnp.bfloat16

CM_HEADS = 8
CM_HEAD_DIM = 128
CM_WIDTH = CM_HEADS * CM_HEAD_DIM
CHUNK = 128
NSA_HEADS = 16
NSA_G = 4
NSA_HPG = 4
NSA_DK = 64
NSA_WIDTH = NSA_HEADS * NSA_DK
KV_COLS = NSA_G * NSA_DK
ROW_W = 2 * KV_COLS
CMP_LEN = 32
CMP_STRIDE = 16
CMP_HALF = CMP_LEN // CMP_STRIDE
SEL_BLOCK = 64
N_SEL = 16
N_FORCED = 3
WINDOW = 512
N_BRANCH = 3
BIG = 1e4
REL_BUCKETS = 32
EPS = 1e-6
PAGE_SIZE = 128

LOG2E = 1.4426950408889634
Q_SCALE = NSA_DK ** -0.5 * LOG2E
NEG = -1e30
LANES = 128
QT = 128
HQ = NSA_HPG * QT
KT_SEL = 512
BLK_PER_KT = KT_SEL // SEL_BLOCK
SUB_PER_KT = KT_SEL // QT
N_NEAR = 8
NEAR_TILES = 3
V_ROWS = NSA_DK + 16
TN_WIN, TN_MASKED, TN_ZERO, TN_COUNT = 8, 9, 10, 11
WT_LEAD = 64
WT_ROWS = 200
REL_THRESHOLDS = (21, 27, 35, 46, 59, 77, 99, 128, 166, 216, 280, 363, 470, 609, 790)
VMEM_LIMIT = 56 * 1024 * 1024
SP_PP = CMP_STRIDE
SP_KEYS = SP_PP * PAGE_SIZE
SP_BLKS = SP_KEYS // SEL_BLOCK


def _cparams(sem):
    return pltpu.CompilerParams(dimension_semantics=sem, vmem_limit_bytes=VMEM_LIMIT)


def _dot(a, b):
    return jnp.dot(a, b, preferred_element_type=F32)


def _dot_nt(a, b):
    return lax.dot_general(a, b, (((1,), (1,)), ((), ())), preferred_element_type=F32)


def _split_bf16(x):
    hi = x.astype(BF16)
    lo = (x - hi.astype(F32)).astype(BF16)
    return hi, lo


def _dot_f32(a, b):
    a_hi, a_lo = _split_bf16(a)
    b_hi, b_lo = _split_bf16(b)
    return _dot(a_hi, b_hi) + _dot(a_lo, b_hi) + _dot(a_hi, b_lo)


def _rms(x, g):
    return x * lax.rsqrt(jnp.mean(x * x, axis=-1, keepdims=True) + EPS) * g


def _row_spec(arr, tm):
    d = arr.shape[-1]
    if arr.shape[0] == 1:
        return pl.BlockSpec((1, d), lambda *idx: (0, 0))
    return pl.BlockSpec((tm, d), lambda *idx: (idx[0], 0))


def _round_up(x, m):
    return -(-x // m) * m


def _mod_kernel(c_ref, w_ref, b_ref, o_ref):
    c = c_ref[...]
    s = c * jax.nn.sigmoid(c)
    o_ref[...] = _dot_f32(s, w_ref[...]) + b_ref[...]


def _mod_proj(c, w_mod, b_mod):
    m, d = c.shape
    n = w_mod.shape[1]
    tn = 512
    return pl.pallas_call(
        _mod_kernel,
        grid=(n // tn,),
        in_specs=[pl.BlockSpec((m, d), lambda j: (0, 0)),
                  pl.BlockSpec((d, tn), lambda j: (0, j)),
                  pl.BlockSpec((1, tn), lambda j: (0, j))],
        out_specs=pl.BlockSpec((m, tn), lambda j: (0, j)),
        out_shape=jax.ShapeDtypeStruct((m, n), F32),
        compiler_params=_cparams(("arbitrary",)),
        name="mod_proj",
    )(c, w_mod, b_mod.reshape(1, n))


def _ffn_kernel(x_ref, sh_ref, sc_ref, gt_ref, gpre_ref, gpost_ref, wg_ref, wu_ref, wd_ref,
                o_ref, a_ref, acc_ref, *, res_w, nf):
    f = pl.program_id(1)

    @pl.when(f == 0)
    def _():
        y = _rms(x_ref[...], gpre_ref[...])
        a_ref[...] = (y * (1.0 + sc_ref[...]) + sh_ref[...]).astype(BF16)
        acc_ref[...] = jnp.zeros_like(acc_ref)

    a = a_ref[...]
    h = _dot(a, wg_ref[...])
    u = _dot(a, wu_ref[...])
    act = (h * jax.nn.sigmoid(h) * u).astype(BF16)
    acc_ref[...] += _dot(act, wd_ref[...])

    @pl.when(f == nf - 1)
    def _():
        o_ref[...] = x_ref[...] + res_w * gt_ref[...] * _rms(acc_ref[...], gpost_ref[...])


FFN_TM = 512
FFN_TF = 512


def _ffn(x, shift, scale, gate, g_pre, g_post, wg, wu, wd, res_w, tm, tf):
    r, d = x.shape
    fdim = wg.shape[1]
    nf = fdim // tf
    return pl.pallas_call(
        functools.partial(_ffn_kernel, res_w=res_w, nf=nf),
        grid=(r // tm, nf),
        in_specs=[pl.BlockSpec((tm, d), lambda i, f: (i, 0)),
                  _row_spec(shift, tm), _row_spec(scale, tm), _row_spec(gate, tm),
                  pl.BlockSpec((1, d), lambda i, f: (0, 0)),
                  pl.BlockSpec((1, d), lambda i, f: (0, 0)),
                  pl.BlockSpec((d, tf), lambda i, f: (0, f)),
                  pl.BlockSpec((d, tf), lambda i, f: (0, f)),
                  pl.BlockSpec((tf, d), lambda i, f: (f, 0))],
        out_specs=pl.BlockSpec((tm, d), lambda i, f: (i, 0)),
        out_shape=jax.ShapeDtypeStruct((r, d), F32),
        scratch_shapes=[pltpu.VMEM((tm, d), BF16), pltpu.VMEM((tm, d), F32)],
        compiler_params=_cparams(("parallel", "arbitrary")),
        name="ffn",
    )(x, shift, scale, gate, g_pre, g_post, wg, wu, wd)


IN_SEGS = (CM_WIDTH, CM_WIDTH, NSA_WIDTH, 2 * KV_COLS, 2 * KV_COLS, 2 * KV_COLS, LANES)


def _in_proj_kernel(x_ref, sh_ref, sc_ref, gpre_ref, w_ref, *o_refs):
    y = _rms(x_ref[...], gpre_ref[...])
    a = (y * (1.0 + sc_ref[...]) + sh_ref[...]).astype(BF16)
    z = _dot(a, w_ref[...])
    off = 0
    for o_ref, width in zip(o_refs, IN_SEGS):
        o_ref[...] = z[:, off:off + width]
        off += width


def _in_proj(x, shift, scale, g_pre, w, tm):
    r, d = x.shape
    n = w.shape[1]
    return pl.pallas_call(
        _in_proj_kernel,
        grid=(r // tm,),
        in_specs=[pl.BlockSpec((tm, d), lambda i: (i, 0)),
                  _row_spec(shift, tm), _row_spec(scale, tm),
                  pl.BlockSpec((1, d), lambda i: (0, 0)),
                  pl.BlockSpec((d, n), lambda i: (0, 0), pipeline_mode=pl.Buffered(1))],
        out_specs=[pl.BlockSpec((tm, s), lambda i: (i, 0)) for s in IN_SEGS],
        out_shape=[jax.ShapeDtypeStruct((r, s), F32) for s in IN_SEGS],
        compiler_params=_cparams(("parallel",)),
        name="in_proj",
    )(x, shift, scale, g_pre, w)


IN_P_TM = 2 * QT
IN_P_QW = NSA_HEADS * LANES
IN_P_KW = NSA_G * LANES
IN_P_ROWS_F = N_BRANCH * ROW_W
IN_P_GATE_ROWS = 64


def _in_proj_prompt_kernel(x_ref, sh_ref, sc_ref, gpre_ref, wr_ref, wf_ref, u_ref, v_ref, kvc_ref,
                           qt_ref, gt_ref, ks_ref, vst_ref, kw_ref, vwt_ref, kvt_ref):
    y = _rms(x_ref[...], gpre_ref[...])
    a = (y * (1.0 + sc_ref[...]) + sh_ref[...]).astype(BF16)
    z = _dot(a, wr_ref[...])
    zt = _dot_nt(wf_ref[...], a)
    nt = IN_P_TM // QT
    tile = lambda t: slice(t * QT, (t + 1) * QT)
    u_ref[...] = z[:, 0:CM_WIDTH]
    v_ref[...] = z[:, CM_WIDTH:2 * CM_WIDTH]
    off = 2 * CM_WIDTH
    for t in range(nt):
        for gh in range(NSA_HEADS):
            g, h = divmod(gh, NSA_HPG)
            qt_ref[g, t, h * QT:(h + 1) * QT, :] = z[tile(t), off + gh * LANES:off + (gh + 1) * LANES].astype(BF16)
    off += IN_P_QW
    kvc_ref[...] = z[:, off:off + ROW_W]
    off += ROW_W
    for g in range(NSA_G):
        ks_ref[g, 0] = z[:, off + g * LANES:off + (g + 1) * LANES].astype(BF16)
    off += IN_P_KW
    for t in range(nt):
        for g in range(NSA_G):
            kw_ref[g, t] = z[tile(t), off + g * LANES:off + (g + 1) * LANES].astype(BF16)
    kvt_ref[...] = zt[0:IN_P_ROWS_F]
    v_rows = lambda branch, g: slice(branch * ROW_W + KV_COLS + g * NSA_DK,
                                     branch * ROW_W + KV_COLS + (g + 1) * NSA_DK)
    for g in range(NSA_G):
        vst_ref[g, 0] = zt[v_rows(1, g), :].astype(BF16)
        for t in range(nt):
            vwt_ref[g, t] = zt[v_rows(2, g), tile(t)].astype(BF16)
            gt_ref[g, t, N_BRANCH:, :] = jnp.zeros((8 - N_BRANCH, HQ), F32)
            for br in range(N_BRANCH):
                for h in range(NSA_HPG):
                    row = IN_P_ROWS_F + (g * N_BRANCH + br) * NSA_HPG + h
                    gt_ref[g, t, br:br + 1, h * QT:(h + 1) * QT] = zt[row:row + 1, tile(t)]


def _in_proj_prompt_weights(w):
    d = w.shape[0]
    q0 = 2 * CM_WIDTH
    k0 = q0 + NSA_WIDTH
    pad_lanes = lambda x: jnp.pad(x, ((0, 0), (0, 0), (0, LANES - NSA_DK))).reshape(d, -1)
    wq = pad_lanes(w[:, q0:k0].reshape(d, NSA_HEADS, NSA_DK) * Q_SCALE)
    k_of = lambda branch: pad_lanes(
        w[:, k0 + branch * ROW_W:k0 + branch * ROW_W + KV_COLS].reshape(d, NSA_G, NSA_DK))
    w_rows = jnp.concatenate([w[:, :q0], wq, w[:, k0:k0 + ROW_W], k_of(1), k_of(2)], axis=1).astype(BF16)
    g0 = k0 + N_BRANCH * ROW_W
    wg = jnp.transpose(w[:, g0:g0 + N_BRANCH * NSA_HEADS].reshape(d, NSA_G, NSA_HPG, N_BRANCH),
                       (0, 1, 3, 2)).reshape(d, N_BRANCH * NSA_HEADS)
    wg = jnp.pad(wg, ((0, 0), (0, IN_P_GATE_ROWS - N_BRANCH * NSA_HEADS)))
    w_feat = jnp.concatenate([w[:, k0:g0], wg], axis=1).T.astype(BF16)
    return w_rows, w_feat


def _in_proj_prompt(x, shift, scale, g_pre, w_rows, w_feat):
    t, d = x.shape
    tm = IN_P_TM
    nq = t // QT
    per = KT_SEL // tm
    const = lambda a: pl.BlockSpec(a.shape, lambda i: (0, 0), pipeline_mode=pl.Buffered(1))
    rows = lambda n: pl.BlockSpec((tm, n), lambda i: (i, 0))
    qtile = lambda a: pl.BlockSpec((NSA_G, tm // QT) + a[2:], lambda i: (0, i, 0, 0))
    shapes = [((t, CM_WIDTH), F32), ((t, CM_WIDTH), F32), ((t, ROW_W), F32),
              ((NSA_G, nq, HQ, LANES), BF16), ((NSA_G, nq, 8, HQ), F32),
              ((NSA_G, t // KT_SEL, KT_SEL, LANES), BF16), ((NSA_G, t // KT_SEL, NSA_DK, KT_SEL), BF16),
              ((NSA_G, nq, QT, LANES), BF16), ((NSA_G, nq, NSA_DK, QT), BF16),
              ((IN_P_ROWS_F, t), F32)]
    out_specs = [rows(CM_WIDTH), rows(CM_WIDTH), rows(ROW_W),
                 qtile(shapes[3][0]), qtile(shapes[4][0]),
                 pl.BlockSpec((NSA_G, 1, tm, LANES), lambda i: (0, i // per, i % per, 0)),
                 pl.BlockSpec((NSA_G, 1, NSA_DK, tm), lambda i: (0, i // per, 0, i % per)),
                 qtile(shapes[7][0]), qtile(shapes[8][0]),
                 pl.BlockSpec((IN_P_ROWS_F, tm), lambda i: (0, i))]
    return pl.pallas_call(
        _in_proj_prompt_kernel,
        grid=(t // tm,),
        in_specs=[rows(d), _row_spec(shift, tm), _row_spec(scale, tm),
                  pl.BlockSpec((1, d), lambda i: (0, 0)), const(w_rows), const(w_feat)],
        out_specs=out_specs,
        out_shape=[jax.ShapeDtypeStruct(s, dt) for s, dt in shapes],
        compiler_params=_cparams(("parallel",)),
        name="in_proj_prompt",
    )(x, shift, scale, g_pre, w_rows, w_feat)


def _chunk_mix_kernel(u_ref, v_ref, nrm_ref, ws_ref, bst_ref, y_ref, vn_ref, *, chunk):
    ri = lax.broadcasted_iota(jnp.int32, (chunk, chunk), 0)
    ci = lax.broadcasted_iota(jnp.int32, (chunk, chunk), 1)
    lower = ci <= ri
    for h in range(CM_HEADS):
        sl = slice(h * CM_HEAD_DIM, (h + 1) * CM_HEAD_DIM)
        vf = jax.nn.gelu(v_ref[:, sl])
        mu = jnp.mean(vf, axis=-1, keepdims=True)
        var = jnp.mean(jnp.square(vf - mu), axis=-1, keepdims=True)
        vn = (vf - mu) * lax.rsqrt(var + EPS) * nrm_ref[:, sl]
        vn_ref[:, sl] = vn
        w = jnp.where(lower, ws_ref[h], 0.0).astype(BF16)
        s = _dot(w, vn.astype(BF16)) + bst_ref[:, h:h + 1]
        y_ref[:, sl] = jax.nn.gelu(u_ref[:, sl]) * s


def _chunk_mix(u, v, cm_norm, cm_ws, cm_bs, chunk):
    r = u.shape[0]
    ws = cm_ws[:, :chunk, :chunk]
    bst = cm_bs[:, :chunk].T
    return pl.pallas_call(
        functools.partial(_chunk_mix_kernel, chunk=chunk),
        grid=(r // chunk,),
        in_specs=[pl.BlockSpec((chunk, CM_WIDTH), lambda i: (i, 0)),
                  pl.BlockSpec((chunk, CM_WIDTH), lambda i: (i, 0)),
                  pl.BlockSpec((1, CM_WIDTH), lambda i: (0, 0)),
                  pl.BlockSpec((CM_HEADS, chunk, chunk), lambda i: (0, 0, 0)),
                  pl.BlockSpec((chunk, CM_HEADS), lambda i: (0, 0))],
        out_specs=[pl.BlockSpec((chunk, CM_WIDTH), lambda i: (i, 0)),
                   pl.BlockSpec((chunk, CM_WIDTH), lambda i: (i, 0))],
        out_shape=[jax.ShapeDtypeStruct((r, CM_WIDTH), F32),
                   jax.ShapeDtypeStruct((r, CM_WIDTH), F32)],
        compiler_params=_cparams(("parallel",)),
        name="chunk_mix",
    )(u, v, cm_norm.reshape(1, CM_WIDTH), ws, bst)


def _gather_cv(x, c):
    return jnp.concatenate(
        [x[:, l * ROW_W + c * KV_COLS:l * ROW_W + (c + 1) * KV_COLS] for l in range(CMP_STRIDE)],
        axis=1)


def _pos_bias(pos_ref, pw_ref, c):
    posb = _dot_f32(pos_ref[c], pw_ref[c])[0:1]
    return jnp.concatenate([posb] * NSA_G, axis=1)


def _compress_weights(phi_pos, phi_w1, phi_w2):
    eye = jnp.eye(NSA_G, dtype=F32)
    w1 = phi_w1.reshape(CMP_HALF, CMP_STRIDE, 2, NSA_DK, NSA_DK)
    w1big = jnp.einsum('mlcde,gh->clgdmhe', w1, eye).reshape(
        2, CMP_STRIDE * KV_COLS, CMP_HALF * KV_COLS).astype(BF16)
    w2big = jnp.einsum('cde,gh->cgdhe', phi_w2, eye).reshape(2, KV_COLS, KV_COLS).astype(BF16)
    pos = jnp.transpose(phi_pos, (1, 0, 2)).reshape(2, 1, CMP_LEN * NSA_DK)
    pos = jnp.broadcast_to(pos, (2, 8, CMP_LEN * NSA_DK))
    pw = jnp.transpose(phi_w1, (1, 0, 2, 3)).reshape(2, CMP_LEN * NSA_DK, NSA_DK)
    return w1big, pos, pw, w2big


def _compress_kernel(x_ref, xn_ref, w1_ref, pos_ref, pw_ref, w2_ref, o_ref, *, rc):
    x = x_ref[0]
    xn = xn_ref[0]
    last_row = lax.broadcasted_iota(jnp.int32, (rc, KV_COLS), 0) == rc - 1
    for c in range(2):
        w1 = w1_ref[c]
        y = _dot(_gather_cv(x, c).astype(BF16), w1)
        yn = _dot(_gather_cv(xn, c).astype(BF16), w1)
        second = pltpu.roll(y[:, KV_COLS:], rc - 1, 0)
        second = jnp.where(last_row, yn[0:1, KV_COLS:], second)
        pre = y[:, :KV_COLS] + second + _pos_bias(pos_ref, pw_ref, c)
        o_ref[0, :, c * KV_COLS:(c + 1) * KV_COLS] = _dot(jax.nn.gelu(pre).astype(BF16), w2_ref[c])


def _pick_rows(n, cap):
    best = 8
    for rc in range(8, cap + 1, 8):
        if n % rc == 0:
            best = rc
    return best


def _compress(x, cw):
    bsz, n, _ = x.shape
    rc = _pick_rows(n, 256)
    w1big, pos, pw, w2big = cw
    nb8 = n // 8
    return pl.pallas_call(
        functools.partial(_compress_kernel, rc=rc),
        grid=(bsz, n // rc),
        in_specs=[pl.BlockSpec((1, rc, x.shape[2]), lambda b, i: (b, i, 0)),
                  pl.BlockSpec((1, 8, x.shape[2]),
                               lambda b, i: (b, jnp.minimum((i + 1) * (rc // 8), nb8 - 1), 0)),
                  pl.BlockSpec(w1big.shape, lambda b, i: (0, 0, 0)),
                  pl.BlockSpec(pos.shape, lambda b, i: (0, 0, 0)),
                  pl.BlockSpec(pw.shape, lambda b, i: (0, 0, 0)),
                  pl.BlockSpec(w2big.shape, lambda b, i: (0, 0, 0))],
        out_specs=pl.BlockSpec((1, rc, ROW_W), lambda b, i: (b, i, 0)),
        out_shape=jax.ShapeDtypeStruct((bsz, n, ROW_W), F32),
        compiler_params=_cparams(("parallel", "arbitrary")),
        name="compress",
    )(x, x, w1big, pos, pw, w2big)


def _compress_tokens(kv_c, cw):
    bsz, t = kv_c.shape[:2]
    n = t // CMP_STRIDE
    npad = _round_up(n, LANES if n >= LANES else 8)
    x = kv_c.reshape(bsz, n, CMP_STRIDE * ROW_W)
    x = jnp.pad(x, ((0, 0), (0, npad - n), (0, 0)))
    return _compress(x, cw)[:, :n - (CMP_HALF - 1)]


def _rel_bucket(d):
    n = jnp.maximum(d, 0)
    large = jnp.full(d.shape, REL_BUCKETS // 2, jnp.int32)
    for thr in REL_THRESHOLDS:
        large = large + (n >= thr).astype(jnp.int32)
    return jnp.where(n < REL_BUCKETS // 2, n, large)


def _rel_bias_lanes(d, tab_ref):
    bkt = _rel_bucket(d)
    f = jnp.zeros(d.shape, F32)
    for k in range(REL_BUCKETS):
        f = jnp.where(bkt == k, tab_ref[0, k:k + 1, :], f)
    return f


def _rel_bias_rows(d, tab_ref):
    bkt = _rel_bucket(d)
    f = jnp.zeros(d.shape, F32)
    for k in range(REL_BUCKETS):
        f = jnp.where(bkt == k, tab_ref[:, k:k + 1], f)
    return f


def _near_tile_kernel(tab_ref, o_ref):
    t = pl.program_id(1)
    delta = jnp.where(t < N_NEAR, t * QT, jnp.where(t == TN_WIN, WINDOW, 0))
    dmax = jnp.where(t == TN_WIN, WINDOW, 1 << 30)
    j = lax.broadcasted_iota(jnp.int32, (QT, HQ), 0)
    i = lax.broadcasted_iota(jnp.int32, (QT, HQ), 1) & (QT - 1)
    d = delta + i - j
    far = tab_ref[0, REL_BUCKETS - 1:REL_BUCKETS, :]
    val = jnp.where((d >= 0) & (d <= dmax), _rel_bias_lanes(d, tab_ref) - far, NEG)
    val = jnp.where(t == TN_MASKED, NEG, jnp.where(t == TN_ZERO, 0.0, val))
    o_ref[0, 0] = val


def _cmp_window_kernel(tab_ref, o_ref):
    e = lax.broadcasted_iota(jnp.int32, (WT_ROWS, HQ), 0) - WT_LEAD
    i = lax.broadcasted_iota(jnp.int32, (WT_ROWS, HQ), 1) & (QT - 1)
    d = i - (CMP_LEN - 1) - CMP_STRIDE * e
    far = tab_ref[0, REL_BUCKETS - 1:REL_BUCKETS, :]
    o_ref[0] = jnp.where(d >= 0, _rel_bias_lanes(d, tab_ref) - far, 0.0)


def _prompt_bias_tiles(rel_bias):
    tab = jnp.repeat((rel_bias * LOG2E).reshape(REL_BUCKETS, NSA_G, NSA_HPG), QT, axis=2)
    tab = jnp.transpose(tab, (1, 0, 2))
    near = pl.pallas_call(
        _near_tile_kernel,
        grid=(NSA_G, TN_COUNT),
        in_specs=[pl.BlockSpec((1, REL_BUCKETS, HQ), lambda g, t: (g, 0, 0))],
        out_specs=pl.BlockSpec((1, 1, QT, HQ), lambda g, t: (g, t, 0, 0)),
        out_shape=jax.ShapeDtypeStruct((NSA_G, TN_COUNT, QT, HQ), F32),
        compiler_params=_cparams(("parallel", "arbitrary")),
        name="bias_near_tiles",
    )(tab)
    cwin = pl.pallas_call(
        _cmp_window_kernel,
        grid=(NSA_G,),
        in_specs=[pl.BlockSpec((1, REL_BUCKETS, HQ), lambda g: (g, 0, 0))],
        out_specs=pl.BlockSpec((1, WT_ROWS, HQ), lambda g: (g, 0, 0)),
        out_shape=jax.ShapeDtypeStruct((NSA_G, WT_ROWS, HQ), F32),
        compiler_params=_cparams(("parallel",)),
        name="bias_cmp_window",
    )(tab)
    return near, cwin


GROUPS_PER_STEP = 2
N_WIN_TILES = WINDOW // QT + 1
NSA_SEGMENTS = 4


def _nsa_group(q_ref, kc_ref, vct_ref, ks_ref, vst_ref, kw_refs, vwt_refs, tn_ref, wt_ref,
               g_ref, o_ref, s_ref, imp_ref, mb_ref, m_ref, acc_ref,
               buf0_ref, buf1_ref, tmax_ref, part_ref, *, ncp, nblk, n_sel, b_off):
    b = pl.program_id(1) + b_off
    qs = b * QT
    q = q_ref[0, 0]

    sc = _dot_nt(kc_ref[0], q)
    ci = lax.broadcasted_iota(jnp.int32, (ncp, HQ), 0)
    li = lax.broadcasted_iota(jnp.int32, (ncp, HQ), 1) & (QT - 1)
    vis = (CMP_STRIDE * ci + (CMP_LEN - 1) - li) <= qs
    s_ref[0:WT_LEAD, :] = jnp.full((WT_LEAD, HQ), NEG, F32)
    s_ref[WT_LEAD + ncp:, :] = jnp.full((s_ref.shape[0] - WT_LEAD - ncp, HQ), NEG, F32)
    s_ref[WT_LEAD:WT_LEAD + ncp, :] = jnp.where(vis, sc, NEG)
    r0 = pl.multiple_of(8 * b, 8)
    s_ref[pl.ds(r0, WT_ROWS), :] += wt_ref[0]
    yield None
    s = s_ref[WT_LEAD:WT_LEAD + ncp, :]
    m = jnp.max(s, axis=0, keepdims=True)
    m = jnp.where(m <= 0.1 * NEG, 0.0, m)
    p = jnp.exp2(s - m)
    rinv = 1.0 / jnp.maximum(jnp.sum(p, axis=0, keepdims=True), 1e-30)
    o_c = _dot(vct_ref[0], p.astype(BF16)) * rinv
    yield None
    pn = p * rinv
    imp = pn[:, 0:QT]
    for h in range(1, NSA_HPG):
        imp = imp + pn[:, h * QT:(h + 1) * QT]
    imp_ref[0:8, :] = jnp.zeros((8, QT), F32)
    imp_ref[8:8 + ncp, :] = imp
    if imp_ref.shape[0] > 8 + ncp:
        imp_ref[8 + ncp:, :] = jnp.zeros((imp_ref.shape[0] - 8 - ncp, QT), F32)
    ratio = SEL_BLOCK // CMP_STRIDE
    span = ratio + CMP_HALF - 1
    psl = imp_ref[pl.ds(8 - (CMP_HALF - 1), nblk, stride=ratio), :]
    for mm in range(1, span):
        psl = psl + imp_ref[pl.ds(8 - (CMP_HALF - 1) + mm, nblk, stride=ratio), :]

    yield None
    n_win = WINDOW // QT
    st_parts, vt_parts = [], []

    def window_tile(w):
        kt = b - n_win + w
        tn_idx = jnp.where(kt >= 0, TN_WIN if w == 0 else n_win - w, TN_MASKED)
        st_parts.append(_dot_nt(kw_refs[w][0, 0], q) + tn_ref[0, tn_idx])
        vt_parts.append(vwt_refs[w][0, 0])

    def window_finish():
        st = jnp.concatenate(st_parts, axis=0)
        p = jnp.exp2(st - jnp.max(st, axis=0, keepdims=True))
        o_w = _dot(jnp.concatenate(vt_parts, axis=1), p.astype(BF16)) * (
            1.0 / jnp.sum(p, axis=0, keepdims=True))
        g = jax.nn.sigmoid(g_ref[0, 0])
        part_ref[...] = g[0:1] * o_c + g[2:3] * o_w

    window_pieces = [functools.partial(window_tile, w) for w in range(n_win + 1)] + [window_finish]

    ji = lax.broadcasted_iota(jnp.int32, (nblk, QT), 0)
    ii = lax.broadcasted_iota(jnp.int32, (nblk, QT), 1)
    cur = (qs + ii) // SEL_BLOCK
    jf = ji.astype(F32)
    forced = (ji == 0) | (ji == cur) | (ji == cur - 1)
    score0 = jnp.where(forced, -jnp.inf, jnp.where(ji <= cur, psl, -BIG))

    score, selb = score0, jnp.where(forced, 0.0, NEG)
    for _ in range(max(n_sel - N_FORCED, 0)):
        yield None
        mx = jnp.max(score, axis=0, keepdims=True)
        first = jnp.min(jnp.where(score == mx, jf, 1e9), axis=0, keepdims=True)
        pick = jf == first
        score, selb = jnp.where(pick, -jnp.inf, score), jnp.where(pick, 0.0, selb)
        if window_pieces:
            window_pieces.pop(0)()
    while window_pieces:
        window_pieces.pop(0)()
    mb_ref[0:nblk, :] = jnp.concatenate([selb] * NSA_HPG, axis=1)
    mb_ref[nblk:, :] = jnp.full((BLK_PER_KT, HQ), NEG, F32)

    m_ref[...] = jnp.full(m_ref.shape, NEG, F32)
    acc_ref[...] = jnp.zeros(acc_ref.shape, F32)

    def sel_logits(kt, valid):
        ktc = jnp.where(valid, kt, 0)
        st = _dot_nt(ks_ref[0, ktc], q)
        row0 = jnp.where(valid, ktc * BLK_PER_KT, nblk)
        mb = mb_ref[pl.ds(pl.multiple_of(row0, BLK_PER_KT), BLK_PER_KT), :]
        st = jnp.concatenate(
            [st[k * SEL_BLOCK:(k + 1) * SEL_BLOCK] + mb[k:k + 1] for k in range(BLK_PER_KT)], axis=0)
        return st, ktc

    def softmax_step(st, tmax, vt):
        nk = vt.shape[1]
        ones = jnp.where(lax.broadcasted_iota(jnp.int32, (V_ROWS - NSA_DK, nk), 0) == 0, 1.0, 0.0)
        vt = jnp.concatenate([vt, ones.astype(BF16)], axis=0)
        m_old = m_ref[...]
        m_new = jnp.maximum(m_old, tmax)
        alpha = jnp.exp2(m_old - m_new)
        p = jnp.exp2(st - m_new)
        acc_ref[...] = alpha * acc_ref[...] + _dot(vt, p.astype(BF16))
        m_ref[...] = m_new

    nt = b // SUB_PER_KT + 1
    n_far = jnp.maximum(nt - NEAR_TILES, 0)

    def far_logits(kt, buf_ref, slot):
        st, _ = sel_logits(kt, kt < n_far)
        buf_ref[...] = st
        tmax_ref[slot:slot + 1, :] = jnp.max(st, axis=0, keepdims=True)

    def far_first():
        far_logits(0, buf0_ref, 0)

    def far_pair(pi):
        k0 = 2 * pi
        far_logits(k0 + 1, buf1_ref, 1)
        softmax_step(buf0_ref[...], tmax_ref[0:1, :], vst_ref[0, k0])
        far_logits(k0 + 2, buf0_ref, 0)
        k1 = jnp.minimum(k0 + 1, jnp.maximum(n_far - 1, 0))
        softmax_step(buf1_ref[...], tmax_ref[1:2, :], vst_ref[0, k1])

    yield far_first, far_pair

    st_parts, vt_parts = [], []
    for w in range(NEAR_TILES):
        kt = nt - NEAR_TILES + w
        st, ktc = sel_logits(kt, kt >= 0)
        r = b - SUB_PER_KT * ktc
        parts = []
        for u in range(SUB_PER_KT):
            ru = r - u
            idx = jnp.where(ru < 0, TN_MASKED, jnp.where(ru >= N_NEAR, TN_ZERO, ru))
            parts.append(tn_ref[0, idx])
        st_parts.append(st + jnp.concatenate(parts, axis=0))
        vt_parts.append(vst_ref[0, ktc])
    yield None
    st = jnp.concatenate(st_parts, axis=0)
    softmax_step(st, jnp.max(st, axis=0, keepdims=True), jnp.concatenate(vt_parts, axis=1))
    acc = acc_ref[...]
    o_s = acc[0:NSA_DK] * (1.0 / acc[NSA_DK:NSA_DK + 1])
    total = part_ref[...] + jax.nn.sigmoid(g_ref[0, 0, 1:2]) * o_s
    pairs = [jnp.concatenate([total[:, h * QT:(h + 1) * QT], total[:, (h + 1) * QT:(h + 2) * QT]], axis=0).T
             for h in range(0, NSA_HPG, 2)]
    o_ref[...] = jnp.concatenate(pairs, axis=1)


GROUP_COLS = NSA_HPG * NSA_DK


def _nsa_kernel(*refs, n_scratch, aliased, **static):
    gp = GROUPS_PER_STEP
    refs = refs[1:] if aliased else refs
    n_all = gp * n_scratch
    ins, o_ref = refs[:len(refs) - n_all - 1], refs[len(refs) - n_all - 1]
    scratch = refs[len(refs) - n_all:]

    b = pl.program_id(1) + static["b_off"]
    n_far = jnp.maximum(b // SUB_PER_KT + 1 - NEAR_TILES, 0)
    programs = []
    for gi in range(gp):
        v = [r.at[pl.ds(gi, 1)] for r in ins]
        kw_refs, vwt_refs = v[5:5 + N_WIN_TILES], v[5 + N_WIN_TILES:5 + 2 * N_WIN_TILES]
        rest = v[5 + 2 * N_WIN_TILES:] + [o_ref.at[:, pl.ds(gi * GROUP_COLS, GROUP_COLS)]]
        programs.append(_nsa_group(*v[:5], kw_refs, vwt_refs, *rest,
                                   *scratch[gi * n_scratch:(gi + 1) * n_scratch], **static))
    loops = [None] * gp
    while any(lp is None for lp in loops):
        for gi, p in enumerate(programs):
            if loops[gi] is None:
                loops[gi] = next(p)

    @pl.when(n_far > 0)
    def _():
        for far_first, _ in loops:
            far_first()

    def in_turns(generators):
        active = list(generators)
        while active:
            for p in list(active):
                if next(p, StopIteration) is StopIteration:
                    active.remove(p)

    def far_body(pi, carry):
        for _, far_pair in loops:
            far_pair(pi)
        return carry

    lax.fori_loop(0, (n_far + 1) // 2, far_body, 0)
    in_turns(programs)


def _nsa_attend(qt, kc, vct, ks, vst, kw, vwt, tn, wt, gt, *, n_sel_blk):
    s_n, nq = qt.shape[:2]
    gp = GROUPS_PER_STEP
    assert kc.shape[1] >= 8 * nq and ks.shape[1] * SUB_PER_KT >= nq and s_n % gp == 0
    n_sel = min(N_SEL, n_sel_blk)
    n_seg = NSA_SEGMENTS if nq % (NSA_SEGMENTS * SUB_PER_KT) == 0 else 1
    nq_seg = nq // n_seg
    y = None
    for seg in range(n_seg):
        b_off = seg * nq_seg
        b_end = b_off + nq_seg
        ncp = min(_round_up(8 * b_end, LANES), kc.shape[1])
        nkt = b_end // SUB_PER_KT
        nblk = nkt * BLK_PER_KT
        imp_rows = max(8 + ncp, 8 + (SEL_BLOCK // CMP_STRIDE) * nblk)
        scratch = [(ncp + WT_ROWS, HQ), (imp_rows, QT), (nblk + BLK_PER_KT, HQ), (1, HQ), (V_ROWS, HQ),
                   (KT_SEL, HQ), (KT_SEL, HQ), (8, HQ), (NSA_DK, HQ)]
        aliased = y is not None
        kern = functools.partial(_nsa_kernel, n_scratch=len(scratch), aliased=aliased, ncp=ncp,
                                 nblk=nblk, n_sel=n_sel, b_off=b_off)
        lead = lambda a, n: pl.BlockSpec((gp, n) + a.shape[2:], lambda s, i: (s,) + (0,) * (a.ndim - 1),
                                         pipeline_mode=pl.Buffered(1))
        per_tile = lambda a: pl.BlockSpec((gp, 1) + a.shape[2:], lambda s, i: (s, i + b_off, 0, 0))
        win = lambda a, w: pl.BlockSpec(
            (gp, 1) + a.shape[2:],
            lambda s, i: (s, jnp.maximum(i + b_off - (N_WIN_TILES - 1) + w, 0), 0, 0))
        in_specs = ([per_tile(qt), lead(kc, ncp),
                     pl.BlockSpec((gp, NSA_DK, ncp), lambda s, i: (s, 0, 0), pipeline_mode=pl.Buffered(1)),
                     lead(ks, nkt), lead(vst, nkt)]
                    + [win(kw, w) for w in range(N_WIN_TILES)]
                    + [win(vwt, w) for w in range(N_WIN_TILES)]
                    + [lead(tn, tn.shape[1]), lead(wt, wt.shape[1]), per_tile(gt)])
        args = (qt, kc, vct, ks, vst, *([kw] * N_WIN_TILES), *([vwt] * N_WIN_TILES), tn, wt, gt)
        if aliased:
            in_specs = [pl.BlockSpec(memory_space=pl.ANY)] + in_specs
            args = (y,) + args
        y = pl.pallas_call(
            kern,
            grid=(s_n // gp, nq_seg),
            in_specs=in_specs,
            out_specs=pl.BlockSpec((QT, gp * GROUP_COLS), lambda s, i: (i + b_off, s)),
            out_shape=jax.ShapeDtypeStruct((nq * QT, s_n * GROUP_COLS), F32),
            scratch_shapes=[pltpu.VMEM(s, F32) for s in scratch] * gp,
            input_output_aliases={0: 0} if aliased else {},
            compiler_params=_cparams(("parallel", "arbitrary")),
            name="nsa_prompt",
        )(*args)
    return y


def _k_tiles(kv, tile):
    nt = kv.shape[0] // tile
    k = kv[..., :KV_COLS].reshape(nt, tile, NSA_G, NSA_DK)
    k = jnp.transpose(k, (2, 0, 1, 3))
    k = jnp.pad(k, ((0, 0), (0, 0), (0, 0), (0, LANES - NSA_DK))).astype(BF16)
    v = kv[..., KV_COLS:].reshape(nt, tile, NSA_G, NSA_DK)
    v = jnp.transpose(v, (2, 0, 3, 1)).astype(BF16)
    return k, v


def _prompt_attend(qt, gt, tok, ks, vst, kw, vwt, bias):
    nq = qt.shape[1]
    t = nq * QT
    near, cwin = bias
    ncp = max(_round_up(tok.shape[0] + 1, LANES), _round_up(8 * nq, LANES))
    kc, vct = _k_tiles(jnp.pad(tok, ((0, ncp - tok.shape[0]), (0, 0))), ncp)
    return _nsa_attend(qt, kc[:, 0], vct[:, 0], ks, vst, kw, vwt, near, cwin, gt,
                       n_sel_blk=t // SEL_BLOCK)


def _sample_bias_kernel(tab_ref, cb_ref, sb_ref, snb_ref, wb_ref, *, past, dseq, ns):
    rows = tab_ref.shape[0]
    far = tab_ref[:, REL_BUCKETS - 1:REL_BUCKETS]

    def tok_of(n):
        return lax.broadcasted_iota(jnp.int32, (rows, n), 0) % dseq

    def lane(n):
        return lax.broadcasted_iota(jnp.int32, (rows, n), 1)

    for idx in range(2):
        rho = LANES * (ns - 1 + idx) + lane(LANES)
        d = past + tok_of(LANES) - (CMP_STRIDE * (rho - 1) + CMP_LEN - 1)
        cb_ref[idx] = jnp.where(d >= 0, _rel_bias_rows(d, tab_ref) - far, 0.0)
    d = SP_KEYS + tok_of(SP_KEYS) - lane(SP_KEYS)
    sb_ref[...] = _rel_bias_rows(d, tab_ref) - far
    d = tok_of(LANES) - lane(LANES)
    snb_ref[...] = jnp.where(d >= 0, _rel_bias_rows(d, tab_ref) - far, NEG)
    d = WINDOW + tok_of(WINDOW + LANES) - lane(WINDOW + LANES)
    wb_ref[...] = jnp.where((d >= 0) & (d <= WINDOW), _rel_bias_rows(d, tab_ref), NEG)


def _sample_bias(rel_bias, past, dseq, ns):
    rows = NSA_HEADS * dseq
    tab = (rel_bias * LOG2E).reshape(REL_BUCKETS, NSA_G, NSA_HPG)
    tab = jnp.transpose(tab, (2, 1, 0))
    tab = jnp.repeat(tab.reshape(NSA_HEADS, 1, REL_BUCKETS), dseq, axis=1).reshape(rows, REL_BUCKETS)
    tab = jnp.pad(tab, ((0, 0), (0, LANES - REL_BUCKETS)))
    shapes = [(2, rows, LANES), (rows, SP_KEYS), (rows, LANES), (rows, WINDOW + LANES)]
    cb, sb, snb, wb = pl.pallas_call(
        functools.partial(_sample_bias_kernel, past=past, dseq=dseq, ns=ns),
        out_shape=[jax.ShapeDtypeStruct(s, F32) for s in shapes],
        compiler_params=pltpu.CompilerParams(vmem_limit_bytes=VMEM_LIMIT),
        name="bias_sample",
    )(tab)
    return cb, sb, snb, wb


def _sample_nsa_kernel(pt_ref, *refs, ns, past, dseq, n_sel, n_sel_blk):
    cmp_pages = refs[:SP_PP]
    slc_pages = refs[SP_PP:2 * SP_PP]
    (xn_ref, ksn_ref, win_ref, kwn_ref, q_ref, g_ref, w1_ref, pos_ref, pw_ref, w2_ref, e_ref,
     cb_ref, sb_ref, snb_ref, wb_ref, o_ref,
     s_ref, vc_ref, carry_ref, sel_ref, m_ref, l_ref, acc_ref, oc_ref, x_ref) = refs[2 * SP_PP:]
    j = pl.program_id(1)
    qbd = q_ref[0]
    rows = qbd.shape[0]
    ntile = ns + 1
    nblk_l = sel_ref.shape[0] * SP_BLKS

    def feat_major(refs_, c):
        return jnp.concatenate([r[0, c].reshape(KV_COLS, r.shape[-1]) for r in refs_], axis=1)

    def tokens_of(gather, prepare=None):
        out = []
        for c in range(2):
            if prepare is not None:
                prepare(c)
            xc = gather(c)
            n = xc.shape[0]
            row0 = lax.broadcasted_iota(jnp.int32, (n, KV_COLS), 0) == 0
            y = _dot(xc.astype(BF16), w1_ref[c])
            first = jnp.where(row0, carry_ref[c:c + 1, :], pltpu.roll(y[:, :KV_COLS], 1, 0))
            carry_ref[c:c + 1, :] = y[n - 1:n, :KV_COLS]
            pre = first + y[:, KV_COLS:] + _pos_bias(pos_ref, pw_ref, c)
            out.append(_dot(jax.nn.gelu(pre).astype(BF16), w2_ref[c]))
        return out

    def online_update(s, v):
        m_old = m_ref[...]
        m_new = jnp.maximum(m_old, jnp.max(s, axis=1, keepdims=True))
        alpha = jnp.exp2(m_old - m_new)
        p = jnp.exp2(s - m_new)
        l_ref[...] = alpha * l_ref[...] + jnp.sum(p, axis=1, keepdims=True)
        acc_ref[...] = alpha * acc_ref[...] + _dot_nt(p.astype(BF16), v)
        m_ref[...] = m_new

    @pl.when(j == 0)
    def _():
        carry_ref[...] = jnp.zeros(carry_ref.shape, F32)

    @pl.when(j < ns)
    def _():
        def to_rows(c):
            for p, r in enumerate(cmp_pages):
                for gp in range(2):
                    tile = r[0, c, 2 * gp:2 * gp + 2].reshape(LANES, PAGE_SIZE)
                    x_ref[2 * c + gp, p * PAGE_SIZE:(p + 1) * PAGE_SIZE, :] = tile.T

        def gather(c):
            return jnp.concatenate(
                [x_ref[2 * c + gp, pl.ds(l, SP_KEYS // CMP_STRIDE, stride=CMP_STRIDE), :]
                 for l in range(CMP_STRIDE) for gp in range(2)], axis=1)

        tok_k, tok_v = tokens_of(gather, to_rows)
        s_ref[j] = _dot_nt(qbd, tok_k.astype(BF16))
        vc_ref[j] = tok_v.astype(BF16)

    @pl.when(j == ns - 1)
    def _():
        tok_k, tok_v = tokens_of(lambda c: _gather_cv(xn_ref[0], c))
        zpad = jnp.zeros((LANES - tok_k.shape[0], KV_COLS), F32)
        s_ref[ns] = _dot_nt(qbd, jnp.concatenate([tok_k, zpad], axis=0).astype(BF16))
        vc_ref[ns] = jnp.concatenate([tok_v, zpad], axis=0).astype(BF16)

        tiles = []
        for t in range(ntile):
            st = s_ref[t]
            if t >= ns - 1:
                st = st + cb_ref[t - (ns - 1)]
            tiles.append(st)
        s = jnp.concatenate(tiles, axis=1)
        width = ntile * LANES
        rho = lax.broadcasted_iota(jnp.int32, (rows, width), 1)
        tq = lax.broadcasted_iota(jnp.int32, (rows, width), 0) % dseq
        n_cmp = (past + _round_up(dseq, SEL_BLOCK)) // CMP_STRIDE - (CMP_HALF - 1)
        vis = (rho >= 1) & (rho <= n_cmp) & (CMP_STRIDE * (rho - 1) + CMP_LEN - 1 <= past + tq)
        s = jnp.where(vis, s, NEG)
        m = jnp.max(s, axis=1, keepdims=True)
        m = jnp.where(m <= 0.1 * NEG, 0.0, m)
        p = jnp.exp2(s - m)
        rinv = 1.0 / jnp.maximum(jnp.sum(p, axis=1, keepdims=True), 1e-30)
        vc = jnp.concatenate([vc_ref[t] for t in range(ntile)], axis=0)
        oc_ref[...] = _dot(p.astype(BF16), vc) * rinv
        pn = p * rinv
        gt = NSA_G * dseq
        imp = pn[0:gt]
        for h in range(1, NSA_HPG):
            imp = imp + pn[h * gt:(h + 1) * gt]
        ratio = SEL_BLOCK // CMP_STRIDE
        span = ratio + CMP_HALF - 1
        ar = lax.broadcasted_iota(jnp.int32, (width, nblk_l), 0)
        ac = lax.broadcasted_iota(jnp.int32, (width, nblk_l), 1)
        band = jnp.where((ar >= ratio * ac) & (ar < ratio * ac + span), 1.0, 0.0).astype(BF16)
        i_hi = imp.astype(BF16)
        r1 = imp - i_hi.astype(F32)
        i_mid = r1.astype(BF16)
        i_lo = (r1 - i_mid.astype(F32)).astype(BF16)
        psl = _dot(i_hi, band) + _dot(i_mid, band) + _dot(i_lo, band)
        ji = lax.broadcasted_iota(jnp.int32, (gt, nblk_l), 1)
        tq2 = lax.broadcasted_iota(jnp.int32, (gt, nblk_l), 0) % dseq
        cur = (past + tq2) // SEL_BLOCK
        jf = ji.astype(F32)
        forced = (ji == 0) | (ji == cur) | (ji == cur - 1)
        score0 = jnp.where(forced, -jnp.inf, jnp.where(ji <= cur, psl, -BIG))
        score0 = jnp.where(ji < n_sel_blk, score0, -jnp.inf)

        def pick_one(_, carry):
            score, sel = carry
            mx = jnp.max(score, axis=1, keepdims=True)
            first = jnp.min(jnp.where(score == mx, jf, 1e9), axis=1, keepdims=True)
            pick = jf == first
            return jnp.where(pick, -jnp.inf, score), jnp.where(pick, 1.0, sel)

        _, sel = lax.fori_loop(0, max(n_sel - N_FORCED, 0), pick_one,
                               (score0, jnp.where(forced, 1.0, 0.0)))
        for t in range(sel_ref.shape[0]):
            piece = sel[:, t * SP_BLKS:(t + 1) * SP_BLKS]
            piece = jnp.concatenate([piece, jnp.zeros((gt, LANES - SP_BLKS), F32)], axis=1)
            sel_ref[t] = jnp.concatenate([piece] * NSA_HPG, axis=0).astype(BF16)
        m_ref[...] = jnp.full(m_ref.shape, NEG, F32)
        l_ref[...] = jnp.zeros(l_ref.shape, F32)
        acc_ref[...] = jnp.zeros(acc_ref.shape, F32)

    @pl.when(j >= ns)
    def _():
        j2 = j - ns
        s = _dot(qbd, feat_major(slc_pages, 0).astype(BF16))
        chosen = _dot(sel_ref[j2], e_ref[...])
        s = s + (chosen - 1.0) * (-NEG)
        s = s + jnp.where(j2 == ns - 1, sb_ref[...], 0.0)
        online_update(s, feat_major(slc_pages, 1).astype(BF16))

    @pl.when(j == 2 * ns - 1)
    def _():
        online_update(_dot(qbd, ksn_ref[0, 0].astype(BF16)) + snb_ref[...],
                      ksn_ref[0, 1].astype(BF16))
        o_s = acc_ref[...] * (1.0 / l_ref[...])
        kw = jnp.concatenate([feat_major([win_ref], 0), kwn_ref[0, 0]], axis=1)
        vw = jnp.concatenate([feat_major([win_ref], 1), kwn_ref[0, 1]], axis=1)
        s = _dot(qbd, kw.astype(BF16)) + wb_ref[...]
        p = jnp.exp2(s - jnp.max(s, axis=1, keepdims=True))
        o_w = _dot_nt(p.astype(BF16), vw.astype(BF16)) * (1.0 / jnp.sum(p, axis=1, keepdims=True))
        g = jax.nn.sigmoid(g_ref[0])
        o_ref[0] = g[:, 0:1] * oc_ref[...] + g[:, 1:2] * o_s + g[:, 2:3] * o_w


def _sample_attend(q, gates, kv_c, kv_s, kv_w, cache_cmp, cache_slc, win_buf, page_table, cw,
                   rel_bias):
    bsz, dseq = q.shape[:2]
    n_pages = page_table.shape[1]
    past = n_pages * PAGE_SIZE
    assert n_pages % SP_PP == 0 and dseq <= SEL_BLOCK and win_buf.shape[1] == WINDOW
    feat_major = lambda a: jnp.transpose(a, (0, 2, 3, 4, 1))
    cache_cmp, cache_slc, win_buf = feat_major(cache_cmp), feat_major(cache_slc), feat_major(win_buf)
    ns = n_pages // SP_PP
    rows = NSA_HEADS * dseq
    n_sel_blk = past // SEL_BLOCK + 1
    n_sel = min(N_SEL, n_sel_blk)
    nsel_tiles = _round_up(-(-n_sel_blk // SP_BLKS), LANES // SP_BLKS)
    w1big, pos, pw, w2big = cw
    cb, sb, snb, wb = _sample_bias(rel_bias, past, dseq, ns)

    qr = jnp.transpose((q * Q_SCALE).reshape(bsz, dseq, NSA_G, NSA_HPG, NSA_DK), (0, 3, 2, 1, 4))
    qbd = jnp.einsum('bhgtd,gk->bhgtkd', qr, jnp.eye(NSA_G, dtype=F32))
    qbd = qbd.reshape(bsz, rows, KV_COLS).astype(BF16)
    gr = gates[..., :N_BRANCH * NSA_HEADS].reshape(bsz, dseq, NSA_G, NSA_HPG, N_BRANCH)
    gr = jnp.transpose(gr, (0, 3, 2, 1, 4)).reshape(bsz, rows, N_BRANCH)
    gr = jnp.pad(gr, ((0, 0), (0, 0), (0, LANES - N_BRANCH)))
    pad_rows = lambda a, n: jnp.pad(a, ((0, 0), (0, n - a.shape[1]), (0, 0)))
    xn = pad_rows(kv_c, SEL_BLOCK).reshape(bsz, SEL_BLOCK // CMP_STRIDE, CMP_STRIDE * ROW_W)
    xn = pad_rows(xn, 8)
    new_feat_major = lambda a: jnp.transpose(
        pad_rows(a, LANES).reshape(bsz, LANES, 2, KV_COLS), (0, 2, 3, 1))
    ksn = new_feat_major(kv_s)
    kwn = new_feat_major(kv_w)
    expand = (jnp.arange(LANES)[:, None] == (jnp.arange(SP_KEYS) // SEL_BLOCK)[None, :]).astype(BF16)
    page_block = (1, 2, NSA_G, NSA_DK, PAGE_SIZE)

    def page_map(k, phase):
        if phase == 0:
            return lambda b, j, pt: (pt[b, jnp.minimum(j, ns - 1) * SP_PP + k], 0, 0, 0, 0)
        return lambda b, j, pt: (pt[b, jnp.maximum(j - ns, 0) * SP_PP + k], 0, 0, 0, 0)

    per_b = lambda a: pl.BlockSpec((1,) + a.shape[1:], lambda b, j, pt: (b,) + (0,) * (a.ndim - 1))
    const = lambda a: pl.BlockSpec(a.shape, lambda b, j, pt: (0,) * a.ndim)
    in_specs = ([pl.BlockSpec(page_block, page_map(k, 0)) for k in range(SP_PP)]
                + [pl.BlockSpec(page_block, page_map(k, 1)) for k in range(SP_PP)]
                + [per_b(xn), per_b(ksn), per_b(win_buf), per_b(kwn), per_b(qbd), per_b(gr),
                   pl.BlockSpec(w1big.shape, lambda b, j, pt: (0, 0, 0), pipeline_mode=pl.Buffered(1)),
                   const(pos), const(pw), const(w2big), const(expand),
                   const(cb), const(sb), const(snb), const(wb)])
    grid_spec = pltpu.PrefetchScalarGridSpec(
        num_scalar_prefetch=1, grid=(bsz, 2 * ns), in_specs=in_specs,
        out_specs=pl.BlockSpec((1, rows, KV_COLS), lambda b, j, pt: (b, 0, 0)),
        scratch_shapes=[pltpu.VMEM((ns + 1, rows, LANES), F32),
                        pltpu.VMEM((ns + 1, LANES, KV_COLS), BF16),
                        pltpu.VMEM((8, KV_COLS), F32),
                        pltpu.VMEM((nsel_tiles, rows, LANES), BF16),
                        pltpu.VMEM((rows, 1), F32),
                        pltpu.VMEM((rows, 1), F32),
                        pltpu.VMEM((rows, KV_COLS), F32),
                        pltpu.VMEM((rows, KV_COLS), F32),
                        pltpu.VMEM((ROW_W // LANES, SP_KEYS, LANES), F32)])
    out = pl.pallas_call(
        functools.partial(_sample_nsa_kernel, ns=ns, past=past, dseq=dseq, n_sel=n_sel,
                          n_sel_blk=n_sel_blk),
        grid_spec=grid_spec,
        out_shape=jax.ShapeDtypeStruct((bsz, rows, KV_COLS), F32),
        compiler_params=_cparams(("parallel", "arbitrary")),
        name="nsa_sample",
    )(page_table, *([cache_cmp] * SP_PP), *([cache_slc] * SP_PP), xn, ksn, win_buf, kwn, qbd, gr,
      w1big, pos, pw, w2big, expand, cb, sb, snb, wb)
    o = out.reshape(bsz, NSA_HPG, NSA_G, dseq, NSA_G, NSA_DK)
    o = jnp.einsum('bhgtge->btghe', o)
    return o.reshape(bsz * dseq, NSA_WIDTH)


def _out_proj_kernel(ycm_ref, ynsa_ref, h_ref, gt_ref, gpost_ref, w1_ref, w2_ref, o_ref):
    out = _dot(ycm_ref[...].astype(BF16), w1_ref[...]) + _dot(ynsa_ref[...].astype(BF16), w2_ref[...])
    o_ref[...] = h_ref[...] + gt_ref[...] * _rms(out, gpost_ref[...])


def _out_proj(ycm, ynsa, h, gate, g_post, w1, w2, tm):
    r, d = h.shape
    return pl.pallas_call(
        _out_proj_kernel,
        grid=(r // tm,),
        in_specs=[pl.BlockSpec((tm, CM_WIDTH), lambda i: (i, 0)),
                  pl.BlockSpec((tm, NSA_WIDTH), lambda i: (i, 0)),
                  pl.BlockSpec((tm, d), lambda i: (i, 0)),
                  _row_spec(gate, tm),
                  pl.BlockSpec((1, d), lambda i: (0, 0)),
                  pl.BlockSpec(w1.shape, lambda i: (0, 0)),
                  pl.BlockSpec(w2.shape, lambda i: (0, 0))],
        out_specs=pl.BlockSpec((tm, d), lambda i: (i, 0)),
        out_shape=jax.ShapeDtypeStruct((r, d), F32),
        compiler_params=_cparams(("parallel",)),
        name="out_proj",
    )(ycm, ynsa, h, gate, g_post, w1, w2)


def _row_tile(r, cap):
    tm = min(r, cap)
    while r % tm:
        tm //= 2
    return tm


def _layer(x, mod, per_row, weights, in_proj, mixer):
    r, d = x.shape
    (norm_pre, norm_post, ffn, w_out1, w_out2) = weights
    if per_row is None:
        mrow = lambda i, j: mod[0, i, j].reshape(1, d)
    else:
        mrow = lambda i, j: jnp.repeat(mod[:, i, j], per_row, axis=0)
    tm = _row_tile(r, 512)
    tm_ffn = _row_tile(r, FFN_TM)
    tf = _row_tile(ffn[0][0].shape[1], FFN_TF)
    h = _ffn(x, mrow(0, 0), mrow(0, 1), mrow(0, 2), norm_pre[0:1], norm_post[0:1], *ffn[0],
             res_w=0.5, tm=tm_ffn, tf=tf)
    mixed_cm, mixed_nsa, state = mixer(*in_proj(h, mrow(1, 0), mrow(1, 1), norm_pre[1:2]))
    h = _out_proj(mixed_cm, mixed_nsa, h, mrow(1, 2), norm_post[1:2], w_out1, w_out2, tm)
    h = _ffn(h, mrow(2, 0), mrow(2, 1), mrow(2, 2), norm_pre[2:3], norm_post[2:3], *ffn[1],
             res_w=0.5, tm=tm_ffn, tf=tf)
    return h, state


def kernel(x_prompt, x_sample, cache_cmp_kv, cache_slc_kv, state_win_kv, page_table, c_prompt,
           c_sample, w_mod, b_mod, norm_pre, norm_post, ffn_w_gate, ffn_w_up, ffn_w_down, w_in,
           w_out, cm_norm, cm_ws, cm_bs, phi_pos, phi_w1, phi_w2, rel_bias):
    depth = w_mod.shape[0]
    assert depth == 1
    bp, seq, d = x_prompt.shape
    bs, dseq, _ = x_sample.shape
    assert bp == 1 and seq % KT_SEL == 0 and seq >= WINDOW
    l = 0

    c_all = jnp.concatenate([c_prompt, c_sample], axis=0)
    mpad = _round_up(c_all.shape[0], 8)
    mod = _mod_proj(jnp.pad(c_all, ((0, mpad - c_all.shape[0]), (0, 0))), w_mod[l], b_mod[l])
    mod = mod[:bp + bs].reshape(bp + bs, 3, 3, d)

    ffn = [(ffn_w_gate[l, i].astype(BF16), ffn_w_up[l, i].astype(BF16), ffn_w_down[l, i].astype(BF16))
           for i in range(2)]
    w_in_p = jnp.pad(w_in[l], ((0, 0), (0, sum(IN_SEGS) - w_in.shape[2]))).astype(BF16)
    w_rows, w_feat = _in_proj_prompt_weights(w_in[l])
    w_o = w_out[l].astype(BF16)
    weights = (norm_pre[l], norm_post[l], ffn, w_o[:CM_WIDTH], w_o[CM_WIDTH:])
    cw = _compress_weights(phi_pos[l], phi_w1[l], phi_w2[l])
    prompt_bias = _prompt_bias_tiles(rel_bias)

    def prompt_in_proj(h, shift, scale, g_pre):
        return _in_proj_prompt(h, shift, scale, g_pre, w_rows, w_feat)

    def prompt_mixer(u, v, kv_c, qt, gt, ks, vst, kw, vwt, kvt):
        y_cm, _ = _chunk_mix(u, v, cm_norm[l], cm_ws[l], cm_bs[l], CHUNK)
        tok = _compress_tokens(kv_c[None], cw)[0]
        y_nsa = _prompt_attend(qt, gt, tok, ks, vst, kw, vwt, prompt_bias)
        return y_cm, y_nsa, kvt

    assert seq % KT_SEL == 0
    hp, kvt = _layer(x_prompt.reshape(seq, d), mod[:bp], None, weights, prompt_in_proj, prompt_mixer)
    kvt = jnp.transpose(kvt.reshape(N_BRANCH, 2, NSA_G, NSA_DK, seq), (0, 4, 1, 2, 3))
    pc, ps, pw = kvt[0], kvt[1], kvt[2]

    rows = _round_up(dseq, 16)

    def sample_mixer(u, v, q, kv_c, kv_s, kv_w, gates):
        per_seq = lambda a: a.reshape(bs, dseq, -1)
        pad_rows = lambda a, n: jnp.pad(per_seq(a), ((0, 0), (0, n - dseq), (0, 0)))
        y_cm, vn = _chunk_mix(pad_rows(u, rows).reshape(bs * rows, CM_WIDTH),
                              pad_rows(v, rows).reshape(bs * rows, CM_WIDTH),
                              cm_norm[l], cm_ws[l], cm_bs[l], rows)
        y_cm = y_cm.reshape(bs, rows, CM_WIDTH)[:, :dseq].reshape(bs * dseq, CM_WIDTH)
        vn = vn.reshape(bs, rows, CM_WIDTH)[:, :dseq]
        y_nsa = _sample_attend(per_seq(q), per_seq(gates), per_seq(kv_c), per_seq(kv_s), per_seq(kv_w),
                               cache_cmp_kv[l], cache_slc_kv[l], state_win_kv[l], page_table, cw,
                               rel_bias)
        return y_cm, y_nsa, (kv_c, kv_s, kv_w, vn)

    def sample_in_proj(h, shift, scale, g_pre):
        return _in_proj(h, shift, scale, g_pre, w_in_p, _row_tile(h.shape[0], 256))

    hs, (sc, ss, sw, sv) = _layer(x_sample.reshape(bs * dseq, d), mod[bp:], dseq, weights,
                                  sample_in_proj, sample_mixer)

    kvshape = lambda a, b, t: a.reshape(1, b, t, 2, NSA_G, NSA_DK)
    wp = min(WINDOW, seq)
    win_s = jnp.concatenate([state_win_kv[l].reshape(bs, -1, ROW_W), sw.reshape(bs, dseq, ROW_W)],
                            axis=1)[:, dseq:]
    return (hp.reshape(bp, seq, d), hs.reshape(bs, dseq, d),
            kvshape(pc, bp, seq), kvshape(sc, bs, dseq),
            kvshape(ps, bp, seq), kvshape(ss, bs, dseq),
            kvshape(pw[seq - wp:], bp, wp), kvshape(win_s, bs, win_s.shape[1]),
            sv.reshape(1, bs, dseq, CM_HEADS, CM_HEAD_DIM))
```

```python
import functools
import math

import jax
import jax.numpy as jnp
from jax import lax
from jax.experimental import pallas as pl
from jax.experimental.pallas import tpu as pltpu

F32 = jnp.float32
BF16 = jnp.bfloat16

CM_HEADS = 8
CM_HEAD_DIM = 128
CM_WIDTH = CM_HEADS * CM_HEAD_DIM
CHUNK = 128
NSA_HEADS = 16
NSA_G = 4
NSA_HPG = 4
NSA_DK = 64
NSA_WIDTH = NSA_HEADS * NSA_DK
KV_COLS = NSA_G * NSA_DK
ROW_W = 2 * KV_COLS
CMP_LEN = 32
CMP_STRIDE = 16
CMP_HALF = CMP_LEN // CMP_STRIDE
SEL_BLOCK = 64
N_SEL = 16
N_FORCED = 3
WINDOW = 512
N_BRANCH = 3
BIG = 1e4
REL_BUCKETS = 32
EPS = 1e-6
PAGE_SIZE = 128

LOG2E = 1.4426950408889634
Q_SCALE = NSA_DK ** -0.5 * LOG2E
NEG = -1e30
LANES = 128
QT = 128
HQ = NSA_HPG * QT
KT_SEL = 512
BLK_PER_KT = KT_SEL // SEL_BLOCK
SUB_PER_KT = KT_SEL // QT
N_NEAR = 8
NEAR_TILES = 3
V_ROWS = NSA_DK + 16
TN_WIN, TN_MASKED, TN_ZERO, TN_COUNT = 8, 9, 10, 11
WT_LEAD = 64
WT_ROWS = 200
REL_THRESHOLDS = (21, 27, 35, 46, 59, 77, 99, 128, 166, 216, 280, 363, 470, 609, 790)
VMEM_LIMIT = 56 * 1024 * 1024
SP_PP = CMP_STRIDE
SP_KEYS = SP_PP * PAGE_SIZE
SP_BLKS = SP_KEYS // SEL_BLOCK


def _cparams(sem):
    return pltpu.CompilerParams(dimension_semantics=sem, vmem_limit_bytes=VMEM_LIMIT)


def _dot(a, b):
    return jnp.dot(a, b, preferred_element_type=F32)


def _dot_nt(a, b):
    return lax.dot_general(a, b, (((1,), (1,)), ((), ())), preferred_element_type=F32)


def _split_bf16(x):
    hi = x.astype(BF16)
    lo = (x - hi.astype(F32)).astype(BF16)
    return hi, lo


def _dot_f32(a, b):
    a_hi, a_lo = _split_bf16(a)
    b_hi, b_lo = _split_bf16(b)
    return _dot(a_hi, b_hi) + _dot(a_lo, b_hi) + _dot(a_hi, b_lo)


def _rms(x, g):
    return x * lax.rsqrt(jnp.mean(x * x, axis=-1, keepdims=True) + EPS) * g


def _row_spec(arr, tm):
    d = arr.shape[-1]
    if arr.shape[0] == 1:
        return pl.BlockSpec((1, d), lambda *idx: (0, 0))
    return pl.BlockSpec((tm, d), lambda *idx: (idx[0], 0))


def _round_up(x, m):
    return -(-x // m) * m


def _mod_kernel(c_ref, w_ref, b_ref, o_ref):
    c = c_ref[...]
    s = c * jax.nn.sigmoid(c)
    o_ref[...] = _dot_f32(s, w_ref[...]) + b_ref[...]


def _mod_proj(c, w_mod, b_mod):
    m, d = c.shape
    n = w_mod.shape[1]
    tn = 512
    return pl.pallas_call(
        _mod_kernel,
        grid=(n // tn,),
        in_specs=[pl.BlockSpec((m, d), lambda j: (0, 0)),
                  pl.BlockSpec((d, tn), lambda j: (0, j)),
                  pl.BlockSpec((1, tn), lambda j: (0, j))],
        out_specs=pl.BlockSpec((m, tn), lambda j: (0, j)),
        out_shape=jax.ShapeDtypeStruct((m, n), F32),
        compiler_params=_cparams(("arbitrary",)),
        name="mod_proj",
    )(c, w_mod, b_mod.reshape(1, n))


def _ffn_kernel(x_ref, sh_ref, sc_ref, gt_ref, gpre_ref, gpost_ref, wg_ref, wu_ref, wd_ref,
                o_ref, a_ref, acc_ref, *, res_w, nf):
    f = pl.program_id(1)

    @pl.when(f == 0)
    def _():
        y = _rms(x_ref[...], gpre_ref[...])
        a_ref[...] = (y * (1.0 + sc_ref[...]) + sh_ref[...]).astype(BF16)
        acc_ref[...] = jnp.zeros_like(acc_ref)

    a = a_ref[...]
    h = _dot(a, wg_ref[...])
    u = _dot(a, wu_ref[...])
    act = (h * jax.nn.sigmoid(h) * u).astype(BF16)
    acc_ref[...] += _dot(act, wd_ref[...])

    @pl.when(f == nf - 1)
    def _():
        o_ref[...] = x_ref[...] + res_w * gt_ref[...] * _rms(acc_ref[...], gpost_ref[...])


FFN_TM = 512
FFN_TF = 512


def _ffn(x, shift, scale, gate, g_pre, g_post, wg, wu, wd, res_w, tm, tf):
    r, d = x.shape
    fdim = wg.shape[1]
    nf = fdim // tf
    return pl.pallas_call(
        functools.partial(_ffn_kernel, res_w=res_w, nf=nf),
        grid=(r // tm, nf),
        in_specs=[pl.BlockSpec((tm, d), lambda i, f: (i, 0)),
                  _row_spec(shift, tm), _row_spec(scale, tm), _row_spec(gate, tm),
                  pl.BlockSpec((1, d), lambda i, f: (0, 0)),
                  pl.BlockSpec((1, d), lambda i, f: (0, 0)),
                  pl.BlockSpec((d, tf), lambda i, f: (0, f)),
                  pl.BlockSpec((d, tf), lambda i, f: (0, f)),
                  pl.BlockSpec((tf, d), lambda i, f: (f, 0))],
        out_specs=pl.BlockSpec((tm, d), lambda i, f: (i, 0)),
        out_shape=jax.ShapeDtypeStruct((r, d), F32),
        scratch_shapes=[pltpu.VMEM((tm, d), BF16), pltpu.VMEM((tm, d), F32)],
        compiler_params=_cparams(("parallel", "arbitrary")),
        name="ffn",
    )(x, shift, scale, gate, g_pre, g_post, wg, wu, wd)


IN_SEGS = (CM_WIDTH, CM_WIDTH, NSA_WIDTH, 2 * KV_COLS, 2 * KV_COLS, 2 * KV_COLS, LANES)


def _in_proj_kernel(x_ref, sh_ref, sc_ref, gpre_ref, w_ref, *o_refs):
    y = _rms(x_ref[...], gpre_ref[...])
    a = (y * (1.0 + sc_ref[...]) + sh_ref[...]).astype(BF16)
    z = _dot(a, w_ref[...])
    off = 0
    for o_ref, width in zip(o_refs, IN_SEGS):
        o_ref[...] = z[:, off:off + width]
        off += width


def _in_proj(x, shift, scale, g_pre, w, tm):
    r, d = x.shape
    n = w.shape[1]
    return pl.pallas_call(
        _in_proj_kernel,
        grid=(r // tm,),
        in_specs=[pl.BlockSpec((tm, d), lambda i: (i, 0)),
                  _row_spec(shift, tm), _row_spec(scale, tm),
                  pl.BlockSpec((1, d), lambda i: (0, 0)),
                  pl.BlockSpec((d, n), lambda i: (0, 0), pipeline_mode=pl.Buffered(1))],
        out_specs=[pl.BlockSpec((tm, s), lambda i: (i, 0)) for s in IN_SEGS],
        out_shape=[jax.ShapeDtypeStruct((r, s), F32) for s in IN_SEGS],
        compiler_params=_cparams(("parallel",)),
        name="in_proj",
    )(x, shift, scale, g_pre, w)


IN_P_TM = 2 * QT
IN_P_QW = NSA_HEADS * LANES
IN_P_KW = NSA_G * LANES
IN_P_ROWS_F = N_BRANCH * ROW_W
IN_P_GATE_ROWS = 64


def _in_proj_prompt_kernel(x_ref, sh_ref, sc_ref, gpre_ref, wr_ref, wf_ref, u_ref, v_ref, kvc_ref,
                           qt_ref, gt_ref, ks_ref, vst_ref, kw_ref, vwt_ref, kvt_ref):
    y = _rms(x_ref[...], gpre_ref[...])
    a = (y * (1.0 + sc_ref[...]) + sh_ref[...]).astype(BF16)
    z = _dot(a, wr_ref[...])
    zt = _dot_nt(wf_ref[...], a)
    nt = IN_P_TM // QT
    tile = lambda t: slice(t * QT, (t + 1) * QT)
    u_ref[...] = z[:, 0:CM_WIDTH]
    v_ref[...] = z[:, CM_WIDTH:2 * CM_WIDTH]
    off = 2 * CM_WIDTH
    for t in range(nt):
        for gh in range(NSA_HEADS):
            g, h = divmod(gh, NSA_HPG)
            qt_ref[g, t, h * QT:(h + 1) * QT, :] = z[tile(t), off + gh * LANES:off + (gh + 1) * LANES].astype(BF16)
    off += IN_P_QW
    kvc_ref[...] = z[:, off:off + ROW_W]
    off += ROW_W
    for g in range(NSA_G):
        ks_ref[g, 0] = z[:, off + g * LANES:off + (g + 1) * LANES].astype(BF16)
    off += IN_P_KW
    for t in range(nt):
        for g in range(NSA_G):
            kw_ref[g, t] = z[tile(t), off + g * LANES:off + (g + 1) * LANES].astype(BF16)
    kvt_ref[...] = zt[0:IN_P_ROWS_F]
    v_rows = lambda branch, g: slice(branch * ROW_W + KV_COLS + g * NSA_DK,
                                     branch * ROW_W + KV_COLS + (g + 1) * NSA_DK)
    for g in range(NSA_G):
        vst_ref[g, 0] = zt[v_rows(1, g), :].astype(BF16)
        for t in range(nt):
            vwt_ref[g, t] = zt[v_rows(2, g), tile(t)].astype(BF16)
            gt_ref[g, t, N_BRANCH:, :] = jnp.zeros((8 - N_BRANCH, HQ), F32)
            for br in range(N_BRANCH):
                for h in range(NSA_HPG):
                    row = IN_P_ROWS_F + (g * N_BRANCH + br) * NSA_HPG + h
                    gt_ref[g, t, br:br + 1, h * QT:(h + 1) * QT] = zt[row:row + 1, tile(t)]


def _in_proj_prompt_weights(w):
    d = w.shape[0]
    q0 = 2 * CM_WIDTH
    k0 = q0 + NSA_WIDTH
    pad_lanes = lambda x: jnp.pad(x, ((0, 0), (0, 0), (0, LANES - NSA_DK))).reshape(d, -1)
    wq = pad_lanes(w[:, q0:k0].reshape(d, NSA_HEADS, NSA_DK) * Q_SCALE)
    k_of = lambda branch: pad_lanes(
        w[:, k0 + branch * ROW_W:k0 + branch * ROW_W + KV_COLS].reshape(d, NSA_G, NSA_DK))
    w_rows = jnp.concatenate([w[:, :q0], wq, w[:, k0:k0 + ROW_W], k_of(1), k_of(2)], axis=1).astype(BF16)
    g0 = k0 + N_BRANCH * ROW_W
    wg = jnp.transpose(w[:, g0:g0 + N_BRANCH * NSA_HEADS].reshape(d, NSA_G, NSA_HPG, N_BRANCH),
                       (0, 1, 3, 2)).reshape(d, N_BRANCH * NSA_HEADS)
    wg = jnp.pad(wg, ((0, 0), (0, IN_P_GATE_ROWS - N_BRANCH * NSA_HEADS)))
    w_feat = jnp.concatenate([w[:, k0:g0], wg], axis=1).T.astype(BF16)
    return w_rows, w_feat


def _in_proj_prompt(x, shift, scale, g_pre, w_rows, w_feat):
    t, d = x.shape
    tm = IN_P_TM
    nq = t // QT
    per = KT_SEL // tm
    const = lambda a: pl.BlockSpec(a.shape, lambda i: (0, 0), pipeline_mode=pl.Buffered(1))
    rows = lambda n: pl.BlockSpec((tm, n), lambda i: (i, 0))
    qtile = lambda a: pl.BlockSpec((NSA_G, tm // QT) + a[2:], lambda i: (0, i, 0, 0))
    shapes = [((t, CM_WIDTH), F32), ((t, CM_WIDTH), F32), ((t, ROW_W), F32),
              ((NSA_G, nq, HQ, LANES), BF16), ((NSA_G, nq, 8, HQ), F32),
              ((NSA_G, t // KT_SEL, KT_SEL, LANES), BF16), ((NSA_G, t // KT_SEL, NSA_DK, KT_SEL), BF16),
              ((NSA_G, nq, QT, LANES), BF16), ((NSA_G, nq, NSA_DK, QT), BF16),
              ((IN_P_ROWS_F, t), F32)]
    out_specs = [rows(CM_WIDTH), rows(CM_WIDTH), rows(ROW_W),
                 qtile(shapes[3][0]), qtile(shapes[4][0]),
                 pl.BlockSpec((NSA_G, 1, tm, LANES), lambda i: (0, i // per, i % per, 0)),
                 pl.BlockSpec((NSA_G, 1, NSA_DK, tm), lambda i: (0, i // per, 0, i % per)),
                 qtile(shapes[7][0]), qtile(shapes[8][0]),
                 pl.BlockSpec((IN_P_ROWS_F, tm), lambda i: (0, i))]
    return pl.pallas_call(
        _in_proj_prompt_kernel,
        grid=(t // tm,),
        in_specs=[rows(d), _row_spec(shift, tm), _row_spec(scale, tm),
                  pl.BlockSpec((1, d), lambda i: (0, 0)), const(w_rows), const(w_feat)],
        out_specs=out_specs,
        out_shape=[jax.ShapeDtypeStruct(s, dt) for s, dt in shapes],
        compiler_params=_cparams(("parallel",)),
        name="in_proj_prompt",
    )(x, shift, scale, g_pre, w_rows, w_feat)


def _chunk_mix_kernel(u_ref, v_ref, nrm_ref, ws_ref, bst_ref, y_ref, vn_ref, *, chunk):
    ri = lax.broadcasted_iota(jnp.int32, (chunk, chunk), 0)
    ci = lax.broadcasted_iota(jnp.int32, (chunk, chunk), 1)
    lower = ci <= ri
    for h in range(CM_HEADS):
        sl = slice(h * CM_HEAD_DIM, (h + 1) * CM_HEAD_DIM)
        vf = jax.nn.gelu(v_ref[:, sl])
        mu = jnp.mean(vf, axis=-1, keepdims=True)
        var = jnp.mean(jnp.square(vf - mu), axis=-1, keepdims=True)
        vn = (vf - mu) * lax.rsqrt(var + EPS) * nrm_ref[:, sl]
        vn_ref[:, sl] = vn
        w = jnp.where(lower, ws_ref[h], 0.0).astype(BF16)
        s = _dot(w, vn.astype(BF16)) + bst_ref[:, h:h + 1]
        y_ref[:, sl] = jax.nn.gelu(u_ref[:, sl]) * s


def _chunk_mix(u, v, cm_norm, cm_ws, cm_bs, chunk):
    r = u.shape[0]
    ws = cm_ws[:, :chunk, :chunk]
    bst = cm_bs[:, :chunk].T
    return pl.pallas_call(
        functools.partial(_chunk_mix_kernel, chunk=chunk),
        grid=(r // chunk,),
        in_specs=[pl.BlockSpec((chunk, CM_WIDTH), lambda i: (i, 0)),
                  pl.BlockSpec((chunk, CM_WIDTH), lambda i: (i, 0)),
                  pl.BlockSpec((1, CM_WIDTH), lambda i: (0, 0)),
                  pl.BlockSpec((CM_HEADS, chunk, chunk), lambda i: (0, 0, 0)),
                  pl.BlockSpec((chunk, CM_HEADS), lambda i: (0, 0))],
        out_specs=[pl.BlockSpec((chunk, CM_WIDTH), lambda i: (i, 0)),
                   pl.BlockSpec((chunk, CM_WIDTH), lambda i: (i, 0))],
        out_shape=[jax.ShapeDtypeStruct((r, CM_WIDTH), F32),
                   jax.ShapeDtypeStruct((r, CM_WIDTH), F32)],
        compiler_params=_cparams(("parallel",)),
        name="chunk_mix",
    )(u, v, cm_norm.reshape(1, CM_WIDTH), ws, bst)


def _gather_cv(x, c):
    return jnp.concatenate(
        [x[:, l * ROW_W + c * KV_COLS:l * ROW_W + (c + 1) * KV_COLS] for l in range(CMP_STRIDE)],
        axis=1)


def _pos_bias(pos_ref, pw_ref, c):
    posb = _dot_f32(pos_ref[c], pw_ref[c])[0:1]
    return jnp.concatenate([posb] * NSA_G, axis=1)


def _compress_weights(phi_pos, phi_w1, phi_w2):
    eye = jnp.eye(NSA_G, dtype=F32)
    w1 = phi_w1.reshape(CMP_HALF, CMP_STRIDE, 2, NSA_DK, NSA_DK)
    w1big = jnp.einsum('mlcde,gh->clgdmhe', w1, eye).reshape(
        2, CMP_STRIDE * KV_COLS, CMP_HALF * KV_COLS).astype(BF16)
    w2big = jnp.einsum('cde,gh->cgdhe', phi_w2, eye).reshape(2, KV_COLS, KV_COLS).astype(BF16)
    pos = jnp.transpose(phi_pos, (1, 0, 2)).reshape(2, 1, CMP_LEN * NSA_DK)
    pos = jnp.broadcast_to(pos, (2, 8, CMP_LEN * NSA_DK))
    pw = jnp.transpose(phi_w1, (1, 0, 2, 3)).reshape(2, CMP_LEN * NSA_DK, NSA_DK)
    return w1big, pos, pw, w2big


def _compress_kernel(x_ref, xn_ref, w1_ref, pos_ref, pw_ref, w2_ref, o_ref, *, rc):
    x = x_ref[0]
    xn = xn_ref[0]
    last_row = lax.broadcasted_iota(jnp.int32, (rc, KV_COLS), 0) == rc - 1
    for c in range(2):
        w1 = w1_ref[c]
        y = _dot(_gather_cv(x, c).astype(BF16), w1)
        yn = _dot(_gather_cv(xn, c).astype(BF16), w1)
        second = pltpu.roll(y[:, KV_COLS:], rc - 1, 0)
        second = jnp.where(last_row, yn[0:1, KV_COLS:], second)
        pre = y[:, :KV_COLS] + second + _pos_bias(pos_ref, pw_ref, c)
        o_ref[0, :, c * KV_COLS:(c + 1) * KV_COLS] = _dot(jax.nn.gelu(pre).astype(BF16), w2_ref[c])


def _pick_rows(n, cap):
    best = 8
    for rc in range(8, cap + 1, 8):
        if n % rc == 0:
            best = rc
    return best


def _compress(x, cw):
    bsz, n, _ = x.shape
    rc = _pick_rows(n, 256)
    w1big, pos, pw, w2big = cw
    nb8 = n // 8
    return pl.pallas_call(
        functools.partial(_compress_kernel, rc=rc),
        grid=(bsz, n // rc),
        in_specs=[pl.BlockSpec((1, rc, x.shape[2]), lambda b, i: (b, i, 0)),
                  pl.BlockSpec((1, 8, x.shape[2]),
                               lambda b, i: (b, jnp.minimum((i + 1) * (rc // 8), nb8 - 1), 0)),
                  pl.BlockSpec(w1big.shape, lambda b, i: (0, 0, 0)),
                  pl.BlockSpec(pos.shape, lambda b, i: (0, 0, 0)),
                  pl.BlockSpec(pw.shape, lambda b, i: (0, 0, 0)),
                  pl.BlockSpec(w2big.shape, lambda b, i: (0, 0, 0))],
        out_specs=pl.BlockSpec((1, rc, ROW_W), lambda b, i: (b, i, 0)),
        out_shape=jax.ShapeDtypeStruct((bsz, n, ROW_W), F32),
        compiler_params=_cparams(("parallel", "arbitrary")),
        name="compress",
    )(x, x, w1big, pos, pw, w2big)


def _compress_tokens(kv_c, cw):
    bsz, t = kv_c.shape[:2]
    n = t // CMP_STRIDE
    npad = _round_up(n, LANES if n >= LANES else 8)
    x = kv_c.reshape(bsz, n, CMP_STRIDE * ROW_W)
    x = jnp.pad(x, ((0, 0), (0, npad - n), (0, 0)))
    return _compress(x, cw)[:, :n - (CMP_HALF - 1)]


def _rel_bucket(d):
    n = jnp.maximum(d, 0)
    large = jnp.full(d.shape, REL_BUCKETS // 2, jnp.int32)
    for thr in REL_THRESHOLDS:
        large = large + (n >= thr).astype(jnp.int32)
    return jnp.where(n < REL_BUCKETS // 2, n, large)


def _rel_bias_lanes(d, tab_ref):
    bkt = _rel_bucket(d)
    f = jnp.zeros(d.shape, F32)
    for k in range(REL_BUCKETS):
        f = jnp.where(bkt == k, tab_ref[0, k:k + 1, :], f)
    return f


def _rel_bias_rows(d, tab_ref):
    bkt = _rel_bucket(d)
    f = jnp.zeros(d.shape, F32)
    for k in range(REL_BUCKETS):
        f = jnp.where(bkt == k, tab_ref[:, k:k + 1], f)
    return f


def _near_tile_kernel(tab_ref, o_ref):
    t = pl.program_id(1)
    delta = jnp.where(t < N_NEAR, t * QT, jnp.where(t == TN_WIN, WINDOW, 0))
    dmax = jnp.where(t == TN_WIN, WINDOW, 1 << 30)
    j = lax.broadcasted_iota(jnp.int32, (QT, HQ), 0)
    i = lax.broadcasted_iota(jnp.int32, (QT, HQ), 1) & (QT - 1)
    d = delta + i - j
    far = tab_ref[0, REL_BUCKETS - 1:REL_BUCKETS, :]
    val = jnp.where((d >= 0) & (d <= dmax), _rel_bias_lanes(d, tab_ref) - far, NEG)
    val = jnp.where(t == TN_MASKED, NEG, jnp.where(t == TN_ZERO, 0.0, val))
    o_ref[0, 0] = val


def _cmp_window_kernel(tab_ref, o_ref):
    e = lax.broadcasted_iota(jnp.int32, (WT_ROWS, HQ), 0) - WT_LEAD
    i = lax.broadcasted_iota(jnp.int32, (WT_ROWS, HQ), 1) & (QT - 1)
    d = i - (CMP_LEN - 1) - CMP_STRIDE * e
    far = tab_ref[0, REL_BUCKETS - 1:REL_BUCKETS, :]
    o_ref[0] = jnp.where(d >= 0, _rel_bias_lanes(d, tab_ref) - far, 0.0)


def _prompt_bias_tiles(rel_bias):
    tab = jnp.repeat((rel_bias * LOG2E).reshape(REL_BUCKETS, NSA_G, NSA_HPG), QT, axis=2)
    tab = jnp.transpose(tab, (1, 0, 2))
    near = pl.pallas_call(
        _near_tile_kernel,
        grid=(NSA_G, TN_COUNT),
        in_specs=[pl.BlockSpec((1, REL_BUCKETS, HQ), lambda g, t: (g, 0, 0))],
        out_specs=pl.BlockSpec((1, 1, QT, HQ), lambda g, t: (g, t, 0, 0)),
        out_shape=jax.ShapeDtypeStruct((NSA_G, TN_COUNT, QT, HQ), F32),
        compiler_params=_cparams(("parallel", "arbitrary")),
        name="bias_near_tiles",
    )(tab)
    cwin = pl.pallas_call(
        _cmp_window_kernel,
        grid=(NSA_G,),
        in_specs=[pl.BlockSpec((1, REL_BUCKETS, HQ), lambda g: (g, 0, 0))],
        out_specs=pl.BlockSpec((1, WT_ROWS, HQ), lambda g: (g, 0, 0)),
        out_shape=jax.ShapeDtypeStruct((NSA_G, WT_ROWS, HQ), F32),
        compiler_params=_cparams(("parallel",)),
        name="bias_cmp_window",
    )(tab)
    return near, cwin


GROUPS_PER_STEP = 2
N_WIN_TILES = WINDOW // QT + 1
NSA_SEGMENTS = 4


def _nsa_group(q_ref, kc_ref, vct_ref, ks_ref, vst_ref, kw_refs, vwt_refs, tn_ref, wt_ref,
               g_ref, o_ref, s_ref, imp_ref, mb_ref, m_ref, acc_ref,
               buf0_ref, buf1_ref, tmax_ref, part_ref, *, ncp, nblk, n_sel, b_off):
    b = pl.program_id(1) + b_off
    qs = b * QT
    q = q_ref[0, 0]

    sc = _dot_nt(kc_ref[0], q)
    ci = lax.broadcasted_iota(jnp.int32, (ncp, HQ), 0)
    li = lax.broadcasted_iota(jnp.int32, (ncp, HQ), 1) & (QT - 1)
    vis = (CMP_STRIDE * ci + (CMP_LEN - 1) - li) <= qs
    s_ref[0:WT_LEAD, :] = jnp.full((WT_LEAD, HQ), NEG, F32)
    s_ref[WT_LEAD + ncp:, :] = jnp.full((s_ref.shape[0] - WT_LEAD - ncp, HQ), NEG, F32)
    s_ref[WT_LEAD:WT_LEAD + ncp, :] = jnp.where(vis, sc, NEG)
    r0 = pl.multiple_of(8 * b, 8)
    s_ref[pl.ds(r0, WT_ROWS), :] += wt_ref[0]
    yield None
    s = s_ref[WT_LEAD:WT_LEAD + ncp, :]
    m = jnp.max(s, axis=0, keepdims=True)
    m = jnp.where(m <= 0.1 * NEG, 0.0, m)
    p = jnp.exp2(s - m)
    rinv = 1.0 / jnp.maximum(jnp.sum(p, axis=0, keepdims=True), 1e-30)
    o_c = _dot(vct_ref[0], p.astype(BF16)) * rinv
    yield None
    pn = p * rinv
    imp = pn[:, 0:QT]
    for h in range(1, NSA_HPG):
        imp = imp + pn[:, h * QT:(h + 1) * QT]
    imp_ref[0:8, :] = jnp.zeros((8, QT), F32)
    imp_ref[8:8 + ncp, :] = imp
    if imp_ref.shape[0] > 8 + ncp:
        imp_ref[8 + ncp:, :] = jnp.zeros((imp_ref.shape[0] - 8 - ncp, QT), F32)
    ratio = SEL_BLOCK // CMP_STRIDE
    span = ratio + CMP_HALF - 1
    psl = imp_ref[pl.ds(8 - (CMP_HALF - 1), nblk, stride=ratio), :]
    for mm in range(1, span):
        psl = psl + imp_ref[pl.ds(8 - (CMP_HALF - 1) + mm, nblk, stride=ratio), :]

    yield None
    n_win = WINDOW // QT
    st_parts, vt_parts = [], []

    def window_tile(w):
        kt = b - n_win + w
        tn_idx = jnp.where(kt >= 0, TN_WIN if w == 0 else n_win - w, TN_MASKED)
        st_parts.append(_dot_nt(kw_refs[w][0, 0], q) + tn_ref[0, tn_idx])
        vt_parts.append(vwt_refs[w][0, 0])

    def window_finish():
        st = jnp.concatenate(st_parts, axis=0)
        p = jnp.exp2(st - jnp.max(st, axis=0, keepdims=True))
        o_w = _dot(jnp.concatenate(vt_parts, axis=1), p.astype(BF16)) * (
            1.0 / jnp.sum(p, axis=0, keepdims=True))
        g = jax.nn.sigmoid(g_ref[0, 0])
        part_ref[...] = g[0:1] * o_c + g[2:3] * o_w

    window_pieces = [functools.partial(window_tile, w) for w in range(n_win + 1)] + [window_finish]

    ji = lax.broadcasted_iota(jnp.int32, (nblk, QT), 0)
    ii = lax.broadcasted_iota(jnp.int32, (nblk, QT), 1)
    cur = (qs + ii) // SEL_BLOCK
    jf = ji.astype(F32)
    forced = (ji == 0) | (ji == cur) | (ji == cur - 1)
    score0 = jnp.where(forced, -jnp.inf, jnp.where(ji <= cur, psl, -BIG))

    score, selb = score0, jnp.where(forced, 0.0, NEG)
    for _ in range(max(n_sel - N_FORCED, 0)):
        yield None
        mx = jnp.max(score, axis=0, keepdims=True)
        first = jnp.min(jnp.where(score == mx, jf, 1e9), axis=0, keepdims=True)
        pick = jf == first
        score, selb = jnp.where(pick, -jnp.inf, score), jnp.where(pick, 0.0, selb)
        if window_pieces:
            window_pieces.pop(0)()
    while window_pieces:
        window_pieces.pop(0)()
    mb_ref[0:nblk, :] = jnp.concatenate([selb] * NSA_HPG, axis=1)
    mb_ref[nblk:, :] = jnp.full((BLK_PER_KT, HQ), NEG, F32)

    m_ref[...] = jnp.full(m_ref.shape, NEG, F32)
    acc_ref[...] = jnp.zeros(acc_ref.shape, F32)

    def sel_logits(kt, valid):
        ktc = jnp.where(valid, kt, 0)
        st = _dot_nt(ks_ref[0, ktc], q)
        row0 = jnp.where(valid, ktc * BLK_PER_KT, nblk)
        mb = mb_ref[pl.ds(pl.multiple_of(row0, BLK_PER_KT), BLK_PER_KT), :]
        st = jnp.concatenate(
            [st[k * SEL_BLOCK:(k + 1) * SEL_BLOCK] + mb[k:k + 1] for k in range(BLK_PER_KT)], axis=0)
        return st, ktc

    def softmax_step(st, tmax, vt):
        nk = vt.shape[1]
        ones = jnp.where(lax.broadcasted_iota(jnp.int32, (V_ROWS - NSA_DK, nk), 0) == 0, 1.0, 0.0)
        vt = jnp.concatenate([vt, ones.astype(BF16)], axis=0)
        m_old = m_ref[...]
        m_new = jnp.maximum(m_old, tmax)
        alpha = jnp.exp2(m_old - m_new)
        p = jnp.exp2(st - m_new)
        acc_ref[...] = alpha * acc_ref[...] + _dot(vt, p.astype(BF16))
        m_ref[...] = m_new

    nt = b // SUB_PER_KT + 1
    n_far = jnp.maximum(nt - NEAR_TILES, 0)

    def far_logits(kt, buf_ref, slot):
        st, _ = sel_logits(kt, kt < n_far)
        buf_ref[...] = st
        tmax_ref[slot:slot + 1, :] = jnp.max(st, axis=0, keepdims=True)

    def far_first():
        far_logits(0, buf0_ref, 0)

    def far_pair(pi):
        k0 = 2 * pi
        far_logits(k0 + 1, buf1_ref, 1)
        softmax_step(buf0_ref[...], tmax_ref[0:1, :], vst_ref[0, k0])
        far_logits(k0 + 2, buf0_ref, 0)
        k1 = jnp.minimum(k0 + 1, jnp.maximum(n_far - 1, 0))
        softmax_step(buf1_ref[...], tmax_ref[1:2, :], vst_ref[0, k1])

    yield far_first, far_pair

    st_parts, vt_parts = [], []
    for w in range(NEAR_TILES):
        kt = nt - NEAR_TILES + w
        st, ktc = sel_logits(kt, kt >= 0)
        r = b - SUB_PER_KT * ktc
        parts = []
        for u in range(SUB_PER_KT):
            ru = r - u
            idx = jnp.where(ru < 0, TN_MASKED, jnp.where(ru >= N_NEAR, TN_ZERO, ru))
            parts.append(tn_ref[0, idx])
        st_parts.append(st + jnp.concatenate(parts, axis=0))
        vt_parts.append(vst_ref[0, ktc])
    yield None
    st = jnp.concatenate(st_parts, axis=0)
    softmax_step(st, jnp.max(st, axis=0, keepdims=True), jnp.concatenate(vt_parts, axis=1))
    acc = acc_ref[...]
    o_s = acc[0:NSA_DK] * (1.0 / acc[NSA_DK:NSA_DK + 1])
    total = part_ref[...] + jax.nn.sigmoid(g_ref[0, 0, 1:2]) * o_s
    pairs = [jnp.concatenate([total[:, h * QT:(h + 1) * QT], total[:, (h + 1) * QT:(h + 2) * QT]], axis=0).T
             for h in range(0, NSA_HPG, 2)]
    o_ref[...] = jnp.concatenate(pairs, axis=1)


GROUP_COLS = NSA_HPG * NSA_DK


def _nsa_kernel(*refs, n_scratch, **static):
    gp = GROUPS_PER_STEP
    refs = refs[1:]
    n_all = gp * n_scratch
    ins, o_ref = refs[:len(refs) - n_all - 1], refs[len(refs) - n_all - 1]
    scratch = refs[len(refs) - n_all:]

    b = pl.program_id(1) + static["b_off"]
    n_far = jnp.maximum(b // SUB_PER_KT + 1 - NEAR_TILES, 0)
    programs = []
    for gi in range(gp):
        v = [r.at[pl.ds(gi, 1)] for r in ins]
        kw_refs, vwt_refs = v[5:5 + N_WIN_TILES], v[5 + N_WIN_TILES:5 + 2 * N_WIN_TILES]
        rest = v[5 + 2 * N_WIN_TILES:] + [o_ref.at[:, pl.ds(gi * GROUP_COLS, GROUP_COLS)]]
        programs.append(_nsa_group(*v[:5], kw_refs, vwt_refs, *rest,
                                   *scratch[gi * n_scratch:(gi + 1) * n_scratch], **static))
    loops = [None] * gp
    while any(lp is None for lp in loops):
        for gi, p in enumerate(programs):
            if loops[gi] is None:
                loops[gi] = next(p)

    @pl.when(n_far > 0)
    def _():
        for far_first, _ in loops:
            far_first()

    def in_turns(generators):
        active = list(generators)
        while active:
            for p in list(active):
                if next(p, StopIteration) is StopIteration:
                    active.remove(p)

    def far_body(pi, carry):
        for _, far_pair in loops:
            far_pair(pi)
        return carry

    lax.fori_loop(0, (n_far + 1) // 2, far_body, 0)
    in_turns(programs)


def _nsa_attend(qt, kc, vct, ks, vst, kw, vwt, tn, wt, gt, *, n_sel_blk):
    s_n, nq = qt.shape[:2]
    gp = GROUPS_PER_STEP
    assert kc.shape[1] >= 8 * nq and ks.shape[1] * SUB_PER_KT >= nq and s_n % gp == 0
    n_sel = min(N_SEL, n_sel_blk)
    n_seg = NSA_SEGMENTS if nq % (NSA_SEGMENTS * SUB_PER_KT) == 0 else 1
    nq_seg = nq // n_seg
    y = jnp.zeros((nq * QT, s_n * GROUP_COLS), F32)
    for seg in range(n_seg):
        b_off = seg * nq_seg
        b_end = b_off + nq_seg
        ncp = min(_round_up(8 * b_end, LANES), kc.shape[1])
        nkt = b_end // SUB_PER_KT
        nblk = nkt * BLK_PER_KT
        imp_rows = max(8 + ncp, 8 + (SEL_BLOCK // CMP_STRIDE) * nblk)
        scratch = [(ncp + WT_ROWS, HQ), (imp_rows, QT), (nblk + BLK_PER_KT, HQ), (1, HQ), (V_ROWS, HQ),
                   (KT_SEL, HQ), (KT_SEL, HQ), (8, HQ), (NSA_DK, HQ)]
        kern = functools.partial(_nsa_kernel, n_scratch=len(scratch), ncp=ncp, nblk=nblk, n_sel=n_sel,
                                 b_off=b_off)
        lead = lambda a, n: pl.BlockSpec((gp, n) + a.shape[2:], lambda s, i: (s,) + (0,) * (a.ndim - 1),
                                         pipeline_mode=pl.Buffered(1))
        per_tile = lambda a: pl.BlockSpec((gp, 1) + a.shape[2:], lambda s, i: (s, i + b_off, 0, 0))
        win = lambda a, w: pl.BlockSpec(
            (gp, 1) + a.shape[2:],
            lambda s, i: (s, jnp.maximum(i + b_off - (N_WIN_TILES - 1) + w, 0), 0, 0))
        in_specs = ([pl.BlockSpec(memory_space=pl.ANY), per_tile(qt), lead(kc, ncp),
                     pl.BlockSpec((gp, NSA_DK, ncp), lambda s, i: (s, 0, 0), pipeline_mode=pl.Buffered(1)),
                     lead(ks, nkt), lead(vst, nkt)]
                    + [win(kw, w) for w in range(N_WIN_TILES)]
                    + [win(vwt, w) for w in range(N_WIN_TILES)]
                    + [lead(tn, tn.shape[1]), lead(wt, wt.shape[1]), per_tile(gt)])
        args = (y, qt, kc, vct, ks, vst, *([kw] * N_WIN_TILES), *([vwt] * N_WIN_TILES), tn, wt, gt)
        y = pl.pallas_call(
            kern,
            grid=(s_n // gp, nq_seg),
            in_specs=in_specs,
            out_specs=pl.BlockSpec((QT, gp * GROUP_COLS), lambda s, i: (i + b_off, s)),
            out_shape=jax.ShapeDtypeStruct((nq * QT, s_n * GROUP_COLS), F32),
            scratch_shapes=[pltpu.VMEM(s, F32) for s in scratch] * gp,
            input_output_aliases={0: 0},
            compiler_params=_cparams(("parallel", "arbitrary")),
            name="nsa_prompt",
        )(*args)
    return y


def _k_tiles(kv, tile):
    nt = kv.shape[0] // tile
    k = kv[..., :KV_COLS].reshape(nt, tile, NSA_G, NSA_DK)
    k = jnp.transpose(k, (2, 0, 1, 3))
    k = jnp.pad(k, ((0, 0), (0, 0), (0, 0), (0, LANES - NSA_DK))).astype(BF16)
    v = kv[..., KV_COLS:].reshape(nt, tile, NSA_G, NSA_DK)
    v = jnp.transpose(v, (2, 0, 3, 1)).astype(BF16)
    return k, v


def _prompt_attend(qt, gt, tok, ks, vst, kw, vwt, bias):
    nq = qt.shape[1]
    t = nq * QT
    near, cwin = bias
    ncp = max(_round_up(tok.shape[0] + 1, LANES), _round_up(8 * nq, LANES))
    kc, vct = _k_tiles(jnp.pad(tok, ((0, ncp - tok.shape[0]), (0, 0))), ncp)
    return _nsa_attend(qt, kc[:, 0], vct[:, 0], ks, vst, kw, vwt, near, cwin, gt,
                       n_sel_blk=t // SEL_BLOCK)


def _sample_bias_kernel(tab_ref, cb_ref, sb_ref, snb_ref, wb_ref, *, past, dseq, ns):
    rows = tab_ref.shape[0]
    far = tab_ref[:, REL_BUCKETS - 1:REL_BUCKETS]

    def tok_of(n):
        return lax.broadcasted_iota(jnp.int32, (rows, n), 0) % dseq

    def lane(n):
        return lax.broadcasted_iota(jnp.int32, (rows, n), 1)

    for idx in range(2):
        rho = LANES * (ns - 1 + idx) + lane(LANES)
        d = past + tok_of(LANES) - (CMP_STRIDE * (rho - 1) + CMP_LEN - 1)
        cb_ref[idx] = jnp.where(d >= 0, _rel_bias_rows(d, tab_ref) - far, 0.0)
    d = SP_KEYS + tok_of(SP_KEYS) - lane(SP_KEYS)
    sb_ref[...] = _rel_bias_rows(d, tab_ref) - far
    d = tok_of(LANES) - lane(LANES)
    snb_ref[...] = jnp.where(d >= 0, _rel_bias_rows(d, tab_ref) - far, NEG)
    d = WINDOW + tok_of(WINDOW + LANES) - lane(WINDOW + LANES)
    wb_ref[...] = jnp.where((d >= 0) & (d <= WINDOW), _rel_bias_rows(d, tab_ref), NEG)


def _sample_bias(rel_bias, past, dseq, ns):
    rows = NSA_HEADS * dseq
    tab = (rel_bias * LOG2E).reshape(REL_BUCKETS, NSA_G, NSA_HPG)
    tab = jnp.transpose(tab, (2, 1, 0))
    tab = jnp.repeat(tab.reshape(NSA_HEADS, 1, REL_BUCKETS), dseq, axis=1).reshape(rows, REL_BUCKETS)
    tab = jnp.pad(tab, ((0, 0), (0, LANES - REL_BUCKETS)))
    shapes = [(2, rows, LANES), (rows, SP_KEYS), (rows, LANES), (rows, WINDOW + LANES)]
    cb, sb, snb, wb = pl.pallas_call(
        functools.partial(_sample_bias_kernel, past=past, dseq=dseq, ns=ns),
        out_shape=[jax.ShapeDtypeStruct(s, F32) for s in shapes],
        compiler_params=pltpu.CompilerParams(vmem_limit_bytes=VMEM_LIMIT),
        name="bias_sample",
    )(tab)
    return cb, sb, snb, wb


def _sample_nsa_kernel(pt_ref, *refs, ns, past, dseq, n_sel, n_sel_blk):
    cmp_pages = refs[:SP_PP]
    slc_pages = refs[SP_PP:2 * SP_PP]
    (xn_ref, ksn_ref, win_ref, kwn_ref, q_ref, g_ref, w1_ref, pos_ref, pw_ref, w2_ref, e_ref,
     cb_ref, sb_ref, snb_ref, wb_ref, o_ref,
     s_ref, vc_ref, carry_ref, sel_ref, m_ref, l_ref, acc_ref, oc_ref, x_ref) = refs[2 * SP_PP:]
    j = pl.program_id(1)
    qbd = q_ref[0]
    rows = qbd.shape[0]
    ntile = ns + 1
    nblk_l = sel_ref.shape[0] * SP_BLKS

    def feat_major(refs_, c):
        return jnp.concatenate([r[0, c].reshape(KV_COLS, r.shape[-1]) for r in refs_], axis=1)

    def tokens_of(gather, prepare=None):
        out = []
        for c in range(2):
            if prepare is not None:
                prepare(c)
            xc = gather(c)
            n = xc.shape[0]
            row0 = lax.broadcasted_iota(jnp.int32, (n, KV_COLS), 0) == 0
            y = _dot(xc.astype(BF16), w1_ref[c])
            first = jnp.where(row0, carry_ref[c:c + 1, :], pltpu.roll(y[:, :KV_COLS], 1, 0))
            carry_ref[c:c + 1, :] = y[n - 1:n, :KV_COLS]
            pre = first + y[:, KV_COLS:] + _pos_bias(pos_ref, pw_ref, c)
            out.append(_dot(jax.nn.gelu(pre).astype(BF16), w2_ref[c]))
        return out

    def online_update(s, v):
        m_old = m_ref[...]
        m_new = jnp.maximum(m_old, jnp.max(s, axis=1, keepdims=True))
        alpha = jnp.exp2(m_old - m_new)
        p = jnp.exp2(s - m_new)
        l_ref[...] = alpha * l_ref[...] + jnp.sum(p, axis=1, keepdims=True)
        acc_ref[...] = alpha * acc_ref[...] + _dot_nt(p.astype(BF16), v)
        m_ref[...] = m_new

    @pl.when(j == 0)
    def _():
        carry_ref[...] = jnp.zeros(carry_ref.shape, F32)

    @pl.when(j < ns)
    def _():
        def to_rows(c):
            for p, r in enumerate(cmp_pages):
                for gp in range(2):
                    tile = r[0, c, 2 * gp:2 * gp + 2].reshape(LANES, PAGE_SIZE)
                    x_ref[2 * c + gp, p * PAGE_SIZE:(p + 1) * PAGE_SIZE, :] = tile.T

        def gather(c):
            return jnp.concatenate(
                [x_ref[2 * c + gp, pl.ds(l, SP_KEYS // CMP_STRIDE, stride=CMP_STRIDE), :]
                 for l in range(CMP_STRIDE) for gp in range(2)], axis=1)

        tok_k, tok_v = tokens_of(gather, to_rows)
        s_ref[j] = _dot_nt(qbd, tok_k.astype(BF16))
        vc_ref[j] = tok_v.astype(BF16)

    @pl.when(j == ns - 1)
    def _():
        tok_k, tok_v = tokens_of(lambda c: _gather_cv(xn_ref[0], c))
        zpad = jnp.zeros((LANES - tok_k.shape[0], KV_COLS), F32)
        s_ref[ns] = _dot_nt(qbd, jnp.concatenate([tok_k, zpad], axis=0).astype(BF16))
        vc_ref[ns] = jnp.concatenate([tok_v, zpad], axis=0).astype(BF16)

        tiles = []
        for t in range(ntile):
            st = s_ref[t]
            if t >= ns - 1:
                st = st + cb_ref[t - (ns - 1)]
            tiles.append(st)
        s = jnp.concatenate(tiles, axis=1)
        width = ntile * LANES
        rho = lax.broadcasted_iota(jnp.int32, (rows, width), 1)
        tq = lax.broadcasted_iota(jnp.int32, (rows, width), 0) % dseq
        n_cmp = (past + _round_up(dseq, SEL_BLOCK)) // CMP_STRIDE - (CMP_HALF - 1)
        vis = (rho >= 1) & (rho <= n_cmp) & (CMP_STRIDE * (rho - 1) + CMP_LEN - 1 <= past + tq)
        s = jnp.where(vis, s, NEG)
        m = jnp.max(s, axis=1, keepdims=True)
        m = jnp.where(m <= 0.1 * NEG, 0.0, m)
        p = jnp.exp2(s - m)
        rinv = 1.0 / jnp.maximum(jnp.sum(p, axis=1, keepdims=True), 1e-30)
        vc = jnp.concatenate([vc_ref[t] for t in range(ntile)], axis=0)
        oc_ref[...] = _dot(p.astype(BF16), vc) * rinv
        pn = p * rinv
        gt = NSA_G * dseq
        imp = pn[0:gt]
        for h in range(1, NSA_HPG):
            imp = imp + pn[h * gt:(h + 1) * gt]
        ratio = SEL_BLOCK // CMP_STRIDE
        span = ratio + CMP_HALF - 1
        ar = lax.broadcasted_iota(jnp.int32, (width, nblk_l), 0)
        ac = lax.broadcasted_iota(jnp.int32, (width, nblk_l), 1)
        band = jnp.where((ar >= ratio * ac) & (ar < ratio * ac + span), 1.0, 0.0).astype(BF16)
        i_hi = imp.astype(BF16)
        r1 = imp - i_hi.astype(F32)
        i_mid = r1.astype(BF16)
        i_lo = (r1 - i_mid.astype(F32)).astype(BF16)
        psl = _dot(i_hi, band) + _dot(i_mid, band) + _dot(i_lo, band)
        ji = lax.broadcasted_iota(jnp.int32, (gt, nblk_l), 1)
        tq2 = lax.broadcasted_iota(jnp.int32, (gt, nblk_l), 0) % dseq
        cur = (past + tq2) // SEL_BLOCK
        jf = ji.astype(F32)
        forced = (ji == 0) | (ji == cur) | (ji == cur - 1)
        score0 = jnp.where(forced, -jnp.inf, jnp.where(ji <= cur, psl, -BIG))
        score0 = jnp.where(ji < n_sel_blk, score0, -jnp.inf)

        def pick_one(_, carry):
            score, sel = carry
            mx = jnp.max(score, axis=1, keepdims=True)
            first = jnp.min(jnp.where(score == mx, jf, 1e9), axis=1, keepdims=True)
            pick = jf == first
            return jnp.where(pick, -jnp.inf, score), jnp.where(pick, 1.0, sel)

        _, sel = lax.fori_loop(0, max(n_sel - N_FORCED, 0), pick_one,
                               (score0, jnp.where(forced, 1.0, 0.0)))
        for t in range(sel_ref.shape[0]):
            piece = sel[:, t * SP_BLKS:(t + 1) * SP_BLKS]
            piece = jnp.concatenate([piece, jnp.zeros((gt, LANES - SP_BLKS), F32)], axis=1)
            sel_ref[t] = jnp.concatenate([piece] * NSA_HPG, axis=0).astype(BF16)
        m_ref[...] = jnp.full(m_ref.shape, NEG, F32)
        l_ref[...] = jnp.zeros(l_ref.shape, F32)
        acc_ref[...] = jnp.zeros(acc_ref.shape, F32)

    @pl.when(j >= ns)
    def _():
        j2 = j - ns
        s = _dot(qbd, feat_major(slc_pages, 0).astype(BF16))
        chosen = _dot(sel_ref[j2], e_ref[...])
        s = s + (chosen - 1.0) * (-NEG)
        s = s + jnp.where(j2 == ns - 1, sb_ref[...], 0.0)
        online_update(s, feat_major(slc_pages, 1).astype(BF16))

    @pl.when(j == 2 * ns - 1)
    def _():
        online_update(_dot(qbd, ksn_ref[0, 0].astype(BF16)) + snb_ref[...],
                      ksn_ref[0, 1].astype(BF16))
        o_s = acc_ref[...] * (1.0 / l_ref[...])
        kw = jnp.concatenate([feat_major([win_ref], 0), kwn_ref[0, 0]], axis=1)
        vw = jnp.concatenate([feat_major([win_ref], 1), kwn_ref[0, 1]], axis=1)
        s = _dot(qbd, kw.astype(BF16)) + wb_ref[...]
        p = jnp.exp2(s - jnp.max(s, axis=1, keepdims=True))
        o_w = _dot_nt(p.astype(BF16), vw.astype(BF16)) * (1.0 / jnp.sum(p, axis=1, keepdims=True))
        g = jax.nn.sigmoid(g_ref[0])
        o_ref[0] = g[:, 0:1] * oc_ref[...] + g[:, 1:2] * o_s + g[:, 2:3] * o_w


def _sample_attend(q, gates, kv_c, kv_s, kv_w, cache_cmp, cache_slc, win_buf, page_table, cw,
                   rel_bias):
    bsz, dseq = q.shape[:2]
    n_pages = page_table.shape[1]
    past = n_pages * PAGE_SIZE
    assert n_pages % SP_PP == 0 and dseq <= SEL_BLOCK and win_buf.shape[1] == WINDOW
    feat_major = lambda a: jnp.transpose(a, (0, 2, 3, 4, 1))
    cache_cmp, cache_slc, win_buf = feat_major(cache_cmp), feat_major(cache_slc), feat_major(win_buf)
    ns = n_pages // SP_PP
    rows = NSA_HEADS * dseq
    n_sel_blk = past // SEL_BLOCK + 1
    n_sel = min(N_SEL, n_sel_blk)
    nsel_tiles = _round_up(-(-n_sel_blk // SP_BLKS), LANES // SP_BLKS)
    w1big, pos, pw, w2big = cw
    cb, sb, snb, wb = _sample_bias(rel_bias, past, dseq, ns)

    qr = jnp.transpose((q * Q_SCALE).reshape(bsz, dseq, NSA_G, NSA_HPG, NSA_DK), (0, 3, 2, 1, 4))
    qbd = jnp.einsum('bhgtd,gk->bhgtkd', qr, jnp.eye(NSA_G, dtype=F32))
    qbd = qbd.reshape(bsz, rows, KV_COLS).astype(BF16)
    gr = gates[..., :N_BRANCH * NSA_HEADS].reshape(bsz, dseq, NSA_G, NSA_HPG, N_BRANCH)
    gr = jnp.transpose(gr, (0, 3, 2, 1, 4)).reshape(bsz, rows, N_BRANCH)
    gr = jnp.pad(gr, ((0, 0), (0, 0), (0, LANES - N_BRANCH)))
    pad_rows = lambda a, n: jnp.pad(a, ((0, 0), (0, n - a.shape[1]), (0, 0)))
    xn = pad_rows(kv_c, SEL_BLOCK).reshape(bsz, SEL_BLOCK // CMP_STRIDE, CMP_STRIDE * ROW_W)
    xn = pad_rows(xn, 8)
    new_feat_major = lambda a: jnp.transpose(
        pad_rows(a, LANES).reshape(bsz, LANES, 2, KV_COLS), (0, 2, 3, 1))
    ksn = new_feat_major(kv_s)
    kwn = new_feat_major(kv_w)
    expand = (jnp.arange(LANES)[:, None] == (jnp.arange(SP_KEYS) // SEL_BLOCK)[None, :]).astype(BF16)
    page_block = (1, 2, NSA_G, NSA_DK, PAGE_SIZE)

    def page_map(k, phase):
        if phase == 0:
            return lambda b, j, pt: (pt[b, jnp.minimum(j, ns - 1) * SP_PP + k], 0, 0, 0, 0)
        return lambda b, j, pt: (pt[b, jnp.maximum(j - ns, 0) * SP_PP + k], 0, 0, 0, 0)

    per_b = lambda a: pl.BlockSpec((1,) + a.shape[1:], lambda b, j, pt: (b,) + (0,) * (a.ndim - 1))
    const = lambda a: pl.BlockSpec(a.shape, lambda b, j, pt: (0,) * a.ndim)
    in_specs = ([pl.BlockSpec(page_block, page_map(k, 0)) for k in range(SP_PP)]
                + [pl.BlockSpec(page_block, page_map(k, 1)) for k in range(SP_PP)]
                + [per_b(xn), per_b(ksn), per_b(win_buf), per_b(kwn), per_b(qbd), per_b(gr),
                   pl.BlockSpec(w1big.shape, lambda b, j, pt: (0, 0, 0), pipeline_mode=pl.Buffered(1)),
                   const(pos), const(pw), const(w2big), const(expand),
                   const(cb), const(sb), const(snb), const(wb)])
    grid_spec = pltpu.PrefetchScalarGridSpec(
        num_scalar_prefetch=1, grid=(bsz, 2 * ns), in_specs=in_specs,
        out_specs=pl.BlockSpec((1, rows, KV_COLS), lambda b, j, pt: (b, 0, 0)),
        scratch_shapes=[pltpu.VMEM((ns + 1, rows, LANES), F32),
                        pltpu.VMEM((ns + 1, LANES, KV_COLS), BF16),
                        pltpu.VMEM((8, KV_COLS), F32),
                        pltpu.VMEM((nsel_tiles, rows, LANES), BF16),
                        pltpu.VMEM((rows, 1), F32),
                        pltpu.VMEM((rows, 1), F32),
                        pltpu.VMEM((rows, KV_COLS), F32),
                        pltpu.VMEM((rows, KV_COLS), F32),
                        pltpu.VMEM((ROW_W // LANES, SP_KEYS, LANES), F32)])
    out = pl.pallas_call(
        functools.partial(_sample_nsa_kernel, ns=ns, past=past, dseq=dseq, n_sel=n_sel,
                          n_sel_blk=n_sel_blk),
        grid_spec=grid_spec,
        out_shape=jax.ShapeDtypeStruct((bsz, rows, KV_COLS), F32),
        compiler_params=_cparams(("parallel", "arbitrary")),
        name="nsa_sample",
    )(page_table, *([cache_cmp] * SP_PP), *([cache_slc] * SP_PP), xn, ksn, win_buf, kwn, qbd, gr,
      w1big, pos, pw, w2big, expand, cb, sb, snb, wb)
    o = out.reshape(bsz, NSA_HPG, NSA_G, dseq, NSA_G, NSA_DK)
    o = jnp.einsum('bhgtge->btghe', o)
    return o.reshape(bsz * dseq, NSA_WIDTH)


def _out_proj_kernel(ycm_ref, ynsa_ref, h_ref, gt_ref, gpost_ref, w1_ref, w2_ref, o_ref):
    out = _dot(ycm_ref[...].astype(BF16), w1_ref[...]) + _dot(ynsa_ref[...].astype(BF16), w2_ref[...])
    o_ref[...] = h_ref[...] + gt_ref[...] * _rms(out, gpost_ref[...])


def _out_proj(ycm, ynsa, h, gate, g_post, w1, w2, tm):
    r, d = h.shape
    return pl.pallas_call(
        _out_proj_kernel,
        grid=(r // tm,),
        in_specs=[pl.BlockSpec((tm, CM_WIDTH), lambda i: (i, 0)),
                  pl.BlockSpec((tm, NSA_WIDTH), lambda i: (i, 0)),
                  pl.BlockSpec((tm, d), lambda i: (i, 0)),
                  _row_spec(gate, tm),
                  pl.BlockSpec((1, d), lambda i: (0, 0)),
                  pl.BlockSpec(w1.shape, lambda i: (0, 0)),
                  pl.BlockSpec(w2.shape, lambda i: (0, 0))],
        out_specs=pl.BlockSpec((tm, d), lambda i: (i, 0)),
        out_shape=jax.ShapeDtypeStruct((r, d), F32),
        compiler_params=_cparams(("parallel",)),
        name="out_proj",
    )(ycm, ynsa, h, gate, g_post, w1, w2)


def _row_tile(r, cap):
    tm = min(r, cap)
    while r % tm:
        tm //= 2
    return tm


def _layer(x, mod, per_row, weights, in_proj, mixer):
    r, d = x.shape
    (norm_pre, norm_post, ffn, w_out1, w_out2) = weights
    if per_row is None:
        mrow = lambda i, j: mod[0, i, j].reshape(1, d)
    else:
        mrow = lambda i, j: jnp.repeat(mod[:, i, j], per_row, axis=0)
    tm = _row_tile(r, 512)
    tm_ffn = _row_tile(r, FFN_TM)
    tf = _row_tile(ffn[0][0].shape[1], FFN_TF)
    h = _ffn(x, mrow(0, 0), mrow(0, 1), mrow(0, 2), norm_pre[0:1], norm_post[0:1], *ffn[0],
             res_w=0.5, tm=tm_ffn, tf=tf)
    mixed_cm, mixed_nsa, state = mixer(*in_proj(h, mrow(1, 0), mrow(1, 1), norm_pre[1:2]))
    h = _out_proj(mixed_cm, mixed_nsa, h, mrow(1, 2), norm_post[1:2], w_out1, w_out2, tm)
    h = _ffn(h, mrow(2, 0), mrow(2, 1), mrow(2, 2), norm_pre[2:3], norm_post[2:3], *ffn[1],
             res_w=0.5, tm=tm_ffn, tf=tf)
    return h, state


def kernel(x_prompt, x_sample, cache_cmp_kv, cache_slc_kv, state_win_kv, page_table, c_prompt,
           c_sample, w_mod, b_mod, norm_pre, norm_post, ffn_w_gate, ffn_w_up, ffn_w_down, w_in,
           w_out, cm_norm, cm_ws, cm_bs, phi_pos, phi_w1, phi_w2, rel_bias):
    depth = w_mod.shape[0]
    assert depth == 1
    bp, seq, d = x_prompt.shape
    bs, dseq, _ = x_sample.shape
    assert bp == 1 and seq % KT_SEL == 0 and seq >= WINDOW
    l = 0

    c_all = jnp.concatenate([c_prompt, c_sample], axis=0)
    mpad = _round_up(c_all.shape[0], 8)
    mod = _mod_proj(jnp.pad(c_all, ((0, mpad - c_all.shape[0]), (0, 0))), w_mod[l], b_mod[l])
    mod = mod[:bp + bs].reshape(bp + bs, 3, 3, d)

    ffn = [(ffn_w_gate[l, i].astype(BF16), ffn_w_up[l, i].astype(BF16), ffn_w_down[l, i].astype(BF16))
           for i in range(2)]
    w_in_p = jnp.pad(w_in[l], ((0, 0), (0, sum(IN_SEGS) - w_in.shape[2]))).astype(BF16)
    w_rows, w_feat = _in_proj_prompt_weights(w_in[l])
    w_o = w_out[l].astype(BF16)
    weights = (norm_pre[l], norm_post[l], ffn, w_o[:CM_WIDTH], w_o[CM_WIDTH:])
    cw = _compress_weights(phi_pos[l], phi_w1[l], phi_w2[l])
    prompt_bias = _prompt_bias_tiles(rel_bias)

    def prompt_in_proj(h, shift, scale, g_pre):
        return _in_proj_prompt(h, shift, scale, g_pre, w_rows, w_feat)

    def prompt_mixer(u, v, kv_c, qt, gt, ks, vst, kw, vwt, kvt):
        y_cm, _ = _chunk_mix(u, v, cm_norm[l], cm_ws[l], cm_bs[l], CHUNK)
        tok = _compress_tokens(kv_c[None], cw)[0]
        y_nsa = _prompt_attend(qt, gt, tok, ks, vst, kw, vwt, prompt_bias)
        return y_cm, y_nsa, kvt

    assert seq % KT_SEL == 0
    hp, kvt = _layer(x_prompt.reshape(seq, d), mod[:bp], None, weights, prompt_in_proj, prompt_mixer)
    kvt = jnp.transpose(kvt.reshape(N_BRANCH, 2, NSA_G, NSA_DK, seq), (0, 4, 1, 2, 3))
    pc, ps, pw = kvt[0], kvt[1], kvt[2]

    rows = _round_up(dseq, 16)

    def sample_mixer(u, v, q, kv_c, kv_s, kv_w, gates):
        per_seq = lambda a: a.reshape(bs, dseq, -1)
        pad_rows = lambda a, n: jnp.pad(per_seq(a), ((0, 0), (0, n - dseq), (0, 0)))
        y_cm, vn = _chunk_mix(pad_rows(u, rows).reshape(bs * rows, CM_WIDTH),
                              pad_rows(v, rows).reshape(bs * rows, CM_WIDTH),
                              cm_norm[l], cm_ws[l], cm_bs[l], rows)
        y_cm = y_cm.reshape(bs, rows, CM_WIDTH)[:, :dseq].reshape(bs * dseq, CM_WIDTH)
        vn = vn.reshape(bs, rows, CM_WIDTH)[:, :dseq]
        y_nsa = _sample_attend(per_seq(q), per_seq(gates), per_seq(kv_c), per_seq(kv_s), per_seq(kv_w),
                               cache_cmp_kv[l], cache_slc_kv[l], state_win_kv[l], page_table, cw,
                               rel_bias)
        return y_cm, y_nsa, (kv_c, kv_s, kv_w, vn)

    def sample_in_proj(h, shift, scale, g_pre):
        return _in_proj(h, shift, scale, g_pre, w_in_p, _row_tile(h.shape[0], 256))

    hs, (sc, ss, sw, sv) = _layer(x_sample.reshape(bs * dseq, d), mod[bp:], dseq, weights,
                                  sample_in_proj, sample_mixer)

    kvshape = lambda a, b, t: a.reshape(1, b, t, 2, NSA_G, NSA_DK)
    wp = min(WINDOW, seq)
    win_s = jnp.concatenate([state_win_kv[l].reshape(bs, -1, ROW_W), sw.reshape(bs, dseq, ROW_W)],
                            axis=1)[:, dseq:]
    return (hp.reshape(bp, seq, d), hs.reshape(bs, dseq, d),
            kvshape(pc, bp, seq), kvshape(sc, bs, dseq),
            kvshape(ps, bp, seq), kvshape(ss, bs, dseq),
            kvshape(pw[seq - wp:], bp, wp), kvshape(win_s, bs, win_s.shape[1]),
            sv.reshape(1, bs, dseq, CM_HEADS, CM_HEAD_DIM))
```

```python
import functools
import math

import jax
import jax.numpy as jnp
from jax import lax
from jax.experimental import pallas as pl
from jax.experimental.pallas import tpu as pltpu

F32 = jnp.float32
BF16 = jnp.bfloat16

CM_HEADS = 8
CM_HEAD_DIM = 128
CM_WIDTH = CM_HEADS * CM_HEAD_DIM
CHUNK = 128
NSA_HEADS = 16
NSA_G = 4
NSA_HPG = 4
NSA_DK = 64
NSA_WIDTH = NSA_HEADS * NSA_DK
KV_COLS = NSA_G * NSA_DK
ROW_W = 2 * KV_COLS
CMP_LEN = 32
CMP_STRIDE = 16
CMP_HALF = CMP_LEN // CMP_STRIDE
SEL_BLOCK = 64
N_SEL = 16
N_FORCED = 3
WINDOW = 512
N_BRANCH = 3
BIG = 1e4
REL_BUCKETS = 32
EPS = 1e-6
PAGE_SIZE = 128

LOG2E = 1.4426950408889634
Q_SCALE = NSA_DK ** -0.5 * LOG2E
NEG = -1e30
LANES = 128
QT = 128
HQ = NSA_HPG * QT
KT_SEL = 512
BLK_PER_KT = KT_SEL // SEL_BLOCK
SUB_PER_KT = KT_SEL // QT
N_NEAR = 8
NEAR_TILES = 3
V_ROWS = NSA_DK + 16
TN_WIN, TN_MASKED, TN_ZERO, TN_COUNT = 8, 9, 10, 11
WT_LEAD = 64
WT_ROWS = 200
REL_THRESHOLDS = (21, 27, 35, 46, 59, 77, 99, 128, 166, 216, 280, 363, 470, 609, 790)
VMEM_LIMIT = 56 * 1024 * 1024
SP_PP = CMP_STRIDE
SP_KEYS = SP_PP * PAGE_SIZE
SP_BLKS = SP_KEYS // SEL_BLOCK


def _cparams(sem):
    return pltpu.CompilerParams(dimension_semantics=sem, vmem_limit_bytes=VMEM_LIMIT)


def _dot(a, b):
    return jnp.dot(a, b, preferred_element_type=F32)


def _dot_nt(a, b):
    return lax.dot_general(a, b, (((1,), (1,)), ((), ())), preferred_element_type=F32)


def _split_bf16(x):
    hi = x.astype(BF16)
    lo = (x - hi.astype(F32)).astype(BF16)
    return hi, lo


def _dot_f32(a, b):
    a_hi, a_lo = _split_bf16(a)
    b_hi, b_lo = _split_bf16(b)
    return _dot(a_hi, b_hi) + _dot(a_lo, b_hi) + _dot(a_hi, b_lo)


def _rms(x, g):
    return x * lax.rsqrt(jnp.mean(x * x, axis=-1, keepdims=True) + EPS) * g


def _row_spec(arr, tm):
    d = arr.shape[-1]
    if arr.shape[0] == 1:
        return pl.BlockSpec((1, d), lambda *idx: (0, 0))
    return pl.BlockSpec((tm, d), lambda *idx: (idx[0], 0))


def _round_up(x, m):
    return -(-x // m) * m


def _mod_kernel(c_ref, w_ref, b_ref, o_ref):
    c = c_ref[...]
    s = c * jax.nn.sigmoid(c)
    o_ref[...] = _dot_f32(s, w_ref[...]) + b_ref[...]


def _mod_proj(c, w_mod, b_mod):
    m, d = c.shape
    n = w_mod.shape[1]
    tn = 512
    return pl.pallas_call(
        _mod_kernel,
        grid=(n // tn,),
        in_specs=[pl.BlockSpec((m, d), lambda j: (0, 0)),
                  pl.BlockSpec((d, tn), lambda j: (0, j)),
                  pl.BlockSpec((1, tn), lambda j: (0, j))],
        out_specs=pl.BlockSpec((m, tn), lambda j: (0, j)),
        out_shape=jax.ShapeDtypeStruct((m, n), F32),
        compiler_params=_cparams(("arbitrary",)),
        name="mod_proj",
    )(c, w_mod, b_mod.reshape(1, n))


def _ffn_kernel(x_ref, sh_ref, sc_ref, gt_ref, gpre_ref, gpost_ref, wg_ref, wu_ref, wd_ref,
                o_ref, a_ref, acc_ref, *, res_w, nf):
    f = pl.program_id(1)

    @pl.when(f == 0)
    def _():
        y = _rms(x_ref[...], gpre_ref[...])
        a_ref[...] = (y * (1.0 + sc_ref[...]) + sh_ref[...]).astype(BF16)
        acc_ref[...] = jnp.zeros_like(acc_ref)

    a = a_ref[...]
    h = _dot(a, wg_ref[...])
    u = _dot(a, wu_ref[...])
    act = (h * jax.nn.sigmoid(h) * u).astype(BF16)
    acc_ref[...] += _dot(act, wd_ref[...])

    @pl.when(f == nf - 1)
    def _():
        o_ref[...] = x_ref[...] + res_w * gt_ref[...] * _rms(acc_ref[...], gpost_ref[...])


FFN_TM = 512
FFN_TF = 512


def _ffn(x, shift, scale, gate, g_pre, g_post, wg, wu, wd, res_w, tm, tf):
    r, d = x.shape
    fdim = wg.shape[1]
    nf = fdim // tf
    return pl.pallas_call(
        functools.partial(_ffn_kernel, res_w=res_w, nf=nf),
        grid=(r // tm, nf),
        in_specs=[pl.BlockSpec((tm, d), lambda i, f: (i, 0)),
                  _row_spec(shift, tm), _row_spec(scale, tm), _row_spec(gate, tm),
                  pl.BlockSpec((1, d), lambda i, f: (0, 0)),
                  pl.BlockSpec((1, d), lambda i, f: (0, 0)),
                  pl.BlockSpec((d, tf), lambda i, f: (0, f)),
                  pl.BlockSpec((d, tf), lambda i, f: (0, f)),
                  pl.BlockSpec((tf, d), lambda i, f: (f, 0))],
        out_specs=pl.BlockSpec((tm, d), lambda i, f: (i, 0)),
        out_shape=jax.ShapeDtypeStruct((r, d), F32),
        scratch_shapes=[pltpu.VMEM((tm, d), BF16), pltpu.VMEM((tm, d), F32)],
        compiler_params=_cparams(("parallel", "arbitrary")),
        name="ffn",
    )(x, shift, scale, gate, g_pre, g_post, wg, wu, wd)


IN_SEGS = (CM_WIDTH, CM_WIDTH, NSA_WIDTH, 2 * KV_COLS, 2 * KV_COLS, 2 * KV_COLS, LANES)


def _in_proj_kernel(x_ref, sh_ref, sc_ref, gpre_ref, w_ref, *o_refs):
    y = _rms(x_ref[...], gpre_ref[...])
    a = (y * (1.0 + sc_ref[...]) + sh_ref[...]).astype(BF16)
    z = _dot(a, w_ref[...])
    off = 0
    for o_ref, width in zip(o_refs, IN_SEGS):
        o_ref[...] = z[:, off:off + width]
        off += width


def _in_proj(x, shift, scale, g_pre, w, tm):
    r, d = x.shape
    n = w.shape[1]
    return pl.pallas_call(
        _in_proj_kernel,
        grid=(r // tm,),
        in_specs=[pl.BlockSpec((tm, d), lambda i: (i, 0)),
                  _row_spec(shift, tm), _row_spec(scale, tm),
                  pl.BlockSpec((1, d), lambda i: (0, 0)),
                  pl.BlockSpec((d, n), lambda i: (0, 0), pipeline_mode=pl.Buffered(1))],
        out_specs=[pl.BlockSpec((tm, s), lambda i: (i, 0)) for s in IN_SEGS],
        out_shape=[jax.ShapeDtypeStruct((r, s), F32) for s in IN_SEGS],
        compiler_params=_cparams(("parallel",)),
        name="in_proj",
    )(x, shift, scale, g_pre, w)


IN_P_TM = 2 * QT
IN_P_QW = NSA_HEADS * LANES
IN_P_KW = NSA_G * LANES
IN_P_ROWS_F = N_BRANCH * ROW_W
IN_P_GATE_ROWS = 64


def _in_proj_prompt_kernel(x_ref, sh_ref, sc_ref, gpre_ref, wr_ref, wf_ref, u_ref, v_ref, kvc_ref,
                           qt_ref, gt_ref, ks_ref, vst_ref, kw_ref, vwt_ref, kvt_ref):
    y = _rms(x_ref[...], gpre_ref[...])
    a = (y * (1.0 + sc_ref[...]) + sh_ref[...]).astype(BF16)
    z = _dot(a, wr_ref[...])
    zt = _dot_nt(wf_ref[...], a)
    nt = IN_P_TM // QT
    tile = lambda t: slice(t * QT, (t + 1) * QT)
    u_ref[...] = z[:, 0:CM_WIDTH]
    v_ref[...] = z[:, CM_WIDTH:2 * CM_WIDTH]
    off = 2 * CM_WIDTH
    for t in range(nt):
        for gh in range(NSA_HEADS):
            g, h = divmod(gh, NSA_HPG)
            qt_ref[g, t, h * QT:(h + 1) * QT, :] = z[tile(t), off + gh * LANES:off + (gh + 1) * LANES].astype(BF16)
    off += IN_P_QW
    kvc_ref[...] = z[:, off:off + ROW_W]
    off += ROW_W
    for g in range(NSA_G):
        ks_ref[g, 0] = z[:, off + g * LANES:off + (g + 1) * LANES].astype(BF16)
    off += IN_P_KW
    for t in range(nt):
        for g in range(NSA_G):
            kw_ref[g, t] = z[tile(t), off + g * LANES:off + (g + 1) * LANES].astype(BF16)
    kvt_ref[...] = zt[0:IN_P_ROWS_F]
    v_rows = lambda branch, g: slice(branch * ROW_W + KV_COLS + g * NSA_DK,
                                     branch * ROW_W + KV_COLS + (g + 1) * NSA_DK)
    for g in range(NSA_G):
        vst_ref[g, 0] = zt[v_rows(1, g), :].astype(BF16)
        for t in range(nt):
            vwt_ref[g, t] = zt[v_rows(2, g), tile(t)].astype(BF16)
            gt_ref[g, t, N_BRANCH:, :] = jnp.zeros((8 - N_BRANCH, HQ), F32)
            for br in range(N_BRANCH):
                for h in range(NSA_HPG):
                    row = IN_P_ROWS_F + (g * N_BRANCH + br) * NSA_HPG + h
                    gt_ref[g, t, br:br + 1, h * QT:(h + 1) * QT] = zt[row:row + 1, tile(t)]


def _in_proj_prompt_weights(w):
    d = w.shape[0]
    q0 = 2 * CM_WIDTH
    k0 = q0 + NSA_WIDTH
    pad_lanes = lambda x: jnp.pad(x, ((0, 0), (0, 0), (0, LANES - NSA_DK))).reshape(d, -1)
    wq = pad_lanes(w[:, q0:k0].reshape(d, NSA_HEADS, NSA_DK) * Q_SCALE)
    k_of = lambda branch: pad_lanes(
        w[:, k0 + branch * ROW_W:k0 + branch * ROW_W + KV_COLS].reshape(d, NSA_G, NSA_DK))
    w_rows = jnp.concatenate([w[:, :q0], wq, w[:, k0:k0 + ROW_W], k_of(1), k_of(2)], axis=1).astype(BF16)
    g0 = k0 + N_BRANCH * ROW_W
    wg = jnp.transpose(w[:, g0:g0 + N_BRANCH * NSA_HEADS].reshape(d, NSA_G, NSA_HPG, N_BRANCH),
                       (0, 1, 3, 2)).reshape(d, N_BRANCH * NSA_HEADS)
    wg = jnp.pad(wg, ((0, 0), (0, IN_P_GATE_ROWS - N_BRANCH * NSA_HEADS)))
    w_feat = jnp.concatenate([w[:, k0:g0], wg], axis=1).T.astype(BF16)
    return w_rows, w_feat


def _in_proj_prompt(x, shift, scale, g_pre, w_rows, w_feat):
    t, d = x.shape
    tm = IN_P_TM
    nq = t // QT
    per = KT_SEL // tm
    const = lambda a: pl.BlockSpec(a.shape, lambda i: (0, 0), pipeline_mode=pl.Buffered(1))
    rows = lambda n: pl.BlockSpec((tm, n), lambda i: (i, 0))
    qtile = lambda a: pl.BlockSpec((NSA_G, tm // QT) + a[2:], lambda i: (0, i, 0, 0))
    shapes = [((t, CM_WIDTH), F32), ((t, CM_WIDTH), F32), ((t, ROW_W), F32),
              ((NSA_G, nq, HQ, LANES), BF16), ((NSA_G, nq, 8, HQ), F32),
              ((NSA_G, t // KT_SEL, KT_SEL, LANES), BF16), ((NSA_G, t // KT_SEL, NSA_DK, KT_SEL), BF16),
              ((NSA_G, nq, QT, LANES), BF16), ((NSA_G, nq, NSA_DK, QT), BF16),
              ((IN_P_ROWS_F, t), F32)]
    out_specs = [rows(CM_WIDTH), rows(CM_WIDTH), rows(ROW_W),
                 qtile(shapes[3][0]), qtile(shapes[4][0]),
                 pl.BlockSpec((NSA_G, 1, tm, LANES), lambda i: (0, i // per, i % per, 0)),
                 pl.BlockSpec((NSA_G, 1, NSA_DK, tm), lambda i: (0, i // per, 0, i % per)),
                 qtile(shapes[7][0]), qtile(shapes[8][0]),
                 pl.BlockSpec((IN_P_ROWS_F, tm), lambda i: (0, i))]
    return pl.pallas_call(
        _in_proj_prompt_kernel,
        grid=(t // tm,),
        in_specs=[rows(d), _row_spec(shift, tm), _row_spec(scale, tm),
                  pl.BlockSpec((1, d), lambda i: (0, 0)), const(w_rows), const(w_feat)],
        out_specs=out_specs,
        out_shape=[jax.ShapeDtypeStruct(s, dt) for s, dt in shapes],
        compiler_params=_cparams(("parallel",)),
        name="in_proj_prompt",
    )(x, shift, scale, g_pre, w_rows, w_feat)


def _chunk_mix_kernel(u_ref, v_ref, nrm_ref, ws_ref, bst_ref, y_ref, vn_ref, *, chunk):
    ri = lax.broadcasted_iota(jnp.int32, (chunk, chunk), 0)
    ci = lax.broadcasted_iota(jnp.int32, (chunk, chunk), 1)
    lower = ci <= ri
    for h in range(CM_HEADS):
        sl = slice(h * CM_HEAD_DIM, (h + 1) * CM_HEAD_DIM)
        vf = jax.nn.gelu(v_ref[:, sl])
        mu = jnp.mean(vf, axis=-1, keepdims=True)
        var = jnp.mean(jnp.square(vf - mu), axis=-1, keepdims=True)
        vn = (vf - mu) * lax.rsqrt(var + EPS) * nrm_ref[:, sl]
        vn_ref[:, sl] = vn
        w = jnp.where(lower, ws_ref[h], 0.0).astype(BF16)
        s = _dot(w, vn.astype(BF16)) + bst_ref[:, h:h + 1]
        y_ref[:, sl] = jax.nn.gelu(u_ref[:, sl]) * s


def _chunk_mix(u, v, cm_norm, cm_ws, cm_bs, chunk):
    r = u.shape[0]
    ws = cm_ws[:, :chunk, :chunk]
    bst = cm_bs[:, :chunk].T
    return pl.pallas_call(
        functools.partial(_chunk_mix_kernel, chunk=chunk),
        grid=(r // chunk,),
        in_specs=[pl.BlockSpec((chunk, CM_WIDTH), lambda i: (i, 0)),
                  pl.BlockSpec((chunk, CM_WIDTH), lambda i: (i, 0)),
                  pl.BlockSpec((1, CM_WIDTH), lambda i: (0, 0)),
                  pl.BlockSpec((CM_HEADS, chunk, chunk), lambda i: (0, 0, 0)),
                  pl.BlockSpec((chunk, CM_HEADS), lambda i: (0, 0))],
        out_specs=[pl.BlockSpec((chunk, CM_WIDTH), lambda i: (i, 0)),
                   pl.BlockSpec((chunk, CM_WIDTH), lambda i: (i, 0))],
        out_shape=[jax.ShapeDtypeStruct((r, CM_WIDTH), F32),
                   jax.ShapeDtypeStruct((r, CM_WIDTH), F32)],
        compiler_params=_cparams(("parallel",)),
        name="chunk_mix",
    )(u, v, cm_norm.reshape(1, CM_WIDTH), ws, bst)


def _gather_cv(x, c):
    return jnp.concatenate(
        [x[:, l * ROW_W + c * KV_COLS:l * ROW_W + (c + 1) * KV_COLS] for l in range(CMP_STRIDE)],
        axis=1)


def _pos_bias(pos_ref, pw_ref, c):
    posb = _dot_f32(pos_ref[c], pw_ref[c])[0:1]
    return jnp.concatenate([posb] * NSA_G, axis=1)


def _compress_weights(phi_pos, phi_w1, phi_w2):
    eye = jnp.eye(NSA_G, dtype=F32)
    w1 = phi_w1.reshape(CMP_HALF, CMP_STRIDE, 2, NSA_DK, NSA_DK)
    w1big = jnp.einsum('mlcde,gh->clgdmhe', w1, eye).reshape(
        2, CMP_STRIDE * KV_COLS, CMP_HALF * KV_COLS).astype(BF16)
    w2big = jnp.einsum('cde,gh->cgdhe', phi_w2, eye).reshape(2, KV_COLS, KV_COLS).astype(BF16)
    pos = jnp.transpose(phi_pos, (1, 0, 2)).reshape(2, 1, CMP_LEN * NSA_DK)
    pos = jnp.broadcast_to(pos, (2, 8, CMP_LEN * NSA_DK))
    pw = jnp.transpose(phi_w1, (1, 0, 2, 3)).reshape(2, CMP_LEN * NSA_DK, NSA_DK)
    return w1big, pos, pw, w2big


def _compress_kernel(x_ref, xn_ref, w1_ref, pos_ref, pw_ref, w2_ref, o_ref, *, rc):
    x = x_ref[0]
    xn = xn_ref[0]
    last_row = lax.broadcasted_iota(jnp.int32, (rc, KV_COLS), 0) == rc - 1
    for c in range(2):
        w1 = w1_ref[c]
        y = _dot(_gather_cv(x, c).astype(BF16), w1)
        yn = _dot(_gather_cv(xn, c).astype(BF16), w1)
        second = pltpu.roll(y[:, KV_COLS:], rc - 1, 0)
        second = jnp.where(last_row, yn[0:1, KV_COLS:], second)
        pre = y[:, :KV_COLS] + second + _pos_bias(pos_ref, pw_ref, c)
        o_ref[0, :, c * KV_COLS:(c + 1) * KV_COLS] = _dot(jax.nn.gelu(pre).astype(BF16), w2_ref[c])


def _pick_rows(n, cap):
    best = 8
    for rc in range(8, cap + 1, 8):
        if n % rc == 0:
            best = rc
    return best


def _compress(x, cw):
    bsz, n, _ = x.shape
    rc = _pick_rows(n, 256)
    w1big, pos, pw, w2big = cw
    nb8 = n // 8
    return pl.pallas_call(
        functools.partial(_compress_kernel, rc=rc),
        grid=(bsz, n // rc),
        in_specs=[pl.BlockSpec((1, rc, x.shape[2]), lambda b, i: (b, i, 0)),
                  pl.BlockSpec((1, 8, x.shape[2]),
                               lambda b, i: (b, jnp.minimum((i + 1) * (rc // 8), nb8 - 1), 0)),
                  pl.BlockSpec(w1big.shape, lambda b, i: (0, 0, 0)),
                  pl.BlockSpec(pos.shape, lambda b, i: (0, 0, 0)),
                  pl.BlockSpec(pw.shape, lambda b, i: (0, 0, 0)),
                  pl.BlockSpec(w2big.shape, lambda b, i: (0, 0, 0))],
        out_specs=pl.BlockSpec((1, rc, ROW_W), lambda b, i: (b, i, 0)),
        out_shape=jax.ShapeDtypeStruct((bsz, n, ROW_W), F32),
        compiler_params=_cparams(("parallel", "arbitrary")),
        name="compress",
    )(x, x, w1big, pos, pw, w2big)


def _compress_tokens(kv_c, cw):
    bsz, t = kv_c.shape[:2]
    n = t // CMP_STRIDE
    npad = _round_up(n, LANES if n >= LANES else 8)
    x = kv_c.reshape(bsz, n, CMP_STRIDE * ROW_W)
    x = jnp.pad(x, ((0, 0), (0, npad - n), (0, 0)))
    return _compress(x, cw)[:, :n - (CMP_HALF - 1)]


def _rel_bucket(d):
    n = jnp.maximum(d, 0)
    large = jnp.full(d.shape, REL_BUCKETS // 2, jnp.int32)
    for thr in REL_THRESHOLDS:
        large = large + (n >= thr).astype(jnp.int32)
    return jnp.where(n < REL_BUCKETS // 2, n, large)


def _rel_bias_lanes(d, tab_ref):
    bkt = _rel_bucket(d)
    f = jnp.zeros(d.shape, F32)
    for k in range(REL_BUCKETS):
        f = jnp.where(bkt == k, tab_ref[0, k:k + 1, :], f)
    return f


def _rel_bias_rows(d, tab_ref):
    bkt = _rel_bucket(d)
    f = jnp.zeros(d.shape, F32)
    for k in range(REL_BUCKETS):
        f = jnp.where(bkt == k, tab_ref[:, k:k + 1], f)
    return f


def _near_tile_kernel(tab_ref, o_ref):
    t = pl.program_id(1)
    delta = jnp.where(t < N_NEAR, t * QT, jnp.where(t == TN_WIN, WINDOW, 0))
    dmax = jnp.where(t == TN_WIN, WINDOW, 1 << 30)
    j = lax.broadcasted_iota(jnp.int32, (QT, HQ), 0)
    i = lax.broadcasted_iota(jnp.int32, (QT, HQ), 1) & (QT - 1)
    d = delta + i - j
    far = tab_ref[0, REL_BUCKETS - 1:REL_BUCKETS, :]
    val = jnp.where((d >= 0) & (d <= dmax), _rel_bias_lanes(d, tab_ref) - far, NEG)
    val = jnp.where(t == TN_MASKED, NEG, jnp.where(t == TN_ZERO, 0.0, val))
    o_ref[0, 0] = val


def _cmp_window_kernel(tab_ref, o_ref):
    e = lax.broadcasted_iota(jnp.int32, (WT_ROWS, HQ), 0) - WT_LEAD
    i = lax.broadcasted_iota(jnp.int32, (WT_ROWS, HQ), 1) & (QT - 1)
    d = i - (CMP_LEN - 1) - CMP_STRIDE * e
    far = tab_ref[0, REL_BUCKETS - 1:REL_BUCKETS, :]
    o_ref[0] = jnp.where(d >= 0, _rel_bias_lanes(d, tab_ref) - far, 0.0)


def _prompt_bias_tiles(rel_bias):
    tab = jnp.repeat((rel_bias * LOG2E).reshape(REL_BUCKETS, NSA_G, NSA_HPG), QT, axis=2)
    tab = jnp.transpose(tab, (1, 0, 2))
    near = pl.pallas_call(
        _near_tile_kernel,
        grid=(NSA_G, TN_COUNT),
        in_specs=[pl.BlockSpec((1, REL_BUCKETS, HQ), lambda g, t: (g, 0, 0))],
        out_specs=pl.BlockSpec((1, 1, QT, HQ), lambda g, t: (g, t, 0, 0)),
        out_shape=jax.ShapeDtypeStruct((NSA_G, TN_COUNT, QT, HQ), F32),
        compiler_params=_cparams(("parallel", "arbitrary")),
        name="bias_near_tiles",
    )(tab)
    cwin = pl.pallas_call(
        _cmp_window_kernel,
        grid=(NSA_G,),
        in_specs=[pl.BlockSpec((1, REL_BUCKETS, HQ), lambda g: (g, 0, 0))],
        out_specs=pl.BlockSpec((1, WT_ROWS, HQ), lambda g: (g, 0, 0)),
        out_shape=jax.ShapeDtypeStruct((NSA_G, WT_ROWS, HQ), F32),
        compiler_params=_cparams(("parallel",)),
        name="bias_cmp_window",
    )(tab)
    return near, cwin


GROUPS_PER_STEP = 2
N_WIN_TILES = WINDOW // QT + 1
NSA_SEGMENTS = 4


def _nsa_group(q_ref, kc_ref, vct_ref, ks_ref, vst_ref, kw_refs, vwt_refs, tn_ref, wt_ref,
               g_ref, o_ref, s_ref, imp_ref, mb_ref, m_ref, acc_ref,
               buf0_ref, buf1_ref, tmax_ref, part_ref, *, ncp, nblk, n_sel, b_off):
    b = pl.program_id(1) + b_off
    qs = b * QT
    q = q_ref[0, 0]

    sc = _dot_nt(kc_ref[0], q)
    ci = lax.broadcasted_iota(jnp.int32, (ncp, HQ), 0)
    li = lax.broadcasted_iota(jnp.int32, (ncp, HQ), 1) & (QT - 1)
    vis = (CMP_STRIDE * ci + (CMP_LEN - 1) - li) <= qs
    s_ref[0:WT_LEAD, :] = jnp.full((WT_LEAD, HQ), NEG, F32)
    s_ref[WT_LEAD + ncp:, :] = jnp.full((s_ref.shape[0] - WT_LEAD - ncp, HQ), NEG, F32)
    s_ref[WT_LEAD:WT_LEAD + ncp, :] = jnp.where(vis, sc, NEG)
    r0 = pl.multiple_of(8 * b, 8)
    s_ref[pl.ds(r0, WT_ROWS), :] += wt_ref[0]
    yield None
    s = s_ref[WT_LEAD:WT_LEAD + ncp, :]
    m = jnp.max(s, axis=0, keepdims=True)
    m = jnp.where(m <= 0.1 * NEG, 0.0, m)
    p = jnp.exp2(s - m)
    rinv = 1.0 / jnp.maximum(jnp.sum(p, axis=0, keepdims=True), 1e-30)
    o_c = _dot(vct_ref[0], p.astype(BF16)) * rinv
    yield None
    pn = p * rinv
    imp = pn[:, 0:QT]
    for h in range(1, NSA_HPG):
        imp = imp + pn[:, h * QT:(h + 1) * QT]
    imp_ref[0:8, :] = jnp.zeros((8, QT), F32)
    imp_ref[8:8 + ncp, :] = imp
    if imp_ref.shape[0] > 8 + ncp:
        imp_ref[8 + ncp:, :] = jnp.zeros((imp_ref.shape[0] - 8 - ncp, QT), F32)
    ratio = SEL_BLOCK // CMP_STRIDE
    span = ratio + CMP_HALF - 1
    psl = imp_ref[pl.ds(8 - (CMP_HALF - 1), nblk, stride=ratio), :]
    for mm in range(1, span):
        psl = psl + imp_ref[pl.ds(8 - (CMP_HALF - 1) + mm, nblk, stride=ratio), :]

    yield None
    n_win = WINDOW // QT
    st_parts, vt_parts = [], []

    def window_tile(w):
        kt = b - n_win + w
        tn_idx = jnp.where(kt >= 0, TN_WIN if w == 0 else n_win - w, TN_MASKED)
        st_parts.append(_dot_nt(kw_refs[w][0, 0], q) + tn_ref[0, tn_idx])
        vt_parts.append(vwt_refs[w][0, 0])

    def window_finish():
        st = jnp.concatenate(st_parts, axis=0)
        p = jnp.exp2(st - jnp.max(st, axis=0, keepdims=True))
        o_w = _dot(jnp.concatenate(vt_parts, axis=1), p.astype(BF16)) * (
            1.0 / jnp.sum(p, axis=0, keepdims=True))
        g = jax.nn.sigmoid(g_ref[0, 0])
        part_ref[...] = g[0:1] * o_c + g[2:3] * o_w

    window_pieces = [functools.partial(window_tile, w) for w in range(n_win + 1)] + [window_finish]

    ji = lax.broadcasted_iota(jnp.int32, (nblk, QT), 0)
    ii = lax.broadcasted_iota(jnp.int32, (nblk, QT), 1)
    cur = (qs + ii) // SEL_BLOCK
    jf = ji.astype(F32)
    forced = (ji == 0) | (ji == cur) | (ji == cur - 1)
    score0 = jnp.where(forced, -jnp.inf, jnp.where(ji <= cur, psl, -BIG))

    score, selb = score0, jnp.where(forced, 0.0, NEG)
    for _ in range(max(n_sel - N_FORCED, 0)):
        yield None
        mx = jnp.max(score, axis=0, keepdims=True)
        first = jnp.min(jnp.where(score == mx, jf, 1e9), axis=0, keepdims=True)
        pick = jf == first
        score, selb = jnp.where(pick, -jnp.inf, score), jnp.where(pick, 0.0, selb)
        if window_pieces:
            window_pieces.pop(0)()
    while window_pieces:
        window_pieces.pop(0)()
    mb_ref[0:nblk, :] = jnp.concatenate([selb] * NSA_HPG, axis=1)
    mb_ref[nblk:, :] = jnp.full((BLK_PER_KT, HQ), NEG, F32)

    m_ref[...] = jnp.full(m_ref.shape, NEG, F32)
    acc_ref[...] = jnp.zeros(acc_ref.shape, F32)

    def sel_logits(kt, valid):
        ktc = jnp.where(valid, kt, 0)
        st = _dot_nt(ks_ref[0, ktc], q)
        row0 = jnp.where(valid, ktc * BLK_PER_KT, nblk)
        mb = mb_ref[pl.ds(pl.multiple_of(row0, BLK_PER_KT), BLK_PER_KT), :]
        st = jnp.concatenate(
            [st[k * SEL_BLOCK:(k + 1) * SEL_BLOCK] + mb[k:k + 1] for k in range(BLK_PER_KT)], axis=0)
        return st, ktc

    def softmax_step(st, tmax, vt):
        nk = vt.shape[1]
        ones = jnp.where(lax.broadcasted_iota(jnp.int32, (V_ROWS - NSA_DK, nk), 0) == 0, 1.0, 0.0)
        vt = jnp.concatenate([vt, ones.astype(BF16)], axis=0)
        m_old = m_ref[...]
        m_new = jnp.maximum(m_old, tmax)
        alpha = jnp.exp2(m_old - m_new)
        p = jnp.exp2(st - m_new)
        acc_ref[...] = alpha * acc_ref[...] + _dot(vt, p.astype(BF16))
        m_ref[...] = m_new

    nt = b // SUB_PER_KT + 1
    n_far = jnp.maximum(nt - NEAR_TILES, 0)

    def far_logits(kt, buf_ref, slot):
        st, _ = sel_logits(kt, kt < n_far)
        buf_ref[...] = st
        tmax_ref[slot:slot + 1, :] = jnp.max(st, axis=0, keepdims=True)

    def far_first():
        far_logits(0, buf0_ref, 0)

    def far_pair(pi):
        k0 = 2 * pi
        far_logits(k0 + 1, buf1_ref, 1)
        softmax_step(buf0_ref[...], tmax_ref[0:1, :], vst_ref[0, k0])
        far_logits(k0 + 2, buf0_ref, 0)
        k1 = jnp.minimum(k0 + 1, jnp.maximum(n_far - 1, 0))
        softmax_step(buf1_ref[...], tmax_ref[1:2, :], vst_ref[0, k1])

    yield far_first, far_pair

    st_parts, vt_parts = [], []
    for w in range(NEAR_TILES):
        kt = nt - NEAR_TILES + w
        st, ktc = sel_logits(kt, kt >= 0)
        r = b - SUB_PER_KT * ktc
        parts = []
        for u in range(SUB_PER_KT):
            ru = r - u
            idx = jnp.where(ru < 0, TN_MASKED, jnp.where(ru >= N_NEAR, TN_ZERO, ru))
            parts.append(tn_ref[0, idx])
        st_parts.append(st + jnp.concatenate(parts, axis=0))
        vt_parts.append(vst_ref[0, ktc])
    yield None
    st = jnp.concatenate(st_parts, axis=0)
    softmax_step(st, jnp.max(st, axis=0, keepdims=True), jnp.concatenate(vt_parts, axis=1))
    acc = acc_ref[...]
    o_s = acc[0:NSA_DK] * (1.0 / acc[NSA_DK:NSA_DK + 1])
    total = part_ref[...] + jax.nn.sigmoid(g_ref[0, 0, 1:2]) * o_s
    pairs = [jnp.concatenate([total[:, h * QT:(h + 1) * QT], total[:, (h + 1) * QT:(h + 2) * QT]], axis=0).T
             for h in range(0, NSA_HPG, 2)]
    o_ref[...] = jnp.concatenate(pairs, axis=1)


GROUP_COLS = NSA_HPG * NSA_DK


def _nsa_kernel(*refs, n_scratch, **static):
    gp = GROUPS_PER_STEP
    refs = refs[1:]
    n_all = gp * n_scratch
    ins, o_ref = refs[:len(refs) - n_all - 1], refs[len(refs) - n_all - 1]
    scratch = refs[len(refs) - n_all:]

    b = pl.program_id(1) + static["b_off"]
    n_far = jnp.maximum(b // SUB_PER_KT + 1 - NEAR_TILES, 0)
    programs = []
    for gi in range(gp):
        v = [r.at[pl.ds(gi, 1)] for r in ins]
        kw_refs, vwt_refs = v[5:5 + N_WIN_TILES], v[5 + N_WIN_TILES:5 + 2 * N_WIN_TILES]
        rest = v[5 + 2 * N_WIN_TILES:] + [o_ref.at[:, pl.ds(gi * GROUP_COLS, GROUP_COLS)]]
        programs.append(_nsa_group(*v[:5], kw_refs, vwt_refs, *rest,
                                   *scratch[gi * n_scratch:(gi + 1) * n_scratch], **static))
    loops = [None] * gp
    while any(lp is None for lp in loops):
        for gi, p in enumerate(programs):
            if loops[gi] is None:
                loops[gi] = next(p)

    @pl.when(n_far > 0)
    def _():
        for far_first, _ in loops:
            far_first()

    def in_turns(generators):
        active = list(generators)
        while active:
            for p in list(active):
                if next(p, StopIteration) is StopIteration:
                    active.remove(p)

    def far_body(pi, carry):
        for _, far_pair in loops:
            far_pair(pi)
        return carry

    lax.fori_loop(0, (n_far + 1) // 2, far_body, 0)
    in_turns(programs)


def _nsa_attend(qt, kc, vct, ks, vst, kw, vwt, tn, wt, gt, *, n_sel_blk):
    s_n, nq = qt.shape[:2]
    gp = GROUPS_PER_STEP
    assert kc.shape[1] >= 8 * nq and ks.shape[1] * SUB_PER_KT >= nq and s_n % gp == 0
    n_sel = min(N_SEL, n_sel_blk)
    n_seg = NSA_SEGMENTS if nq % (NSA_SEGMENTS * SUB_PER_KT) == 0 else 1
    nq_seg = nq // n_seg
    y = jnp.zeros((nq * QT, s_n * GROUP_COLS), F32)
    for seg in range(n_seg):
        b_off = seg * nq_seg
        b_end = b_off + nq_seg
        ncp = min(_round_up(8 * b_end, LANES), kc.shape[1])
        nkt = b_end // SUB_PER_KT
        nblk = nkt * BLK_PER_KT
        imp_rows = max(8 + ncp, 8 + (SEL_BLOCK // CMP_STRIDE) * nblk)
        scratch = [(ncp + WT_ROWS, HQ), (imp_rows, QT), (nblk + BLK_PER_KT, HQ), (1, HQ), (V_ROWS, HQ),
                   (KT_SEL, HQ), (KT_SEL, HQ), (8, HQ), (NSA_DK, HQ)]
        kern = functools.partial(_nsa_kernel, n_scratch=len(scratch), ncp=ncp, nblk=nblk, n_sel=n_sel,
                                 b_off=b_off)
        lead = lambda a, n: pl.BlockSpec((gp, n) + a.shape[2:], lambda s, i: (s,) + (0,) * (a.ndim - 1),
                                         pipeline_mode=pl.Buffered(1))
        per_tile = lambda a: pl.BlockSpec((gp, 1) + a.shape[2:], lambda s, i: (s, i + b_off, 0, 0))
        win = lambda a, w: pl.BlockSpec(
            (gp, 1) + a.shape[2:],
            lambda s, i: (s, jnp.maximum(i + b_off - (N_WIN_TILES - 1) + w, 0), 0, 0))
        in_specs = ([pl.BlockSpec(memory_space=pl.ANY), per_tile(qt), lead(kc, ncp),
                     pl.BlockSpec((gp, NSA_DK, ncp), lambda s, i: (s, 0, 0), pipeline_mode=pl.Buffered(1)),
                     lead(ks, nkt), lead(vst, nkt)]
                    + [win(kw, w) for w in range(N_WIN_TILES)]
                    + [win(vwt, w) for w in range(N_WIN_TILES)]
                    + [lead(tn, tn.shape[1]), lead(wt, wt.shape[1]), per_tile(gt)])
        args = (y, qt, kc, vct, ks, vst, *([kw] * N_WIN_TILES), *([vwt] * N_WIN_TILES), tn, wt, gt)
        y = pl.pallas_call(
            kern,
            grid=(s_n // gp, nq_seg),
            in_specs=in_specs,
            out_specs=pl.BlockSpec((QT, gp * GROUP_COLS), lambda s, i: (i + b_off, s)),
            out_shape=jax.ShapeDtypeStruct((nq * QT, s_n * GROUP_COLS), F32),
            scratch_shapes=[pltpu.VMEM(s, F32) for s in scratch] * gp,
            input_output_aliases={0: 0},
            compiler_params=_cparams(("parallel", "arbitrary")),
            name="nsa_prompt",
        )(*args)
    return y


def _k_tiles(kv, tile):
    nt = kv.shape[0] // tile
    k = kv[..., :KV_COLS].reshape(nt, tile, NSA_G, NSA_DK)
    k = jnp.transpose(k, (2, 0, 1, 3))
    k = jnp.pad(k, ((0, 0), (0, 0), (0, 0), (0, LANES - NSA_DK))).astype(BF16)
    v = kv[..., KV_COLS:].reshape(nt, tile, NSA_G, NSA_DK)
    v = jnp.transpose(v, (2, 0, 3, 1)).astype(BF16)
    return k, v


def _prompt_attend(qt, gt, tok, ks, vst, kw, vwt, bias):
    nq = qt.shape[1]
    t = nq * QT
    near, cwin = bias
    ncp = max(_round_up(tok.shape[0] + 1, LANES), _round_up(8 * nq, LANES))
    kc, vct = _k_tiles(jnp.pad(tok, ((0, ncp - tok.shape[0]), (0, 0))), ncp)
    return _nsa_attend(qt, kc[:, 0], vct[:, 0], ks, vst, kw, vwt, near, cwin, gt,
                       n_sel_blk=t // SEL_BLOCK)


def _sample_bias_kernel(tab_ref, cb_ref, sb_ref, snb_ref, wb_ref, *, past, dseq, ns):
    rows = tab_ref.shape[0]
    far = tab_ref[:, REL_BUCKETS - 1:REL_BUCKETS]

    def tok_of(n):
        return lax.broadcasted_iota(jnp.int32, (rows, n), 0) % dseq

    def lane(n):
        return lax.broadcasted_iota(jnp.int32, (rows, n), 1)

    for idx in range(2):
        rho = LANES * (ns - 1 + idx) + lane(LANES)
        d = past + tok_of(LANES) - (CMP_STRIDE * (rho - 1) + CMP_LEN - 1)
        cb_ref[idx] = jnp.where(d >= 0, _rel_bias_rows(d, tab_ref) - far, 0.0)
    d = SP_KEYS + tok_of(SP_KEYS) - lane(SP_KEYS)
    sb_ref[...] = _rel_bias_rows(d, tab_ref) - far
    d = tok_of(LANES) - lane(LANES)
    snb_ref[...] = jnp.where(d >= 0, _rel_bias_rows(d, tab_ref) - far, NEG)
    d = WINDOW + tok_of(WINDOW + LANES) - lane(WINDOW + LANES)
    wb_ref[...] = jnp.where((d >= 0) & (d <= WINDOW), _rel_bias_rows(d, tab_ref), NEG)


def _sample_bias(rel_bias, past, dseq, ns):
    rows = NSA_HEADS * dseq
    tab = (rel_bias * LOG2E).reshape(REL_BUCKETS, NSA_G, NSA_HPG)
    tab = jnp.transpose(tab, (2, 1, 0))
    tab = jnp.repeat(tab.reshape(NSA_HEADS, 1, REL_BUCKETS), dseq, axis=1).reshape(rows, REL_BUCKETS)
    tab = jnp.pad(tab, ((0, 0), (0, LANES - REL_BUCKETS)))
    shapes = [(2, rows, LANES), (rows, SP_KEYS), (rows, LANES), (rows, WINDOW + LANES)]
    cb, sb, snb, wb = pl.pallas_call(
        functools.partial(_sample_bias_kernel, past=past, dseq=dseq, ns=ns),
        out_shape=[jax.ShapeDtypeStruct(s, F32) for s in shapes],
        compiler_params=pltpu.CompilerParams(vmem_limit_bytes=VMEM_LIMIT),
        name="bias_sample",
    )(tab)
    return cb, sb, snb, wb


def _sample_nsa_kernel(pt_ref, *refs, ns, past, dseq, n_sel, n_sel_blk):
    cmp_pages = refs[:SP_PP]
    slc_pages = refs[SP_PP:2 * SP_PP]
    (xn_ref, ksn_ref, win_ref, kwn_ref, q_ref, g_ref, w1_ref, pos_ref, pw_ref, w2_ref, e_ref,
     cb_ref, sb_ref, snb_ref, wb_ref, o_ref,
     s_ref, vc_ref, carry_ref, sel_ref, m_ref, l_ref, acc_ref, oc_ref, x_ref) = refs[2 * SP_PP:]
    j = pl.program_id(1)
    qbd = q_ref[0]
    rows = qbd.shape[0]
    ntile = ns + 1
    nblk_l = sel_ref.shape[0] * SP_BLKS

    def feat_major(refs_, c):
        return jnp.concatenate([r[0, c].reshape(KV_COLS, r.shape[-1]) for r in refs_], axis=1)

    def tokens_of(gather, prepare=None):
        out = []
        for c in range(2):
            if prepare is not None:
                prepare(c)
            xc = gather(c)
            n = xc.shape[0]
            row0 = lax.broadcasted_iota(jnp.int32, (n, KV_COLS), 0) == 0
            y = _dot(xc.astype(BF16), w1_ref[c])
            first = jnp.where(row0, carry_ref[c:c + 1, :], pltpu.roll(y[:, :KV_COLS], 1, 0))
            carry_ref[c:c + 1, :] = y[n - 1:n, :KV_COLS]
            pre = first + y[:, KV_COLS:] + _pos_bias(pos_ref, pw_ref, c)
            out.append(_dot(jax.nn.gelu(pre).astype(BF16), w2_ref[c]))
        return out

    def online_update(s, v):
        m_old = m_ref[...]
        m_new = jnp.maximum(m_old, jnp.max(s, axis=1, keepdims=True))
        alpha = jnp.exp2(m_old - m_new)
        p = jnp.exp2(s - m_new)
        l_ref[...] = alpha * l_ref[...] + jnp.sum(p, axis=1, keepdims=True)
        acc_ref[...] = alpha * acc_ref[...] + _dot_nt(p.astype(BF16), v)
        m_ref[...] = m_new

    @pl.when(j == 0)
    def _():
        carry_ref[...] = jnp.zeros(carry_ref.shape, F32)

    @pl.when(j < ns)
    def _():
        def to_rows(c):
            for p, r in enumerate(cmp_pages):
                for gp in range(2):
                    tile = r[0, c, 2 * gp:2 * gp + 2].reshape(LANES, PAGE_SIZE)
                    x_ref[2 * c + gp, p * PAGE_SIZE:(p + 1) * PAGE_SIZE, :] = tile.T

        def gather(c):
            return jnp.concatenate(
                [pltpu.einshape("(nl)k->n(lk)", x_ref[2 * c + gp], l=CMP_STRIDE) for gp in range(2)], axis=1)

        tok_k, tok_v = tokens_of(gather, to_rows)
        s_ref[j] = _dot_nt(qbd, tok_k.astype(BF16))
        vc_ref[j] = tok_v.astype(BF16)

    @pl.when(j == ns - 1)
    def _():
        half = KV_COLS // 2
        tok_k, tok_v = tokens_of(lambda c: jnp.concatenate(
            [xn_ref[0, :, l * ROW_W + c * KV_COLS + gp * half:l * ROW_W + c * KV_COLS + (gp + 1) * half]
             for gp in range(2) for l in range(CMP_STRIDE)], axis=1))
        zpad = jnp.zeros((LANES - tok_k.shape[0], KV_COLS), F32)
        s_ref[ns] = _dot_nt(qbd, jnp.concatenate([tok_k, zpad], axis=0).astype(BF16))
        vc_ref[ns] = jnp.concatenate([tok_v, zpad], axis=0).astype(BF16)

        tiles = []
        for t in range(ntile):
            st = s_ref[t]
            if t >= ns - 1:
                st = st + cb_ref[t - (ns - 1)]
            tiles.append(st)
        s = jnp.concatenate(tiles, axis=1)
        width = ntile * LANES
        rho = lax.broadcasted_iota(jnp.int32, (rows, width), 1)
        tq = lax.broadcasted_iota(jnp.int32, (rows, width), 0) % dseq
        n_cmp = (past + _round_up(dseq, SEL_BLOCK)) // CMP_STRIDE - (CMP_HALF - 1)
        vis = (rho >= 1) & (rho <= n_cmp) & (CMP_STRIDE * (rho - 1) + CMP_LEN - 1 <= past + tq)
        s = jnp.where(vis, s, NEG)
        m = jnp.max(s, axis=1, keepdims=True)
        m = jnp.where(m <= 0.1 * NEG, 0.0, m)
        p = jnp.exp2(s - m)
        rinv = 1.0 / jnp.maximum(jnp.sum(p, axis=1, keepdims=True), 1e-30)
        vc = jnp.concatenate([vc_ref[t] for t in range(ntile)], axis=0)
        oc_ref[...] = _dot(p.astype(BF16), vc) * rinv
        pn = p * rinv
        gt = NSA_G * dseq
        imp = pn[0:gt]
        for h in range(1, NSA_HPG):
            imp = imp + pn[h * gt:(h + 1) * gt]
        ratio = SEL_BLOCK // CMP_STRIDE
        span = ratio + CMP_HALF - 1
        ar = lax.broadcasted_iota(jnp.int32, (width, nblk_l), 0)
        ac = lax.broadcasted_iota(jnp.int32, (width, nblk_l), 1)
        band = jnp.where((ar >= ratio * ac) & (ar < ratio * ac + span), 1.0, 0.0).astype(BF16)
        i_hi = imp.astype(BF16)
        r1 = imp - i_hi.astype(F32)
        i_mid = r1.astype(BF16)
        i_lo = (r1 - i_mid.astype(F32)).astype(BF16)
        psl = _dot(i_hi, band) + _dot(i_mid, band) + _dot(i_lo, band)
        ji = lax.broadcasted_iota(jnp.int32, (gt, nblk_l), 1)
        tq2 = lax.broadcasted_iota(jnp.int32, (gt, nblk_l), 0) % dseq
        cur = (past + tq2) // SEL_BLOCK
        jf = ji.astype(F32)
        forced = (ji == 0) | (ji == cur) | (ji == cur - 1)
        score0 = jnp.where(forced, -jnp.inf, jnp.where(ji <= cur, psl, -BIG))
        score0 = jnp.where(ji < n_sel_blk, score0, -jnp.inf)

        def pick_one(_, carry):
            score, sel = carry
            mx = jnp.max(score, axis=1, keepdims=True)
            first = jnp.min(jnp.where(score == mx, jf, 1e9), axis=1, keepdims=True)
            pick = jf == first
            return jnp.where(pick, -jnp.inf, score), jnp.where(pick, 1.0, sel)

        _, sel = lax.fori_loop(0, max(n_sel - N_FORCED, 0), pick_one,
                               (score0, jnp.where(forced, 1.0, 0.0)))
        for t in range(sel_ref.shape[0]):
            piece = sel[:, t * SP_BLKS:(t + 1) * SP_BLKS]
            piece = jnp.concatenate([piece, jnp.zeros((gt, LANES - SP_BLKS), F32)], axis=1)
            sel_ref[t] = jnp.concatenate([piece] * NSA_HPG, axis=0).astype(BF16)
        m_ref[...] = jnp.full(m_ref.shape, NEG, F32)
        l_ref[...] = jnp.zeros(l_ref.shape, F32)
        acc_ref[...] = jnp.zeros(acc_ref.shape, F32)

    @pl.when(j >= ns)
    def _():
        j2 = j - ns
        s = _dot(qbd, feat_major(slc_pages, 0).astype(BF16))
        chosen = _dot(sel_ref[j2], e_ref[...])
        s = s + (chosen - 1.0) * (-NEG)
        s = s + jnp.where(j2 == ns - 1, sb_ref[...], 0.0)
        online_update(s, feat_major(slc_pages, 1).astype(BF16))

    @pl.when(j == 2 * ns - 1)
    def _():
        online_update(_dot(qbd, ksn_ref[0, 0].astype(BF16)) + snb_ref[...],
                      ksn_ref[0, 1].astype(BF16))
        o_s = acc_ref[...] * (1.0 / l_ref[...])
        kw = jnp.concatenate([feat_major([win_ref], 0), kwn_ref[0, 0]], axis=1)
        vw = jnp.concatenate([feat_major([win_ref], 1), kwn_ref[0, 1]], axis=1)
        s = _dot(qbd, kw.astype(BF16)) + wb_ref[...]
        p = jnp.exp2(s - jnp.max(s, axis=1, keepdims=True))
        o_w = _dot_nt(p.astype(BF16), vw.astype(BF16)) * (1.0 / jnp.sum(p, axis=1, keepdims=True))
        g = jax.nn.sigmoid(g_ref[0])
        o_ref[0] = g[:, 0:1] * oc_ref[...] + g[:, 1:2] * o_s + g[:, 2:3] * o_w


def _sample_attend(q, gates, kv_c, kv_s, kv_w, cache_cmp, cache_slc, win_buf, page_table, cw,
                   rel_bias):
    bsz, dseq = q.shape[:2]
    n_pages = page_table.shape[1]
    past = n_pages * PAGE_SIZE
    assert n_pages % SP_PP == 0 and dseq <= SEL_BLOCK and win_buf.shape[1] == WINDOW
    feat_major = lambda a: jnp.transpose(a, (0, 2, 3, 4, 1))
    cache_cmp, cache_slc, win_buf = feat_major(cache_cmp), feat_major(cache_slc), feat_major(win_buf)
    ns = n_pages // SP_PP
    rows = NSA_HEADS * dseq
    n_sel_blk = past // SEL_BLOCK + 1
    n_sel = min(N_SEL, n_sel_blk)
    nsel_tiles = _round_up(-(-n_sel_blk // SP_BLKS), LANES // SP_BLKS)
    w1big, pos, pw, w2big = cw
    w1big = jnp.transpose(w1big.reshape(2, CMP_STRIDE, 2, LANES, -1), (0, 2, 1, 3, 4)).reshape(w1big.shape)
    cb, sb, snb, wb = _sample_bias(rel_bias, past, dseq, ns)

    qr = jnp.transpose((q * Q_SCALE).reshape(bsz, dseq, NSA_G, NSA_HPG, NSA_DK), (0, 3, 2, 1, 4))
    qbd = jnp.einsum('bhgtd,gk->bhgtkd', qr, jnp.eye(NSA_G, dtype=F32))
    qbd = qbd.reshape(bsz, rows, KV_COLS).astype(BF16)
    gr = gates[..., :N_BRANCH * NSA_HEADS].reshape(bsz, dseq, NSA_G, NSA_HPG, N_BRANCH)
    gr = jnp.transpose(gr, (0, 3, 2, 1, 4)).reshape(bsz, rows, N_BRANCH)
    gr = jnp.pad(gr, ((0, 0), (0, 0), (0, LANES - N_BRANCH)))
    pad_rows = lambda a, n: jnp.pad(a, ((0, 0), (0, n - a.shape[1]), (0, 0)))
    xn = pad_rows(kv_c, SEL_BLOCK).reshape(bsz, SEL_BLOCK // CMP_STRIDE, CMP_STRIDE * ROW_W)
    xn = pad_rows(xn, 8)
    new_feat_major = lambda a: jnp.transpose(
        pad_rows(a, LANES).reshape(bsz, LANES, 2, KV_COLS), (0, 2, 3, 1))
    ksn = new_feat_major(kv_s)
    kwn = new_feat_major(kv_w)
    expand = (jnp.arange(LANES)[:, None] == (jnp.arange(SP_KEYS) // SEL_BLOCK)[None, :]).astype(BF16)
    page_block = (1, 2, NSA_G, NSA_DK, PAGE_SIZE)

    def page_map(k, phase):
        if phase == 0:
            return lambda b, j, pt: (pt[b, jnp.minimum(j, ns - 1) * SP_PP + k], 0, 0, 0, 0)
        return lambda b, j, pt: (pt[b, jnp.maximum(j - ns, 0) * SP_PP + k], 0, 0, 0, 0)

    per_b = lambda a: pl.BlockSpec((1,) + a.shape[1:], lambda b, j, pt: (b,) + (0,) * (a.ndim - 1))
    const = lambda a: pl.BlockSpec(a.shape, lambda b, j, pt: (0,) * a.ndim)
    in_specs = ([pl.BlockSpec(page_block, page_map(k, 0)) for k in range(SP_PP)]
                + [pl.BlockSpec(page_block, page_map(k, 1)) for k in range(SP_PP)]
                + [per_b(xn), per_b(ksn), per_b(win_buf), per_b(kwn), per_b(qbd), per_b(gr),
                   pl.BlockSpec(w1big.shape, lambda b, j, pt: (0, 0, 0), pipeline_mode=pl.Buffered(1)),
                   const(pos), const(pw), const(w2big), const(expand),
                   const(cb), const(sb), const(snb), const(wb)])
    grid_spec = pltpu.PrefetchScalarGridSpec(
        num_scalar_prefetch=1, grid=(bsz, 2 * ns), in_specs=in_specs,
        out_specs=pl.BlockSpec((1, rows, KV_COLS), lambda b, j, pt: (b, 0, 0)),
        scratch_shapes=[pltpu.VMEM((ns + 1, rows, LANES), F32),
                        pltpu.VMEM((ns + 1, LANES, KV_COLS), BF16),
                        pltpu.VMEM((8, KV_COLS), F32),
                        pltpu.VMEM((nsel_tiles, rows, LANES), BF16),
                        pltpu.VMEM((rows, 1), F32),
                        pltpu.VMEM((rows, 1), F32),
                        pltpu.VMEM((rows, KV_COLS), F32),
                        pltpu.VMEM((rows, KV_COLS), F32),
                        pltpu.VMEM((ROW_W // LANES, SP_KEYS, LANES), F32)])
    out = pl.pallas_call(
        functools.partial(_sample_nsa_kernel, ns=ns, past=past, dseq=dseq, n_sel=n_sel,
                          n_sel_blk=n_sel_blk),
        grid_spec=grid_spec,
        out_shape=jax.ShapeDtypeStruct((bsz, rows, KV_COLS), F32),
        compiler_params=_cparams(("parallel", "arbitrary")),
        name="nsa_sample",
    )(page_table, *([cache_cmp] * SP_PP), *([cache_slc] * SP_PP), xn, ksn, win_buf, kwn, qbd, gr,
      w1big, pos, pw, w2big, expand, cb, sb, snb, wb)
    o = out.reshape(bsz, NSA_HPG, NSA_G, dseq, NSA_G, NSA_DK)
    o = jnp.einsum('bhgtge->btghe', o)
    return o.reshape(bsz * dseq, NSA_WIDTH)


def _out_proj_kernel(ycm_ref, ynsa_ref, h_ref, gt_ref, gpost_ref, w1_ref, w2_ref, o_ref):
    out = _dot(ycm_ref[...].astype(BF16), w1_ref[...]) + _dot(ynsa_ref[...].astype(BF16), w2_ref[...])
    o_ref[...] = h_ref[...] + gt_ref[...] * _rms(out, gpost_ref[...])


def _out_proj(ycm, ynsa, h, gate, g_post, w1, w2, tm):
    r, d = h.shape
    return pl.pallas_call(
        _out_proj_kernel,
        grid=(r // tm,),
        in_specs=[pl.BlockSpec((tm, CM_WIDTH), lambda i: (i, 0)),
                  pl.BlockSpec((tm, NSA_WIDTH), lambda i: (i, 0)),
                  pl.BlockSpec((tm, d), lambda i: (i, 0)),
                  _row_spec(gate, tm),
                  pl.BlockSpec((1, d), lambda i: (0, 0)),
                  pl.BlockSpec(w1.shape, lambda i: (0, 0)),
                  pl.BlockSpec(w2.shape, lambda i: (0, 0))],
        out_specs=pl.BlockSpec((tm, d), lambda i: (i, 0)),
        out_shape=jax.ShapeDtypeStruct((r, d), F32),
        compiler_params=_cparams(("parallel",)),
        name="out_proj",
    )(ycm, ynsa, h, gate, g_post, w1, w2)


def _row_tile(r, cap):
    tm = min(r, cap)
    while r % tm:
        tm //= 2
    return tm


def _layer(x, mod, per_row, weights, in_proj, mixer):
    r, d = x.shape
    (norm_pre, norm_post, ffn, w_out1, w_out2) = weights
    if per_row is None:
        mrow = lambda i, j: mod[0, i, j].reshape(1, d)
    else:
        mrow = lambda i, j: jnp.repeat(mod[:, i, j], per_row, axis=0)
    tm = _row_tile(r, 512)
    tm_ffn = _row_tile(r, FFN_TM)
    tf = _row_tile(ffn[0][0].shape[1], FFN_TF)
    h = _ffn(x, mrow(0, 0), mrow(0, 1), mrow(0, 2), norm_pre[0:1], norm_post[0:1], *ffn[0],
             res_w=0.5, tm=tm_ffn, tf=tf)
    mixed_cm, mixed_nsa, state = mixer(*in_proj(h, mrow(1, 0), mrow(1, 1), norm_pre[1:2]))
    h = _out_proj(mixed_cm, mixed_nsa, h, mrow(1, 2), norm_post[1:2], w_out1, w_out2, tm)
    h = _ffn(h, mrow(2, 0), mrow(2, 1), mrow(2, 2), norm_pre[2:3], norm_post[2:3], *ffn[1],
             res_w=0.5, tm=tm_ffn, tf=tf)
    return h, state


def kernel(x_prompt, x_sample, cache_cmp_kv, cache_slc_kv, state_win_kv, page_table, c_prompt,
           c_sample, w_mod, b_mod, norm_pre, norm_post, ffn_w_gate, ffn_w_up, ffn_w_down, w_in,
           w_out, cm_norm, cm_ws, cm_bs, phi_pos, phi_w1, phi_w2, rel_bias):
    depth = w_mod.shape[0]
    assert depth == 1
    bp, seq, d = x_prompt.shape
    bs, dseq, _ = x_sample.shape
    assert bp == 1 and seq % KT_SEL == 0 and seq >= WINDOW
    l = 0

    c_all = jnp.concatenate([c_prompt, c_sample], axis=0)
    mpad = _round_up(c_all.shape[0], 8)
    mod = _mod_proj(jnp.pad(c_all, ((0, mpad - c_all.shape[0]), (0, 0))), w_mod[l], b_mod[l])
    mod = mod[:bp + bs].reshape(bp + bs, 3, 3, d)

    ffn = [(ffn_w_gate[l, i].astype(BF16), ffn_w_up[l, i].astype(BF16), ffn_w_down[l, i].astype(BF16))
           for i in range(2)]
    w_in_p = jnp.pad(w_in[l], ((0, 0), (0, sum(IN_SEGS) - w_in.shape[2]))).astype(BF16)
    w_rows, w_feat = _in_proj_prompt_weights(w_in[l])
    w_o = w_out[l].astype(BF16)
    weights = (norm_pre[l], norm_post[l], ffn, w_o[:CM_WIDTH], w_o[CM_WIDTH:])
    cw = _compress_weights(phi_pos[l], phi_w1[l], phi_w2[l])
    prompt_bias = _prompt_bias_tiles(rel_bias)

    def prompt_in_proj(h, shift, scale, g_pre):
        return _in_proj_prompt(h, shift, scale, g_pre, w_rows, w_feat)

    def prompt_mixer(u, v, kv_c, qt, gt, ks, vst, kw, vwt, kvt):
        y_cm, _ = _chunk_mix(u, v, cm_norm[l], cm_ws[l], cm_bs[l], CHUNK)
        tok = _compress_tokens(kv_c[None], cw)[0]
        y_nsa = _prompt_attend(qt, gt, tok, ks, vst, kw, vwt, prompt_bias)
        return y_cm, y_nsa, kvt

    assert seq % KT_SEL == 0
    hp, kvt = _layer(x_prompt.reshape(seq, d), mod[:bp], None, weights, prompt_in_proj, prompt_mixer)
    kvt = jnp.transpose(kvt.reshape(N_BRANCH, 2, NSA_G, NSA_DK, seq), (0, 4, 1, 2, 3))
    pc, ps, pw = kvt[0], kvt[1], kvt[2]

    rows = _round_up(dseq, 16)

    def sample_mixer(u, v, q, kv_c, kv_s, kv_w, gates):
        per_seq = lambda a: a.reshape(bs, dseq, -1)
        pad_rows = lambda a, n: jnp.pad(per_seq(a), ((0, 0), (0, n - dseq), (0, 0)))
        y_cm, vn = _chunk_mix(pad_rows(u, rows).reshape(bs * rows, CM_WIDTH),
                              pad_rows(v, rows).reshape(bs * rows, CM_WIDTH),
                              cm_norm[l], cm_ws[l], cm_bs[l], rows)
        y_cm = y_cm.reshape(bs, rows, CM_WIDTH)[:, :dseq].reshape(bs * dseq, CM_WIDTH)
        vn = vn.reshape(bs, rows, CM_WIDTH)[:, :dseq]
        y_nsa = _sample_attend(per_seq(q), per_seq(gates), per_seq(kv_c), per_seq(kv_s), per_seq(kv_w),
                               cache_cmp_kv[l], cache_slc_kv[l], state_win_kv[l], page_table, cw,
                               rel_bias)
        return y_cm, y_nsa, (kv_c, kv_s, kv_w, vn)

    def sample_in_proj(h, shift, scale, g_pre):
        return _in_proj(h, shift, scale, g_pre, w_in_p, _row_tile(h.shape[0], 256))

    hs, (sc, ss, sw, sv) = _layer(x_sample.reshape(bs * dseq, d), mod[bp:], dseq, weights,
                                  sample_in_proj, sample_mixer)

    kvshape = lambda a, b, t: a.reshape(1, b, t, 2, NSA_G, NSA_DK)
    wp = min(WINDOW, seq)
    win_s = jnp.concatenate([state_win_kv[l].reshape(bs, -1, ROW_W), sw.reshape(bs, dseq, ROW_W)],
                            axis=1)[:, dseq:]
    return (hp.reshape(bp, seq, d), hs.reshape(bs, dseq, d),
            kvshape(pc, bp, seq), kvshape(sc, bs, dseq),
            kvshape(ps, bp, seq), kvshape(ss, bs, dseq),
            kvshape(pw[seq - wp:], bp, wp), kvshape(win_s, bs, win_s.shape[1]),
            sv.reshape(1, bs, dseq, CM_HEADS, CM_HEAD_DIM))
```

```python
import functools
import math

import jax
import jax.numpy as jnp
from jax import lax
from jax.experimental import pallas as pl
from jax.experimental.pallas import tpu as pltpu

F32 = jnp.float32
BF16 = jnp.bfloat16

CM_HEADS = 8
CM_HEAD_DIM = 128
CM_WIDTH = CM_HEADS * CM_HEAD_DIM
CHUNK = 128
NSA_HEADS = 16
NSA_G = 4
NSA_HPG = 4
NSA_DK = 64
NSA_WIDTH = NSA_HEADS * NSA_DK
KV_COLS = NSA_G * NSA_DK
ROW_W = 2 * KV_COLS
CMP_LEN = 32
CMP_STRIDE = 16
CMP_HALF = CMP_LEN // CMP_STRIDE
SEL_BLOCK = 64
N_SEL = 16
N_FORCED = 3
WINDOW = 512
N_BRANCH = 3
BIG = 1e4
REL_BUCKETS = 32
EPS = 1e-6
PAGE_SIZE = 128

LOG2E = 1.4426950408889634
Q_SCALE = NSA_DK ** -0.5 * LOG2E
NEG = -1e30
LANES = 128
QT = 128
HQ = NSA_HPG * QT
KT_SEL = 512
BLK_PER_KT = KT_SEL // SEL_BLOCK
SUB_PER_KT = KT_SEL // QT
N_NEAR = 8
NEAR_TILES = 3
V_ROWS = NSA_DK + 16
TN_WIN, TN_MASKED, TN_ZERO, TN_COUNT = 8, 9, 10, 11
WT_LEAD = 64
WT_ROWS = 200
REL_THRESHOLDS = (21, 27, 35, 46, 59, 77, 99, 128, 166, 216, 280, 363, 470, 609, 790)
VMEM_LIMIT = 56 * 1024 * 1024
SP_PP = CMP_STRIDE
SP_KEYS = SP_PP * PAGE_SIZE
SP_BLKS = SP_KEYS // SEL_BLOCK


def _cparams(sem):
    return pltpu.CompilerParams(dimension_semantics=sem, vmem_limit_bytes=VMEM_LIMIT)


def _dot(a, b):
    return jnp.dot(a, b, preferred_element_type=F32)


def _dot_nt(a, b):
    return lax.dot_general(a, b, (((1,), (1,)), ((), ())), preferred_element_type=F32)


def _split_bf16(x):
    hi = x.astype(BF16)
    lo = (x - hi.astype(F32)).astype(BF16)
    return hi, lo


def _dot_f32(a, b):
    a_hi, a_lo = _split_bf16(a)
    b_hi, b_lo = _split_bf16(b)
    return _dot(a_hi, b_hi) + _dot(a_lo, b_hi) + _dot(a_hi, b_lo)


def _rms(x, g):
    return x * lax.rsqrt(jnp.mean(x * x, axis=-1, keepdims=True) + EPS) * g


def _row_spec(arr, tm):
    d = arr.shape[-1]
    if arr.shape[0] == 1:
        return pl.BlockSpec((1, d), lambda *idx: (0, 0))
    return pl.BlockSpec((tm, d), lambda *idx: (idx[0], 0))


def _round_up(x, m):
    return -(-x // m) * m


def _mod_kernel(c_ref, w_ref, b_ref, o_ref):
    c = c_ref[...]
    s = c * jax.nn.sigmoid(c)
    o_ref[...] = _dot_f32(s, w_ref[...]) + b_ref[...]


def _mod_proj(c, w_mod, b_mod):
    m, d = c.shape
    n = w_mod.shape[1]
    tn = 512
    return pl.pallas_call(
        _mod_kernel,
        grid=(n // tn,),
        in_specs=[pl.BlockSpec((m, d), lambda j: (0, 0)),
                  pl.BlockSpec((d, tn), lambda j: (0, j)),
                  pl.BlockSpec((1, tn), lambda j: (0, j))],
        out_specs=pl.BlockSpec((m, tn), lambda j: (0, j)),
        out_shape=jax.ShapeDtypeStruct((m, n), F32),
        compiler_params=_cparams(("arbitrary",)),
        name="mod_proj",
    )(c, w_mod, b_mod.reshape(1, n))


def _ffn_kernel(x_ref, sh_ref, sc_ref, gt_ref, gpre_ref, gpost_ref, wg_ref, wu_ref, wd_ref,
                o_ref, a_ref, acc_ref, *, res_w, nf):
    f = pl.program_id(1)

    @pl.when(f == 0)
    def _():
        y = _rms(x_ref[...], gpre_ref[...])
        a_ref[...] = (y * (1.0 + sc_ref[...]) + sh_ref[...]).astype(BF16)
        acc_ref[...] = jnp.zeros_like(acc_ref)

    a = a_ref[...]
    h = _dot(a, wg_ref[...])
    u = _dot(a, wu_ref[...])
    act = (h * jax.nn.sigmoid(h) * u).astype(BF16)
    acc_ref[...] += _dot(act, wd_ref[...])

    @pl.when(f == nf - 1)
    def _():
        o_ref[...] = x_ref[...] + res_w * gt_ref[...] * _rms(acc_ref[...], gpost_ref[...])


FFN_TM = 512
FFN_TF = 512


def _ffn(x, shift, scale, gate, g_pre, g_post, wg, wu, wd, res_w, tm, tf):
    r, d = x.shape
    fdim = wg.shape[1]
    nf = fdim // tf
    return pl.pallas_call(
        functools.partial(_ffn_kernel, res_w=res_w, nf=nf),
        grid=(r // tm, nf),
        in_specs=[pl.BlockSpec((tm, d), lambda i, f: (i, 0)),
                  _row_spec(shift, tm), _row_spec(scale, tm), _row_spec(gate, tm),
                  pl.BlockSpec((1, d), lambda i, f: (0, 0)),
                  pl.BlockSpec((1, d), lambda i, f: (0, 0)),
                  pl.BlockSpec((d, tf), lambda i, f: (0, f)),
                  pl.BlockSpec((d, tf), lambda i, f: (0, f)),
                  pl.BlockSpec((tf, d), lambda i, f: (f, 0))],
        out_specs=pl.BlockSpec((tm, d), lambda i, f: (i, 0)),
        out_shape=jax.ShapeDtypeStruct((r, d), F32),
        scratch_shapes=[pltpu.VMEM((tm, d), BF16), pltpu.VMEM((tm, d), F32)],
        compiler_params=_cparams(("parallel", "arbitrary")),
        name="ffn",
    )(x, shift, scale, gate, g_pre, g_post, wg, wu, wd)


IN_SEGS = (CM_WIDTH, CM_WIDTH, NSA_WIDTH, 2 * KV_COLS, 2 * KV_COLS, 2 * KV_COLS, LANES)


def _in_proj_kernel(x_ref, sh_ref, sc_ref, gpre_ref, w_ref, *o_refs):
    y = _rms(x_ref[...], gpre_ref[...])
    a = (y * (1.0 + sc_ref[...]) + sh_ref[...]).astype(BF16)
    z = _dot(a, w_ref[...])
    off = 0
    for o_ref, width in zip(o_refs, IN_SEGS):
        o_ref[...] = z[:, off:off + width]
        off += width


def _in_proj(x, shift, scale, g_pre, w, tm):
    r, d = x.shape
    n = w.shape[1]
    return pl.pallas_call(
        _in_proj_kernel,
        grid=(r // tm,),
        in_specs=[pl.BlockSpec((tm, d), lambda i: (i, 0)),
                  _row_spec(shift, tm), _row_spec(scale, tm),
                  pl.BlockSpec((1, d), lambda i: (0, 0)),
                  pl.BlockSpec((d, n), lambda i: (0, 0), pipeline_mode=pl.Buffered(1))],
        out_specs=[pl.BlockSpec((tm, s), lambda i: (i, 0)) for s in IN_SEGS],
        out_shape=[jax.ShapeDtypeStruct((r, s), F32) for s in IN_SEGS],
        compiler_params=_cparams(("parallel",)),
        name="in_proj",
    )(x, shift, scale, g_pre, w)


IN_P_TM = 2 * QT
TILE_W = NSA_DK
IN_P_QW = NSA_HEADS * TILE_W
IN_P_KW = NSA_G * TILE_W
IN_P_ROWS_F = N_BRANCH * ROW_W
IN_P_GATE_ROWS = 64


def _in_proj_prompt_kernel(x_ref, sh_ref, sc_ref, gpre_ref, wr_ref, wf_ref, u_ref, v_ref, kvc_ref,
                           qt_ref, gt_ref, ks_ref, vst_ref, kw_ref, vwt_ref, kvt_ref):
    y = _rms(x_ref[...], gpre_ref[...])
    a = (y * (1.0 + sc_ref[...]) + sh_ref[...]).astype(BF16)
    z = _dot(a, wr_ref[...])
    zt = _dot_nt(wf_ref[...], a)
    nt = IN_P_TM // QT
    tile = lambda t: slice(t * QT, (t + 1) * QT)
    u_ref[...] = z[:, 0:CM_WIDTH]
    v_ref[...] = z[:, CM_WIDTH:2 * CM_WIDTH]
    off = 2 * CM_WIDTH
    for t in range(nt):
        for gh in range(NSA_HEADS):
            g, h = divmod(gh, NSA_HPG)
            qt_ref[g, t, h * QT:(h + 1) * QT, :] = z[tile(t), off + gh * TILE_W:off + (gh + 1) * TILE_W].astype(BF16)
    off += IN_P_QW
    kvc_ref[...] = z[:, off:off + ROW_W]
    off += ROW_W
    for g in range(NSA_G):
        ks_ref[g, 0] = z[:, off + g * TILE_W:off + (g + 1) * TILE_W].astype(BF16)
    off += IN_P_KW
    for t in range(nt):
        for g in range(NSA_G):
            kw_ref[g, t] = z[tile(t), off + g * TILE_W:off + (g + 1) * TILE_W].astype(BF16)
    kvt_ref[...] = zt[0:IN_P_ROWS_F]
    v_rows = lambda branch, g: slice(branch * ROW_W + KV_COLS + g * NSA_DK,
                                     branch * ROW_W + KV_COLS + (g + 1) * NSA_DK)
    for g in range(NSA_G):
        vst_ref[g, 0] = zt[v_rows(1, g), :].astype(BF16)
        for t in range(nt):
            vwt_ref[g, t] = zt[v_rows(2, g), tile(t)].astype(BF16)
            gt_ref[g, t, N_BRANCH:, :] = jnp.zeros((8 - N_BRANCH, HQ), F32)
            for br in range(N_BRANCH):
                for h in range(NSA_HPG):
                    row = IN_P_ROWS_F + (g * N_BRANCH + br) * NSA_HPG + h
                    gt_ref[g, t, br:br + 1, h * QT:(h + 1) * QT] = zt[row:row + 1, tile(t)]


def _in_proj_prompt_weights(w):
    d = w.shape[0]
    q0 = 2 * CM_WIDTH
    k0 = q0 + NSA_WIDTH
    pad_lanes = lambda x: jnp.pad(x, ((0, 0), (0, 0), (0, TILE_W - NSA_DK))).reshape(d, -1)
    wq = pad_lanes(w[:, q0:k0].reshape(d, NSA_HEADS, NSA_DK) * Q_SCALE)
    k_of = lambda branch: pad_lanes(
        w[:, k0 + branch * ROW_W:k0 + branch * ROW_W + KV_COLS].reshape(d, NSA_G, NSA_DK))
    w_rows = jnp.concatenate([w[:, :q0], wq, w[:, k0:k0 + ROW_W], k_of(1), k_of(2)], axis=1).astype(BF16)
    g0 = k0 + N_BRANCH * ROW_W
    wg = jnp.transpose(w[:, g0:g0 + N_BRANCH * NSA_HEADS].reshape(d, NSA_G, NSA_HPG, N_BRANCH),
                       (0, 1, 3, 2)).reshape(d, N_BRANCH * NSA_HEADS)
    wg = jnp.pad(wg, ((0, 0), (0, IN_P_GATE_ROWS - N_BRANCH * NSA_HEADS)))
    w_feat = jnp.concatenate([w[:, k0:g0], wg], axis=1).T.astype(BF16)
    return w_rows, w_feat


def _in_proj_prompt(x, shift, scale, g_pre, w_rows, w_feat):
    t, d = x.shape
    tm = IN_P_TM
    nq = t // QT
    per = KT_SEL // tm
    const = lambda a: pl.BlockSpec(a.shape, lambda i: (0, 0), pipeline_mode=pl.Buffered(1))
    rows = lambda n: pl.BlockSpec((tm, n), lambda i: (i, 0))
    qtile = lambda a: pl.BlockSpec((NSA_G, tm // QT) + a[2:], lambda i: (0, i, 0, 0))
    shapes = [((t, CM_WIDTH), F32), ((t, CM_WIDTH), F32), ((t, ROW_W), F32),
              ((NSA_G, nq, HQ, TILE_W), BF16), ((NSA_G, nq, 8, HQ), F32),
              ((NSA_G, t // KT_SEL, KT_SEL, TILE_W), BF16), ((NSA_G, t // KT_SEL, NSA_DK, KT_SEL), BF16),
              ((NSA_G, nq, QT, TILE_W), BF16), ((NSA_G, nq, NSA_DK, QT), BF16),
              ((IN_P_ROWS_F, t), F32)]
    out_specs = [rows(CM_WIDTH), rows(CM_WIDTH), rows(ROW_W),
                 qtile(shapes[3][0]), qtile(shapes[4][0]),
                 pl.BlockSpec((NSA_G, 1, tm, TILE_W), lambda i: (0, i // per, i % per, 0)),
                 pl.BlockSpec((NSA_G, 1, NSA_DK, tm), lambda i: (0, i // per, 0, i % per)),
                 qtile(shapes[7][0]), qtile(shapes[8][0]),
                 pl.BlockSpec((IN_P_ROWS_F, tm), lambda i: (0, i))]
    return pl.pallas_call(
        _in_proj_prompt_kernel,
        grid=(t // tm,),
        in_specs=[rows(d), _row_spec(shift, tm), _row_spec(scale, tm),
                  pl.BlockSpec((1, d), lambda i: (0, 0)), const(w_rows), const(w_feat)],
        out_specs=out_specs,
        out_shape=[jax.ShapeDtypeStruct(s, dt) for s, dt in shapes],
        compiler_params=_cparams(("parallel",)),
        name="in_proj_prompt",
    )(x, shift, scale, g_pre, w_rows, w_feat)


def _chunk_mix_kernel(u_ref, v_ref, nrm_ref, ws_ref, bst_ref, y_ref, vn_ref, *, chunk):
    ri = lax.broadcasted_iota(jnp.int32, (chunk, chunk), 0)
    ci = lax.broadcasted_iota(jnp.int32, (chunk, chunk), 1)
    lower = ci <= ri
    for h in range(CM_HEADS):
        sl = slice(h * CM_HEAD_DIM, (h + 1) * CM_HEAD_DIM)
        vf = jax.nn.gelu(v_ref[:, sl])
        mu = jnp.mean(vf, axis=-1, keepdims=True)
        var = jnp.mean(jnp.square(vf - mu), axis=-1, keepdims=True)
        vn = (vf - mu) * lax.rsqrt(var + EPS) * nrm_ref[:, sl]
        vn_ref[:, sl] = vn
        w = jnp.where(lower, ws_ref[h], 0.0).astype(BF16)
        s = _dot(w, vn.astype(BF16)) + bst_ref[:, h:h + 1]
        y_ref[:, sl] = jax.nn.gelu(u_ref[:, sl]) * s


def _chunk_mix(u, v, cm_norm, cm_ws, cm_bs, chunk):
    r = u.shape[0]
    ws = cm_ws[:, :chunk, :chunk]
    bst = cm_bs[:, :chunk].T
    return pl.pallas_call(
        functools.partial(_chunk_mix_kernel, chunk=chunk),
        grid=(r // chunk,),
        in_specs=[pl.BlockSpec((chunk, CM_WIDTH), lambda i: (i, 0)),
                  pl.BlockSpec((chunk, CM_WIDTH), lambda i: (i, 0)),
                  pl.BlockSpec((1, CM_WIDTH), lambda i: (0, 0)),
                  pl.BlockSpec((CM_HEADS, chunk, chunk), lambda i: (0, 0, 0)),
                  pl.BlockSpec((chunk, CM_HEADS), lambda i: (0, 0))],
        out_specs=[pl.BlockSpec((chunk, CM_WIDTH), lambda i: (i, 0)),
                   pl.BlockSpec((chunk, CM_WIDTH), lambda i: (i, 0))],
        out_shape=[jax.ShapeDtypeStruct((r, CM_WIDTH), F32),
                   jax.ShapeDtypeStruct((r, CM_WIDTH), F32)],
        compiler_params=_cparams(("parallel",)),
        name="chunk_mix",
    )(u, v, cm_norm.reshape(1, CM_WIDTH), ws, bst)


def _gather_cv(x, c):
    return jnp.concatenate(
        [x[:, l * ROW_W + c * KV_COLS:l * ROW_W + (c + 1) * KV_COLS] for l in range(CMP_STRIDE)],
        axis=1)


def _pos_bias(pos_ref, pw_ref, c):
    posb = _dot_f32(pos_ref[c], pw_ref[c])[0:1]
    return jnp.concatenate([posb] * NSA_G, axis=1)


def _compress_weights(phi_pos, phi_w1, phi_w2):
    eye = jnp.eye(NSA_G, dtype=F32)
    w1 = phi_w1.reshape(CMP_HALF, CMP_STRIDE, 2, NSA_DK, NSA_DK)
    w1big = jnp.einsum('mlcde,gh->clgdmhe', w1, eye).reshape(
        2, CMP_STRIDE * KV_COLS, CMP_HALF * KV_COLS).astype(BF16)
    w2big = jnp.einsum('cde,gh->cgdhe', phi_w2, eye).reshape(2, KV_COLS, KV_COLS).astype(BF16)
    pos = jnp.transpose(phi_pos, (1, 0, 2)).reshape(2, 1, CMP_LEN * NSA_DK)
    pos = jnp.broadcast_to(pos, (2, 8, CMP_LEN * NSA_DK))
    pw = jnp.transpose(phi_w1, (1, 0, 2, 3)).reshape(2, CMP_LEN * NSA_DK, NSA_DK)
    return w1big, pos, pw, w2big


def _compress_kernel(x_ref, xn_ref, w1_ref, pos_ref, pw_ref, w2_ref, o_ref, *, rc):
    x = x_ref[0]
    xn = xn_ref[0]
    last_row = lax.broadcasted_iota(jnp.int32, (rc, KV_COLS), 0) == rc - 1
    for c in range(2):
        w1 = w1_ref[c]
        y = _dot(_gather_cv(x, c).astype(BF16), w1)
        yn = _dot(_gather_cv(xn, c).astype(BF16), w1)
        second = pltpu.roll(y[:, KV_COLS:], rc - 1, 0)
        second = jnp.where(last_row, yn[0:1, KV_COLS:], second)
        pre = y[:, :KV_COLS] + second + _pos_bias(pos_ref, pw_ref, c)
        o_ref[0, :, c * KV_COLS:(c + 1) * KV_COLS] = _dot(jax.nn.gelu(pre).astype(BF16), w2_ref[c])


def _pick_rows(n, cap):
    best = 8
    for rc in range(8, cap + 1, 8):
        if n % rc == 0:
            best = rc
    return best


def _compress(x, cw):
    bsz, n, _ = x.shape
    rc = _pick_rows(n, 256)
    w1big, pos, pw, w2big = cw
    nb8 = n // 8
    return pl.pallas_call(
        functools.partial(_compress_kernel, rc=rc),
        grid=(bsz, n // rc),
        in_specs=[pl.BlockSpec((1, rc, x.shape[2]), lambda b, i: (b, i, 0)),
                  pl.BlockSpec((1, 8, x.shape[2]),
                               lambda b, i: (b, jnp.minimum((i + 1) * (rc // 8), nb8 - 1), 0)),
                  pl.BlockSpec(w1big.shape, lambda b, i: (0, 0, 0)),
                  pl.BlockSpec(pos.shape, lambda b, i: (0, 0, 0)),
                  pl.BlockSpec(pw.shape, lambda b, i: (0, 0, 0)),
                  pl.BlockSpec(w2big.shape, lambda b, i: (0, 0, 0))],
        out_specs=pl.BlockSpec((1, rc, ROW_W), lambda b, i: (b, i, 0)),
        out_shape=jax.ShapeDtypeStruct((bsz, n, ROW_W), F32),
        compiler_params=_cparams(("parallel", "arbitrary")),
        name="compress",
    )(x, x, w1big, pos, pw, w2big)


def _compress_tokens(kv_c, cw):
    bsz, t = kv_c.shape[:2]
    n = t // CMP_STRIDE
    npad = _round_up(n, LANES if n >= LANES else 8)
    x = kv_c.reshape(bsz, n, CMP_STRIDE * ROW_W)
    x = jnp.pad(x, ((0, 0), (0, npad - n), (0, 0)))
    return _compress(x, cw)[:, :n - (CMP_HALF - 1)]


def _rel_bucket(d):
    n = jnp.maximum(d, 0)
    large = jnp.full(d.shape, REL_BUCKETS // 2, jnp.int32)
    for thr in REL_THRESHOLDS:
        large = large + (n >= thr).astype(jnp.int32)
    return jnp.where(n < REL_BUCKETS // 2, n, large)


def _rel_bias_lanes(d, tab_ref):
    bkt = _rel_bucket(d)
    f = jnp.zeros(d.shape, F32)
    for k in range(REL_BUCKETS):
        f = jnp.where(bkt == k, tab_ref[0, k:k + 1, :], f)
    return f


def _rel_bias_rows(d, tab_ref):
    bkt = _rel_bucket(d)
    f = jnp.zeros(d.shape, F32)
    for k in range(REL_BUCKETS):
        f = jnp.where(bkt == k, tab_ref[:, k:k + 1], f)
    return f


def _near_tile_kernel(tab_ref, o_ref):
    t = pl.program_id(1)
    delta = jnp.where(t < N_NEAR, t * QT, jnp.where(t == TN_WIN, WINDOW, 0))
    dmax = jnp.where(t == TN_WIN, WINDOW, 1 << 30)
    j = lax.broadcasted_iota(jnp.int32, (QT, HQ), 0)
    i = lax.broadcasted_iota(jnp.int32, (QT, HQ), 1) & (QT - 1)
    d = delta + i - j
    far = tab_ref[0, REL_BUCKETS - 1:REL_BUCKETS, :]
    val = jnp.where((d >= 0) & (d <= dmax), _rel_bias_lanes(d, tab_ref) - far, NEG)
    val = jnp.where(t == TN_MASKED, NEG, jnp.where(t == TN_ZERO, 0.0, val))
    o_ref[0, 0] = val


def _cmp_window_kernel(tab_ref, o_ref):
    e = lax.broadcasted_iota(jnp.int32, (WT_ROWS, HQ), 0) - WT_LEAD
    i = lax.broadcasted_iota(jnp.int32, (WT_ROWS, HQ), 1) & (QT - 1)
    d = i - (CMP_LEN - 1) - CMP_STRIDE * e
    far = tab_ref[0, REL_BUCKETS - 1:REL_BUCKETS, :]
    o_ref[0] = jnp.where(d >= 0, _rel_bias_lanes(d, tab_ref) - far, 0.0)


def _prompt_bias_tiles(rel_bias):
    tab = jnp.repeat((rel_bias * LOG2E).reshape(REL_BUCKETS, NSA_G, NSA_HPG), QT, axis=2)
    tab = jnp.transpose(tab, (1, 0, 2))
    near = pl.pallas_call(
        _near_tile_kernel,
        grid=(NSA_G, TN_COUNT),
        in_specs=[pl.BlockSpec((1, REL_BUCKETS, HQ), lambda g, t: (g, 0, 0))],
        out_specs=pl.BlockSpec((1, 1, QT, HQ), lambda g, t: (g, t, 0, 0)),
        out_shape=jax.ShapeDtypeStruct((NSA_G, TN_COUNT, QT, HQ), F32),
        compiler_params=_cparams(("parallel", "arbitrary")),
        name="bias_near_tiles",
    )(tab)
    cwin = pl.pallas_call(
        _cmp_window_kernel,
        grid=(NSA_G,),
        in_specs=[pl.BlockSpec((1, REL_BUCKETS, HQ), lambda g: (g, 0, 0))],
        out_specs=pl.BlockSpec((1, WT_ROWS, HQ), lambda g: (g, 0, 0)),
        out_shape=jax.ShapeDtypeStruct((NSA_G, WT_ROWS, HQ), F32),
        compiler_params=_cparams(("parallel",)),
        name="bias_cmp_window",
    )(tab)
    return near, cwin


GROUPS_PER_STEP = 2
N_WIN_TILES = WINDOW // QT + 1
NSA_SEGMENTS = 4


def _nsa_group(q_ref, kc_ref, vct_ref, ks_ref, vst_ref, kw_refs, vwt_refs, tn_ref, wt_ref,
               g_ref, o_ref, s_ref, imp_ref, mb_ref, m_ref, acc_ref,
               buf0_ref, buf1_ref, tmax_ref, part_ref, *, ncp, nblk, n_sel, b_off):
    b = pl.program_id(1) + b_off
    qs = b * QT
    q = q_ref[0, 0]

    sc = _dot_nt(kc_ref[0], q)
    ci = lax.broadcasted_iota(jnp.int32, (ncp, HQ), 0)
    li = lax.broadcasted_iota(jnp.int32, (ncp, HQ), 1) & (QT - 1)
    vis = (CMP_STRIDE * ci + (CMP_LEN - 1) - li) <= qs
    s_ref[0:WT_LEAD, :] = jnp.full((WT_LEAD, HQ), NEG, F32)
    s_ref[WT_LEAD + ncp:, :] = jnp.full((s_ref.shape[0] - WT_LEAD - ncp, HQ), NEG, F32)
    s_ref[WT_LEAD:WT_LEAD + ncp, :] = jnp.where(vis, sc, NEG)
    r0 = pl.multiple_of(8 * b, 8)
    s_ref[pl.ds(r0, WT_ROWS), :] += wt_ref[0]
    yield None
    s = s_ref[WT_LEAD:WT_LEAD + ncp, :]
    m = jnp.max(s, axis=0, keepdims=True)
    m = jnp.where(m <= 0.1 * NEG, 0.0, m)
    p = jnp.exp2(s - m)
    rinv = 1.0 / jnp.maximum(jnp.sum(p, axis=0, keepdims=True), 1e-30)
    o_c = _dot(vct_ref[0], p.astype(BF16)) * rinv
    yield None
    pn = p * rinv
    imp = pn[:, 0:QT]
    for h in range(1, NSA_HPG):
        imp = imp + pn[:, h * QT:(h + 1) * QT]
    imp_ref[0:8, :] = jnp.zeros((8, QT), F32)
    imp_ref[8:8 + ncp, :] = imp
    if imp_ref.shape[0] > 8 + ncp:
        imp_ref[8 + ncp:, :] = jnp.zeros((imp_ref.shape[0] - 8 - ncp, QT), F32)
    ratio = SEL_BLOCK // CMP_STRIDE
    span = ratio + CMP_HALF - 1
    psl = imp_ref[pl.ds(8 - (CMP_HALF - 1), nblk, stride=ratio), :]
    for mm in range(1, span):
        psl = psl + imp_ref[pl.ds(8 - (CMP_HALF - 1) + mm, nblk, stride=ratio), :]

    yield None
    n_win = WINDOW // QT
    st_parts, vt_parts = [], []

    def window_tile(w):
        kt = b - n_win + w
        tn_idx = jnp.where(kt >= 0, TN_WIN if w == 0 else n_win - w, TN_MASKED)
        st_parts.append(_dot_nt(kw_refs[w][0, 0], q) + tn_ref[0, tn_idx])
        vt_parts.append(vwt_refs[w][0, 0])

    def window_finish():
        st = jnp.concatenate(st_parts, axis=0)
        p = jnp.exp2(st - jnp.max(st, axis=0, keepdims=True))
        o_w = _dot(jnp.concatenate(vt_parts, axis=1), p.astype(BF16)) * (
            1.0 / jnp.sum(p, axis=0, keepdims=True))
        g = jax.nn.sigmoid(g_ref[0, 0])
        part_ref[...] = g[0:1] * o_c + g[2:3] * o_w

    window_pieces = [functools.partial(window_tile, w) for w in range(n_win + 1)] + [window_finish]

    ji = lax.broadcasted_iota(jnp.int32, (nblk, QT), 0)
    ii = lax.broadcasted_iota(jnp.int32, (nblk, QT), 1)
    cur = (qs + ii) // SEL_BLOCK
    jf = ji.astype(F32)
    forced = (ji == 0) | (ji == cur) | (ji == cur - 1)
    score0 = jnp.where(forced, -jnp.inf, jnp.where(ji <= cur, psl, -BIG))

    score, selb = score0, jnp.where(forced, 0.0, NEG)
    for _ in range(max(n_sel - N_FORCED, 0)):
        yield None
        mx = jnp.max(score, axis=0, keepdims=True)
        first = jnp.min(jnp.where(score == mx, jf, 1e9), axis=0, keepdims=True)
        pick = jf == first
        score, selb = jnp.where(pick, -jnp.inf, score), jnp.where(pick, 0.0, selb)
        if window_pieces:
            window_pieces.pop(0)()
    while window_pieces:
        window_pieces.pop(0)()
    mb_ref[0:nblk, :] = jnp.concatenate([selb] * NSA_HPG, axis=1)
    mb_ref[nblk:, :] = jnp.full((BLK_PER_KT, HQ), NEG, F32)

    m_ref[...] = jnp.full(m_ref.shape, NEG, F32)
    acc_ref[...] = jnp.zeros(acc_ref.shape, F32)

    def sel_logits(kt, valid):
        ktc = jnp.where(valid, kt, 0)
        st = _dot_nt(ks_ref[0, ktc], q)
        row0 = jnp.where(valid, ktc * BLK_PER_KT, nblk)
        mb = mb_ref[pl.ds(pl.multiple_of(row0, BLK_PER_KT), BLK_PER_KT), :]
        st = jnp.concatenate(
            [st[k * SEL_BLOCK:(k + 1) * SEL_BLOCK] + mb[k:k + 1] for k in range(BLK_PER_KT)], axis=0)
        return st, ktc

    def softmax_step(st, tmax, vt):
        nk = vt.shape[1]
        ones = jnp.where(lax.broadcasted_iota(jnp.int32, (V_ROWS - NSA_DK, nk), 0) == 0, 1.0, 0.0)
        vt = jnp.concatenate([vt, ones.astype(BF16)], axis=0)
        m_old = m_ref[...]
        m_new = jnp.maximum(m_old, tmax)
        alpha = jnp.exp2(m_old - m_new)
        p = jnp.exp2(st - m_new)
        acc_ref[...] = alpha * acc_ref[...] + _dot(vt, p.astype(BF16))
        m_ref[...] = m_new

    nt = b // SUB_PER_KT + 1
    n_far = jnp.maximum(nt - NEAR_TILES, 0)

    def far_logits(kt, buf_ref, slot):
        st, _ = sel_logits(kt, kt < n_far)
        buf_ref[...] = st
        tmax_ref[slot:slot + 1, :] = jnp.max(st, axis=0, keepdims=True)

    def far_first():
        far_logits(0, buf0_ref, 0)

    def far_pair(pi):
        k0 = 2 * pi
        far_logits(k0 + 1, buf1_ref, 1)
        softmax_step(buf0_ref[...], tmax_ref[0:1, :], vst_ref[0, k0])
        far_logits(k0 + 2, buf0_ref, 0)
        k1 = jnp.minimum(k0 + 1, jnp.maximum(n_far - 1, 0))
        softmax_step(buf1_ref[...], tmax_ref[1:2, :], vst_ref[0, k1])

    yield far_first, far_pair

    st_parts, vt_parts = [], []
    for w in range(NEAR_TILES):
        kt = nt - NEAR_TILES + w
        st, ktc = sel_logits(kt, kt >= 0)
        r = b - SUB_PER_KT * ktc
        parts = []
        for u in range(SUB_PER_KT):
            ru = r - u
            idx = jnp.where(ru < 0, TN_MASKED, jnp.where(ru >= N_NEAR, TN_ZERO, ru))
            parts.append(tn_ref[0, idx])
        st_parts.append(st + jnp.concatenate(parts, axis=0))
        vt_parts.append(vst_ref[0, ktc])
    yield None
    st = jnp.concatenate(st_parts, axis=0)
    softmax_step(st, jnp.max(st, axis=0, keepdims=True), jnp.concatenate(vt_parts, axis=1))
    acc = acc_ref[...]
    o_s = acc[0:NSA_DK] * (1.0 / acc[NSA_DK:NSA_DK + 1])
    total = part_ref[...] + jax.nn.sigmoid(g_ref[0, 0, 1:2]) * o_s
    pairs = [jnp.concatenate([total[:, h * QT:(h + 1) * QT], total[:, (h + 1) * QT:(h + 2) * QT]], axis=0).T
             for h in range(0, NSA_HPG, 2)]
    o_ref[...] = jnp.concatenate(pairs, axis=1)


GROUP_COLS = NSA_HPG * NSA_DK


def _nsa_kernel(*refs, n_scratch, **static):
    gp = GROUPS_PER_STEP
    refs = refs[1:]
    n_all = gp * n_scratch
    ins, o_ref = refs[:len(refs) - n_all - 1], refs[len(refs) - n_all - 1]
    scratch = refs[len(refs) - n_all:]

    b = pl.program_id(1) + static["b_off"]
    n_far = jnp.maximum(b // SUB_PER_KT + 1 - NEAR_TILES, 0)
    programs = []
    for gi in range(gp):
        v = [r.at[pl.ds(gi, 1)] for r in ins]
        kw_refs, vwt_refs = v[5:5 + N_WIN_TILES], v[5 + N_WIN_TILES:5 + 2 * N_WIN_TILES]
        rest = v[5 + 2 * N_WIN_TILES:] + [o_ref.at[:, pl.ds(gi * GROUP_COLS, GROUP_COLS)]]
        programs.append(_nsa_group(*v[:5], kw_refs, vwt_refs, *rest,
                                   *scratch[gi * n_scratch:(gi + 1) * n_scratch], **static))
    loops = [None] * gp
    while any(lp is None for lp in loops):
        for gi, p in enumerate(programs):
            if loops[gi] is None:
                loops[gi] = next(p)

    @pl.when(n_far > 0)
    def _():
        for far_first, _ in loops:
            far_first()

    def in_turns(generators):
        active = list(generators)
        while active:
            for p in list(active):
                if next(p, StopIteration) is StopIteration:
                    active.remove(p)

    def far_body(pi, carry):
        for _, far_pair in loops:
            far_pair(pi)
        return carry

    lax.fori_loop(0, (n_far + 1) // 2, far_body, 0)
    in_turns(programs)


def _nsa_attend(qt, kc, vct, ks, vst, kw, vwt, tn, wt, gt, *, n_sel_blk):
    s_n, nq = qt.shape[:2]
    gp = GROUPS_PER_STEP
    assert kc.shape[1] >= 8 * nq and ks.shape[1] * SUB_PER_KT >= nq and s_n % gp == 0
    n_sel = min(N_SEL, n_sel_blk)
    n_seg = NSA_SEGMENTS if nq % (NSA_SEGMENTS * SUB_PER_KT) == 0 else 1
    nq_seg = nq // n_seg
    y = jnp.zeros((nq * QT, s_n * GROUP_COLS), F32)
    for seg in range(n_seg):
        b_off = seg * nq_seg
        b_end = b_off + nq_seg
        ncp = min(_round_up(8 * b_end, LANES), kc.shape[1])
        nkt = b_end // SUB_PER_KT
        nblk = nkt * BLK_PER_KT
        imp_rows = max(8 + ncp, 8 + (SEL_BLOCK // CMP_STRIDE) * nblk)
        scratch = [(ncp + WT_ROWS, HQ), (imp_rows, QT), (nblk + BLK_PER_KT, HQ), (1, HQ), (V_ROWS, HQ),
                   (KT_SEL, HQ), (KT_SEL, HQ), (8, HQ), (NSA_DK, HQ)]
        kern = functools.partial(_nsa_kernel, n_scratch=len(scratch), ncp=ncp, nblk=nblk, n_sel=n_sel,
                                 b_off=b_off)
        lead = lambda a, n: pl.BlockSpec((gp, n) + a.shape[2:], lambda s, i: (s,) + (0,) * (a.ndim - 1),
                                         pipeline_mode=pl.Buffered(1))
        per_tile = lambda a: pl.BlockSpec((gp, 1) + a.shape[2:], lambda s, i: (s, i + b_off, 0, 0))
        win = lambda a, w: pl.BlockSpec(
            (gp, 1) + a.shape[2:],
            lambda s, i: (s, jnp.maximum(i + b_off - (N_WIN_TILES - 1) + w, 0), 0, 0))
        in_specs = ([pl.BlockSpec(memory_space=pl.ANY), per_tile(qt), lead(kc, ncp),
                     pl.BlockSpec((gp, NSA_DK, ncp), lambda s, i: (s, 0, 0), pipeline_mode=pl.Buffered(1)),
                     lead(ks, nkt), lead(vst, nkt)]
                    + [win(kw, w) for w in range(N_WIN_TILES)]
                    + [win(vwt, w) for w in range(N_WIN_TILES)]
                    + [lead(tn, tn.shape[1]), lead(wt, wt.shape[1]), per_tile(gt)])
        args = (y, qt, kc, vct, ks, vst, *([kw] * N_WIN_TILES), *([vwt] * N_WIN_TILES), tn, wt, gt)
        y = pl.pallas_call(
            kern,
            grid=(s_n // gp, nq_seg),
            in_specs=in_specs,
            out_specs=pl.BlockSpec((QT, gp * GROUP_COLS), lambda s, i: (i + b_off, s)),
            out_shape=jax.ShapeDtypeStruct((nq * QT, s_n * GROUP_COLS), F32),
            scratch_shapes=[pltpu.VMEM(s, F32) for s in scratch] * gp,
            input_output_aliases={0: 0},
            compiler_params=_cparams(("parallel", "arbitrary")),
            name="nsa_prompt",
        )(*args)
    return y


def _k_tiles(kv, tile):
    nt = kv.shape[0] // tile
    k = kv[..., :KV_COLS].reshape(nt, tile, NSA_G, NSA_DK)
    k = jnp.transpose(k, (2, 0, 1, 3))
    k = jnp.pad(k, ((0, 0), (0, 0), (0, 0), (0, TILE_W - NSA_DK))).astype(BF16)
    v = kv[..., KV_COLS:].reshape(nt, tile, NSA_G, NSA_DK)
    v = jnp.transpose(v, (2, 0, 3, 1)).astype(BF16)
    return k, v


def _prompt_attend(qt, gt, tok, ks, vst, kw, vwt, bias):
    nq = qt.shape[1]
    t = nq * QT
    near, cwin = bias
    ncp = max(_round_up(tok.shape[0] + 1, LANES), _round_up(8 * nq, LANES))
    kc, vct = _k_tiles(jnp.pad(tok, ((0, ncp - tok.shape[0]), (0, 0))), ncp)
    return _nsa_attend(qt, kc[:, 0], vct[:, 0], ks, vst, kw, vwt, near, cwin, gt,
                       n_sel_blk=t // SEL_BLOCK)


def _sample_bias_kernel(tab_ref, cb_ref, sb_ref, snb_ref, wb_ref, *, past, dseq, ns):
    rows = tab_ref.shape[0]
    far = tab_ref[:, REL_BUCKETS - 1:REL_BUCKETS]

    def tok_of(n):
        return lax.broadcasted_iota(jnp.int32, (rows, n), 0) % dseq

    def lane(n):
        return lax.broadcasted_iota(jnp.int32, (rows, n), 1)

    for idx in range(2):
        rho = LANES * (ns - 1 + idx) + lane(LANES)
        d = past + tok_of(LANES) - (CMP_STRIDE * (rho - 1) + CMP_LEN - 1)
        cb_ref[idx] = jnp.where(d >= 0, _rel_bias_rows(d, tab_ref) - far, 0.0)
    d = SP_KEYS + tok_of(SP_KEYS) - lane(SP_KEYS)
    sb_ref[...] = _rel_bias_rows(d, tab_ref) - far
    d = tok_of(LANES) - lane(LANES)
    snb_ref[...] = jnp.where(d >= 0, _rel_bias_rows(d, tab_ref) - far, NEG)
    d = WINDOW + tok_of(WINDOW + LANES) - lane(WINDOW + LANES)
    wb_ref[...] = jnp.where((d >= 0) & (d <= WINDOW), _rel_bias_rows(d, tab_ref), NEG)


def _sample_bias(rel_bias, past, dseq, ns):
    rows = NSA_HEADS * dseq
    tab = (rel_bias * LOG2E).reshape(REL_BUCKETS, NSA_G, NSA_HPG)
    tab = jnp.transpose(tab, (2, 1, 0))
    tab = jnp.repeat(tab.reshape(NSA_HEADS, 1, REL_BUCKETS), dseq, axis=1).reshape(rows, REL_BUCKETS)
    tab = jnp.pad(tab, ((0, 0), (0, LANES - REL_BUCKETS)))
    shapes = [(2, rows, LANES), (rows, SP_KEYS), (rows, LANES), (rows, WINDOW + LANES)]
    cb, sb, snb, wb = pl.pallas_call(
        functools.partial(_sample_bias_kernel, past=past, dseq=dseq, ns=ns),
        out_shape=[jax.ShapeDtypeStruct(s, F32) for s in shapes],
        compiler_params=pltpu.CompilerParams(vmem_limit_bytes=VMEM_LIMIT),
        name="bias_sample",
    )(tab)
    return cb, sb, snb, wb


def _sample_nsa_kernel(pt_ref, *refs, ns, past, dseq, n_sel, n_sel_blk):
    cmp_pages = refs[:SP_PP]
    slc_pages = refs[SP_PP:2 * SP_PP]
    (xn_ref, ksn_ref, win_ref, kwn_ref, q_ref, g_ref, w1_ref, pos_ref, pw_ref, w2_ref, e_ref,
     cb_ref, sb_ref, snb_ref, wb_ref, o_ref,
     s_ref, vc_ref, carry_ref, sel_ref, m_ref, l_ref, acc_ref, oc_ref, x_ref) = refs[2 * SP_PP:]
    j = pl.program_id(1)
    qbd = q_ref[0]
    rows = qbd.shape[0]
    ntile = ns + 1
    nblk_l = sel_ref.shape[0] * SP_BLKS

    def feat_major(refs_, c):
        return jnp.concatenate([r[0, c].reshape(KV_COLS, r.shape[-1]) for r in refs_], axis=1)

    def tokens_of(gather, prepare=None):
        out = []
        for c in range(2):
            if prepare is not None:
                prepare(c)
            xc = gather(c)
            n = xc.shape[0]
            row0 = lax.broadcasted_iota(jnp.int32, (n, KV_COLS), 0) == 0
            y = _dot(xc.astype(BF16), w1_ref[c])
            first = jnp.where(row0, carry_ref[c:c + 1, :], pltpu.roll(y[:, :KV_COLS], 1, 0))
            carry_ref[c:c + 1, :] = y[n - 1:n, :KV_COLS]
            pre = first + y[:, KV_COLS:] + _pos_bias(pos_ref, pw_ref, c)
            out.append(_dot(jax.nn.gelu(pre).astype(BF16), w2_ref[c]))
        return out

    def online_update(s, v):
        m_old = m_ref[...]
        m_new = jnp.maximum(m_old, jnp.max(s, axis=1, keepdims=True))
        alpha = jnp.exp2(m_old - m_new)
        p = jnp.exp2(s - m_new)
        l_ref[...] = alpha * l_ref[...] + jnp.sum(p, axis=1, keepdims=True)
        acc_ref[...] = alpha * acc_ref[...] + _dot_nt(p.astype(BF16), v)
        m_ref[...] = m_new

    @pl.when(j == 0)
    def _():
        carry_ref[...] = jnp.zeros(carry_ref.shape, F32)

    @pl.when(j < ns)
    def _():
        def to_rows(c):
            for p, r in enumerate(cmp_pages):
                for gp in range(2):
                    tile = r[0, c, 2 * gp:2 * gp + 2].reshape(LANES, PAGE_SIZE)
                    x_ref[2 * c + gp, p * PAGE_SIZE:(p + 1) * PAGE_SIZE, :] = tile.T

        def gather(c):
            return jnp.concatenate(
                [pltpu.einshape("(nl)k->n(lk)", x_ref[2 * c + gp], l=CMP_STRIDE) for gp in range(2)], axis=1)

        tok_k, tok_v = tokens_of(gather, to_rows)
        s_ref[j] = _dot_nt(qbd, tok_k.astype(BF16))
        vc_ref[j] = tok_v.astype(BF16)

    @pl.when(j == ns - 1)
    def _():
        half = KV_COLS // 2
        tok_k, tok_v = tokens_of(lambda c: jnp.concatenate(
            [xn_ref[0, :, l * ROW_W + c * KV_COLS + gp * half:l * ROW_W + c * KV_COLS + (gp + 1) * half]
             for gp in range(2) for l in range(CMP_STRIDE)], axis=1))
        zpad = jnp.zeros((LANES - tok_k.shape[0], KV_COLS), F32)
        s_ref[ns] = _dot_nt(qbd, jnp.concatenate([tok_k, zpad], axis=0).astype(BF16))
        vc_ref[ns] = jnp.concatenate([tok_v, zpad], axis=0).astype(BF16)

        tiles = []
        for t in range(ntile):
            st = s_ref[t]
            if t >= ns - 1:
                st = st + cb_ref[t - (ns - 1)]
            tiles.append(st)
        s = jnp.concatenate(tiles, axis=1)
        width = ntile * LANES
        rho = lax.broadcasted_iota(jnp.int32, (rows, width), 1)
        tq = lax.broadcasted_iota(jnp.int32, (rows, width), 0) % dseq
        n_cmp = (past + _round_up(dseq, SEL_BLOCK)) // CMP_STRIDE - (CMP_HALF - 1)
        vis = (rho >= 1) & (rho <= n_cmp) & (CMP_STRIDE * (rho - 1) + CMP_LEN - 1 <= past + tq)
        s = jnp.where(vis, s, NEG)
        m = jnp.max(s, axis=1, keepdims=True)
        m = jnp.where(m <= 0.1 * NEG, 0.0, m)
        p = jnp.exp2(s - m)
        rinv = 1.0 / jnp.maximum(jnp.sum(p, axis=1, keepdims=True), 1e-30)
        vc = jnp.concatenate([vc_ref[t] for t in range(ntile)], axis=0)
        oc_ref[...] = _dot(p.astype(BF16), vc) * rinv
        pn = p * rinv
        gt = NSA_G * dseq
        imp = pn[0:gt]
        for h in range(1, NSA_HPG):
            imp = imp + pn[h * gt:(h + 1) * gt]
        ratio = SEL_BLOCK // CMP_STRIDE
        span = ratio + CMP_HALF - 1
        ar = lax.broadcasted_iota(jnp.int32, (width, nblk_l), 0)
        ac = lax.broadcasted_iota(jnp.int32, (width, nblk_l), 1)
        band = jnp.where((ar >= ratio * ac) & (ar < ratio * ac + span), 1.0, 0.0).astype(BF16)
        i_hi = imp.astype(BF16)
        r1 = imp - i_hi.astype(F32)
        i_mid = r1.astype(BF16)
        i_lo = (r1 - i_mid.astype(F32)).astype(BF16)
        psl = _dot(i_hi, band) + _dot(i_mid, band) + _dot(i_lo, band)
        ji = lax.broadcasted_iota(jnp.int32, (gt, nblk_l), 1)
        tq2 = lax.broadcasted_iota(jnp.int32, (gt, nblk_l), 0) % dseq
        cur = (past + tq2) // SEL_BLOCK
        jf = ji.astype(F32)
        forced = (ji == 0) | (ji == cur) | (ji == cur - 1)
        score0 = jnp.where(forced, -jnp.inf, jnp.where(ji <= cur, psl, -BIG))
        score0 = jnp.where(ji < n_sel_blk, score0, -jnp.inf)

        def pick_one(_, carry):
            score, sel = carry
            mx = jnp.max(score, axis=1, keepdims=True)
            first = jnp.min(jnp.where(score == mx, jf, 1e9), axis=1, keepdims=True)
            pick = jf == first
            return jnp.where(pick, -jnp.inf, score), jnp.where(pick, 1.0, sel)

        _, sel = lax.fori_loop(0, max(n_sel - N_FORCED, 0), pick_one,
                               (score0, jnp.where(forced, 1.0, 0.0)))
        for t in range(sel_ref.shape[0]):
            piece = sel[:, t * SP_BLKS:(t + 1) * SP_BLKS]
            piece = jnp.concatenate([piece, jnp.zeros((gt, LANES - SP_BLKS), F32)], axis=1)
            sel_ref[t] = jnp.concatenate([piece] * NSA_HPG, axis=0).astype(BF16)
        m_ref[...] = jnp.full(m_ref.shape, NEG, F32)
        l_ref[...] = jnp.zeros(l_ref.shape, F32)
        acc_ref[...] = jnp.zeros(acc_ref.shape, F32)

    @pl.when(j >= ns)
    def _():
        j2 = j - ns
        s = _dot(qbd, feat_major(slc_pages, 0).astype(BF16))
        chosen = _dot(sel_ref[j2], e_ref[...])
        s = s + (chosen - 1.0) * (-NEG)
        s = s + jnp.where(j2 == ns - 1, sb_ref[...], 0.0)
        online_update(s, feat_major(slc_pages, 1).astype(BF16))

    @pl.when(j == 2 * ns - 1)
    def _():
        online_update(_dot(qbd, ksn_ref[0, 0].astype(BF16)) + snb_ref[...],
                      ksn_ref[0, 1].astype(BF16))
        o_s = acc_ref[...] * (1.0 / l_ref[...])
        kw = jnp.concatenate([feat_major([win_ref], 0), kwn_ref[0, 0]], axis=1)
        vw = jnp.concatenate([feat_major([win_ref], 1), kwn_ref[0, 1]], axis=1)
        s = _dot(qbd, kw.astype(BF16)) + wb_ref[...]
        p = jnp.exp2(s - jnp.max(s, axis=1, keepdims=True))
        o_w = _dot_nt(p.astype(BF16), vw.astype(BF16)) * (1.0 / jnp.sum(p, axis=1, keepdims=True))
        g = jax.nn.sigmoid(g_ref[0])
        o_ref[0] = g[:, 0:1] * oc_ref[...] + g[:, 1:2] * o_s + g[:, 2:3] * o_w


def _sample_attend(q, gates, kv_c, kv_s, kv_w, cache_cmp, cache_slc, win_buf, page_table, cw,
                   rel_bias):
    bsz, dseq = q.shape[:2]
    n_pages = page_table.shape[1]
    past = n_pages * PAGE_SIZE
    assert n_pages % SP_PP == 0 and dseq <= SEL_BLOCK and win_buf.shape[1] == WINDOW
    feat_major = lambda a: jnp.transpose(a, (0, 2, 3, 4, 1))
    cache_cmp, cache_slc, win_buf = feat_major(cache_cmp), feat_major(cache_slc), feat_major(win_buf)
    ns = n_pages // SP_PP
    rows = NSA_HEADS * dseq
    n_sel_blk = past // SEL_BLOCK + 1
    n_sel = min(N_SEL, n_sel_blk)
    nsel_tiles = _round_up(-(-n_sel_blk // SP_BLKS), LANES // SP_BLKS)
    w1big, pos, pw, w2big = cw
    w1big = jnp.transpose(w1big.reshape(2, CMP_STRIDE, 2, LANES, -1), (0, 2, 1, 3, 4)).reshape(w1big.shape)
    cb, sb, snb, wb = _sample_bias(rel_bias, past, dseq, ns)

    qr = jnp.transpose((q * Q_SCALE).reshape(bsz, dseq, NSA_G, NSA_HPG, NSA_DK), (0, 3, 2, 1, 4))
    qbd = jnp.einsum('bhgtd,gk->bhgtkd', qr, jnp.eye(NSA_G, dtype=F32))
    qbd = qbd.reshape(bsz, rows, KV_COLS).astype(BF16)
    gr = gates[..., :N_BRANCH * NSA_HEADS].reshape(bsz, dseq, NSA_G, NSA_HPG, N_BRANCH)
    gr = jnp.transpose(gr, (0, 3, 2, 1, 4)).reshape(bsz, rows, N_BRANCH)
    gr = jnp.pad(gr, ((0, 0), (0, 0), (0, LANES - N_BRANCH)))
    pad_rows = lambda a, n: jnp.pad(a, ((0, 0), (0, n - a.shape[1]), (0, 0)))
    xn = pad_rows(kv_c, SEL_BLOCK).reshape(bsz, SEL_BLOCK // CMP_STRIDE, CMP_STRIDE * ROW_W)
    xn = pad_rows(xn, 8)
    new_feat_major = lambda a: jnp.transpose(
        pad_rows(a, LANES).reshape(bsz, LANES, 2, KV_COLS), (0, 2, 3, 1))
    ksn = new_feat_major(kv_s)
    kwn = new_feat_major(kv_w)
    expand = (jnp.arange(LANES)[:, None] == (jnp.arange(SP_KEYS) // SEL_BLOCK)[None, :]).astype(BF16)
    page_block = (1, 2, NSA_G, NSA_DK, PAGE_SIZE)

    def page_map(k, phase):
        if phase == 0:
            return lambda b, j, pt: (pt[b, jnp.minimum(j, ns - 1) * SP_PP + k], 0, 0, 0, 0)
        return lambda b, j, pt: (pt[b, jnp.maximum(j - ns, 0) * SP_PP + k], 0, 0, 0, 0)

    per_b = lambda a: pl.BlockSpec((1,) + a.shape[1:], lambda b, j, pt: (b,) + (0,) * (a.ndim - 1))
    const = lambda a: pl.BlockSpec(a.shape, lambda b, j, pt: (0,) * a.ndim)
    in_specs = ([pl.BlockSpec(page_block, page_map(k, 0)) for k in range(SP_PP)]
                + [pl.BlockSpec(page_block, page_map(k, 1)) for k in range(SP_PP)]
                + [per_b(xn), per_b(ksn), per_b(win_buf), per_b(kwn), per_b(qbd), per_b(gr),
                   pl.BlockSpec(w1big.shape, lambda b, j, pt: (0, 0, 0), pipeline_mode=pl.Buffered(1)),
                   const(pos), const(pw), const(w2big), const(expand),
                   const(cb), const(sb), const(snb), const(wb)])
    grid_spec = pltpu.PrefetchScalarGridSpec(
        num_scalar_prefetch=1, grid=(bsz, 2 * ns), in_specs=in_specs,
        out_specs=pl.BlockSpec((1, rows, KV_COLS), lambda b, j, pt: (b, 0, 0)),
        scratch_shapes=[pltpu.VMEM((ns + 1, rows, LANES), F32),
                        pltpu.VMEM((ns + 1, LANES, KV_COLS), BF16),
                        pltpu.VMEM((8, KV_COLS), F32),
                        pltpu.VMEM((nsel_tiles, rows, LANES), BF16),
                        pltpu.VMEM((rows, 1), F32),
                        pltpu.VMEM((rows, 1), F32),
                        pltpu.VMEM((rows, KV_COLS), F32),
                        pltpu.VMEM((rows, KV_COLS), F32),
                        pltpu.VMEM((ROW_W // LANES, SP_KEYS, LANES), F32)])
    out = pl.pallas_call(
        functools.partial(_sample_nsa_kernel, ns=ns, past=past, dseq=dseq, n_sel=n_sel,
                          n_sel_blk=n_sel_blk),
        grid_spec=grid_spec,
        out_shape=jax.ShapeDtypeStruct((bsz, rows, KV_COLS), F32),
        compiler_params=_cparams(("parallel", "arbitrary")),
        name="nsa_sample",
    )(page_table, *([cache_cmp] * SP_PP), *([cache_slc] * SP_PP), xn, ksn, win_buf, kwn, qbd, gr,
      w1big, pos, pw, w2big, expand, cb, sb, snb, wb)
    o = out.reshape(bsz, NSA_HPG, NSA_G, dseq, NSA_G, NSA_DK)
    o = jnp.einsum('bhgtge->btghe', o)
    return o.reshape(bsz * dseq, NSA_WIDTH)


def _out_proj_kernel(ycm_ref, ynsa_ref, h_ref, gt_ref, gpost_ref, w1_ref, w2_ref, o_ref):
    out = _dot(ycm_ref[...].astype(BF16), w1_ref[...]) + _dot(ynsa_ref[...].astype(BF16), w2_ref[...])
    o_ref[...] = h_ref[...] + gt_ref[...] * _rms(out, gpost_ref[...])


def _out_proj(ycm, ynsa, h, gate, g_post, w1, w2, tm):
    r, d = h.shape
    return pl.pallas_call(
        _out_proj_kernel,
        grid=(r // tm,),
        in_specs=[pl.BlockSpec((tm, CM_WIDTH), lambda i: (i, 0)),
                  pl.BlockSpec((tm, NSA_WIDTH), lambda i: (i, 0)),
                  pl.BlockSpec((tm, d), lambda i: (i, 0)),
                  _row_spec(gate, tm),
                  pl.BlockSpec((1, d), lambda i: (0, 0)),
                  pl.BlockSpec(w1.shape, lambda i: (0, 0)),
                  pl.BlockSpec(w2.shape, lambda i: (0, 0))],
        out_specs=pl.BlockSpec((tm, d), lambda i: (i, 0)),
        out_shape=jax.ShapeDtypeStruct((r, d), F32),
        compiler_params=_cparams(("parallel",)),
        name="out_proj",
    )(ycm, ynsa, h, gate, g_post, w1, w2)


def _row_tile(r, cap):
    tm = min(r, cap)
    while r % tm:
        tm //= 2
    return tm


def _layer(x, mod, per_row, weights, in_proj, mixer):
    r, d = x.shape
    (norm_pre, norm_post, ffn, w_out1, w_out2) = weights
    if per_row is None:
        mrow = lambda i, j: mod[0, i, j].reshape(1, d)
    else:
        mrow = lambda i, j: jnp.repeat(mod[:, i, j], per_row, axis=0)
    tm = _row_tile(r, 512)
    tm_ffn = _row_tile(r, FFN_TM)
    tf = _row_tile(ffn[0][0].shape[1], FFN_TF)
    h = _ffn(x, mrow(0, 0), mrow(0, 1), mrow(0, 2), norm_pre[0:1], norm_post[0:1], *ffn[0],
             res_w=0.5, tm=tm_ffn, tf=tf)
    mixed_cm, mixed_nsa, state = mixer(*in_proj(h, mrow(1, 0), mrow(1, 1), norm_pre[1:2]))
    h = _out_proj(mixed_cm, mixed_nsa, h, mrow(1, 2), norm_post[1:2], w_out1, w_out2, tm)
    h = _ffn(h, mrow(2, 0), mrow(2, 1), mrow(2, 2), norm_pre[2:3], norm_post[2:3], *ffn[1],
             res_w=0.5, tm=tm_ffn, tf=tf)
    return h, state


def kernel(x_prompt, x_sample, cache_cmp_kv, cache_slc_kv, state_win_kv, page_table, c_prompt,
           c_sample, w_mod, b_mod, norm_pre, norm_post, ffn_w_gate, ffn_w_up, ffn_w_down, w_in,
           w_out, cm_norm, cm_ws, cm_bs, phi_pos, phi_w1, phi_w2, rel_bias):
    depth = w_mod.shape[0]
    assert depth == 1
    bp, seq, d = x_prompt.shape
    bs, dseq, _ = x_sample.shape
    assert bp == 1 and seq % KT_SEL == 0 and seq >= WINDOW
    l = 0

    c_all = jnp.concatenate([c_prompt, c_sample], axis=0)
    mpad = _round_up(c_all.shape[0], 8)
    mod = _mod_proj(jnp.pad(c_all, ((0, mpad - c_all.shape[0]), (0, 0))), w_mod[l], b_mod[l])
    mod = mod[:bp + bs].reshape(bp + bs, 3, 3, d)

    ffn = [(ffn_w_gate[l, i].astype(BF16), ffn_w_up[l, i].astype(BF16), ffn_w_down[l, i].astype(BF16))
           for i in range(2)]
    w_in_p = jnp.pad(w_in[l], ((0, 0), (0, sum(IN_SEGS) - w_in.shape[2]))).astype(BF16)
    w_rows, w_feat = _in_proj_prompt_weights(w_in[l])
    w_o = w_out[l].astype(BF16)
    weights = (norm_pre[l], norm_post[l], ffn, w_o[:CM_WIDTH], w_o[CM_WIDTH:])
    cw = _compress_weights(phi_pos[l], phi_w1[l], phi_w2[l])
    prompt_bias = _prompt_bias_tiles(rel_bias)

    def prompt_in_proj(h, shift, scale, g_pre):
        return _in_proj_prompt(h, shift, scale, g_pre, w_rows, w_feat)

    def prompt_mixer(u, v, kv_c, qt, gt, ks, vst, kw, vwt, kvt):
        y_cm, _ = _chunk_mix(u, v, cm_norm[l], cm_ws[l], cm_bs[l], CHUNK)
        tok = _compress_tokens(kv_c[None], cw)[0]
        y_nsa = _prompt_attend(qt, gt, tok, ks, vst, kw, vwt, prompt_bias)
        return y_cm, y_nsa, kvt

    assert seq % KT_SEL == 0
    hp, kvt = _layer(x_prompt.reshape(seq, d), mod[:bp], None, weights, prompt_in_proj, prompt_mixer)
    kvt = jnp.transpose(kvt.reshape(N_BRANCH, 2, NSA_G, NSA_DK, seq), (0, 4, 1, 2, 3))
    pc, ps, pw = kvt[0], kvt[1], kvt[2]

    rows = _round_up(dseq, 16)

    def sample_mixer(u, v, q, kv_c, kv_s, kv_w, gates):
        per_seq = lambda a: a.reshape(bs, dseq, -1)
        pad_rows = lambda a, n: jnp.pad(per_seq(a), ((0, 0), (0, n - dseq), (0, 0)))
        y_cm, vn = _chunk_mix(pad_rows(u, rows).reshape(bs * rows, CM_WIDTH),
                              pad_rows(v, rows).reshape(bs * rows, CM_WIDTH),
                              cm_norm[l], cm_ws[l], cm_bs[l], rows)
        y_cm = y_cm.reshape(bs, rows, CM_WIDTH)[:, :dseq].reshape(bs * dseq, CM_WIDTH)
        vn = vn.reshape(bs, rows, CM_WIDTH)[:, :dseq]
        y_nsa = _sample_attend(per_seq(q), per_seq(gates), per_seq(kv_c), per_seq(kv_s), per_seq(kv_w),
                               cache_cmp_kv[l], cache_slc_kv[l], state_win_kv[l], page_table, cw,
                               rel_bias)
        return y_cm, y_nsa, (kv_c, kv_s, kv_w, vn)

    def sample_in_proj(h, shift, scale, g_pre):
        return _in_proj(h, shift, scale, g_pre, w_in_p, _row_tile(h.shape[0], 256))

    hs, (sc, ss, sw, sv) = _layer(x_sample.reshape(bs * dseq, d), mod[bp:], dseq, weights,
                                  sample_in_proj, sample_mixer)

    kvshape = lambda a, b, t: a.reshape(1, b, t, 2, NSA_G, NSA_DK)
    wp = min(WINDOW, seq)
    win_s = jnp.concatenate([state_win_kv[l].reshape(bs, -1, ROW_W), sw.reshape(bs, dseq, ROW_W)],
                            axis=1)[:, dseq:]
    return (hp.reshape(bp, seq, d), hs.reshape(bs, dseq, d),
            kvshape(pc, bp, seq), kvshape(sc, bs, dseq),
            kvshape(ps, bp, seq), kvshape(ss, bs, dseq),
            kvshape(pw[seq - wp:], bp, wp), kvshape(win_s, bs, win_s.shape[1]),
            sv.reshape(1, bs, dseq, CM_HEADS, CM_HEAD_DIM))
```

```python
import functools
import math

import jax
import jax.numpy as jnp
from jax import lax
from jax.experimental import pallas as pl
from jax.experimental.pallas import tpu as pltpu

F32 = jnp.float32
BF16 = jnp.bfloat16

CM_HEADS = 8
CM_HEAD_DIM = 128
CM_WIDTH = CM_HEADS * CM_HEAD_DIM
CHUNK = 128
NSA_HEADS = 16
NSA_G = 4
NSA_HPG = 4
NSA_DK = 64
NSA_WIDTH = NSA_HEADS * NSA_DK
KV_COLS = NSA_G * NSA_DK
ROW_W = 2 * KV_COLS
CMP_LEN = 32
CMP_STRIDE = 16
CMP_HALF = CMP_LEN // CMP_STRIDE
SEL_BLOCK = 64
N_SEL = 16
N_FORCED = 3
WINDOW = 512
N_BRANCH = 3
BIG = 1e4
REL_BUCKETS = 32
EPS = 1e-6
PAGE_SIZE = 128

LOG2E = 1.4426950408889634
Q_SCALE = NSA_DK ** -0.5 * LOG2E
NEG = -1e30
LANES = 128
QT = 128
HQ = NSA_HPG * QT
KT_SEL = 512
BLK_PER_KT = KT_SEL // SEL_BLOCK
SUB_PER_KT = KT_SEL // QT
N_NEAR = 8
NEAR_TILES = 3
V_ROWS = NSA_DK + 16
TN_WIN, TN_MASKED, TN_ZERO, TN_COUNT = 8, 9, 10, 11
WT_LEAD = 64
WT_ROWS = 200
REL_THRESHOLDS = (21, 27, 35, 46, 59, 77, 99, 128, 166, 216, 280, 363, 470, 609, 790)
VMEM_LIMIT = 56 * 1024 * 1024
SP_PP = CMP_STRIDE
SP_KEYS = SP_PP * PAGE_SIZE
SP_BLKS = SP_KEYS // SEL_BLOCK


def _cparams(sem):
    return pltpu.CompilerParams(dimension_semantics=sem, vmem_limit_bytes=VMEM_LIMIT)


def _dot(a, b):
    return jnp.dot(a, b, preferred_element_type=F32)


def _dot_nt(a, b):
    return lax.dot_general(a, b, (((1,), (1,)), ((), ())), preferred_element_type=F32)


def _split_bf16(x):
    hi = x.astype(BF16)
    lo = (x - hi.astype(F32)).astype(BF16)
    return hi, lo


def _dot_f32(a, b):
    a_hi, a_lo = _split_bf16(a)
    b_hi, b_lo = _split_bf16(b)
    return _dot(a_hi, b_hi) + _dot(a_lo, b_hi) + _dot(a_hi, b_lo)


def _rms(x, g):
    return x * lax.rsqrt(jnp.mean(x * x, axis=-1, keepdims=True) + EPS) * g


def _row_spec(arr, tm):
    d = arr.shape[-1]
    if arr.shape[0] == 1:
        return pl.BlockSpec((1, d), lambda *idx: (0, 0))
    return pl.BlockSpec((tm, d), lambda *idx: (idx[0], 0))


def _round_up(x, m):
    return -(-x // m) * m


def _mod_kernel(c_ref, w_ref, b_ref, o_ref):
    c = c_ref[...]
    s = c * jax.nn.sigmoid(c)
    o_ref[...] = _dot_f32(s, w_ref[...]) + b_ref[...]


def _mod_proj(c, w_mod, b_mod):
    m, d = c.shape
    n = w_mod.shape[1]
    tn = 512
    return pl.pallas_call(
        _mod_kernel,
        grid=(n // tn,),
        in_specs=[pl.BlockSpec((m, d), lambda j: (0, 0)),
                  pl.BlockSpec((d, tn), lambda j: (0, j)),
                  pl.BlockSpec((1, tn), lambda j: (0, j))],
        out_specs=pl.BlockSpec((m, tn), lambda j: (0, j)),
        out_shape=jax.ShapeDtypeStruct((m, n), F32),
        compiler_params=_cparams(("arbitrary",)),
        name="mod_proj",
    )(c, w_mod, b_mod.reshape(1, n))


def _ffn_kernel(x_ref, sh_ref, sc_ref, gt_ref, gpre_ref, gpost_ref, wg_ref, wu_ref, wd_ref,
                o_ref, a_ref, acc_ref, *, res_w, nf):
    f = pl.program_id(1)

    @pl.when(f == 0)
    def _():
        y = _rms(x_ref[...], gpre_ref[...])
        a_ref[...] = (y * (1.0 + sc_ref[...]) + sh_ref[...]).astype(BF16)
        acc_ref[...] = jnp.zeros_like(acc_ref)

    a = a_ref[...]
    h = _dot(a, wg_ref[...])
    u = _dot(a, wu_ref[...])
    act = (h * jax.nn.sigmoid(h) * u).astype(BF16)
    acc_ref[...] += _dot(act, wd_ref[...])

    @pl.when(f == nf - 1)
    def _():
        o_ref[...] = x_ref[...] + res_w * gt_ref[...] * _rms(acc_ref[...], gpost_ref[...])


FFN_TM = 512
FFN_TF = 512


def _ffn(x, shift, scale, gate, g_pre, g_post, wg, wu, wd, res_w, tm, tf):
    r, d = x.shape
    fdim = wg.shape[1]
    nf = fdim // tf
    return pl.pallas_call(
        functools.partial(_ffn_kernel, res_w=res_w, nf=nf),
        grid=(r // tm, nf),
        in_specs=[pl.BlockSpec((tm, d), lambda i, f: (i, 0)),
                  _row_spec(shift, tm), _row_spec(scale, tm), _row_spec(gate, tm),
                  pl.BlockSpec((1, d), lambda i, f: (0, 0)),
                  pl.BlockSpec((1, d), lambda i, f: (0, 0)),
                  pl.BlockSpec((d, tf), lambda i, f: (0, f)),
                  pl.BlockSpec((d, tf), lambda i, f: (0, f)),
                  pl.BlockSpec((tf, d), lambda i, f: (f, 0))],
        out_specs=pl.BlockSpec((tm, d), lambda i, f: (i, 0)),
        out_shape=jax.ShapeDtypeStruct((r, d), F32),
        scratch_shapes=[pltpu.VMEM((tm, d), BF16), pltpu.VMEM((tm, d), F32)],
        compiler_params=_cparams(("parallel", "arbitrary")),
        name="ffn",
    )(x, shift, scale, gate, g_pre, g_post, wg, wu, wd)


IN_SEGS = (CM_WIDTH, CM_WIDTH, NSA_WIDTH, 2 * KV_COLS, 2 * KV_COLS, 2 * KV_COLS, LANES)


def _in_proj_kernel(x_ref, sh_ref, sc_ref, gpre_ref, w_ref, *o_refs):
    y = _rms(x_ref[...], gpre_ref[...])
    a = (y * (1.0 + sc_ref[...]) + sh_ref[...]).astype(BF16)
    z = _dot(a, w_ref[...])
    off = 0
    for o_ref, width in zip(o_refs, IN_SEGS):
        o_ref[...] = z[:, off:off + width]
        off += width


def _in_proj(x, shift, scale, g_pre, w, tm):
    r, d = x.shape
    n = w.shape[1]
    return pl.pallas_call(
        _in_proj_kernel,
        grid=(r // tm,),
        in_specs=[pl.BlockSpec((tm, d), lambda i: (i, 0)),
                  _row_spec(shift, tm), _row_spec(scale, tm),
                  pl.BlockSpec((1, d), lambda i: (0, 0)),
                  pl.BlockSpec((d, n), lambda i: (0, 0), pipeline_mode=pl.Buffered(1))],
        out_specs=[pl.BlockSpec((tm, s), lambda i: (i, 0)) for s in IN_SEGS],
        out_shape=[jax.ShapeDtypeStruct((r, s), F32) for s in IN_SEGS],
        compiler_params=_cparams(("parallel",)),
        name="in_proj",
    )(x, shift, scale, g_pre, w)


IN_P_TM = 2 * QT
TILE_W = NSA_DK
IN_P_QW = NSA_HEADS * TILE_W
IN_P_KW = NSA_G * TILE_W
IN_P_ROWS_F = N_BRANCH * ROW_W
IN_P_GATE_ROWS = 64


def _in_proj_prompt_kernel(x_ref, sh_ref, sc_ref, gpre_ref, wr_ref, wf_ref, u_ref, v_ref, kvc_ref,
                           qt_ref, gt_ref, ks_ref, vst_ref, kw_ref, vwt_ref, kvt_ref, y0_ref):
    y0_ref[...] = jnp.zeros(y0_ref.shape, F32)
    y = _rms(x_ref[...], gpre_ref[...])
    a = (y * (1.0 + sc_ref[...]) + sh_ref[...]).astype(BF16)
    z = _dot(a, wr_ref[...])
    zt = _dot_nt(wf_ref[...], a)
    nt = IN_P_TM // QT
    tile = lambda t: slice(t * QT, (t + 1) * QT)
    u_ref[...] = z[:, 0:CM_WIDTH]
    v_ref[...] = z[:, CM_WIDTH:2 * CM_WIDTH]
    off = 2 * CM_WIDTH
    for t in range(nt):
        for gh in range(NSA_HEADS):
            g, h = divmod(gh, NSA_HPG)
            qt_ref[g, t, h * QT:(h + 1) * QT, :] = z[tile(t), off + gh * TILE_W:off + (gh + 1) * TILE_W].astype(BF16)
    off += IN_P_QW
    kvc_ref[...] = z[:, off:off + ROW_W]
    off += ROW_W
    for g in range(NSA_G):
        ks_ref[g, 0] = z[:, off + g * TILE_W:off + (g + 1) * TILE_W].astype(BF16)
    off += IN_P_KW
    for t in range(nt):
        for g in range(NSA_G):
            kw_ref[g, t] = z[tile(t), off + g * TILE_W:off + (g + 1) * TILE_W].astype(BF16)
    kvt_ref[...] = zt[0:IN_P_ROWS_F]
    v_rows = lambda branch, g: slice(branch * ROW_W + KV_COLS + g * NSA_DK,
                                     branch * ROW_W + KV_COLS + (g + 1) * NSA_DK)
    for g in range(NSA_G):
        vst_ref[g, 0] = zt[v_rows(1, g), :].astype(BF16)
        for t in range(nt):
            vwt_ref[g, t] = zt[v_rows(2, g), tile(t)].astype(BF16)
            gt_ref[g, t, N_BRANCH:, :] = jnp.zeros((8 - N_BRANCH, HQ), F32)
            for br in range(N_BRANCH):
                for h in range(NSA_HPG):
                    row = IN_P_ROWS_F + (g * N_BRANCH + br) * NSA_HPG + h
                    gt_ref[g, t, br:br + 1, h * QT:(h + 1) * QT] = zt[row:row + 1, tile(t)]


def _in_proj_prompt_weights(w):
    d = w.shape[0]
    q0 = 2 * CM_WIDTH
    k0 = q0 + NSA_WIDTH
    pad_lanes = lambda x: jnp.pad(x, ((0, 0), (0, 0), (0, TILE_W - NSA_DK))).reshape(d, -1)
    wq = pad_lanes(w[:, q0:k0].reshape(d, NSA_HEADS, NSA_DK) * Q_SCALE)
    k_of = lambda branch: pad_lanes(
        w[:, k0 + branch * ROW_W:k0 + branch * ROW_W + KV_COLS].reshape(d, NSA_G, NSA_DK))
    w_rows = jnp.concatenate([w[:, :q0], wq, w[:, k0:k0 + ROW_W], k_of(1), k_of(2)], axis=1).astype(BF16)
    g0 = k0 + N_BRANCH * ROW_W
    wg = jnp.transpose(w[:, g0:g0 + N_BRANCH * NSA_HEADS].reshape(d, NSA_G, NSA_HPG, N_BRANCH),
                       (0, 1, 3, 2)).reshape(d, N_BRANCH * NSA_HEADS)
    wg = jnp.pad(wg, ((0, 0), (0, IN_P_GATE_ROWS - N_BRANCH * NSA_HEADS)))
    w_feat = jnp.concatenate([w[:, k0:g0], wg], axis=1).T.astype(BF16)
    return w_rows, w_feat


def _in_proj_prompt(x, shift, scale, g_pre, w_rows, w_feat):
    t, d = x.shape
    tm = IN_P_TM
    nq = t // QT
    per = KT_SEL // tm
    const = lambda a: pl.BlockSpec(a.shape, lambda i: (0, 0), pipeline_mode=pl.Buffered(1))
    rows = lambda n: pl.BlockSpec((tm, n), lambda i: (i, 0))
    qtile = lambda a: pl.BlockSpec((NSA_G, tm // QT) + a[2:], lambda i: (0, i, 0, 0))
    shapes = [((t, CM_WIDTH), F32), ((t, CM_WIDTH), F32), ((t, ROW_W), F32),
              ((NSA_G, nq, HQ, TILE_W), BF16), ((NSA_G, nq, 8, HQ), F32),
              ((NSA_G, t // KT_SEL, KT_SEL, TILE_W), BF16), ((NSA_G, t // KT_SEL, NSA_DK, KT_SEL), BF16),
              ((NSA_G, nq, QT, TILE_W), BF16), ((NSA_G, nq, NSA_DK, QT), BF16),
              ((IN_P_ROWS_F, t), F32), ((t, NSA_WIDTH), F32)]
    out_specs = [rows(CM_WIDTH), rows(CM_WIDTH), rows(ROW_W),
                 qtile(shapes[3][0]), qtile(shapes[4][0]),
                 pl.BlockSpec((NSA_G, 1, tm, TILE_W), lambda i: (0, i // per, i % per, 0)),
                 pl.BlockSpec((NSA_G, 1, NSA_DK, tm), lambda i: (0, i // per, 0, i % per)),
                 qtile(shapes[7][0]), qtile(shapes[8][0]),
                 pl.BlockSpec((IN_P_ROWS_F, tm), lambda i: (0, i)), rows(NSA_WIDTH)]
    return pl.pallas_call(
        _in_proj_prompt_kernel,
        grid=(t // tm,),
        in_specs=[rows(d), _row_spec(shift, tm), _row_spec(scale, tm),
                  pl.BlockSpec((1, d), lambda i: (0, 0)), const(w_rows), const(w_feat)],
        out_specs=out_specs,
        out_shape=[jax.ShapeDtypeStruct(s, dt) for s, dt in shapes],
        compiler_params=_cparams(("parallel",)),
        name="in_proj_prompt",
    )(x, shift, scale, g_pre, w_rows, w_feat)


def _chunk_mix_kernel(u_ref, v_ref, nrm_ref, ws_ref, bst_ref, y_ref, vn_ref, *, chunk):
    ri = lax.broadcasted_iota(jnp.int32, (chunk, chunk), 0)
    ci = lax.broadcasted_iota(jnp.int32, (chunk, chunk), 1)
    lower = ci <= ri
    for h in range(CM_HEADS):
        sl = slice(h * CM_HEAD_DIM, (h + 1) * CM_HEAD_DIM)
        vf = jax.nn.gelu(v_ref[:, sl])
        mu = jnp.mean(vf, axis=-1, keepdims=True)
        var = jnp.mean(jnp.square(vf - mu), axis=-1, keepdims=True)
        vn = (vf - mu) * lax.rsqrt(var + EPS) * nrm_ref[:, sl]
        vn_ref[:, sl] = vn
        w = jnp.where(lower, ws_ref[h], 0.0).astype(BF16)
        s = _dot(w, vn.astype(BF16)) + bst_ref[:, h:h + 1]
        y_ref[:, sl] = jax.nn.gelu(u_ref[:, sl]) * s


def _chunk_mix(u, v, cm_norm, cm_ws, cm_bs, chunk):
    r = u.shape[0]
    ws = cm_ws[:, :chunk, :chunk]
    bst = cm_bs[:, :chunk].T
    return pl.pallas_call(
        functools.partial(_chunk_mix_kernel, chunk=chunk),
        grid=(r // chunk,),
        in_specs=[pl.BlockSpec((chunk, CM_WIDTH), lambda i: (i, 0)),
                  pl.BlockSpec((chunk, CM_WIDTH), lambda i: (i, 0)),
                  pl.BlockSpec((1, CM_WIDTH), lambda i: (0, 0)),
                  pl.BlockSpec((CM_HEADS, chunk, chunk), lambda i: (0, 0, 0)),
                  pl.BlockSpec((chunk, CM_HEADS), lambda i: (0, 0))],
        out_specs=[pl.BlockSpec((chunk, CM_WIDTH), lambda i: (i, 0)),
                   pl.BlockSpec((chunk, CM_WIDTH), lambda i: (i, 0))],
        out_shape=[jax.ShapeDtypeStruct((r, CM_WIDTH), F32),
                   jax.ShapeDtypeStruct((r, CM_WIDTH), F32)],
        compiler_params=_cparams(("parallel",)),
        name="chunk_mix",
    )(u, v, cm_norm.reshape(1, CM_WIDTH), ws, bst)


def _gather_cv(x, c):
    return jnp.concatenate(
        [x[:, l * ROW_W + c * KV_COLS:l * ROW_W + (c + 1) * KV_COLS] for l in range(CMP_STRIDE)],
        axis=1)


def _pos_bias(pos_ref, pw_ref, c):
    posb = _dot_f32(pos_ref[c], pw_ref[c])[0:1]
    return jnp.concatenate([posb] * NSA_G, axis=1)


def _compress_weights(phi_pos, phi_w1, phi_w2):
    eye = jnp.eye(NSA_G, dtype=F32)
    w1 = phi_w1.reshape(CMP_HALF, CMP_STRIDE, 2, NSA_DK, NSA_DK)
    w1big = jnp.einsum('mlcde,gh->clgdmhe', w1, eye).reshape(
        2, CMP_STRIDE * KV_COLS, CMP_HALF * KV_COLS).astype(BF16)
    w2big = jnp.einsum('cde,gh->cgdhe', phi_w2, eye).reshape(2, KV_COLS, KV_COLS).astype(BF16)
    pos = jnp.transpose(phi_pos, (1, 0, 2)).reshape(2, 1, CMP_LEN * NSA_DK)
    pos = jnp.broadcast_to(pos, (2, 8, CMP_LEN * NSA_DK))
    pw = jnp.transpose(phi_w1, (1, 0, 2, 3)).reshape(2, CMP_LEN * NSA_DK, NSA_DK)
    return w1big, pos, pw, w2big


def _compress_kernel(x_ref, xn_ref, w1_ref, pos_ref, pw_ref, w2_ref, o_ref, *, rc):
    x = x_ref[0]
    xn = xn_ref[0]
    last_row = lax.broadcasted_iota(jnp.int32, (rc, KV_COLS), 0) == rc - 1
    for c in range(2):
        w1 = w1_ref[c]
        y = _dot(_gather_cv(x, c).astype(BF16), w1)
        yn = _dot(_gather_cv(xn, c).astype(BF16), w1)
        second = pltpu.roll(y[:, KV_COLS:], rc - 1, 0)
        second = jnp.where(last_row, yn[0:1, KV_COLS:], second)
        pre = y[:, :KV_COLS] + second + _pos_bias(pos_ref, pw_ref, c)
        o_ref[0, :, c * KV_COLS:(c + 1) * KV_COLS] = _dot(jax.nn.gelu(pre).astype(BF16), w2_ref[c])


def _pick_rows(n, cap):
    best = 8
    for rc in range(8, cap + 1, 8):
        if n % rc == 0:
            best = rc
    return best


def _compress(x, cw):
    bsz, n, _ = x.shape
    rc = _pick_rows(n, 256)
    w1big, pos, pw, w2big = cw
    nb8 = n // 8
    return pl.pallas_call(
        functools.partial(_compress_kernel, rc=rc),
        grid=(bsz, n // rc),
        in_specs=[pl.BlockSpec((1, rc, x.shape[2]), lambda b, i: (b, i, 0)),
                  pl.BlockSpec((1, 8, x.shape[2]),
                               lambda b, i: (b, jnp.minimum((i + 1) * (rc // 8), nb8 - 1), 0)),
                  pl.BlockSpec(w1big.shape, lambda b, i: (0, 0, 0)),
                  pl.BlockSpec(pos.shape, lambda b, i: (0, 0, 0)),
                  pl.BlockSpec(pw.shape, lambda b, i: (0, 0, 0)),
                  pl.BlockSpec(w2big.shape, lambda b, i: (0, 0, 0))],
        out_specs=pl.BlockSpec((1, rc, ROW_W), lambda b, i: (b, i, 0)),
        out_shape=jax.ShapeDtypeStruct((bsz, n, ROW_W), F32),
        compiler_params=_cparams(("parallel", "arbitrary")),
        name="compress",
    )(x, x, w1big, pos, pw, w2big)


def _compress_tokens(kv_c, cw):
    bsz, t = kv_c.shape[:2]
    n = t // CMP_STRIDE
    npad = _round_up(n, LANES if n >= LANES else 8)
    x = kv_c.reshape(bsz, n, CMP_STRIDE * ROW_W)
    x = jnp.pad(x, ((0, 0), (0, npad - n), (0, 0)))
    return _compress(x, cw)[:, :n - (CMP_HALF - 1)]


def _rel_bucket(d):
    n = jnp.maximum(d, 0)
    large = jnp.full(d.shape, REL_BUCKETS // 2, jnp.int32)
    for thr in REL_THRESHOLDS:
        large = large + (n >= thr).astype(jnp.int32)
    return jnp.where(n < REL_BUCKETS // 2, n, large)


def _rel_bias_lanes(d, tab_ref):
    bkt = _rel_bucket(d)
    f = jnp.zeros(d.shape, F32)
    for k in range(REL_BUCKETS):
        f = jnp.where(bkt == k, tab_ref[0, k:k + 1, :], f)
    return f


def _rel_bias_rows(d, tab_ref):
    bkt = _rel_bucket(d)
    f = jnp.zeros(d.shape, F32)
    for k in range(REL_BUCKETS):
        f = jnp.where(bkt == k, tab_ref[:, k:k + 1], f)
    return f


def _near_tile_kernel(tab_ref, o_ref):
    t = pl.program_id(1)
    delta = jnp.where(t < N_NEAR, t * QT, jnp.where(t == TN_WIN, WINDOW, 0))
    dmax = jnp.where(t == TN_WIN, WINDOW, 1 << 30)
    j = lax.broadcasted_iota(jnp.int32, (QT, HQ), 0)
    i = lax.broadcasted_iota(jnp.int32, (QT, HQ), 1) & (QT - 1)
    d = delta + i - j
    far = tab_ref[0, REL_BUCKETS - 1:REL_BUCKETS, :]
    val = jnp.where((d >= 0) & (d <= dmax), _rel_bias_lanes(d, tab_ref) - far, NEG)
    val = jnp.where(t == TN_MASKED, NEG, jnp.where(t == TN_ZERO, 0.0, val))
    o_ref[0, 0] = val


def _cmp_window_kernel(tab_ref, o_ref):
    e = lax.broadcasted_iota(jnp.int32, (WT_ROWS, HQ), 0) - WT_LEAD
    i = lax.broadcasted_iota(jnp.int32, (WT_ROWS, HQ), 1) & (QT - 1)
    d = i - (CMP_LEN - 1) - CMP_STRIDE * e
    far = tab_ref[0, REL_BUCKETS - 1:REL_BUCKETS, :]
    o_ref[0] = jnp.where(d >= 0, _rel_bias_lanes(d, tab_ref) - far, 0.0)


def _prompt_bias_tiles(rel_bias):
    tab = jnp.repeat((rel_bias * LOG2E).reshape(REL_BUCKETS, NSA_G, NSA_HPG), QT, axis=2)
    tab = jnp.transpose(tab, (1, 0, 2))
    near = pl.pallas_call(
        _near_tile_kernel,
        grid=(NSA_G, TN_COUNT),
        in_specs=[pl.BlockSpec((1, REL_BUCKETS, HQ), lambda g, t: (g, 0, 0))],
        out_specs=pl.BlockSpec((1, 1, QT, HQ), lambda g, t: (g, t, 0, 0)),
        out_shape=jax.ShapeDtypeStruct((NSA_G, TN_COUNT, QT, HQ), F32),
        compiler_params=_cparams(("parallel", "arbitrary")),
        name="bias_near_tiles",
    )(tab)
    cwin = pl.pallas_call(
        _cmp_window_kernel,
        grid=(NSA_G,),
        in_specs=[pl.BlockSpec((1, REL_BUCKETS, HQ), lambda g: (g, 0, 0))],
        out_specs=pl.BlockSpec((1, WT_ROWS, HQ), lambda g: (g, 0, 0)),
        out_shape=jax.ShapeDtypeStruct((NSA_G, WT_ROWS, HQ), F32),
        compiler_params=_cparams(("parallel",)),
        name="bias_cmp_window",
    )(tab)
    return near, cwin


GROUPS_PER_STEP = 2
N_WIN_TILES = WINDOW // QT + 1
NSA_SEGMENTS = 4


def _nsa_group(q_ref, kc_ref, vct_ref, ks_ref, vst_ref, kw_refs, vwt_refs, tn_ref, wt_ref,
               g_ref, o_ref, s_ref, imp_ref, mb_ref, m_ref, acc_ref,
               buf0_ref, buf1_ref, tmax_ref, part_ref, *, ncp, nblk, n_sel, b_off):
    b = pl.program_id(1) + b_off
    qs = b * QT
    q = q_ref[0, 0]

    sc = _dot_nt(kc_ref[0], q)
    ci = lax.broadcasted_iota(jnp.int32, (ncp, HQ), 0)
    li = lax.broadcasted_iota(jnp.int32, (ncp, HQ), 1) & (QT - 1)
    vis = (CMP_STRIDE * ci + (CMP_LEN - 1) - li) <= qs
    s_ref[0:WT_LEAD, :] = jnp.full((WT_LEAD, HQ), NEG, F32)
    s_ref[WT_LEAD + ncp:, :] = jnp.full((s_ref.shape[0] - WT_LEAD - ncp, HQ), NEG, F32)
    s_ref[WT_LEAD:WT_LEAD + ncp, :] = jnp.where(vis, sc, NEG)
    r0 = pl.multiple_of(8 * b, 8)
    s_ref[pl.ds(r0, WT_ROWS), :] += wt_ref[0]
    yield None
    s = s_ref[WT_LEAD:WT_LEAD + ncp, :]
    m = jnp.max(s, axis=0, keepdims=True)
    m = jnp.where(m <= 0.1 * NEG, 0.0, m)
    p = jnp.exp2(s - m)
    rinv = 1.0 / jnp.maximum(jnp.sum(p, axis=0, keepdims=True), 1e-30)
    o_c = _dot(vct_ref[0], p.astype(BF16)) * rinv
    yield None
    pn = p * rinv
    imp = pn[:, 0:QT]
    for h in range(1, NSA_HPG):
        imp = imp + pn[:, h * QT:(h + 1) * QT]
    imp_ref[0:8, :] = jnp.zeros((8, QT), F32)
    imp_ref[8:8 + ncp, :] = imp
    if imp_ref.shape[0] > 8 + ncp:
        imp_ref[8 + ncp:, :] = jnp.zeros((imp_ref.shape[0] - 8 - ncp, QT), F32)
    ratio = SEL_BLOCK // CMP_STRIDE
    span = ratio + CMP_HALF - 1
    psl = imp_ref[pl.ds(8 - (CMP_HALF - 1), nblk, stride=ratio), :]
    for mm in range(1, span):
        psl = psl + imp_ref[pl.ds(8 - (CMP_HALF - 1) + mm, nblk, stride=ratio), :]

    yield None
    n_win = WINDOW // QT
    st_parts, vt_parts = [], []

    def window_tile(w):
        kt = b - n_win + w
        tn_idx = jnp.where(kt >= 0, TN_WIN if w == 0 else n_win - w, TN_MASKED)
        st_parts.append(_dot_nt(kw_refs[w][0, 0], q) + tn_ref[0, tn_idx])
        vt_parts.append(vwt_refs[w][0, 0])

    def window_finish():
        st = jnp.concatenate(st_parts, axis=0)
        p = jnp.exp2(st - jnp.max(st, axis=0, keepdims=True))
        o_w = _dot(jnp.concatenate(vt_parts, axis=1), p.astype(BF16)) * (
            1.0 / jnp.sum(p, axis=0, keepdims=True))
        g = jax.nn.sigmoid(g_ref[0, 0])
        part_ref[...] = g[0:1] * o_c + g[2:3] * o_w

    window_pieces = [functools.partial(window_tile, w) for w in range(n_win + 1)] + [window_finish]

    ji = lax.broadcasted_iota(jnp.int32, (nblk, QT), 0)
    ii = lax.broadcasted_iota(jnp.int32, (nblk, QT), 1)
    cur = (qs + ii) // SEL_BLOCK
    jf = ji.astype(F32)
    forced = (ji == 0) | (ji == cur) | (ji == cur - 1)
    score0 = jnp.where(forced, -jnp.inf, jnp.where(ji <= cur, psl, -BIG))

    score, selb = score0, jnp.where(forced, 0.0, NEG)
    for _ in range(max(n_sel - N_FORCED, 0)):
        yield None
        mx = jnp.max(score, axis=0, keepdims=True)
        first = jnp.min(jnp.where(score == mx, jf, 1e9), axis=0, keepdims=True)
        pick = jf == first
        score, selb = jnp.where(pick, -jnp.inf, score), jnp.where(pick, 0.0, selb)
        if window_pieces:
            window_pieces.pop(0)()
    while window_pieces:
        window_pieces.pop(0)()
    mb_ref[0:nblk, :] = jnp.concatenate([selb] * NSA_HPG, axis=1)
    mb_ref[nblk:, :] = jnp.full((BLK_PER_KT, HQ), NEG, F32)

    m_ref[...] = jnp.full(m_ref.shape, NEG, F32)
    acc_ref[...] = jnp.zeros(acc_ref.shape, F32)

    def sel_logits(kt, valid):
        ktc = jnp.where(valid, kt, 0)
        st = _dot_nt(ks_ref[0, ktc], q)
        row0 = jnp.where(valid, ktc * BLK_PER_KT, nblk)
        mb = mb_ref[pl.ds(pl.multiple_of(row0, BLK_PER_KT), BLK_PER_KT), :]
        st = jnp.concatenate(
            [st[k * SEL_BLOCK:(k + 1) * SEL_BLOCK] + mb[k:k + 1] for k in range(BLK_PER_KT)], axis=0)
        return st, ktc

    def softmax_step(st, tmax, vt):
        nk = vt.shape[1]
        ones = jnp.where(lax.broadcasted_iota(jnp.int32, (V_ROWS - NSA_DK, nk), 0) == 0, 1.0, 0.0)
        vt = jnp.concatenate([vt, ones.astype(BF16)], axis=0)
        m_old = m_ref[...]
        m_new = jnp.maximum(m_old, tmax)
        alpha = jnp.exp2(m_old - m_new)
        p = jnp.exp2(st - m_new)
        acc_ref[...] = alpha * acc_ref[...] + _dot(vt, p.astype(BF16))
        m_ref[...] = m_new

    nt = b // SUB_PER_KT + 1
    n_far = jnp.maximum(nt - NEAR_TILES, 0)

    def far_logits(kt, buf_ref, slot):
        st, _ = sel_logits(kt, kt < n_far)
        buf_ref[...] = st
        tmax_ref[slot:slot + 1, :] = jnp.max(st, axis=0, keepdims=True)

    def far_first():
        far_logits(0, buf0_ref, 0)

    def far_pair(pi):
        k0 = 2 * pi
        far_logits(k0 + 1, buf1_ref, 1)
        softmax_step(buf0_ref[...], tmax_ref[0:1, :], vst_ref[0, k0])
        far_logits(k0 + 2, buf0_ref, 0)
        k1 = jnp.minimum(k0 + 1, jnp.maximum(n_far - 1, 0))
        softmax_step(buf1_ref[...], tmax_ref[1:2, :], vst_ref[0, k1])

    yield far_first, far_pair

    st_parts, vt_parts = [], []
    for w in range(NEAR_TILES):
        kt = nt - NEAR_TILES + w
        st, ktc = sel_logits(kt, kt >= 0)
        r = b - SUB_PER_KT * ktc
        parts = []
        for u in range(SUB_PER_KT):
            ru = r - u
            idx = jnp.where(ru < 0, TN_MASKED, jnp.where(ru >= N_NEAR, TN_ZERO, ru))
            parts.append(tn_ref[0, idx])
        st_parts.append(st + jnp.concatenate(parts, axis=0))
        vt_parts.append(vst_ref[0, ktc])
    yield None
    st = jnp.concatenate(st_parts, axis=0)
    softmax_step(st, jnp.max(st, axis=0, keepdims=True), jnp.concatenate(vt_parts, axis=1))
    acc = acc_ref[...]
    o_s = acc[0:NSA_DK] * (1.0 / acc[NSA_DK:NSA_DK + 1])
    total = part_ref[...] + jax.nn.sigmoid(g_ref[0, 0, 1:2]) * o_s
    pairs = [jnp.concatenate([total[:, h * QT:(h + 1) * QT], total[:, (h + 1) * QT:(h + 2) * QT]], axis=0).T
             for h in range(0, NSA_HPG, 2)]
    o_ref[...] = jnp.concatenate(pairs, axis=1)


GROUP_COLS = NSA_HPG * NSA_DK


def _nsa_kernel(*refs, n_scratch, **static):
    gp = GROUPS_PER_STEP
    refs = refs[1:]
    n_all = gp * n_scratch
    ins, o_ref = refs[:len(refs) - n_all - 1], refs[len(refs) - n_all - 1]
    scratch = refs[len(refs) - n_all:]

    b = pl.program_id(1) + static["b_off"]
    n_far = jnp.maximum(b // SUB_PER_KT + 1 - NEAR_TILES, 0)
    programs = []
    for gi in range(gp):
        v = [r.at[pl.ds(gi, 1)] for r in ins]
        kw_refs, vwt_refs = v[5:5 + N_WIN_TILES], v[5 + N_WIN_TILES:5 + 2 * N_WIN_TILES]
        rest = v[5 + 2 * N_WIN_TILES:] + [o_ref.at[:, pl.ds(gi * GROUP_COLS, GROUP_COLS)]]
        programs.append(_nsa_group(*v[:5], kw_refs, vwt_refs, *rest,
                                   *scratch[gi * n_scratch:(gi + 1) * n_scratch], **static))
    loops = [None] * gp
    while any(lp is None for lp in loops):
        for gi, p in enumerate(programs):
            if loops[gi] is None:
                loops[gi] = next(p)

    @pl.when(n_far > 0)
    def _():
        for far_first, _ in loops:
            far_first()

    def in_turns(generators):
        active = list(generators)
        while active:
            for p in list(active):
                if next(p, StopIteration) is StopIteration:
                    active.remove(p)

    def far_body(pi, carry):
        for _, far_pair in loops:
            far_pair(pi)
        return carry

    lax.fori_loop(0, (n_far + 1) // 2, far_body, 0)
    in_turns(programs)


def _nsa_attend(qt, kc, vct, ks, vst, kw, vwt, tn, wt, gt, y, *, n_sel_blk):
    s_n, nq = qt.shape[:2]
    gp = GROUPS_PER_STEP
    assert kc.shape[1] >= 8 * nq and ks.shape[1] * SUB_PER_KT >= nq and s_n % gp == 0
    n_sel = min(N_SEL, n_sel_blk)
    n_seg = NSA_SEGMENTS if nq % (NSA_SEGMENTS * SUB_PER_KT) == 0 else 1
    nq_seg = nq // n_seg
    assert y.shape == (nq * QT, s_n * GROUP_COLS)
    for seg in range(n_seg):
        b_off = seg * nq_seg
        b_end = b_off + nq_seg
        ncp = min(_round_up(8 * b_end, LANES), kc.shape[1])
        nkt = b_end // SUB_PER_KT
        nblk = nkt * BLK_PER_KT
        imp_rows = max(8 + ncp, 8 + (SEL_BLOCK // CMP_STRIDE) * nblk)
        scratch = [(ncp + WT_ROWS, HQ), (imp_rows, QT), (nblk + BLK_PER_KT, HQ), (1, HQ), (V_ROWS, HQ),
                   (KT_SEL, HQ), (KT_SEL, HQ), (8, HQ), (NSA_DK, HQ)]
        kern = functools.partial(_nsa_kernel, n_scratch=len(scratch), ncp=ncp, nblk=nblk, n_sel=n_sel,
                                 b_off=b_off)
        lead = lambda a, n: pl.BlockSpec((gp, n) + a.shape[2:], lambda s, i: (s,) + (0,) * (a.ndim - 1),
                                         pipeline_mode=pl.Buffered(1))
        per_tile = lambda a: pl.BlockSpec((gp, 1) + a.shape[2:], lambda s, i: (s, i + b_off, 0, 0))
        win = lambda a, w: pl.BlockSpec(
            (gp, 1) + a.shape[2:],
            lambda s, i: (s, jnp.maximum(i + b_off - (N_WIN_TILES - 1) + w, 0), 0, 0))
        in_specs = ([pl.BlockSpec(memory_space=pl.ANY), per_tile(qt), lead(kc, ncp),
                     pl.BlockSpec((gp, NSA_DK, ncp), lambda s, i: (s, 0, 0), pipeline_mode=pl.Buffered(1)),
                     lead(ks, nkt), lead(vst, nkt)]
                    + [win(kw, w) for w in range(N_WIN_TILES)]
                    + [win(vwt, w) for w in range(N_WIN_TILES)]
                    + [lead(tn, tn.shape[1]), lead(wt, wt.shape[1]), per_tile(gt)])
        args = (y, qt, kc, vct, ks, vst, *([kw] * N_WIN_TILES), *([vwt] * N_WIN_TILES), tn, wt, gt)
        y = pl.pallas_call(
            kern,
            grid=(s_n // gp, nq_seg),
            in_specs=in_specs,
            out_specs=pl.BlockSpec((QT, gp * GROUP_COLS), lambda s, i: (i + b_off, s)),
            out_shape=jax.ShapeDtypeStruct((nq * QT, s_n * GROUP_COLS), F32),
            scratch_shapes=[pltpu.VMEM(s, F32) for s in scratch] * gp,
            input_output_aliases={0: 0},
            compiler_params=_cparams(("parallel", "arbitrary")),
            name="nsa_prompt",
        )(*args)
    return y


def _k_tiles(kv, tile):
    nt = kv.shape[0] // tile
    k = kv[..., :KV_COLS].reshape(nt, tile, NSA_G, NSA_DK)
    k = jnp.transpose(k, (2, 0, 1, 3))
    k = jnp.pad(k, ((0, 0), (0, 0), (0, 0), (0, TILE_W - NSA_DK))).astype(BF16)
    v = kv[..., KV_COLS:].reshape(nt, tile, NSA_G, NSA_DK)
    v = jnp.transpose(v, (2, 0, 3, 1)).astype(BF16)
    return k, v


def _prompt_attend(qt, gt, tok, ks, vst, kw, vwt, bias, y0):
    nq = qt.shape[1]
    t = nq * QT
    near, cwin = bias
    ncp = max(_round_up(tok.shape[0] + 1, LANES), _round_up(8 * nq, LANES))
    kc, vct = _k_tiles(jnp.pad(tok, ((0, ncp - tok.shape[0]), (0, 0))), ncp)
    return _nsa_attend(qt, kc[:, 0], vct[:, 0], ks, vst, kw, vwt, near, cwin, gt, y0,
                       n_sel_blk=t // SEL_BLOCK)


def _sample_bias_kernel(tab_ref, cb_ref, sb_ref, snb_ref, wb_ref, *, past, dseq, ns):
    rows = tab_ref.shape[0]
    far = tab_ref[:, REL_BUCKETS - 1:REL_BUCKETS]

    def tok_of(n):
        return lax.broadcasted_iota(jnp.int32, (rows, n), 0) % dseq

    def lane(n):
        return lax.broadcasted_iota(jnp.int32, (rows, n), 1)

    for idx in range(2):
        rho = LANES * (ns - 1 + idx) + lane(LANES)
        d = past + tok_of(LANES) - (CMP_STRIDE * (rho - 1) + CMP_LEN - 1)
        cb_ref[idx] = jnp.where(d >= 0, _rel_bias_rows(d, tab_ref) - far, 0.0)
    d = SP_KEYS + tok_of(SP_KEYS) - lane(SP_KEYS)
    sb_ref[...] = _rel_bias_rows(d, tab_ref) - far
    d = tok_of(LANES) - lane(LANES)
    snb_ref[...] = jnp.where(d >= 0, _rel_bias_rows(d, tab_ref) - far, NEG)
    d = WINDOW + tok_of(WINDOW + LANES) - lane(WINDOW + LANES)
    wb_ref[...] = jnp.where((d >= 0) & (d <= WINDOW), _rel_bias_rows(d, tab_ref), NEG)


def _sample_bias(rel_bias, past, dseq, ns):
    rows = NSA_HEADS * dseq
    tab = (rel_bias * LOG2E).reshape(REL_BUCKETS, NSA_G, NSA_HPG)
    tab = jnp.transpose(tab, (2, 1, 0))
    tab = jnp.repeat(tab.reshape(NSA_HEADS, 1, REL_BUCKETS), dseq, axis=1).reshape(rows, REL_BUCKETS)
    tab = jnp.pad(tab, ((0, 0), (0, LANES - REL_BUCKETS)))
    shapes = [(2, rows, LANES), (rows, SP_KEYS), (rows, LANES), (rows, WINDOW + LANES)]
    cb, sb, snb, wb = pl.pallas_call(
        functools.partial(_sample_bias_kernel, past=past, dseq=dseq, ns=ns),
        out_shape=[jax.ShapeDtypeStruct(s, F32) for s in shapes],
        compiler_params=pltpu.CompilerParams(vmem_limit_bytes=VMEM_LIMIT),
        name="bias_sample",
    )(tab)
    return cb, sb, snb, wb


def _sample_nsa_kernel(pt_ref, *refs, ns, past, dseq, n_sel, n_sel_blk):
    cmp_pages = refs[:SP_PP]
    slc_pages = refs[SP_PP:2 * SP_PP]
    (xn_ref, ksn_ref, win_ref, kwn_ref, q_ref, g_ref, w1_ref, pos_ref, pw_ref, w2_ref, e_ref,
     cb_ref, sb_ref, snb_ref, wb_ref, o_ref,
     s_ref, vc_ref, carry_ref, sel_ref, m_ref, l_ref, acc_ref, oc_ref, x_ref) = refs[2 * SP_PP:]
    j = pl.program_id(1)
    qbd = q_ref[0]
    rows = qbd.shape[0]
    ntile = ns + 1
    nblk_l = sel_ref.shape[0] * SP_BLKS

    def feat_major(refs_, c):
        return jnp.concatenate([r[0, c].reshape(KV_COLS, r.shape[-1]) for r in refs_], axis=1)

    def tokens_of(gather, prepare=None):
        out = []
        for c in range(2):
            if prepare is not None:
                prepare(c)
            xc = gather(c)
            n = xc.shape[0]
            row0 = lax.broadcasted_iota(jnp.int32, (n, KV_COLS), 0) == 0
            y = _dot(xc.astype(BF16), w1_ref[c])
            first = jnp.where(row0, carry_ref[c:c + 1, :], pltpu.roll(y[:, :KV_COLS], 1, 0))
            carry_ref[c:c + 1, :] = y[n - 1:n, :KV_COLS]
            pre = first + y[:, KV_COLS:] + _pos_bias(pos_ref, pw_ref, c)
            out.append(_dot(jax.nn.gelu(pre).astype(BF16), w2_ref[c]))
        return out

    def online_update(s, v):
        m_old = m_ref[...]
        m_new = jnp.maximum(m_old, jnp.max(s, axis=1, keepdims=True))
        alpha = jnp.exp2(m_old - m_new)
        p = jnp.exp2(s - m_new)
        l_ref[...] = alpha * l_ref[...] + jnp.sum(p, axis=1, keepdims=True)
        acc_ref[...] = alpha * acc_ref[...] + _dot_nt(p.astype(BF16), v)
        m_ref[...] = m_new

    @pl.when(j == 0)
    def _():
        carry_ref[...] = jnp.zeros(carry_ref.shape, F32)

    @pl.when(j < ns)
    def _():
        def to_rows(c):
            for p, r in enumerate(cmp_pages):
                for gp in range(2):
                    tile = r[0, c, 2 * gp:2 * gp + 2].reshape(LANES, PAGE_SIZE)
                    x_ref[2 * c + gp, p * PAGE_SIZE:(p + 1) * PAGE_SIZE, :] = tile.T

        def gather(c):
            return jnp.concatenate(
                [pltpu.einshape("(nl)k->n(lk)", x_ref[2 * c + gp], l=CMP_STRIDE) for gp in range(2)], axis=1)

        tok_k, tok_v = tokens_of(gather, to_rows)
        s_ref[j] = _dot_nt(qbd, tok_k.astype(BF16))
        vc_ref[j] = tok_v.astype(BF16)

    @pl.when(j == ns - 1)
    def _():
        half = KV_COLS // 2
        tok_k, tok_v = tokens_of(lambda c: jnp.concatenate(
            [xn_ref[0, :, l * ROW_W + c * KV_COLS + gp * half:l * ROW_W + c * KV_COLS + (gp + 1) * half]
             for gp in range(2) for l in range(CMP_STRIDE)], axis=1))
        zpad = jnp.zeros((LANES - tok_k.shape[0], KV_COLS), F32)
        s_ref[ns] = _dot_nt(qbd, jnp.concatenate([tok_k, zpad], axis=0).astype(BF16))
        vc_ref[ns] = jnp.concatenate([tok_v, zpad], axis=0).astype(BF16)

        tiles = []
        for t in range(ntile):
            st = s_ref[t]
            if t >= ns - 1:
                st = st + cb_ref[t - (ns - 1)]
            tiles.append(st)
        s = jnp.concatenate(tiles, axis=1)
        width = ntile * LANES
        rho = lax.broadcasted_iota(jnp.int32, (rows, width), 1)
        tq = lax.broadcasted_iota(jnp.int32, (rows, width), 0) % dseq
        n_cmp = (past + _round_up(dseq, SEL_BLOCK)) // CMP_STRIDE - (CMP_HALF - 1)
        vis = (rho >= 1) & (rho <= n_cmp) & (CMP_STRIDE * (rho - 1) + CMP_LEN - 1 <= past + tq)
        s = jnp.where(vis, s, NEG)
        m = jnp.max(s, axis=1, keepdims=True)
        m = jnp.where(m <= 0.1 * NEG, 0.0, m)
        p = jnp.exp2(s - m)
        rinv = 1.0 / jnp.maximum(jnp.sum(p, axis=1, keepdims=True), 1e-30)
        vc = jnp.concatenate([vc_ref[t] for t in range(ntile)], axis=0)
        oc_ref[...] = _dot(p.astype(BF16), vc) * rinv
        pn = p * rinv
        gt = NSA_G * dseq
        imp = pn[0:gt]
        for h in range(1, NSA_HPG):
            imp = imp + pn[h * gt:(h + 1) * gt]
        ratio = SEL_BLOCK // CMP_STRIDE
        span = ratio + CMP_HALF - 1
        ar = lax.broadcasted_iota(jnp.int32, (width, nblk_l), 0)
        ac = lax.broadcasted_iota(jnp.int32, (width, nblk_l), 1)
        band = jnp.where((ar >= ratio * ac) & (ar < ratio * ac + span), 1.0, 0.0).astype(BF16)
        i_hi = imp.astype(BF16)
        r1 = imp - i_hi.astype(F32)
        i_mid = r1.astype(BF16)
        i_lo = (r1 - i_mid.astype(F32)).astype(BF16)
        psl = _dot(i_hi, band) + _dot(i_mid, band) + _dot(i_lo, band)
        ji = lax.broadcasted_iota(jnp.int32, (gt, nblk_l), 1)
        tq2 = lax.broadcasted_iota(jnp.int32, (gt, nblk_l), 0) % dseq
        cur = (past + tq2) // SEL_BLOCK
        jf = ji.astype(F32)
        forced = (ji == 0) | (ji == cur) | (ji == cur - 1)
        score0 = jnp.where(forced, -jnp.inf, jnp.where(ji <= cur, psl, -BIG))
        score0 = jnp.where(ji < n_sel_blk, score0, -jnp.inf)

        def pick_one(_, carry):
            score, sel = carry
            mx = jnp.max(score, axis=1, keepdims=True)
            first = jnp.min(jnp.where(score == mx, jf, 1e9), axis=1, keepdims=True)
            pick = jf == first
            return jnp.where(pick, -jnp.inf, score), jnp.where(pick, 1.0, sel)

        _, sel = lax.fori_loop(0, max(n_sel - N_FORCED, 0), pick_one,
                               (score0, jnp.where(forced, 1.0, 0.0)))
        for t in range(sel_ref.shape[0]):
            piece = sel[:, t * SP_BLKS:(t + 1) * SP_BLKS]
            piece = jnp.concatenate([piece, jnp.zeros((gt, LANES - SP_BLKS), F32)], axis=1)
            sel_ref[t] = jnp.concatenate([piece] * NSA_HPG, axis=0).astype(BF16)
        m_ref[...] = jnp.full(m_ref.shape, NEG, F32)
        l_ref[...] = jnp.zeros(l_ref.shape, F32)
        acc_ref[...] = jnp.zeros(acc_ref.shape, F32)

    @pl.when(j >= ns)
    def _():
        j2 = j - ns
        s = _dot(qbd, feat_major(slc_pages, 0).astype(BF16))
        chosen = _dot(sel_ref[j2], e_ref[...])
        s = s + (chosen - 1.0) * (-NEG)
        s = s + jnp.where(j2 == ns - 1, sb_ref[...], 0.0)
        online_update(s, feat_major(slc_pages, 1).astype(BF16))

    @pl.when(j == 2 * ns - 1)
    def _():
        online_update(_dot(qbd, ksn_ref[0, 0].astype(BF16)) + snb_ref[...],
                      ksn_ref[0, 1].astype(BF16))
        o_s = acc_ref[...] * (1.0 / l_ref[...])
        kw = jnp.concatenate([feat_major([win_ref], 0), kwn_ref[0, 0]], axis=1)
        vw = jnp.concatenate([feat_major([win_ref], 1), kwn_ref[0, 1]], axis=1)
        s = _dot(qbd, kw.astype(BF16)) + wb_ref[...]
        p = jnp.exp2(s - jnp.max(s, axis=1, keepdims=True))
        o_w = _dot_nt(p.astype(BF16), vw.astype(BF16)) * (1.0 / jnp.sum(p, axis=1, keepdims=True))
        g = jax.nn.sigmoid(g_ref[0])
        o_ref[0] = g[:, 0:1] * oc_ref[...] + g[:, 1:2] * o_s + g[:, 2:3] * o_w


def _sample_attend(q, gates, kv_c, kv_s, kv_w, cache_cmp, cache_slc, win_buf, page_table, cw,
                   rel_bias):
    bsz, dseq = q.shape[:2]
    n_pages = page_table.shape[1]
    past = n_pages * PAGE_SIZE
    assert n_pages % SP_PP == 0 and dseq <= SEL_BLOCK and win_buf.shape[1] == WINDOW
    feat_major = lambda a: jnp.transpose(a, (0, 2, 3, 4, 1))
    cache_cmp, cache_slc, win_buf = feat_major(cache_cmp), feat_major(cache_slc), feat_major(win_buf)
    ns = n_pages // SP_PP
    rows = NSA_HEADS * dseq
    n_sel_blk = past // SEL_BLOCK + 1
    n_sel = min(N_SEL, n_sel_blk)
    nsel_tiles = _round_up(-(-n_sel_blk // SP_BLKS), LANES // SP_BLKS)
    w1big, pos, pw, w2big = cw
    w1big = jnp.transpose(w1big.reshape(2, CMP_STRIDE, 2, LANES, -1), (0, 2, 1, 3, 4)).reshape(w1big.shape)
    cb, sb, snb, wb = _sample_bias(rel_bias, past, dseq, ns)

    qr = jnp.transpose((q * Q_SCALE).reshape(bsz, dseq, NSA_G, NSA_HPG, NSA_DK), (0, 3, 2, 1, 4))
    qbd = jnp.einsum('bhgtd,gk->bhgtkd', qr, jnp.eye(NSA_G, dtype=F32))
    qbd = qbd.reshape(bsz, rows, KV_COLS).astype(BF16)
    gr = gates[..., :N_BRANCH * NSA_HEADS].reshape(bsz, dseq, NSA_G, NSA_HPG, N_BRANCH)
    gr = jnp.transpose(gr, (0, 3, 2, 1, 4)).reshape(bsz, rows, N_BRANCH)
    gr = jnp.pad(gr, ((0, 0), (0, 0), (0, LANES - N_BRANCH)))
    pad_rows = lambda a, n: jnp.pad(a, ((0, 0), (0, n - a.shape[1]), (0, 0)))
    xn = pad_rows(kv_c, SEL_BLOCK).reshape(bsz, SEL_BLOCK // CMP_STRIDE, CMP_STRIDE * ROW_W)
    xn = pad_rows(xn, 8)
    new_feat_major = lambda a: jnp.transpose(
        pad_rows(a, LANES).reshape(bsz, LANES, 2, KV_COLS), (0, 2, 3, 1))
    ksn = new_feat_major(kv_s)
    kwn = new_feat_major(kv_w)
    expand = (jnp.arange(LANES)[:, None] == (jnp.arange(SP_KEYS) // SEL_BLOCK)[None, :]).astype(BF16)
    page_block = (1, 2, NSA_G, NSA_DK, PAGE_SIZE)

    def page_map(k, phase):
        if phase == 0:
            return lambda b, j, pt: (pt[b, jnp.minimum(j, ns - 1) * SP_PP + k], 0, 0, 0, 0)
        return lambda b, j, pt: (pt[b, jnp.maximum(j - ns, 0) * SP_PP + k], 0, 0, 0, 0)

    per_b = lambda a: pl.BlockSpec((1,) + a.shape[1:], lambda b, j, pt: (b,) + (0,) * (a.ndim - 1))
    const = lambda a: pl.BlockSpec(a.shape, lambda b, j, pt: (0,) * a.ndim)
    in_specs = ([pl.BlockSpec(page_block, page_map(k, 0)) for k in range(SP_PP)]
                + [pl.BlockSpec(page_block, page_map(k, 1)) for k in range(SP_PP)]
                + [per_b(xn), per_b(ksn), per_b(win_buf), per_b(kwn), per_b(qbd), per_b(gr),
                   pl.BlockSpec(w1big.shape, lambda b, j, pt: (0, 0, 0), pipeline_mode=pl.Buffered(1)),
                   const(pos), const(pw), const(w2big), const(expand),
                   const(cb), const(sb), const(snb), const(wb)])
    grid_spec = pltpu.PrefetchScalarGridSpec(
        num_scalar_prefetch=1, grid=(bsz, 2 * ns), in_specs=in_specs,
        out_specs=pl.BlockSpec((1, rows, KV_COLS), lambda b, j, pt: (b, 0, 0)),
        scratch_shapes=[pltpu.VMEM((ns + 1, rows, LANES), F32),
                        pltpu.VMEM((ns + 1, LANES, KV_COLS), BF16),
                        pltpu.VMEM((8, KV_COLS), F32),
                        pltpu.VMEM((nsel_tiles, rows, LANES), BF16),
                        pltpu.VMEM((rows, 1), F32),
                        pltpu.VMEM((rows, 1), F32),
                        pltpu.VMEM((rows, KV_COLS), F32),
                        pltpu.VMEM((rows, KV_COLS), F32),
                        pltpu.VMEM((ROW_W // LANES, SP_KEYS, LANES), F32)])
    out = pl.pallas_call(
        functools.partial(_sample_nsa_kernel, ns=ns, past=past, dseq=dseq, n_sel=n_sel,
                          n_sel_blk=n_sel_blk),
        grid_spec=grid_spec,
        out_shape=jax.ShapeDtypeStruct((bsz, rows, KV_COLS), F32),
        compiler_params=_cparams(("parallel", "arbitrary")),
        name="nsa_sample",
    )(page_table, *([cache_cmp] * SP_PP), *([cache_slc] * SP_PP), xn, ksn, win_buf, kwn, qbd, gr,
      w1big, pos, pw, w2big, expand, cb, sb, snb, wb)
    o = out.reshape(bsz, NSA_HPG, NSA_G, dseq, NSA_G, NSA_DK)
    o = jnp.einsum('bhgtge->btghe', o)
    return o.reshape(bsz * dseq, NSA_WIDTH)


def _out_proj_kernel(ycm_ref, ynsa_ref, h_ref, gt_ref, gpost_ref, w1_ref, w2_ref, o_ref):
    out = _dot(ycm_ref[...].astype(BF16), w1_ref[...]) + _dot(ynsa_ref[...].astype(BF16), w2_ref[...])
    o_ref[...] = h_ref[...] + gt_ref[...] * _rms(out, gpost_ref[...])


def _out_proj(ycm, ynsa, h, gate, g_post, w1, w2, tm):
    r, d = h.shape
    return pl.pallas_call(
        _out_proj_kernel,
        grid=(r // tm,),
        in_specs=[pl.BlockSpec((tm, CM_WIDTH), lambda i: (i, 0)),
                  pl.BlockSpec((tm, NSA_WIDTH), lambda i: (i, 0)),
                  pl.BlockSpec((tm, d), lambda i: (i, 0)),
                  _row_spec(gate, tm),
                  pl.BlockSpec((1, d), lambda i: (0, 0)),
                  pl.BlockSpec(w1.shape, lambda i: (0, 0)),
                  pl.BlockSpec(w2.shape, lambda i: (0, 0))],
        out_specs=pl.BlockSpec((tm, d), lambda i: (i, 0)),
        out_shape=jax.ShapeDtypeStruct((r, d), F32),
        compiler_params=_cparams(("parallel",)),
        name="out_proj",
    )(ycm, ynsa, h, gate, g_post, w1, w2)


def _row_tile(r, cap):
    tm = min(r, cap)
    while r % tm:
        tm //= 2
    return tm


def _layer(x, mod, per_row, weights, in_proj, mixer):
    r, d = x.shape
    (norm_pre, norm_post, ffn, w_out1, w_out2) = weights
    if per_row is None:
        mrow = lambda i, j: mod[0, i, j].reshape(1, d)
    else:
        mrow = lambda i, j: jnp.repeat(mod[:, i, j], per_row, axis=0)
    tm = _row_tile(r, 512)
    tm_ffn = _row_tile(r, FFN_TM)
    tf = _row_tile(ffn[0][0].shape[1], FFN_TF)
    h = _ffn(x, mrow(0, 0), mrow(0, 1), mrow(0, 2), norm_pre[0:1], norm_post[0:1], *ffn[0],
             res_w=0.5, tm=tm_ffn, tf=tf)
    mixed_cm, mixed_nsa, state = mixer(*in_proj(h, mrow(1, 0), mrow(1, 1), norm_pre[1:2]))
    h = _out_proj(mixed_cm, mixed_nsa, h, mrow(1, 2), norm_post[1:2], w_out1, w_out2, tm)
    h = _ffn(h, mrow(2, 0), mrow(2, 1), mrow(2, 2), norm_pre[2:3], norm_post[2:3], *ffn[1],
             res_w=0.5, tm=tm_ffn, tf=tf)
    return h, state


def kernel(x_prompt, x_sample, cache_cmp_kv, cache_slc_kv, state_win_kv, page_table, c_prompt,
           c_sample, w_mod, b_mod, norm_pre, norm_post, ffn_w_gate, ffn_w_up, ffn_w_down, w_in,
           w_out, cm_norm, cm_ws, cm_bs, phi_pos, phi_w1, phi_w2, rel_bias):
    depth = w_mod.shape[0]
    assert depth == 1
    bp, seq, d = x_prompt.shape
    bs, dseq, _ = x_sample.shape
    assert bp == 1 and seq % KT_SEL == 0 and seq >= WINDOW
    l = 0

    c_all = jnp.concatenate([c_prompt, c_sample], axis=0)
    mpad = _round_up(c_all.shape[0], 8)
    mod = _mod_proj(jnp.pad(c_all, ((0, mpad - c_all.shape[0]), (0, 0))), w_mod[l], b_mod[l])
    mod = mod[:bp + bs].reshape(bp + bs, 3, 3, d)

    ffn = [(ffn_w_gate[l, i].astype(BF16), ffn_w_up[l, i].astype(BF16), ffn_w_down[l, i].astype(BF16))
           for i in range(2)]
    w_in_p = jnp.pad(w_in[l], ((0, 0), (0, sum(IN_SEGS) - w_in.shape[2]))).astype(BF16)
    w_rows, w_feat = _in_proj_prompt_weights(w_in[l])
    w_o = w_out[l].astype(BF16)
    weights = (norm_pre[l], norm_post[l], ffn, w_o[:CM_WIDTH], w_o[CM_WIDTH:])
    cw = _compress_weights(phi_pos[l], phi_w1[l], phi_w2[l])
    prompt_bias = _prompt_bias_tiles(rel_bias)

    def prompt_in_proj(h, shift, scale, g_pre):
        return _in_proj_prompt(h, shift, scale, g_pre, w_rows, w_feat)

    def prompt_mixer(u, v, kv_c, qt, gt, ks, vst, kw, vwt, kvt, y0):
        y_cm, _ = _chunk_mix(u, v, cm_norm[l], cm_ws[l], cm_bs[l], CHUNK)
        tok = _compress_tokens(kv_c[None], cw)[0]
        y_nsa = _prompt_attend(qt, gt, tok, ks, vst, kw, vwt, prompt_bias, y0)
        return y_cm, y_nsa, kvt

    assert seq % KT_SEL == 0
    hp, kvt = _layer(x_prompt.reshape(seq, d), mod[:bp], None, weights, prompt_in_proj, prompt_mixer)
    kvt = jnp.transpose(kvt.reshape(N_BRANCH, 2, NSA_G, NSA_DK, seq), (0, 4, 1, 2, 3))
    pc, ps, pw = kvt[0], kvt[1], kvt[2]

    rows = _round_up(dseq, 16)

    def sample_mixer(u, v, q, kv_c, kv_s, kv_w, gates):
        per_seq = lambda a: a.reshape(bs, dseq, -1)
        pad_rows = lambda a, n: jnp.pad(per_seq(a), ((0, 0), (0, n - dseq), (0, 0)))
        y_cm, vn = _chunk_mix(pad_rows(u, rows).reshape(bs * rows, CM_WIDTH),
                              pad_rows(v, rows).reshape(bs * rows, CM_WIDTH),
                              cm_norm[l], cm_ws[l], cm_bs[l], rows)
        y_cm = y_cm.reshape(bs, rows, CM_WIDTH)[:, :dseq].reshape(bs * dseq, CM_WIDTH)
        vn = vn.reshape(bs, rows, CM_WIDTH)[:, :dseq]
        y_nsa = _sample_attend(per_seq(q), per_seq(gates), per_seq(kv_c), per_seq(kv_s), per_seq(kv_w),
                               cache_cmp_kv[l], cache_slc_kv[l], state_win_kv[l], page_table, cw,
                               rel_bias)
        return y_cm, y_nsa, (kv_c, kv_s, kv_w, vn)

    def sample_in_proj(h, shift, scale, g_pre):
        return _in_proj(h, shift, scale, g_pre, w_in_p, _row_tile(h.shape[0], 256))

    hs, (sc, ss, sw, sv) = _layer(x_sample.reshape(bs * dseq, d), mod[bp:], dseq, weights,
                                  sample_in_proj, sample_mixer)

    kvshape = lambda a, b, t: a.reshape(1, b, t, 2, NSA_G, NSA_DK)
    wp = min(WINDOW, seq)
    win_s = jnp.concatenate([state_win_kv[l].reshape(bs, -1, ROW_W), sw.reshape(bs, dseq, ROW_W)],
                            axis=1)[:, dseq:]
    return (hp.reshape(bp, seq, d), hs.reshape(bs, dseq, d),
            kvshape(pc, bp, seq), kvshape(sc, bs, dseq),
            kvshape(ps, bp, seq), kvshape(ss, bs, dseq),
            kvshape(pw[seq - wp:], bp, wp), kvshape(win_s, bs, win_s.shape[1]),
            sv.reshape(1, bs, dseq, CM_HEADS, CM_HEAD_DIM))
```
